```python
import jax, jax.numpy as jnp
from jax import lax
import numpy as np

D_MODEL = 1024
BATCH = 32
SEQ = 256
DEPTH = 4
DEC_BATCH = 2
DEC_SEQ = 1024
PAST_LEN = 256

GRID_W = 64
HEAD_DIM = 64
A_HEADS = 8
A_KV_HEADS = 2
B_HEADS = 8
B_KV_HEADS = 2
WINDOW = 128
Q_BLOCK = 128
ROPE_BASE = 10000.0
A_WIDTH = A_HEADS * HEAD_DIM
B_WIDTH = B_HEADS * HEAD_DIM
KV_A = A_KV_HEADS * HEAD_DIM
KV_B = B_KV_HEADS * HEAD_DIM
EVEN_IN = 2 * A_WIDTH + 2 * KV_A + 2 * B_WIDTH + 2 * KV_B
EVEN_MIX = A_WIDTH + B_WIDTH
C_WIDTH = D_MODEL
C_GROUPS = 4
C_GROUP_DIM = C_WIDTH // C_GROUPS
ODD_IN = 2 * C_WIDTH
N_EVEN = (DEPTH + 1) // 2
N_ODD = DEPTH // 2
EPS = 1e-6
NEG_BIG = -1e30

kernel_name = "hybrid_dit_prefix_attn_fourier_step"


def rms_norm(x, g):
    xf = x.astype(jnp.float32)
    y = xf * lax.rsqrt(jnp.mean(xf * xf, axis=-1, keepdims=True) + EPS)
    return (y * g.astype(jnp.float32)).astype(x.dtype)


def ada_mod(cvec, w, b):
    m = jax.nn.silu(cvec) @ w + b
    return jnp.split(m, 3, axis=-1)


def modulate(h, shift, scale):
    return h * (1 + scale) + shift


def grid_rope_tables(n_tok):
    rows = n_tok // GRID_W
    row = jnp.repeat(jnp.arange(rows), GRID_W).astype(jnp.float32)
    col = jnp.tile(jnp.arange(GRID_W), rows).astype(jnp.float32)
    half = HEAD_DIM // 2
    inv = 1.0 / (ROPE_BASE ** (jnp.arange(0, half, 2, dtype=jnp.float32) / half))
    ang = jnp.concatenate([row[:, None] * inv, col[:, None] * inv], axis=-1)
    return jnp.cos(ang), jnp.sin(ang)


def apply_rope(x, cos, sin):
    b, s, h, d = x.shape
    xr = x.astype(jnp.float32).reshape(b, s, h, 2, 2, d // 4)
    x1, x2 = xr[..., 0, :], xr[..., 1, :]
    c = cos.reshape(1, s, 1, 2, d // 4)
    sn = sin.reshape(1, s, 1, 2, d // 4)
    y = jnp.stack([x1 * c - x2 * sn, x2 * c + x1 * sn], axis=-2)
    return y.reshape(b, s, h, d).astype(x.dtype)


def softmax_sink(sc, sink):
    sink = sink.astype(jnp.float32)
    m = jnp.maximum(jnp.max(sc, axis=-1, keepdims=True), sink)
    p = jnp.exp(sc - m)
    return p / (jnp.sum(p, axis=-1, keepdims=True) + jnp.exp(sink - m))


def dense_block_attention(q, k, v, sink=None):
    b, s, h, d = q.shape
    kvh = k.shape[2]
    g = h // kvh
    nb = s // Q_BLOCK
    qb = (q * (d ** -0.5)).reshape(b, nb, Q_BLOCK, kvh, g, d).transpose(1, 0, 2, 3, 4, 5)

    def one_block(qblk):
        sc = jnp.einsum("bqkgd,btkd->bkgqt", qblk, k).astype(jnp.float32)
        if sink is None:
            p = jax.nn.softmax(sc, axis=-1)
        else:
            p = softmax_sink(sc, sink.reshape(1, kvh, g, 1, 1))
        return jnp.einsum("bkgqt,btkd->bqkgd", p.astype(v.dtype), v)

    o = lax.map(one_block, qb)
    return o.transpose(1, 0, 2, 3, 4, 5).reshape(b, s, h * d)


def window_attention_with_ctx(q, k, v, k_ctx, v_ctx, sink):
    b, s, h, d = q.shape
    kvh = k.shape[2]
    g = h // kvh
    nb = s // Q_BLOCK
    nband = 2 * (WINDOW // Q_BLOCK) + 1
    L = nband * Q_BLOCK
    pad = ((0, 0), (WINDOW, WINDOW), (0, 0), (0, 0))
    kp = jnp.pad(k, pad).reshape(b, nb + nband - 1, Q_BLOCK, kvh, d)
    vp = jnp.pad(v, pad).reshape(b, nb + nband - 1, Q_BLOCK, kvh, d)
    kband = jnp.concatenate([kp[:, j:j + nb] for j in range(nband)], axis=2)
    vband = jnp.concatenate([vp[:, j:j + nb] for j in range(nband)], axis=2)
    qi = jnp.arange(Q_BLOCK)[:, None]
    kj = jnp.arange(L)[None, :]
    rel = kj - WINDOW - qi
    kpos = jnp.arange(nb)[:, None, None] * Q_BLOCK + kj[None] - WINDOW
    mask = (jnp.abs(rel)[None] <= WINDOW) & (kpos >= 0) & (kpos < s)
    qb = (q * (d ** -0.5)).reshape(b, nb, Q_BLOCK, kvh, g, d)
    s_loc = jnp.einsum("bnqkgd,bnlkd->bnkgql", qb, kband).astype(jnp.float32)
    s_loc = jnp.where(mask[None, :, None, None], s_loc, NEG_BIG)
    s_ctx = jnp.einsum("bnqkgd,btkd->bnkgqt", qb, k_ctx).astype(jnp.float32)
    p = softmax_sink(jnp.concatenate([s_loc, s_ctx], axis=-1), sink.reshape(1, 1, kvh, g, 1, 1))
    p = p.astype(v.dtype)
    o = (jnp.einsum("bnkgql,bnlkd->bnqkgd", p[..., :L], vband)
         + jnp.einsum("bnkgqt,btkd->bnqkgd", p[..., L:], v_ctx))
    return o.reshape(b, s, h * d)


def split_even(p):
    sizes = [A_WIDTH, KV_A, KV_A, A_WIDTH, B_WIDTH, KV_B, KV_B, B_WIDTH]
    idx = [int(i) for i in np.cumsum(sizes)[:-1]]
    return jnp.split(p, idx, axis=-1)


def even_projections(h, w_in, gq, gk):
    b, s, _ = h.shape
    qa, ka, va, ga, qb, kb, vb, gb = split_even(h @ w_in)
    qa = rms_norm(qa.reshape(b, s, A_HEADS, HEAD_DIM), gq)
    ka = rms_norm(ka.reshape(b, s, A_KV_HEADS, HEAD_DIM), gk)
    va = va.reshape(b, s, A_KV_HEADS, HEAD_DIM)
    qb = qb.reshape(b, s, B_HEADS, HEAD_DIM)
    kb = kb.reshape(b, s, B_KV_HEADS, HEAD_DIM)
    vb = vb.reshape(b, s, B_KV_HEADS, HEAD_DIM)
    return qa, ka, va, ga, qb, kb, vb, gb


def even_ctx(h, w_in, w_out, gq, gk, sink):
    qa, ka, va, ga, qb, kb, vb, gb = even_projections(h, w_in, gq, gk)
    oa = dense_block_attention(qa, ka, va)
    ob = dense_block_attention(qb, kb, vb, sink)
    y = jnp.concatenate([oa * jax.nn.silu(ga), ob * jax.nn.silu(gb)], axis=-1) @ w_out
    return y, ka, va, kb, vb


def even_lat(h, w_in, w_out, gq, gk, sink, ka_ctx, va_ctx, kb_ctx, vb_ctx, cos, sin):
    qa, ka, va, ga, qb, kb, vb, gb = even_projections(h, w_in, gq, gk)
    qa, ka = apply_rope(qa, cos, sin), apply_rope(ka, cos, sin)
    qb, kb = apply_rope(qb, cos, sin), apply_rope(kb, cos, sin)
    oa = dense_block_attention(qa, jnp.concatenate([ka, ka_ctx], axis=1),
                               jnp.concatenate([va, va_ctx], axis=1))
    ob = window_attention_with_ctx(qb, kb, vb, kb_ctx, vb_ctx, sink)
    return jnp.concatenate([oa * jax.nn.silu(ga), ob * jax.nn.silu(gb)], axis=-1) @ w_out


def fourier_mix(u):
    b, s, _ = u.shape
    ug = u.astype(jnp.float32).reshape(b, s, C_GROUPS, C_GROUP_DIM)
    f = jnp.fft.fft2(ug, axes=(1, 3), norm="ortho")
    return jnp.real(f).reshape(b, s, C_WIDTH).astype(u.dtype)


def odd_mix(h, w_in, w_out):
    u, gate = jnp.split(h @ w_in, 2, axis=-1)
    return (fourier_mix(u) * jax.nn.silu(gate)) @ w_out


def setup_inputs(seed: int = 0) -> dict:
    key = jax.random.key(seed)
    ks = jax.random.split(key, 20)

    def nrm(k, shape, s):
        return jax.random.normal(k, shape, jnp.float32) * s

    cache_shape = (DEC_BATCH, N_EVEN, PAST_LEN, A_KV_HEADS, HEAD_DIM)
    return {
        "x_prompt": nrm(ks[0], (BATCH, SEQ, D_MODEL), 1.0),
        "x_sample": nrm(ks[1], (DEC_BATCH, DEC_SEQ, D_MODEL), 1.0),
        "cache_k_a": nrm(ks[2], cache_shape, 1.0),
        "cache_v_a": nrm(ks[3], cache_shape, 1.0),
        "cache_k_b": nrm(ks[4], (DEC_BATCH, N_EVEN, PAST_LEN, B_KV_HEADS, HEAD_DIM), 1.0),
        "cache_v_b": nrm(ks[5], (DEC_BATCH, N_EVEN, PAST_LEN, B_KV_HEADS, HEAD_DIM), 1.0),
        "c": nrm(ks[6], (DEC_BATCH, D_MODEL), 1.0),
        "c_ctx": nrm(ks[7], (D_MODEL,), 1.0),
        "norm_g": 1.0 + nrm(ks[8], (DEPTH, D_MODEL), 0.02),
        "ada_w": nrm(ks[9], (DEPTH, D_MODEL, 3 * D_MODEL), 0.5 * D_MODEL ** -0.5),
        "ada_b": nrm(ks[10], (DEPTH, 3 * D_MODEL), 0.02),
        "even_w_in": nrm(ks[11], (N_EVEN, D_MODEL, EVEN_IN), D_MODEL ** -0.5),
        "even_w_out": nrm(ks[12], (N_EVEN, EVEN_MIX, D_MODEL), EVEN_MIX ** -0.5),
        "qk_g_q": 1.0 + nrm(ks[13], (N_EVEN, HEAD_DIM), 0.02),
        "qk_g_k": 1.0 + nrm(ks[14], (N_EVEN, HEAD_DIM), 0.02),
        "sink_logit": nrm(ks[15], (N_EVEN, B_HEADS), 0.5),
        "odd_w_in": nrm(ks[16], (N_ODD, D_MODEL, ODD_IN), D_MODEL ** -0.5),
        "odd_w_out": nrm(ks[17], (N_ODD, C_WIDTH, D_MODEL), C_WIDTH ** -0.5),
        "final_g": 1.0 + nrm(ks[18], (D_MODEL,), 0.02),
    }


def reference(x_prompt, x_sample, cache_k_a, cache_v_a, cache_k_b, cache_v_b, c, c_ctx,
              norm_g, ada_w, ada_b, even_w_in, even_w_out, qk_g_q, qk_g_k, sink_logit,
              odd_w_in, odd_w_out, final_g):
    cos, sin = grid_rope_tables(x_sample.shape[1])
    xc = x_prompt
    xl = x_sample
    ka_list, va_list, kb_list, vb_list = [], [], [], []
    for l in range(DEPTH):
        i = l // 2
        shc, scc, gtc = ada_mod(c_ctx, ada_w[l], ada_b[l])
        shl, scl, gtl = ada_mod(c, ada_w[l], ada_b[l])
        hc = modulate(rms_norm(xc, norm_g[l]), shc, scc)
        hl = modulate(rms_norm(xl, norm_g[l]), shl[:, None], scl[:, None])
        if l % 2 == 0:
            yc, ka, va, kb, vb = even_ctx(hc, even_w_in[i], even_w_out[i],
                                          qk_g_q[i], qk_g_k[i], sink_logit[i])
            ka_list.append(ka)
            va_list.append(va)
            kb_list.append(kb)
            vb_list.append(vb)
            yl = even_lat(hl, even_w_in[i], even_w_out[i], qk_g_q[i], qk_g_k[i], sink_logit[i],
                          cache_k_a[:, i], cache_v_a[:, i], cache_k_b[:, i], cache_v_b[:, i],
                          cos, sin)
        else:
            yc = odd_mix(hc, odd_w_in[i], odd_w_out[i])
            yl = odd_mix(hl, odd_w_in[i], odd_w_out[i])
        xc = xc + gtc * yc
        xl = xl + gtl[:, None] * yl
    y_prompt = rms_norm(xc, final_g)
    y_sample = rms_norm(xl, final_g)
    new_k_a = jnp.stack(ka_list, axis=1)
    new_v_a = jnp.stack(va_list, axis=1)
    new_k_b = jnp.stack(kb_list, axis=1)
    new_v_b = jnp.stack(vb_list, axis=1)
    return (y_prompt, y_sample, new_k_a, new_v_a, new_k_b, new_v_b)
```

```python
import functools

import numpy as np
import jax
import jax.numpy as jnp
from jax import lax
from jax.experimental import pallas as pl
from jax.experimental.pallas import tpu as pltpu

D_MODEL = 1024
DEPTH = 4
HEAD_DIM = 64
N_HEADS = 8
N_KV = 2
GROUP = N_HEADS // N_KV
MIX_W = N_HEADS * HEAD_DIM
KV_W = N_KV * HEAD_DIM
EVEN_IN = 2 * (2 * MIX_W + 2 * KV_W)
ODD_IN = 2 * D_MODEL
GRID_W = 64
WINDOW = 128
ROPE_BASE = 10000.0
C_GROUPS = 4
C_GROUP_DIM = D_MODEL // C_GROUPS
EPS = 1e-6
NEG_BIG = -1e30
ROWS = 256
COND_ROWS = 8
VMEM_LIMIT = 48 * 1024 * 1024

QA, KA, VA, GA = 0, 512, 640, 768
QB, KB, VB, GB = 1280, 1792, 1920, 2048

F32 = jnp.float32
BF16 = jnp.bfloat16


def _dft_tables(n):
    k = np.arange(n, dtype=np.int64)
    ang = ((k[:, None] * k[None, :]) % n).astype(np.float64) * (2.0 * np.pi / n)
    return np.cos(ang).astype(np.float32), np.sin(ang).astype(np.float32)


def _channel_dft():
    c, s = _dft_tables(C_GROUP_DIM)
    return np.concatenate([c, s], axis=1)


def _position_dft(n):
    c, s = _dft_tables(n)
    return np.concatenate([c, -s], axis=1)


def _rope_tables(n_tok):
    rows = n_tok // GRID_W
    row = np.repeat(np.arange(rows), GRID_W).astype(np.float64)
    col = np.tile(np.arange(GRID_W), rows).astype(np.float64)
    half = HEAD_DIM // 2
    inv = 1.0 / (ROPE_BASE ** (np.arange(0, half, 2, dtype=np.float64) / half))
    ang_r = row[:, None] * inv
    ang_c = col[:, None] * inv
    zeros = np.zeros_like(ang_r)
    cos_h = np.concatenate([np.cos(ang_r), np.cos(ang_r), np.cos(ang_c), np.cos(ang_c)], axis=1)
    nxt_h = np.concatenate([-np.sin(ang_r), zeros, -np.sin(ang_c), zeros], axis=1)
    prv_h = np.concatenate([zeros, np.sin(ang_r), zeros, np.sin(ang_c)], axis=1)
    two = lambda t: np.concatenate([t, t], axis=1).astype(np.float32)
    return two(cos_h), two(nxt_h), two(prv_h)


def _silu(x):
    return x / (1.0 + jnp.exp(-x))


def _norm_mod(x, g, shift, scale):
    ms = jnp.mean(x * x, axis=-1, keepdims=True)
    return x * lax.rsqrt(ms + EPS) * (g * (1.0 + scale)) + shift


def _rms_head(xh, g):
    ms = jnp.mean(xh * xh, axis=-1, keepdims=True)
    return xh * lax.rsqrt(ms + EPS) * g


def _rms_pair(xc, g2):
    lo = lax.broadcasted_iota(jnp.int32, xc.shape, 1) < HEAD_DIM
    ss = xc * xc
    s_lo = jnp.sum(jnp.where(lo, ss, 0.0), axis=-1, keepdims=True)
    s_hi = jnp.sum(jnp.where(lo, 0.0, ss), axis=-1, keepdims=True)
    ms = jnp.where(lo, s_lo, s_hi) * (1.0 / HEAD_DIM)
    return xc * lax.rsqrt(ms + EPS) * g2


def _rope(xc, cos, sin_next, sin_prev):
    nxt = pltpu.roll(xc, 128 - 16, 1)
    prv = pltpu.roll(xc, 16, 1)
    return xc * cos + nxt * sin_next + prv * sin_prev


def _attend(q, k, v, sink=None, mask=None):
    s = lax.dot_general(q.astype(BF16), k, (((1,), (1,)), ((), ())), preferred_element_type=F32)
    if mask is not None:
        s = jnp.where(mask, s, NEG_BIG)
    m = jnp.max(s, axis=-1, keepdims=True)
    if sink is not None:
        m = jnp.maximum(m, sink)
    e = jnp.exp(s - m)
    l = jnp.sum(e, axis=-1, keepdims=True)
    if sink is not None:
        l = l + jnp.exp(sink - m)
    o = jnp.dot(e.astype(BF16), v, preferred_element_type=F32)
    return o / l


def _mod_rows(mod_ref, row):
    return mod_ref[0, pl.ds(row, 1), :], mod_ref[1, pl.ds(row, 1), :], mod_ref[2, pl.ds(row, 1), :]


def _ada_kernel(cond_ref, w_ref, b_ref, o_ref):
    a = _silu(cond_ref[...]).astype(BF16)
    o_ref[...] = jnp.dot(a, w_ref[...].astype(BF16), preferred_element_type=F32) + b_ref[...]


def _ada_all(cond, ada_w, ada_b):
    return pl.pallas_call(
        _ada_kernel,
        grid=(DEPTH, 3),
        in_specs=[
            pl.BlockSpec((COND_ROWS, D_MODEL), lambda l, p: (0, 0)),
            pl.BlockSpec((None, D_MODEL, D_MODEL), lambda l, p: (l, 0, p)),
            pl.BlockSpec((None, None, 1, D_MODEL), lambda l, p: (l, p, 0, 0)),
        ],
        out_specs=pl.BlockSpec((None, None, COND_ROWS, D_MODEL), lambda l, p: (l, p, 0, 0)),
        out_shape=jax.ShapeDtypeStruct((DEPTH, 3, COND_ROWS, D_MODEL), F32),
        compiler_params=pltpu.CompilerParams(
            dimension_semantics=("parallel", "parallel"), vmem_limit_bytes=VMEM_LIMIT),
        name="ada_mod",
    )(cond, ada_w, ada_b.reshape(DEPTH, 3, 1, D_MODEL))


def _ctx_even_kernel(sink_ref, x_ref, mod_ref, g_ref, win_ref, wout_ref, gq_ref, gk_ref,
                     xo_ref, ka_ref, va_ref, kb_ref, vb_ref, p_ref, o_ref, *, layer_i):
    shift, scale, gate = _mod_rows(mod_ref, 0)
    h = _norm_mod(x_ref[0], g_ref[...], shift, scale)
    p_ref[...] = jnp.dot(h.astype(BF16), win_ref[...], preferred_element_type=F32)
    gq = gq_ref[...] * (HEAD_DIM ** -0.5)
    gk = gk_ref[...]
    for mixer, (q0, k0, v0, g0) in enumerate(((QA, KA, VA, GA), (QB, KB, VB, GB))):
        k_out, v_out = (ka_ref, va_ref) if mixer == 0 else (kb_ref, vb_ref)
        for j in range(N_KV):
            k = p_ref[:, k0 + HEAD_DIM * j:k0 + HEAD_DIM * (j + 1)]
            v = p_ref[:, v0 + HEAD_DIM * j:v0 + HEAD_DIM * (j + 1)]
            if mixer == 0:
                k = _rms_head(k, gk)
            k_out[0, :, HEAD_DIM * j:HEAD_DIM * (j + 1)] = k
            v_out[0, :, HEAD_DIM * j:HEAD_DIM * (j + 1)] = v
            kb16, vb16 = k.astype(BF16), v.astype(BF16)
            for hh in range(GROUP):
                head = GROUP * j + hh
                q = p_ref[:, q0 + HEAD_DIM * head:q0 + HEAD_DIM * (head + 1)]
                if mixer == 0:
                    q = _rms_head(q, gq)
                    sink = None
                else:
                    q = q * (HEAD_DIM ** -0.5)
                    sink = sink_ref[layer_i, head]
                o = _attend(q, kb16, vb16, sink)
                gt = p_ref[:, g0 + HEAD_DIM * head:g0 + HEAD_DIM * (head + 1)]
                c0 = mixer * MIX_W + HEAD_DIM * head
                o_ref[:, c0:c0 + HEAD_DIM] = o * _silu(gt)
    y = jnp.dot(o_ref[...].astype(BF16), wout_ref[...], preferred_element_type=F32)
    xo_ref[0] = x_ref[0] + gate * y


def _ctx_even(x, mod_l, g, w_in, w_out, gq, gk, sink, layer_i):
    b, s, _ = x.shape
    kv_shape = jax.ShapeDtypeStruct((b, s, KV_W), F32)
    kv_spec = pl.BlockSpec((1, s, KV_W), lambda i: (i, 0, 0))
    const2 = lambda i: (0, 0)
    return pl.pallas_call(
        functools.partial(_ctx_even_kernel, layer_i=layer_i),
        grid=(b,),
        in_specs=[
            pl.BlockSpec(memory_space=pltpu.SMEM),
            pl.BlockSpec((1, s, D_MODEL), lambda i: (i, 0, 0)),
            pl.BlockSpec((3, COND_ROWS, D_MODEL), lambda i: (0, 0, 0)),
            pl.BlockSpec((1, D_MODEL), const2),
            pl.BlockSpec((D_MODEL, EVEN_IN), const2),
            pl.BlockSpec((D_MODEL, D_MODEL), const2),
            pl.BlockSpec((1, HEAD_DIM), const2),
            pl.BlockSpec((1, HEAD_DIM), const2),
        ],
        out_specs=[pl.BlockSpec((1, s, D_MODEL), lambda i: (i, 0, 0)), kv_spec, kv_spec, kv_spec, kv_spec],
        out_shape=[jax.ShapeDtypeStruct(x.shape, F32), kv_shape, kv_shape, kv_shape, kv_shape],
        scratch_shapes=[pltpu.VMEM((s, EVEN_IN), F32), pltpu.VMEM((s, D_MODEL), F32)],
        compiler_params=pltpu.CompilerParams(
            dimension_semantics=("parallel",), vmem_limit_bytes=VMEM_LIMIT),
        name="ctx_even",
    )(sink, x, mod_l, g, w_in, w_out, gq, gk)


def _fourier_rows(p_ref, ccs_ref, ucs_ref, rows):
    for grp in range(C_GROUPS):
        c0 = C_GROUP_DIM * grp
        ug = p_ref[:, c0:c0 + C_GROUP_DIM].astype(BF16)
        t = jnp.dot(ug, ccs_ref[...], preferred_element_type=F32)
        ucs_ref[0:rows, c0:c0 + C_GROUP_DIM] = t[:, :C_GROUP_DIM].astype(BF16)
        ucs_ref[rows:2 * rows, c0:c0 + C_GROUP_DIM] = t[:, C_GROUP_DIM:].astype(BF16)


def _ctx_odd_kernel(x_ref, mod_ref, g_ref, win_ref, wout_ref, ccs_ref, fs_ref, fg_ref,
                    xo_ref, p_ref, ucs_ref, *, final):
    rows = x_ref.shape[1]
    shift, scale, gate = _mod_rows(mod_ref, 0)
    h = _norm_mod(x_ref[0], g_ref[...], shift, scale)
    p_ref[...] = jnp.dot(h.astype(BF16), win_ref[...], preferred_element_type=F32)
    _fourier_rows(p_ref, ccs_ref, ucs_ref, rows)
    r = jnp.dot(fs_ref[...], ucs_ref[...], preferred_element_type=F32)
    r = r * float(1.0 / np.sqrt(float(rows * C_GROUP_DIM)))
    mix = r * _silu(p_ref[:, D_MODEL:2 * D_MODEL])
    y = jnp.dot(mix.astype(BF16), wout_ref[...], preferred_element_type=F32)
    xn = x_ref[0] + gate * y
    if final:
        xn = _rms_head(xn, fg_ref[...])
    xo_ref[0] = xn


def _ctx_odd(x, mod_l, g, w_in, w_out, ccs, fs, fg, final):
    b, s, _ = x.shape
    const2 = lambda i: (0, 0)
    return pl.pallas_call(
        functools.partial(_ctx_odd_kernel, final=final),
        grid=(b,),
        in_specs=[
            pl.BlockSpec((1, s, D_MODEL), lambda i: (i, 0, 0)),
            pl.BlockSpec((3, COND_ROWS, D_MODEL), lambda i: (0, 0, 0)),
            pl.BlockSpec((1, D_MODEL), const2),
            pl.BlockSpec((D_MODEL, ODD_IN), const2),
            pl.BlockSpec((D_MODEL, D_MODEL), const2),
            pl.BlockSpec((C_GROUP_DIM, 2 * C_GROUP_DIM), const2),
            pl.BlockSpec((s, 2 * s), const2),
            pl.BlockSpec((1, D_MODEL), const2),
        ],
        out_specs=pl.BlockSpec((1, s, D_MODEL), lambda i: (i, 0, 0)),
        out_shape=jax.ShapeDtypeStruct(x.shape, F32),
        scratch_shapes=[pltpu.VMEM((s, ODD_IN), F32), pltpu.VMEM((2 * s, D_MODEL), BF16)],
        compiler_params=pltpu.CompilerParams(
            dimension_semantics=("parallel",), vmem_limit_bytes=VMEM_LIMIT),
        name="ctx_odd",
    )(x, mod_l, g, w_in, w_out, ccs, fs, fg)


def _lat_even_proj_kernel(x_ref, mod_ref, g_ref, win_ref, gq_ref, gk_ref, cos_ref, sn_ref, sp_ref,
                          p_ref):
    shift, scale, _ = _mod_rows(mod_ref, 1 + pl.program_id(0))
    h = _norm_mod(x_ref[0], g_ref[...], shift, scale)
    p_ref[0] = jnp.dot(h.astype(BF16), win_ref[...], preferred_element_type=F32)
    cos, sn, sp = cos_ref[...], sn_ref[...], sp_ref[...]
    gq2 = gq_ref[...] * (HEAD_DIM ** -0.5)
    gk2 = gk_ref[...]

    def chunk(c0):
        return p_ref[0, :, c0:c0 + 128]

    for c0 in range(QA, QA + MIX_W, 128):
        p_ref[0, :, c0:c0 + 128] = _rope(_rms_pair(chunk(c0), gq2), cos, sn, sp)
    p_ref[0, :, KA:KA + 128] = _rope(_rms_pair(chunk(KA), gk2), cos, sn, sp)
    for c0 in range(QB, QB + MIX_W, 128):
        p_ref[0, :, c0:c0 + 128] = _rope(chunk(c0) * (HEAD_DIM ** -0.5), cos, sn, sp)
    p_ref[0, :, KB:KB + 128] = _rope(chunk(KB), cos, sn, sp)
    for g0 in (GA, GB):
        for c0 in range(g0, g0 + MIX_W, 128):
            p_ref[0, :, c0:c0 + 128] = _silu(chunk(c0))


def _lat_even_proj(x, mod_l, g, w_in, gq2, gk2, cos, sn, sp):
    b, s, _ = x.shape
    const2 = lambda i, n: (0, 0)
    rope_spec = pl.BlockSpec((ROWS, 128), lambda i, n: (n, 0))
    return pl.pallas_call(
        _lat_even_proj_kernel,
        grid=(b, s // ROWS),
        in_specs=[
            pl.BlockSpec((1, ROWS, D_MODEL), lambda i, n: (i, n, 0)),
            pl.BlockSpec((3, COND_ROWS, D_MODEL), lambda i, n: (0, 0, 0)),
            pl.BlockSpec((1, D_MODEL), const2),
            pl.BlockSpec((D_MODEL, EVEN_IN), const2),
            pl.BlockSpec((1, 128), const2),
            pl.BlockSpec((1, 128), const2),
            rope_spec, rope_spec, rope_spec,
        ],
        out_specs=pl.BlockSpec((1, ROWS, EVEN_IN), lambda i, n: (i, n, 0)),
        out_shape=jax.ShapeDtypeStruct((b, s, EVEN_IN), F32),
        compiler_params=pltpu.CompilerParams(
            dimension_semantics=("parallel", "parallel"), vmem_limit_bytes=VMEM_LIMIT),
        name="lat_even_proj",
    )(x, mod_l, g, w_in, gq2, gk2, cos, sn, sp)


def _lat_even_attn_kernel(sink_ref, x_ref, mod_ref, pq_ref, ka_ref, va_ref, kb_ref, vb_ref,
                          cka_ref, cva_ref, ckb_ref, cvb_ref, wout_ref, xo_ref, o_ref, *, layer_i):
    n = pl.program_id(1)
    seq = ka_ref.shape[1]
    _, _, gate = _mod_rows(mod_ref, 1 + pl.program_id(0))

    prev0 = pl.multiple_of(jnp.maximum(n * ROWS - WINDOW, 0), WINDOW)
    own0 = pl.multiple_of(n * ROWS, ROWS)
    next0 = pl.multiple_of(jnp.minimum(n * ROWS + ROWS, seq - WINDOW), WINDOW)
    win_len = ROWS + 2 * WINDOW
    ctx_len = cka_ref.shape[0]
    qi = lax.broadcasted_iota(jnp.int32, (ROWS, win_len + ctx_len), 0)
    cj = lax.broadcasted_iota(jnp.int32, (ROWS, win_len + ctx_len), 1)
    kpos = n * ROWS - WINDOW + cj
    in_win = (jnp.abs(cj - WINDOW - qi) <= WINDOW) & (kpos >= 0) & (kpos < seq)
    mask_b = in_win | (cj >= win_len)

    for j in range(N_KV):
        hs = slice(HEAD_DIM * j, HEAD_DIM * (j + 1))
        k_a = jnp.concatenate([ka_ref[0, :, hs], cka_ref[:, hs]], axis=0).astype(BF16)
        v_a = jnp.concatenate([va_ref[0, :, hs], cva_ref[:, hs]], axis=0).astype(BF16)
        k_b = jnp.concatenate([kb_ref[0, pl.ds(prev0, WINDOW), hs], kb_ref[0, pl.ds(own0, ROWS), hs],
                               kb_ref[0, pl.ds(next0, WINDOW), hs], ckb_ref[:, hs]], axis=0).astype(BF16)
        v_b = jnp.concatenate([vb_ref[0, pl.ds(prev0, WINDOW), hs], vb_ref[0, pl.ds(own0, ROWS), hs],
                               vb_ref[0, pl.ds(next0, WINDOW), hs], cvb_ref[:, hs]], axis=0).astype(BF16)
        for hh in range(GROUP):
            head = GROUP * j + hh
            cs = HEAD_DIM * head
            oa = _attend(pq_ref[0, :, QA + cs:QA + cs + HEAD_DIM], k_a, v_a)
            o_ref[:, cs:cs + HEAD_DIM] = oa * pq_ref[0, :, GA + cs:GA + cs + HEAD_DIM]
            ob = _attend(pq_ref[0, :, QB + cs:QB + cs + HEAD_DIM], k_b, v_b,
                         sink=sink_ref[layer_i, head], mask=mask_b)
            o_ref[:, MIX_W + cs:MIX_W + cs + HEAD_DIM] = ob * pq_ref[0, :, GB + cs:GB + cs + HEAD_DIM]
    y = jnp.dot(o_ref[...].astype(BF16), wout_ref[...], preferred_element_type=F32)
    xo_ref[0] = x_ref[0] + gate * y


def _lat_even_attn(x, mod_l, p, cka, cva, ckb, cvb, w_out, sink, layer_i):
    b, s, _ = x.shape
    past = cka.shape[2]
    kv_spec = lambda blk: pl.BlockSpec((1, s, KV_W), lambda i, n, blk=blk: (i, 0, blk))
    cache_spec = pl.BlockSpec((None, None, past, KV_W), lambda i, n: (i, layer_i, 0, 0))
    return pl.pallas_call(
        functools.partial(_lat_even_attn_kernel, layer_i=layer_i),
        grid=(b, s // ROWS),
        in_specs=[
            pl.BlockSpec(memory_space=pltpu.SMEM),
            pl.BlockSpec((1, ROWS, D_MODEL), lambda i, n: (i, n, 0)),
            pl.BlockSpec((3, COND_ROWS, D_MODEL), lambda i, n: (0, 0, 0)),
            pl.BlockSpec((1, ROWS, EVEN_IN), lambda i, n: (i, n, 0)),
            kv_spec(KA // KV_W), kv_spec(VA // KV_W), kv_spec(KB // KV_W), kv_spec(VB // KV_W),
            cache_spec, cache_spec, cache_spec, cache_spec,
            pl.BlockSpec((D_MODEL, D_MODEL), lambda i, n: (0, 0)),
        ],
        out_specs=pl.BlockSpec((1, ROWS, D_MODEL), lambda i, n: (i, n, 0)),
        out_shape=jax.ShapeDtypeStruct(x.shape, F32),
        scratch_shapes=[pltpu.VMEM((ROWS, D_MODEL), F32)],
        compiler_params=pltpu.CompilerParams(
            dimension_semantics=("parallel", "parallel"), vmem_limit_bytes=VMEM_LIMIT),
        name="lat_even_attn",
    )(sink, x, mod_l, p, p, p, p, p, cka, cva, ckb, cvb, w_out)


def _lat_odd_proj_kernel(x_ref, mod_ref, g_ref, win_ref, ccs_ref, ucs_ref, gate_ref, p_ref, t_ref):
    shift, scale, _ = _mod_rows(mod_ref, 1 + pl.program_id(0))
    h = _norm_mod(x_ref[0], g_ref[...], shift, scale)
    p_ref[...] = jnp.dot(h.astype(BF16), win_ref[...], preferred_element_type=F32)
    _fourier_rows(p_ref, ccs_ref, t_ref, ROWS)
    ucs_ref[0, 0] = t_ref[0:ROWS, :]
    ucs_ref[0, 1] = t_ref[ROWS:2 * ROWS, :]
    gate_ref[0] = _silu(p_ref[:, D_MODEL:2 * D_MODEL])


def _lat_odd_proj(x, mod_l, g, w_in, ccs):
    b, s, _ = x.shape
    const2 = lambda i, n: (0, 0)
    return pl.pallas_call(
        _lat_odd_proj_kernel,
        grid=(b, s // ROWS),
        in_specs=[
            pl.BlockSpec((1, ROWS, D_MODEL), lambda i, n: (i, n, 0)),
            pl.BlockSpec((3, COND_ROWS, D_MODEL), lambda i, n: (0, 0, 0)),
            pl.BlockSpec((1, D_MODEL), const2),
            pl.BlockSpec((D_MODEL, ODD_IN), const2),
            pl.BlockSpec((C_GROUP_DIM, 2 * C_GROUP_DIM), const2),
        ],
        out_specs=[
            pl.BlockSpec((1, 2, ROWS, D_MODEL), lambda i, n: (i, 0, n, 0)),
            pl.BlockSpec((1, ROWS, D_MODEL), lambda i, n: (i, n, 0)),
        ],
        out_shape=[jax.ShapeDtypeStruct((b, 2, s, D_MODEL), BF16),
                   jax.ShapeDtypeStruct((b, s, D_MODEL), F32)],
        scratch_shapes=[pltpu.VMEM((ROWS, ODD_IN), F32), pltpu.VMEM((2 * ROWS, D_MODEL), BF16)],
        compiler_params=pltpu.CompilerParams(
            dimension_semantics=("parallel", "parallel"), vmem_limit_bytes=VMEM_LIMIT),
        name="lat_odd_proj",
    )(x, mod_l, g, w_in, ccs)


def _lat_odd_mix_kernel(x_ref, mod_ref, fs_ref, ucs_ref, gate_ref, wout_ref, fg_ref, xo_ref, *, final):
    seq = ucs_ref.shape[1] // 2
    _, _, gate = _mod_rows(mod_ref, 1 + pl.program_id(0))
    r = jnp.dot(fs_ref[...], ucs_ref[0], preferred_element_type=F32)
    r = r * float(1.0 / np.sqrt(float(seq * C_GROUP_DIM)))
    mix = r * gate_ref[0]
    y = jnp.dot(mix.astype(BF16), wout_ref[...], preferred_element_type=F32)
    xn = x_ref[0] + gate * y
    if final:
        xn = _rms_head(xn, fg_ref[...])
    xo_ref[0] = xn


def _lat_odd_mix(x, mod_l, fs, ucs, gsilu, w_out, fg, final):
    b, s, _ = x.shape
    return pl.pallas_call(
        functools.partial(_lat_odd_mix_kernel, final=final),
        grid=(b, s // ROWS),
        in_specs=[
            pl.BlockSpec((1, ROWS, D_MODEL), lambda i, n: (i, n, 0)),
            pl.BlockSpec((3, COND_ROWS, D_MODEL), lambda i, n: (0, 0, 0)),
            pl.BlockSpec((ROWS, 2 * s), lambda i, n: (n, 0)),
            pl.BlockSpec((1, 2 * s, D_MODEL), lambda i, n: (i, 0, 0)),
            pl.BlockSpec((1, ROWS, D_MODEL), lambda i, n: (i, n, 0)),
            pl.BlockSpec((D_MODEL, D_MODEL), lambda i, n: (0, 0)),
            pl.BlockSpec((1, D_MODEL), lambda i, n: (0, 0)),
        ],
        out_specs=pl.BlockSpec((1, ROWS, D_MODEL), lambda i, n: (i, n, 0)),
        out_shape=jax.ShapeDtypeStruct(x.shape, F32),
        compiler_params=pltpu.CompilerParams(
            dimension_semantics=("parallel", "parallel"), vmem_limit_bytes=VMEM_LIMIT),
        name="lat_odd_mix",
    )(x, mod_l, fs, ucs.reshape(b, 2 * s, D_MODEL), gsilu, w_out, fg)


def kernel(x_prompt, x_sample, cache_k_a, cache_v_a, cache_k_b, cache_v_b, c, c_ctx, norm_g, ada_w, ada_b,
           even_w_in, even_w_out, qk_g_q, qk_g_k, sink_logit, odd_w_in, odd_w_out, final_g):
    batch, seq, _ = x_prompt.shape
    dec_batch, dec_seq, _ = x_sample.shape
    n_even = even_w_in.shape[0]
    past = cache_k_a.shape[2]

    cond = jnp.concatenate(
        [c_ctx[None, :], c, jnp.zeros((COND_ROWS - 1 - dec_batch, D_MODEL), F32)], axis=0)
    mod = _ada_all(cond, ada_w, ada_b)

    ccs = jnp.asarray(_channel_dft()).astype(BF16)
    fs_ctx = jnp.asarray(_position_dft(seq)).astype(BF16)
    fs_lat = jnp.asarray(_position_dft(dec_seq)).astype(BF16)
    cos, sn, sp = (jnp.asarray(t) for t in _rope_tables(dec_seq))

    caches = [a.reshape(dec_batch, n_even, past, KV_W) for a in (cache_k_a, cache_v_a, cache_k_b, cache_v_b)]
    fg = final_g.reshape(1, D_MODEL)

    xc, xl = x_prompt, x_sample
    new_kv = [[], [], [], []]
    for l in range(DEPTH):
        i = l // 2
        g = norm_g[l].reshape(1, D_MODEL)
        final = l == DEPTH - 1
        if l % 2 == 0:
            w_in, w_out = even_w_in[i].astype(BF16), even_w_out[i].astype(BF16)
            gq, gk = qk_g_q[i].reshape(1, HEAD_DIM), qk_g_k[i].reshape(1, HEAD_DIM)
            xc, ka, va, kb, vb = _ctx_even(xc, mod[l], g, w_in, w_out, gq, gk, sink_logit, i)
            for lst, a in zip(new_kv, (ka, va, kb, vb)):
                lst.append(a.reshape(batch, seq, N_KV, HEAD_DIM))
            p = _lat_even_proj(xl, mod[l], g, w_in, jnp.tile(gq, (1, 2)), jnp.tile(gk, (1, 2)), cos, sn, sp)
            xl = _lat_even_attn(xl, mod[l], p, *caches, w_out, sink_logit, i)
        else:
            w_in, w_out = odd_w_in[i].astype(BF16), odd_w_out[i].astype(BF16)
            xc = _ctx_odd(xc, mod[l], g, w_in, w_out, ccs, fs_ctx, fg, final)
            ucs, gsilu = _lat_odd_proj(xl, mod[l], g, w_in, ccs)
            xl = _lat_odd_mix(xl, mod[l], fs_lat, ucs, gsilu, w_out, fg, final)
    new_k_a, new_v_a, new_k_b, new_v_b = (jnp.stack(lst, axis=1) for lst in new_kv)
    return (xc, xl, new_k_a, new_v_a, new_k_b, new_v_b)
```

```python
import functools

import numpy as np
import jax
import jax.numpy as jnp
from jax import lax
from jax.experimental import pallas as pl
from jax.experimental.pallas import tpu as pltpu

D_MODEL = 1024
DEPTH = 4
HEAD_DIM = 64
N_HEADS = 8
N_KV = 2
GROUP = N_HEADS // N_KV
MIX_W = N_HEADS * HEAD_DIM
KV_W = N_KV * HEAD_DIM
EVEN_IN = 2 * (2 * MIX_W + 2 * KV_W)
ODD_IN = 2 * D_MODEL
GRID_W = 64
WINDOW = 128
ROPE_BASE = 10000.0
C_GROUPS = 4
C_GROUP_DIM = D_MODEL // C_GROUPS
EPS = 1e-6
NEG_BIG = -1e30
ROWS = 256
COND_ROWS = 8
VMEM_LIMIT = 48 * 1024 * 1024

QA, KA, VA, GA = 0, 512, 640, 768
QB, KB, VB, GB = 1280, 1792, 1920, 2048

F32 = jnp.float32
BF16 = jnp.bfloat16


def _dft_tables(n):
    k = np.arange(n, dtype=np.int64)
    ang = ((k[:, None] * k[None, :]) % n).astype(np.float64) * (2.0 * np.pi / n)
    return np.cos(ang).astype(np.float32), np.sin(ang).astype(np.float32)


def _channel_dft():
    c, s = _dft_tables(C_GROUP_DIM)
    return np.concatenate([c, s], axis=1)


def _position_dft(n):
    c, s = _dft_tables(n)
    return np.concatenate([c, -s], axis=1)


def _rope_tables(n_tok):
    rows = n_tok // GRID_W
    row = np.repeat(np.arange(rows), GRID_W).astype(np.float64)
    col = np.tile(np.arange(GRID_W), rows).astype(np.float64)
    half = HEAD_DIM // 2
    inv = 1.0 / (ROPE_BASE ** (np.arange(0, half, 2, dtype=np.float64) / half))
    ang_r = row[:, None] * inv
    ang_c = col[:, None] * inv
    zeros = np.zeros_like(ang_r)
    cos_h = np.concatenate([np.cos(ang_r), np.cos(ang_r), np.cos(ang_c), np.cos(ang_c)], axis=1)
    nxt_h = np.concatenate([-np.sin(ang_r), zeros, -np.sin(ang_c), zeros], axis=1)
    prv_h = np.concatenate([zeros, np.sin(ang_r), zeros, np.sin(ang_c)], axis=1)
    two = lambda t: np.concatenate([t, t], axis=1).astype(np.float32)
    return two(cos_h), two(nxt_h), two(prv_h)


def _silu(x):
    return x / (1.0 + jnp.exp(-x))


def _norm_mod(x, g, shift, scale):
    ms = jnp.mean(x * x, axis=-1, keepdims=True)
    return x * lax.rsqrt(ms + EPS) * (g * (1.0 + scale)) + shift


def _rms_head(xh, g):
    ms = jnp.mean(xh * xh, axis=-1, keepdims=True)
    return xh * lax.rsqrt(ms + EPS) * g


def _rms_pair(xc, g2):
    lo = lax.broadcasted_iota(jnp.int32, xc.shape, 1) < HEAD_DIM
    ss = xc * xc
    s_lo = jnp.sum(jnp.where(lo, ss, 0.0), axis=-1, keepdims=True)
    s_hi = jnp.sum(jnp.where(lo, 0.0, ss), axis=-1, keepdims=True)
    ms = jnp.where(lo, s_lo, s_hi) * (1.0 / HEAD_DIM)
    return xc * lax.rsqrt(ms + EPS) * g2


def _rope(xc, cos, sin_next, sin_prev):
    nxt = pltpu.roll(xc, 128 - 16, 1)
    prv = pltpu.roll(xc, 16, 1)
    return xc * cos + nxt * sin_next + prv * sin_prev


def _kv_operands(k2, v2, j):
    low = lax.broadcasted_iota(jnp.int32, k2.shape, 1) < HEAD_DIM
    mine = low if j == 0 else jnp.logical_not(low)
    km = jnp.where(mine, k2, 0.0)
    vm = jnp.where(mine, v2, 0.0)
    kr = pltpu.roll(km, HEAD_DIM, 1)
    vr = pltpu.roll(vm, HEAD_DIM, 1)
    k_lo, k_hi, v_lo, v_hi = (km, kr, vm, vr) if j == 0 else (kr, km, vr, vm)
    ones_lo = jnp.where(low, 1.0, 0.0)
    k_cat = jnp.concatenate([k_lo, k_hi], axis=0).astype(BF16)
    w = jnp.concatenate([jnp.concatenate([v_lo, ones_lo], axis=1),
                         jnp.concatenate([v_hi, 1.0 - ones_lo], axis=1)], axis=0).astype(BF16)
    return k_cat, w


def _attend_pair(qc, k_cat, w, sinks=None, mask=None):
    tk = k_cat.shape[0] // 2
    s = lax.dot_general(qc.astype(BF16), k_cat, (((1,), (1,)), ((), ())), preferred_element_type=F32)
    es, ms = [], []
    for hh in range(2):
        sh = s[:, hh * tk:(hh + 1) * tk]
        if mask is not None:
            sh = jnp.where(mask, sh, NEG_BIG)
        m = jnp.max(sh, axis=-1, keepdims=True)
        if sinks is not None:
            m = jnp.maximum(m, sinks[hh])
        es.append(jnp.exp(sh - m))
        ms.append(m)
    e = jnp.concatenate(es, axis=1).astype(BF16)
    nd = jnp.dot(e, w, preferred_element_type=F32)
    num, den = nd[:, :128], nd[:, 128:]
    if sinks is not None:
        low = lax.broadcasted_iota(jnp.int32, num.shape, 1) < HEAD_DIM
        den = den + jnp.where(low, jnp.exp(sinks[0] - ms[0]), jnp.exp(sinks[1] - ms[1]))
    return num / den


def _mod_rows(mod_ref, row):
    return mod_ref[0, pl.ds(row, 1), :], mod_ref[1, pl.ds(row, 1), :], mod_ref[2, pl.ds(row, 1), :]


def _ada_kernel(cond_ref, w_ref, b_ref, o_ref):
    a = _silu(cond_ref[...]).astype(BF16)
    o_ref[...] = jnp.dot(a, w_ref[...].astype(BF16), preferred_element_type=F32) + b_ref[...]


def _ada_all(cond, ada_w, ada_b):
    return pl.pallas_call(
        _ada_kernel,
        grid=(DEPTH, 3),
        in_specs=[
            pl.BlockSpec((COND_ROWS, D_MODEL), lambda l, p: (0, 0)),
            pl.BlockSpec((None, D_MODEL, D_MODEL), lambda l, p: (l, 0, p)),
            pl.BlockSpec((None, None, 1, D_MODEL), lambda l, p: (l, p, 0, 0)),
        ],
        out_specs=pl.BlockSpec((None, None, COND_ROWS, D_MODEL), lambda l, p: (l, p, 0, 0)),
        out_shape=jax.ShapeDtypeStruct((DEPTH, 3, COND_ROWS, D_MODEL), F32),
        compiler_params=pltpu.CompilerParams(
            dimension_semantics=("parallel", "parallel"), vmem_limit_bytes=VMEM_LIMIT),
        name="ada_mod",
    )(cond, ada_w, ada_b.reshape(DEPTH, 3, 1, D_MODEL))


def _ctx_even_kernel(sink_ref, x_ref, mod_ref, g_ref, win_ref, wout_ref, gq_ref, gk_ref,
                     xo_ref, ka_ref, va_ref, kb_ref, vb_ref, p_ref, o_ref, *, layer_i):
    shift, scale, gate = _mod_rows(mod_ref, 0)
    h = _norm_mod(x_ref[0], g_ref[...], shift, scale)
    p_ref[...] = jnp.dot(h.astype(BF16), win_ref[...], preferred_element_type=F32)
    gq2 = gq_ref[...] * (HEAD_DIM ** -0.5)
    gk2 = gk_ref[...]
    for mixer, (q0, k0, v0, g0) in enumerate(((QA, KA, VA, GA), (QB, KB, VB, GB))):
        k_out, v_out = (ka_ref, va_ref) if mixer == 0 else (kb_ref, vb_ref)
        k2 = p_ref[:, k0:k0 + KV_W]
        v2 = p_ref[:, v0:v0 + KV_W]
        if mixer == 0:
            k2 = _rms_pair(k2, gk2)
        k_out[0] = k2
        v_out[0] = v2
        for j in range(N_KV):
            k_cat, w = _kv_operands(k2, v2, j)
            for cc in range(GROUP // 2):
                c0 = 128 * (j * (GROUP // 2) + cc)
                qc = p_ref[:, q0 + c0:q0 + c0 + 128]
                if mixer == 0:
                    qc = _rms_pair(qc, gq2)
                    sinks = None
                else:
                    qc = qc * (HEAD_DIM ** -0.5)
                    head = c0 // HEAD_DIM
                    sinks = (sink_ref[layer_i, head], sink_ref[layer_i, head + 1])
                o = _attend_pair(qc, k_cat, w, sinks)
                gt = p_ref[:, g0 + c0:g0 + c0 + 128]
                o_ref[:, mixer * MIX_W + c0:mixer * MIX_W + c0 + 128] = (o * _silu(gt)).astype(BF16)
    y = jnp.dot(o_ref[...], wout_ref[...], preferred_element_type=F32)
    xo_ref[0] = x_ref[0] + gate * y


def _ctx_even(x, mod_l, g, w_in, w_out, gq, gk, sink, layer_i):
    b, s, _ = x.shape
    kv_shape = jax.ShapeDtypeStruct((b, s, KV_W), F32)
    kv_spec = pl.BlockSpec((1, s, KV_W), lambda i: (i, 0, 0))
    const2 = lambda i: (0, 0)
    return pl.pallas_call(
        functools.partial(_ctx_even_kernel, layer_i=layer_i),
        grid=(b,),
        in_specs=[
            pl.BlockSpec(memory_space=pltpu.SMEM),
            pl.BlockSpec((1, s, D_MODEL), lambda i: (i, 0, 0)),
            pl.BlockSpec((3, COND_ROWS, D_MODEL), lambda i: (0, 0, 0)),
            pl.BlockSpec((1, D_MODEL), const2),
            pl.BlockSpec((D_MODEL, EVEN_IN), const2),
            pl.BlockSpec((D_MODEL, D_MODEL), const2),
            pl.BlockSpec((1, 128), const2),
            pl.BlockSpec((1, 128), const2),
        ],
        out_specs=[pl.BlockSpec((1, s, D_MODEL), lambda i: (i, 0, 0)), kv_spec, kv_spec, kv_spec, kv_spec],
        out_shape=[jax.ShapeDtypeStruct(x.shape, F32), kv_shape, kv_shape, kv_shape, kv_shape],
        scratch_shapes=[pltpu.VMEM((s, EVEN_IN), F32), pltpu.VMEM((s, D_MODEL), BF16)],
        compiler_params=pltpu.CompilerParams(
            dimension_semantics=("parallel",), vmem_limit_bytes=VMEM_LIMIT),
        name="ctx_even",
    )(sink, x, mod_l, g, w_in, w_out, gq, gk)


def _fourier_rows(p_ref, ccs_ref, ucs_ref, rows):
    for grp in range(C_GROUPS):
        c0 = C_GROUP_DIM * grp
        ug = p_ref[:, c0:c0 + C_GROUP_DIM].astype(BF16)
        t = jnp.dot(ug, ccs_ref[...], preferred_element_type=F32)
        ucs_ref[0:rows, c0:c0 + C_GROUP_DIM] = t[:, :C_GROUP_DIM].astype(BF16)
        ucs_ref[rows:2 * rows, c0:c0 + C_GROUP_DIM] = t[:, C_GROUP_DIM:].astype(BF16)


def _ctx_odd_kernel(x_ref, mod_ref, g_ref, win_ref, wout_ref, ccs_ref, fs_ref, fg_ref,
                    xo_ref, p_ref, ucs_ref, *, final):
    rows = x_ref.shape[1]
    shift, scale, gate = _mod_rows(mod_ref, 0)
    h = _norm_mod(x_ref[0], g_ref[...], shift, scale)
    p_ref[...] = jnp.dot(h.astype(BF16), win_ref[...], preferred_element_type=F32)
    _fourier_rows(p_ref, ccs_ref, ucs_ref, rows)
    r = jnp.dot(fs_ref[...], ucs_ref[...], preferred_element_type=F32)
    r = r * float(1.0 / np.sqrt(float(rows * C_GROUP_DIM)))
    mix = r * _silu(p_ref[:, D_MODEL:2 * D_MODEL])
    y = jnp.dot(mix.astype(BF16), wout_ref[...], preferred_element_type=F32)
    xn = x_ref[0] + gate * y
    if final:
        xn = _rms_head(xn, fg_ref[...])
    xo_ref[0] = xn


def _ctx_odd(x, mod_l, g, w_in, w_out, ccs, fs, fg, final):
    b, s, _ = x.shape
    const2 = lambda i: (0, 0)
    return pl.pallas_call(
        functools.partial(_ctx_odd_kernel, final=final),
        grid=(b,),
        in_specs=[
            pl.BlockSpec((1, s, D_MODEL), lambda i: (i, 0, 0)),
            pl.BlockSpec((3, COND_ROWS, D_MODEL), lambda i: (0, 0, 0)),
            pl.BlockSpec((1, D_MODEL), const2),
            pl.BlockSpec((D_MODEL, ODD_IN), const2),
            pl.BlockSpec((D_MODEL, D_MODEL), const2),
            pl.BlockSpec((C_GROUP_DIM, 2 * C_GROUP_DIM), const2),
            pl.BlockSpec((s, 2 * s), const2),
            pl.BlockSpec((1, D_MODEL), const2),
        ],
        out_specs=pl.BlockSpec((1, s, D_MODEL), lambda i: (i, 0, 0)),
        out_shape=jax.ShapeDtypeStruct(x.shape, F32),
        scratch_shapes=[pltpu.VMEM((s, ODD_IN), F32), pltpu.VMEM((2 * s, D_MODEL), BF16)],
        compiler_params=pltpu.CompilerParams(
            dimension_semantics=("parallel",), vmem_limit_bytes=VMEM_LIMIT),
        name="ctx_odd",
    )(x, mod_l, g, w_in, w_out, ccs, fs, fg)


def _lat_even_proj_kernel(x_ref, mod_ref, g_ref, win_ref, gq_ref, gk_ref, cos_ref, sn_ref, sp_ref,
                          p_ref):
    shift, scale, _ = _mod_rows(mod_ref, 1 + pl.program_id(0))
    h = _norm_mod(x_ref[0], g_ref[...], shift, scale)
    p_ref[0] = jnp.dot(h.astype(BF16), win_ref[...], preferred_element_type=F32)
    cos, sn, sp = cos_ref[...], sn_ref[...], sp_ref[...]
    gq2 = gq_ref[...] * (HEAD_DIM ** -0.5)
    gk2 = gk_ref[...]

    def chunk(c0):
        return p_ref[0, :, c0:c0 + 128]

    for c0 in range(QA, QA + MIX_W, 128):
        p_ref[0, :, c0:c0 + 128] = _rope(_rms_pair(chunk(c0), gq2), cos, sn, sp)
    p_ref[0, :, KA:KA + 128] = _rope(_rms_pair(chunk(KA), gk2), cos, sn, sp)
    for c0 in range(QB, QB + MIX_W, 128):
        p_ref[0, :, c0:c0 + 128] = _rope(chunk(c0) * (HEAD_DIM ** -0.5), cos, sn, sp)
    p_ref[0, :, KB:KB + 128] = _rope(chunk(KB), cos, sn, sp)
    for g0 in (GA, GB):
        for c0 in range(g0, g0 + MIX_W, 128):
            p_ref[0, :, c0:c0 + 128] = _silu(chunk(c0))


def _lat_even_proj(x, mod_l, g, w_in, gq2, gk2, cos, sn, sp):
    b, s, _ = x.shape
    const2 = lambda i, n: (0, 0)
    rope_spec = pl.BlockSpec((ROWS, 128), lambda i, n: (n, 0))
    return pl.pallas_call(
        _lat_even_proj_kernel,
        grid=(b, s // ROWS),
        in_specs=[
            pl.BlockSpec((1, ROWS, D_MODEL), lambda i, n: (i, n, 0)),
            pl.BlockSpec((3, COND_ROWS, D_MODEL), lambda i, n: (0, 0, 0)),
            pl.BlockSpec((1, D_MODEL), const2),
            pl.BlockSpec((D_MODEL, EVEN_IN), const2),
            pl.BlockSpec((1, 128), const2),
            pl.BlockSpec((1, 128), const2),
            rope_spec, rope_spec, rope_spec,
        ],
        out_specs=pl.BlockSpec((1, ROWS, EVEN_IN), lambda i, n: (i, n, 0)),
        out_shape=jax.ShapeDtypeStruct((b, s, EVEN_IN), F32),
        compiler_params=pltpu.CompilerParams(
            dimension_semantics=("parallel", "parallel"), vmem_limit_bytes=VMEM_LIMIT),
        name="lat_even_proj",
    )(x, mod_l, g, w_in, gq2, gk2, cos, sn, sp)


def _lat_even_attn_kernel(sink_ref, x_ref, mod_ref, pq_ref, ka_ref, va_ref, kb_ref, vb_ref,
                          cka_ref, cva_ref, ckb_ref, cvb_ref, wout_ref, xo_ref, o_ref, *, layer_i):
    n = pl.program_id(1)
    seq = ka_ref.shape[1]
    _, _, gate = _mod_rows(mod_ref, 1 + pl.program_id(0))

    prev0 = pl.multiple_of(jnp.maximum(n * ROWS - WINDOW, 0), WINDOW)
    own0 = pl.multiple_of(n * ROWS, ROWS)
    next0 = pl.multiple_of(jnp.minimum(n * ROWS + ROWS, seq - WINDOW), WINDOW)
    win_len = ROWS + 2 * WINDOW
    ctx_len = cka_ref.shape[0]
    qi = lax.broadcasted_iota(jnp.int32, (ROWS, win_len + ctx_len), 0)
    cj = lax.broadcasted_iota(jnp.int32, (ROWS, win_len + ctx_len), 1)
    kpos = n * ROWS - WINDOW + cj
    in_win = (jnp.abs(cj - WINDOW - qi) <= WINDOW) & (kpos >= 0) & (kpos < seq)
    mask_b = in_win | (cj >= win_len)

    def window(ref, cache_ref):
        return jnp.concatenate([ref[0, pl.ds(prev0, WINDOW), :], ref[0, pl.ds(own0, ROWS), :],
                                ref[0, pl.ds(next0, WINDOW), :], cache_ref[...]], axis=0)

    k2a = jnp.concatenate([ka_ref[0], cka_ref[...]], axis=0)
    v2a = jnp.concatenate([va_ref[0], cva_ref[...]], axis=0)
    k2b, v2b = window(kb_ref, ckb_ref), window(vb_ref, cvb_ref)
    for mixer, (q0, g0, k2, v2) in enumerate(((QA, GA, k2a, v2a), (QB, GB, k2b, v2b))):
        for j in range(N_KV):
            k_cat, w = _kv_operands(k2, v2, j)
            for cc in range(GROUP // 2):
                c0 = 128 * (j * (GROUP // 2) + cc)
                if mixer == 0:
                    sinks, mask = None, None
                else:
                    head = c0 // HEAD_DIM
                    sinks, mask = (sink_ref[layer_i, head], sink_ref[layer_i, head + 1]), mask_b
                o = _attend_pair(pq_ref[0, :, q0 + c0:q0 + c0 + 128], k_cat, w, sinks, mask)
                gt = pq_ref[0, :, g0 + c0:g0 + c0 + 128]
                o_ref[:, mixer * MIX_W + c0:mixer * MIX_W + c0 + 128] = (o * gt).astype(BF16)
    y = jnp.dot(o_ref[...], wout_ref[...], preferred_element_type=F32)
    xo_ref[0] = x_ref[0] + gate * y


def _lat_even_attn(x, mod_l, p, cka, cva, ckb, cvb, w_out, sink, layer_i):
    b, s, _ = x.shape
    past = cka.shape[2]
    kv_spec = lambda blk: pl.BlockSpec((1, s, KV_W), lambda i, n, blk=blk: (i, 0, blk))
    cache_spec = pl.BlockSpec((None, None, past, KV_W), lambda i, n: (i, layer_i, 0, 0))
    return pl.pallas_call(
        functools.partial(_lat_even_attn_kernel, layer_i=layer_i),
        grid=(b, s // ROWS),
        in_specs=[
            pl.BlockSpec(memory_space=pltpu.SMEM),
            pl.BlockSpec((1, ROWS, D_MODEL), lambda i, n: (i, n, 0)),
            pl.BlockSpec((3, COND_ROWS, D_MODEL), lambda i, n: (0, 0, 0)),
            pl.BlockSpec((1, ROWS, EVEN_IN), lambda i, n: (i, n, 0)),
            kv_spec(KA // KV_W), kv_spec(VA // KV_W), kv_spec(KB // KV_W), kv_spec(VB // KV_W),
            cache_spec, cache_spec, cache_spec, cache_spec,
            pl.BlockSpec((D_MODEL, D_MODEL), lambda i, n: (0, 0)),
        ],
        out_specs=pl.BlockSpec((1, ROWS, D_MODEL), lambda i, n: (i, n, 0)),
        out_shape=jax.ShapeDtypeStruct(x.shape, F32),
        scratch_shapes=[pltpu.VMEM((ROWS, D_MODEL), BF16)],
        compiler_params=pltpu.CompilerParams(
            dimension_semantics=("parallel", "parallel"), vmem_limit_bytes=VMEM_LIMIT),
        name="lat_even_attn",
    )(sink, x, mod_l, p, p, p, p, p, cka, cva, ckb, cvb, w_out)


def _lat_odd_proj_kernel(x_ref, mod_ref, g_ref, win_ref, ccs_ref, ucs_ref, gate_ref, p_ref, t_ref):
    shift, scale, _ = _mod_rows(mod_ref, 1 + pl.program_id(0))
    h = _norm_mod(x_ref[0], g_ref[...], shift, scale)
    p_ref[...] = jnp.dot(h.astype(BF16), win_ref[...], preferred_element_type=F32)
    _fourier_rows(p_ref, ccs_ref, t_ref, ROWS)
    ucs_ref[0, 0] = t_ref[0:ROWS, :]
    ucs_ref[0, 1] = t_ref[ROWS:2 * ROWS, :]
    gate_ref[0] = _silu(p_ref[:, D_MODEL:2 * D_MODEL])


def _lat_odd_proj(x, mod_l, g, w_in, ccs):
    b, s, _ = x.shape
    const2 = lambda i, n: (0, 0)
    return pl.pallas_call(
        _lat_odd_proj_kernel,
        grid=(b, s // ROWS),
        in_specs=[
            pl.BlockSpec((1, ROWS, D_MODEL), lambda i, n: (i, n, 0)),
            pl.BlockSpec((3, COND_ROWS, D_MODEL), lambda i, n: (0, 0, 0)),
            pl.BlockSpec((1, D_MODEL), const2),
            pl.BlockSpec((D_MODEL, ODD_IN), const2),
            pl.BlockSpec((C_GROUP_DIM, 2 * C_GROUP_DIM), const2),
        ],
        out_specs=[
            pl.BlockSpec((1, 2, ROWS, D_MODEL), lambda i, n: (i, 0, n, 0)),
            pl.BlockSpec((1, ROWS, D_MODEL), lambda i, n: (i, n, 0)),
        ],
        out_shape=[jax.ShapeDtypeStruct((b, 2, s, D_MODEL), BF16),
                   jax.ShapeDtypeStruct((b, s, D_MODEL), F32)],
        scratch_shapes=[pltpu.VMEM((ROWS, ODD_IN), F32), pltpu.VMEM((2 * ROWS, D_MODEL), BF16)],
        compiler_params=pltpu.CompilerParams(
            dimension_semantics=("parallel", "parallel"), vmem_limit_bytes=VMEM_LIMIT),
        name="lat_odd_proj",
    )(x, mod_l, g, w_in, ccs)


def _lat_odd_mix_kernel(x_ref, mod_ref, fs_ref, ucs_ref, gate_ref, wout_ref, fg_ref, xo_ref, *, final):
    seq = ucs_ref.shape[1] // 2
    _, _, gate = _mod_rows(mod_ref, 1 + pl.program_id(0))
    r = jnp.dot(fs_ref[...], ucs_ref[0], preferred_element_type=F32)
    r = r * float(1.0 / np.sqrt(float(seq * C_GROUP_DIM)))
    mix = r * gate_ref[0]
    y = jnp.dot(mix.astype(BF16), wout_ref[...], preferred_element_type=F32)
    xn = x_ref[0] + gate * y
    if final:
        xn = _rms_head(xn, fg_ref[...])
    xo_ref[0] = xn


def _lat_odd_mix(x, mod_l, fs, ucs, gsilu, w_out, fg, final):
    b, s, _ = x.shape
    return pl.pallas_call(
        functools.partial(_lat_odd_mix_kernel, final=final),
        grid=(b, s // ROWS),
        in_specs=[
            pl.BlockSpec((1, ROWS, D_MODEL), lambda i, n: (i, n, 0)),
            pl.BlockSpec((3, COND_ROWS, D_MODEL), lambda i, n: (0, 0, 0)),
            pl.BlockSpec((ROWS, 2 * s), lambda i, n: (n, 0)),
            pl.BlockSpec((1, 2 * s, D_MODEL), lambda i, n: (i, 0, 0)),
            pl.BlockSpec((1, ROWS, D_MODEL), lambda i, n: (i, n, 0)),
            pl.BlockSpec((D_MODEL, D_MODEL), lambda i, n: (0, 0)),
            pl.BlockSpec((1, D_MODEL), lambda i, n: (0, 0)),
        ],
        out_specs=pl.BlockSpec((1, ROWS, D_MODEL), lambda i, n: (i, n, 0)),
        out_shape=jax.ShapeDtypeStruct(x.shape, F32),
        compiler_params=pltpu.CompilerParams(
            dimension_semantics=("parallel", "parallel"), vmem_limit_bytes=VMEM_LIMIT),
        name="lat_odd_mix",
    )(x, mod_l, fs, ucs.reshape(b, 2 * s, D_MODEL), gsilu, w_out, fg)


def kernel(x_prompt, x_sample, cache_k_a, cache_v_a, cache_k_b, cache_v_b, c, c_ctx, norm_g, ada_w, ada_b,
           even_w_in, even_w_out, qk_g_q, qk_g_k, sink_logit, odd_w_in, odd_w_out, final_g):
    batch, seq, _ = x_prompt.shape
    dec_batch, dec_seq, _ = x_sample.shape
    n_even = even_w_in.shape[0]
    past = cache_k_a.shape[2]

    cond = jnp.concatenate(
        [c_ctx[None, :], c, jnp.zeros((COND_ROWS - 1 - dec_batch, D_MODEL), F32)], axis=0)
    mod = _ada_all(cond, ada_w, ada_b)

    ccs = jnp.asarray(_channel_dft()).astype(BF16)
    fs_ctx = jnp.asarray(_position_dft(seq)).astype(BF16)
    fs_lat = jnp.asarray(_position_dft(dec_seq)).astype(BF16)
    cos, sn, sp = (jnp.asarray(t) for t in _rope_tables(dec_seq))

    caches = [a.reshape(dec_batch, n_even, past, KV_W) for a in (cache_k_a, cache_v_a, cache_k_b, cache_v_b)]
    fg = final_g.reshape(1, D_MODEL)

    xc, xl = x_prompt, x_sample
    new_kv = [[], [], [], []]
    for l in range(DEPTH):
        i = l // 2
        g = norm_g[l].reshape(1, D_MODEL)
        final = l == DEPTH - 1
        if l % 2 == 0:
            w_in, w_out = even_w_in[i].astype(BF16), even_w_out[i].astype(BF16)
            gq2 = jnp.tile(qk_g_q[i].reshape(1, HEAD_DIM), (1, 2))
            gk2 = jnp.tile(qk_g_k[i].reshape(1, HEAD_DIM), (1, 2))
            xc, ka, va, kb, vb = _ctx_even(xc, mod[l], g, w_in, w_out, gq2, gk2, sink_logit, i)
            for lst, a in zip(new_kv, (ka, va, kb, vb)):
                lst.append(a.reshape(batch, seq, N_KV, HEAD_DIM))
            p = _lat_even_proj(xl, mod[l], g, w_in, gq2, gk2, cos, sn, sp)
            xl = _lat_even_attn(xl, mod[l], p, *caches, w_out, sink_logit, i)
        else:
            w_in, w_out = odd_w_in[i].astype(BF16), odd_w_out[i].astype(BF16)
            xc = _ctx_odd(xc, mod[l], g, w_in, w_out, ccs, fs_ctx, fg, final)
            ucs, gsilu = _lat_odd_proj(xl, mod[l], g, w_in, ccs)
            xl = _lat_odd_mix(xl, mod[l], fs_lat, ucs, gsilu, w_out, fg, final)
    new_k_a, new_v_a, new_k_b, new_v_b = (jnp.stack(lst, axis=1) for lst in new_kv)
    return (xc, xl, new_k_a, new_v_a, new_k_b, new_v_b)
```

```python
import functools

import numpy as np
import jax
import jax.numpy as jnp
from jax import lax
from jax.experimental import pallas as pl
from jax.experimental.pallas import tpu as pltpu

D_MODEL = 1024
DEPTH = 4
HEAD_DIM = 64
N_HEADS = 8
N_KV = 2
GROUP = N_HEADS // N_KV
MIX_W = N_HEADS * HEAD_DIM
KV_W = N_KV * HEAD_DIM
EVEN_IN = 2 * (2 * MIX_W + 2 * KV_W)
ODD_IN = 2 * D_MODEL
GRID_W = 64
WINDOW = 128
ROPE_BASE = 10000.0
C_GROUPS = 4
C_GROUP_DIM = D_MODEL // C_GROUPS
EPS = 1e-6
NEG_BIG = -1e30
ROWS = 256
COND_ROWS = 8
VMEM_LIMIT = 48 * 1024 * 1024
CTX_VMEM_LIMIT = 56 * 1024 * 1024

QA, KA, VA, GA = 0, 512, 640, 768
QB, KB, VB, GB = 1280, 1792, 1920, 2048

F32 = jnp.float32
BF16 = jnp.bfloat16


def _dft_tables(n):
    k = np.arange(n, dtype=np.int64)
    ang = ((k[:, None] * k[None, :]) % n).astype(np.float64) * (2.0 * np.pi / n)
    return np.cos(ang).astype(np.float32), np.sin(ang).astype(np.float32)


def _channel_dft():
    c, s = _dft_tables(C_GROUP_DIM)
    return np.concatenate([c, s], axis=1)


def _position_dft(n):
    c, s = _dft_tables(n)
    return np.concatenate([c, -s], axis=1)


def _rope_tables(n_tok):
    rows = n_tok // GRID_W
    row = np.repeat(np.arange(rows), GRID_W).astype(np.float64)
    col = np.tile(np.arange(GRID_W), rows).astype(np.float64)
    half = HEAD_DIM // 2
    inv = 1.0 / (ROPE_BASE ** (np.arange(0, half, 2, dtype=np.float64) / half))
    ang_r = row[:, None] * inv
    ang_c = col[:, None] * inv
    zeros = np.zeros_like(ang_r)
    cos_h = np.concatenate([np.cos(ang_r), np.cos(ang_r), np.cos(ang_c), np.cos(ang_c)], axis=1)
    nxt_h = np.concatenate([-np.sin(ang_r), zeros, -np.sin(ang_c), zeros], axis=1)
    prv_h = np.concatenate([zeros, np.sin(ang_r), zeros, np.sin(ang_c)], axis=1)
    two = lambda t: np.concatenate([t, t], axis=1).astype(np.float32)
    return two(cos_h), two(nxt_h), two(prv_h)


def _silu(x):
    return x / (1.0 + jnp.exp(-x))


def _norm_mod(x, g, shift, scale):
    ms = jnp.mean(x * x, axis=-1, keepdims=True)
    return x * lax.rsqrt(ms + EPS) * (g * (1.0 + scale)) + shift


def _rms_head(xh, g):
    ms = jnp.mean(xh * xh, axis=-1, keepdims=True)
    return xh * lax.rsqrt(ms + EPS) * g


def _rms_pair(xc, g2):
    lo = lax.broadcasted_iota(jnp.int32, xc.shape, 1) < HEAD_DIM
    ss = xc * xc
    s_lo = jnp.sum(jnp.where(lo, ss, 0.0), axis=-1, keepdims=True)
    s_hi = jnp.sum(jnp.where(lo, 0.0, ss), axis=-1, keepdims=True)
    ms = jnp.where(lo, s_lo, s_hi) * (1.0 / HEAD_DIM)
    return xc * lax.rsqrt(ms + EPS) * g2


def _rope(xc, cos, sin_next, sin_prev):
    nxt = pltpu.roll(xc, 128 - 16, 1)
    prv = pltpu.roll(xc, 16, 1)
    return xc * cos + nxt * sin_next + prv * sin_prev


def _kv_operands(k2, v2, j):
    low = lax.broadcasted_iota(jnp.int32, k2.shape, 1) < HEAD_DIM
    mine = low if j == 0 else jnp.logical_not(low)
    km = jnp.where(mine, k2, 0.0)
    vm = jnp.where(mine, v2, 0.0)
    kr = pltpu.roll(km, HEAD_DIM, 1)
    vr = pltpu.roll(vm, HEAD_DIM, 1)
    k_lo, k_hi, v_lo, v_hi = (km, kr, vm, vr) if j == 0 else (kr, km, vr, vm)
    ones_lo = jnp.where(low, 1.0, 0.0)
    k_cat = jnp.concatenate([k_lo, k_hi], axis=0).astype(BF16)
    w = jnp.concatenate([jnp.concatenate([v_lo, ones_lo], axis=1),
                         jnp.concatenate([v_hi, 1.0 - ones_lo], axis=1)], axis=0).astype(BF16)
    return k_cat, w


def _attend_pair(qc, k_cat, w, sinks=None, mask=None):
    tk = k_cat.shape[0] // 2
    s = lax.dot_general(qc.astype(BF16), k_cat, (((1,), (1,)), ((), ())), preferred_element_type=F32)
    es, ms = [], []
    for hh in range(2):
        sh = s[:, hh * tk:(hh + 1) * tk]
        if mask is not None:
            sh = jnp.where(mask, sh, NEG_BIG)
        m = jnp.max(sh, axis=-1, keepdims=True)
        if sinks is not None:
            m = jnp.maximum(m, sinks[hh])
        es.append(jnp.exp(sh - m))
        ms.append(m)
    e = jnp.concatenate(es, axis=1).astype(BF16)
    nd = jnp.dot(e, w, preferred_element_type=F32)
    num, den = nd[:, :128], nd[:, 128:]
    if sinks is not None:
        low = lax.broadcasted_iota(jnp.int32, num.shape, 1) < HEAD_DIM
        den = den + jnp.where(low, jnp.exp(sinks[0] - ms[0]), jnp.exp(sinks[1] - ms[1]))
    return num / den


def _mod_rows(mod_ref, row):
    return mod_ref[0, pl.ds(row, 1), :], mod_ref[1, pl.ds(row, 1), :], mod_ref[2, pl.ds(row, 1), :]


def _ada_kernel(cond_ref, w_ref, b_ref, o_ref):
    a = _silu(cond_ref[...]).astype(BF16)
    o_ref[...] = jnp.dot(a, w_ref[...].astype(BF16), preferred_element_type=F32) + b_ref[...]


def _ada_all(cond, ada_w, ada_b):
    return pl.pallas_call(
        _ada_kernel,
        grid=(DEPTH, 3),
        in_specs=[
            pl.BlockSpec((COND_ROWS, D_MODEL), lambda l, p: (0, 0)),
            pl.BlockSpec((None, D_MODEL, D_MODEL), lambda l, p: (l, 0, p)),
            pl.BlockSpec((None, None, 1, D_MODEL), lambda l, p: (l, p, 0, 0)),
        ],
        out_specs=pl.BlockSpec((None, None, COND_ROWS, D_MODEL), lambda l, p: (l, p, 0, 0)),
        out_shape=jax.ShapeDtypeStruct((DEPTH, 3, COND_ROWS, D_MODEL), F32),
        compiler_params=pltpu.CompilerParams(
            dimension_semantics=("parallel", "parallel"), vmem_limit_bytes=VMEM_LIMIT),
        name="ada_mod",
    )(cond, ada_w, ada_b.reshape(DEPTH, 3, 1, D_MODEL))


def _fourier_rows(p_ref, ccs_ref, ucs_ref, rows):
    for grp in range(C_GROUPS):
        c0 = C_GROUP_DIM * grp
        ug = p_ref[:, c0:c0 + C_GROUP_DIM].astype(BF16)
        t = jnp.dot(ug, ccs_ref[...], preferred_element_type=F32)
        ucs_ref[0:rows, c0:c0 + C_GROUP_DIM] = t[:, :C_GROUP_DIM].astype(BF16)
        ucs_ref[rows:2 * rows, c0:c0 + C_GROUP_DIM] = t[:, C_GROUP_DIM:].astype(BF16)


def _ctx_even_layer(h, i, sink_ref, win_ref, wout_ref, gq_ref, gk_ref, kv_refs, p_ref, o_ref):
    p_ref[...] = jnp.dot(h.astype(BF16), win_ref[i], preferred_element_type=F32)
    gq2 = gq_ref[i:i + 1, :] * (HEAD_DIM ** -0.5)
    gk2 = gk_ref[i:i + 1, :]
    for mixer, (q0, k0, v0, g0) in enumerate(((QA, KA, VA, GA), (QB, KB, VB, GB))):
        k2 = p_ref[:, k0:k0 + KV_W]
        v2 = p_ref[:, v0:v0 + KV_W]
        if mixer == 0:
            k2 = _rms_pair(k2, gk2)
        kv_refs[2 * mixer][0, i] = k2.T.reshape(N_KV, HEAD_DIM, k2.shape[0])
        kv_refs[2 * mixer + 1][0, i] = v2.T.reshape(N_KV, HEAD_DIM, v2.shape[0])
        for j in range(N_KV):
            k_cat, w = _kv_operands(k2, v2, j)
            for cc in range(GROUP // 2):
                c0 = 128 * (j * (GROUP // 2) + cc)
                qc = p_ref[:, q0 + c0:q0 + c0 + 128]
                if mixer == 0:
                    qc = _rms_pair(qc, gq2)
                    sinks = None
                else:
                    qc = qc * (HEAD_DIM ** -0.5)
                    head = c0 // HEAD_DIM
                    sinks = (sink_ref[i, head], sink_ref[i, head + 1])
                o = _attend_pair(qc, k_cat, w, sinks)
                gt = p_ref[:, g0 + c0:g0 + c0 + 128]
                o_ref[:, mixer * MIX_W + c0:mixer * MIX_W + c0 + 128] = (o * _silu(gt)).astype(BF16)
    return jnp.dot(o_ref[...], wout_ref[i], preferred_element_type=F32)


def _ctx_odd_layer(h, i, win_ref, wout_ref, ccs_ref, fs_ref, p_ref, ucs_ref):
    rows = h.shape[0]
    p_ref[:, 0:ODD_IN] = jnp.dot(h.astype(BF16), win_ref[i], preferred_element_type=F32)
    _fourier_rows(p_ref, ccs_ref, ucs_ref, rows)
    r = jnp.dot(fs_ref[...], ucs_ref[...], preferred_element_type=F32)
    r = r * float(1.0 / np.sqrt(float(rows * C_GROUP_DIM)))
    mix = r * _silu(p_ref[:, D_MODEL:2 * D_MODEL])
    return jnp.dot(mix.astype(BF16), wout_ref[i], preferred_element_type=F32)


def _ctx_kernel(sink_ref, x_ref, mod_ref, g_ref, ewin_ref, ewout_ref, owin_ref, owout_ref, gq_ref, gk_ref,
                ccs_ref, fs_ref, fg_ref, xo_ref, ka_ref, va_ref, kb_ref, vb_ref,
                xs_ref, p_ref, o_ref, ucs_ref):
    xs_ref[...] = x_ref[0]
    for l in range(DEPTH):
        i = l // 2
        shift, scale, gate = (mod_ref[l, part, 0:1, :] for part in range(3))
        h = _norm_mod(xs_ref[...], g_ref[l:l + 1, :], shift, scale)
        if l % 2 == 0:
            y = _ctx_even_layer(h, i, sink_ref, ewin_ref, ewout_ref, gq_ref, gk_ref,
                                (ka_ref, va_ref, kb_ref, vb_ref), p_ref, o_ref)
        else:
            y = _ctx_odd_layer(h, i, owin_ref, owout_ref, ccs_ref, fs_ref, p_ref, ucs_ref)
        xn = xs_ref[...] + gate * y
        if l == DEPTH - 1:
            xo_ref[0] = _rms_head(xn, fg_ref[...])
        else:
            xs_ref[...] = xn


def _resident(shape):
    return pl.BlockSpec(shape, lambda i: (0,) * len(shape), pipeline_mode=pl.Buffered(1))


def _ctx_path(x, mod, norm_g, ewin, ewout, owin, owout, gq2, gk2, sink, ccs, fs, fg):
    b, s, _ = x.shape
    n_even = ewin.shape[0]
    kv_shape = jax.ShapeDtypeStruct((b, n_even, N_KV, HEAD_DIM, s), F32)
    kv_spec = pl.BlockSpec((1, n_even, N_KV, HEAD_DIM, s), lambda i: (i, 0, 0, 0, 0))
    x_spec = pl.BlockSpec((1, s, D_MODEL), lambda i: (i, 0, 0))
    return pl.pallas_call(
        _ctx_kernel,
        grid=(b,),
        in_specs=[
            pl.BlockSpec(memory_space=pltpu.SMEM),
            x_spec,
            _resident(mod.shape), _resident(norm_g.shape),
            _resident(ewin.shape), _resident(ewout.shape), _resident(owin.shape), _resident(owout.shape),
            _resident(gq2.shape), _resident(gk2.shape),
            _resident(ccs.shape), _resident(fs.shape), _resident(fg.shape),
        ],
        out_specs=[x_spec, kv_spec, kv_spec, kv_spec, kv_spec],
        out_shape=[jax.ShapeDtypeStruct(x.shape, F32), kv_shape, kv_shape, kv_shape, kv_shape],
        scratch_shapes=[pltpu.VMEM((s, D_MODEL), F32), pltpu.VMEM((s, EVEN_IN), F32),
                        pltpu.VMEM((s, D_MODEL), BF16), pltpu.VMEM((2 * s, D_MODEL), BF16)],
        compiler_params=pltpu.CompilerParams(
            dimension_semantics=("parallel",), vmem_limit_bytes=CTX_VMEM_LIMIT),
        name="ctx_path",
    )(sink, x, mod, norm_g, ewin, ewout, owin, owout, gq2, gk2, ccs, fs, fg)


def _lat_even_proj_kernel(x_ref, mod_ref, g_ref, win_ref, gq_ref, gk_ref, cos_ref, sn_ref, sp_ref,
                          p_ref):
    shift, scale, _ = _mod_rows(mod_ref, 1 + pl.program_id(0))
    h = _norm_mod(x_ref[0], g_ref[...], shift, scale)
    p_ref[0] = jnp.dot(h.astype(BF16), win_ref[...], preferred_element_type=F32)
    cos, sn, sp = cos_ref[...], sn_ref[...], sp_ref[...]
    gq2 = gq_ref[...] * (HEAD_DIM ** -0.5)
    gk2 = gk_ref[...]

    def chunk(c0):
        return p_ref[0, :, c0:c0 + 128]

    for c0 in range(QA, QA + MIX_W, 128):
        p_ref[0, :, c0:c0 + 128] = _rope(_rms_pair(chunk(c0), gq2), cos, sn, sp)
    p_ref[0, :, KA:KA + 128] = _rope(_rms_pair(chunk(KA), gk2), cos, sn, sp)
    for c0 in range(QB, QB + MIX_W, 128):
        p_ref[0, :, c0:c0 + 128] = _rope(chunk(c0) * (HEAD_DIM ** -0.5), cos, sn, sp)
    p_ref[0, :, KB:KB + 128] = _rope(chunk(KB), cos, sn, sp)
    for g0 in (GA, GB):
        for c0 in range(g0, g0 + MIX_W, 128):
            p_ref[0, :, c0:c0 + 128] = _silu(chunk(c0))


def _layer_spec(tail, idx):
    return pl.BlockSpec((None,) + tuple(tail), lambda i, n: (idx,) + (0,) * len(tail))


def _lat_even_proj(x, mod, g3, ewin, gq3, gk3, cos, sn, sp, l):
    b, s, _ = x.shape
    rope_spec = pl.BlockSpec((ROWS, 128), lambda i, n: (n, 0))
    return pl.pallas_call(
        _lat_even_proj_kernel,
        grid=(b, s // ROWS),
        in_specs=[
            pl.BlockSpec((1, ROWS, D_MODEL), lambda i, n: (i, n, 0)),
            _layer_spec((3, COND_ROWS, D_MODEL), l),
            _layer_spec((1, D_MODEL), l),
            _layer_spec((D_MODEL, EVEN_IN), l // 2),
            _layer_spec((1, 128), l // 2),
            _layer_spec((1, 128), l // 2),
            rope_spec, rope_spec, rope_spec,
        ],
        out_specs=pl.BlockSpec((1, ROWS, EVEN_IN), lambda i, n: (i, n, 0)),
        out_shape=jax.ShapeDtypeStruct((b, s, EVEN_IN), F32),
        compiler_params=pltpu.CompilerParams(
            dimension_semantics=("parallel", "parallel"), vmem_limit_bytes=VMEM_LIMIT),
        name="lat_even_proj",
    )(x, mod, g3, ewin, gq3, gk3, cos, sn, sp)


def _lat_even_attn_kernel(sink_ref, x_ref, mod_ref, pq_ref, ka_ref, va_ref, kb_ref, vb_ref,
                          cka_ref, cva_ref, ckb_ref, cvb_ref, wout_ref, xo_ref, o_ref, *, layer_i):
    n = pl.program_id(1)
    seq = ka_ref.shape[1]
    _, _, gate = _mod_rows(mod_ref, 1 + pl.program_id(0))

    prev0 = pl.multiple_of(jnp.maximum(n * ROWS - WINDOW, 0), WINDOW)
    own0 = pl.multiple_of(n * ROWS, ROWS)
    next0 = pl.multiple_of(jnp.minimum(n * ROWS + ROWS, seq - WINDOW), WINDOW)
    win_len = ROWS + 2 * WINDOW
    ctx_len = cka_ref.shape[0]
    qi = lax.broadcasted_iota(jnp.int32, (ROWS, win_len + ctx_len), 0)
    cj = lax.broadcasted_iota(jnp.int32, (ROWS, win_len + ctx_len), 1)
    kpos = n * ROWS - WINDOW + cj
    in_win = (jnp.abs(cj - WINDOW - qi) <= WINDOW) & (kpos >= 0) & (kpos < seq)
    mask_b = in_win | (cj >= win_len)

    def window(ref, cache_ref):
        return jnp.concatenate([ref[0, pl.ds(prev0, WINDOW), :], ref[0, pl.ds(own0, ROWS), :],
                                ref[0, pl.ds(next0, WINDOW), :], cache_ref[...]], axis=0)

    k2a = jnp.concatenate([ka_ref[0], cka_ref[...]], axis=0)
    v2a = jnp.concatenate([va_ref[0], cva_ref[...]], axis=0)
    k2b, v2b = window(kb_ref, ckb_ref), window(vb_ref, cvb_ref)
    for mixer, (q0, g0, k2, v2) in enumerate(((QA, GA, k2a, v2a), (QB, GB, k2b, v2b))):
        for j in range(N_KV):
            k_cat, w = _kv_operands(k2, v2, j)
            for cc in range(GROUP // 2):
                c0 = 128 * (j * (GROUP // 2) + cc)
                if mixer == 0:
                    sinks, mask = None, None
                else:
                    head = c0 // HEAD_DIM
                    sinks, mask = (sink_ref[layer_i, head], sink_ref[layer_i, head + 1]), mask_b
                o = _attend_pair(pq_ref[0, :, q0 + c0:q0 + c0 + 128], k_cat, w, sinks, mask)
                gt = pq_ref[0, :, g0 + c0:g0 + c0 + 128]
                o_ref[:, mixer * MIX_W + c0:mixer * MIX_W + c0 + 128] = (o * gt).astype(BF16)
    y = jnp.dot(o_ref[...], wout_ref[...], preferred_element_type=F32)
    xo_ref[0] = x_ref[0] + gate * y


def _lat_even_attn(x, mod, p, cka, cva, ckb, cvb, ewout, sink, l):
    b, s, _ = x.shape
    past = cka.shape[2]
    layer_i = l // 2
    kv_spec = lambda blk: pl.BlockSpec((1, s, KV_W), lambda i, n, blk=blk: (i, 0, blk))
    cache_spec = pl.BlockSpec((None, None, past, KV_W), lambda i, n: (i, layer_i, 0, 0))
    return pl.pallas_call(
        functools.partial(_lat_even_attn_kernel, layer_i=layer_i),
        grid=(b, s // ROWS),
        in_specs=[
            pl.BlockSpec(memory_space=pltpu.SMEM),
            pl.BlockSpec((1, ROWS, D_MODEL), lambda i, n: (i, n, 0)),
            _layer_spec((3, COND_ROWS, D_MODEL), l),
            pl.BlockSpec((1, ROWS, EVEN_IN), lambda i, n: (i, n, 0)),
            kv_spec(KA // KV_W), kv_spec(VA // KV_W), kv_spec(KB // KV_W), kv_spec(VB // KV_W),
            cache_spec, cache_spec, cache_spec, cache_spec,
            _layer_spec((D_MODEL, D_MODEL), layer_i),
        ],
        out_specs=pl.BlockSpec((1, ROWS, D_MODEL), lambda i, n: (i, n, 0)),
        out_shape=jax.ShapeDtypeStruct(x.shape, F32),
        scratch_shapes=[pltpu.VMEM((ROWS, D_MODEL), BF16)],
        compiler_params=pltpu.CompilerParams(
            dimension_semantics=("parallel", "parallel"), vmem_limit_bytes=VMEM_LIMIT),
        name="lat_even_attn",
    )(sink, x, mod, p, p, p, p, p, cka, cva, ckb, cvb, ewout)


def _lat_odd_proj_kernel(x_ref, mod_ref, g_ref, win_ref, ccs_ref, ucs_ref, gate_ref, p_ref, t_ref):
    shift, scale, _ = _mod_rows(mod_ref, 1 + pl.program_id(0))
    h = _norm_mod(x_ref[0], g_ref[...], shift, scale)
    p_ref[...] = jnp.dot(h.astype(BF16), win_ref[...], preferred_element_type=F32)
    _fourier_rows(p_ref, ccs_ref, t_ref, ROWS)
    ucs_ref[0, 0] = t_ref[0:ROWS, :]
    ucs_ref[0, 1] = t_ref[ROWS:2 * ROWS, :]
    gate_ref[0] = _silu(p_ref[:, D_MODEL:2 * D_MODEL])


def _lat_odd_proj(x, mod, g3, owin, ccs, l):
    b, s, _ = x.shape
    return pl.pallas_call(
        _lat_odd_proj_kernel,
        grid=(b, s // ROWS),
        in_specs=[
            pl.BlockSpec((1, ROWS, D_MODEL), lambda i, n: (i, n, 0)),
            _layer_spec((3, COND_ROWS, D_MODEL), l),
            _layer_spec((1, D_MODEL), l),
            _layer_spec((D_MODEL, ODD_IN), l // 2),
            pl.BlockSpec((C_GROUP_DIM, 2 * C_GROUP_DIM), lambda i, n: (0, 0)),
        ],
        out_specs=[
            pl.BlockSpec((1, 2, ROWS, D_MODEL), lambda i, n: (i, 0, n, 0)),
            pl.BlockSpec((1, ROWS, D_MODEL), lambda i, n: (i, n, 0)),
        ],
        out_shape=[jax.ShapeDtypeStruct((b, 2, s, D_MODEL), BF16),
                   jax.ShapeDtypeStruct((b, s, D_MODEL), F32)],
        scratch_shapes=[pltpu.VMEM((ROWS, ODD_IN), F32), pltpu.VMEM((2 * ROWS, D_MODEL), BF16)],
        compiler_params=pltpu.CompilerParams(
            dimension_semantics=("parallel", "parallel"), vmem_limit_bytes=VMEM_LIMIT),
        name="lat_odd_proj",
    )(x, mod, g3, owin, ccs)


def _lat_odd_mix_kernel(x_ref, mod_ref, fs_ref, ucs_ref, gate_ref, wout_ref, fg_ref, xo_ref, *, final):
    seq = ucs_ref.shape[1] // 2
    _, _, gate = _mod_rows(mod_ref, 1 + pl.program_id(0))
    r = jnp.dot(fs_ref[...], ucs_ref[0], preferred_element_type=F32)
    r = r * float(1.0 / np.sqrt(float(seq * C_GROUP_DIM)))
    mix = r * gate_ref[0]
    y = jnp.dot(mix.astype(BF16), wout_ref[...], preferred_element_type=F32)
    xn = x_ref[0] + gate * y
    if final:
        xn = _rms_head(xn, fg_ref[...])
    xo_ref[0] = xn


def _lat_odd_mix(x, mod, fs, ucs, gsilu, owout, fg, l):
    b, s, _ = x.shape
    return pl.pallas_call(
        functools.partial(_lat_odd_mix_kernel, final=l == DEPTH - 1),
        grid=(b, s // ROWS),
        in_specs=[
            pl.BlockSpec((1, ROWS, D_MODEL), lambda i, n: (i, n, 0)),
            _layer_spec((3, COND_ROWS, D_MODEL), l),
            pl.BlockSpec((ROWS, 2 * s), lambda i, n: (n, 0)),
            pl.BlockSpec((1, 2 * s, D_MODEL), lambda i, n: (i, 0, 0)),
            pl.BlockSpec((1, ROWS, D_MODEL), lambda i, n: (i, n, 0)),
            _layer_spec((D_MODEL, D_MODEL), l // 2),
            pl.BlockSpec((1, D_MODEL), lambda i, n: (0, 0)),
        ],
        out_specs=pl.BlockSpec((1, ROWS, D_MODEL), lambda i, n: (i, n, 0)),
        out_shape=jax.ShapeDtypeStruct(x.shape, F32),
        compiler_params=pltpu.CompilerParams(
            dimension_semantics=("parallel", "parallel"), vmem_limit_bytes=VMEM_LIMIT),
        name="lat_odd_mix",
    )(x, mod, fs, ucs.reshape(b, 2 * s, D_MODEL), gsilu, owout, fg)


def kernel(x_prompt, x_sample, cache_k_a, cache_v_a, cache_k_b, cache_v_b, c, c_ctx, norm_g, ada_w, ada_b,
           even_w_in, even_w_out, qk_g_q, qk_g_k, sink_logit, odd_w_in, odd_w_out, final_g):
    batch, seq, _ = x_prompt.shape
    dec_batch, dec_seq, _ = x_sample.shape
    n_even = even_w_in.shape[0]
    past = cache_k_a.shape[2]

    cond = jnp.concatenate(
        [c_ctx[None, :], c, jnp.zeros((COND_ROWS - 1 - dec_batch, D_MODEL), F32)], axis=0)
    mod = _ada_all(cond, ada_w, ada_b)

    ccs = jnp.asarray(_channel_dft()).astype(BF16)
    fs_ctx = jnp.asarray(_position_dft(seq)).astype(BF16)
    fs_lat = jnp.asarray(_position_dft(dec_seq)).astype(BF16)
    cos, sn, sp = (jnp.asarray(t) for t in _rope_tables(dec_seq))

    caches = [a.reshape(dec_batch, n_even, past, KV_W) for a in (cache_k_a, cache_v_a, cache_k_b, cache_v_b)]
    fg = final_g.reshape(1, D_MODEL)
    ewin, ewout = even_w_in.astype(BF16), even_w_out.astype(BF16)
    owin, owout = odd_w_in.astype(BF16), odd_w_out.astype(BF16)
    gq2 = jnp.tile(qk_g_q, (1, 2))
    gk2 = jnp.tile(qk_g_k, (1, 2))

    xc, *new_kv = _ctx_path(x_prompt, mod, norm_g, ewin, ewout, owin, owout, gq2, gk2, sink_logit,
                            ccs, fs_ctx, fg)
    new_kv = [jnp.transpose(a, (0, 1, 4, 2, 3)) for a in new_kv]

    xl = x_sample
    g3 = norm_g.reshape(DEPTH, 1, D_MODEL)
    gq3, gk3 = gq2.reshape(n_even, 1, 128), gk2.reshape(n_even, 1, 128)
    for l in range(DEPTH):
        if l % 2 == 0:
            p = _lat_even_proj(xl, mod, g3, ewin, gq3, gk3, cos, sn, sp, l)
            xl = _lat_even_attn(xl, mod, p, *caches, ewout, sink_logit, l)
        else:
            ucs, gsilu = _lat_odd_proj(xl, mod, g3, owin, ccs, l)
            xl = _lat_odd_mix(xl, mod, fs_lat, ucs, gsilu, owout, fg, l)
    return (xc, xl, *new_kv)
```

```python
import functools

import numpy as np
import jax
import jax.numpy as jnp
from jax import lax
from jax.experimental import pallas as pl
from jax.experimental.pallas import tpu as pltpu

D_MODEL = 1024
DEPTH = 4
HEAD_DIM = 64
N_HEADS = 8
N_KV = 2
GROUP = N_HEADS // N_KV
MIX_W = N_HEADS * HEAD_DIM
KV_W = N_KV * HEAD_DIM
EVEN_IN = 2 * (2 * MIX_W + 2 * KV_W)
ODD_IN = 2 * D_MODEL
GRID_W = 64
WINDOW = 128
ROPE_BASE = 10000.0
C_GROUPS = 4
C_GROUP_DIM = D_MODEL // C_GROUPS
EPS = 1e-6
NEG_BIG = -1e30
ROWS = 256
COND_ROWS = 8
VMEM_LIMIT = 48 * 1024 * 1024
CTX_VMEM_LIMIT = 56 * 1024 * 1024
CTX_BATCH_PER_STEP = 2

QA, KA, VA, GA = 0, 512, 640, 768
QB, KB, VB, GB = 1280, 1792, 1920, 2048

F32 = jnp.float32
BF16 = jnp.bfloat16


def _dft_tables(n):
    k = np.arange(n, dtype=np.int64)
    ang = ((k[:, None] * k[None, :]) % n).astype(np.float64) * (2.0 * np.pi / n)
    return np.cos(ang).astype(np.float32), np.sin(ang).astype(np.float32)


def _channel_dft():
    c, s = _dft_tables(C_GROUP_DIM)
    return np.concatenate([c, s], axis=1)


def _position_dft(n):
    c, s = _dft_tables(n)
    return np.concatenate([c, -s], axis=1)


def _rope_tables(n_tok):
    rows = n_tok // GRID_W
    row = np.repeat(np.arange(rows), GRID_W).astype(np.float64)
    col = np.tile(np.arange(GRID_W), rows).astype(np.float64)
    half = HEAD_DIM // 2
    inv = 1.0 / (ROPE_BASE ** (np.arange(0, half, 2, dtype=np.float64) / half))
    ang_r = row[:, None] * inv
    ang_c = col[:, None] * inv
    zeros = np.zeros_like(ang_r)
    cos_h = np.concatenate([np.cos(ang_r), np.cos(ang_r), np.cos(ang_c), np.cos(ang_c)], axis=1)
    nxt_h = np.concatenate([-np.sin(ang_r), zeros, -np.sin(ang_c), zeros], axis=1)
    prv_h = np.concatenate([zeros, np.sin(ang_r), zeros, np.sin(ang_c)], axis=1)
    two = lambda t: np.concatenate([t, t], axis=1).astype(np.float32)
    return two(cos_h), two(nxt_h), two(prv_h)


def _silu(x):
    return x / (1.0 + jnp.exp(-x))


def _norm_mod(x, g, shift, scale):
    ms = jnp.mean(x * x, axis=-1, keepdims=True)
    return x * lax.rsqrt(ms + EPS) * (g * (1.0 + scale)) + shift


def _rms_head(xh, g):
    ms = jnp.mean(xh * xh, axis=-1, keepdims=True)
    return xh * lax.rsqrt(ms + EPS) * g


def _rms_pair(xc, g2):
    lo = lax.broadcasted_iota(jnp.int32, xc.shape, 1) < HEAD_DIM
    ss = xc * xc
    s_lo = jnp.sum(jnp.where(lo, ss, 0.0), axis=-1, keepdims=True)
    s_hi = jnp.sum(jnp.where(lo, 0.0, ss), axis=-1, keepdims=True)
    ms = jnp.where(lo, s_lo, s_hi) * (1.0 / HEAD_DIM)
    return xc * lax.rsqrt(ms + EPS) * g2


def _rope(xc, cos, sin_next, sin_prev):
    nxt = pltpu.roll(xc, 128 - 16, 1)
    prv = pltpu.roll(xc, 16, 1)
    return xc * cos + nxt * sin_next + prv * sin_prev


def _kv_operands(k2, v2, j):
    low = lax.broadcasted_iota(jnp.int32, k2.shape, 1) < HEAD_DIM
    mine = low if j == 0 else jnp.logical_not(low)
    km = jnp.where(mine, k2, 0.0)
    vm = jnp.where(mine, v2, 0.0)
    kr = pltpu.roll(km, HEAD_DIM, 1)
    vr = pltpu.roll(vm, HEAD_DIM, 1)
    k_lo, k_hi, v_lo, v_hi = (km, kr, vm, vr) if j == 0 else (kr, km, vr, vm)
    ones_lo = jnp.where(low, 1.0, 0.0)
    k_cat = jnp.concatenate([k_lo, k_hi], axis=0).astype(BF16)
    w = jnp.concatenate([jnp.concatenate([v_lo, ones_lo], axis=1),
                         jnp.concatenate([v_hi, 1.0 - ones_lo], axis=1)], axis=0).astype(BF16)
    return k_cat, w


def _attend_pair(qc, k_cat, w, sinks=None, mask=None):
    tk = k_cat.shape[0] // 2
    s = lax.dot_general(qc.astype(BF16), k_cat, (((1,), (1,)), ((), ())), preferred_element_type=F32)
    es, ms = [], []
    for hh in range(2):
        sh = s[:, hh * tk:(hh + 1) * tk]
        if mask is not None:
            sh = jnp.where(mask, sh, NEG_BIG)
        m = jnp.max(sh, axis=-1, keepdims=True)
        if sinks is not None:
            m = jnp.maximum(m, sinks[hh])
        es.append(jnp.exp(sh - m))
        ms.append(m)
    e = jnp.concatenate(es, axis=1).astype(BF16)
    nd = jnp.dot(e, w, preferred_element_type=F32)
    num, den = nd[:, :128], nd[:, 128:]
    if sinks is not None:
        low = lax.broadcasted_iota(jnp.int32, num.shape, 1) < HEAD_DIM
        den = den + jnp.where(low, jnp.exp(sinks[0] - ms[0]), jnp.exp(sinks[1] - ms[1]))
    return num / den


def _mod_rows(mod_ref, row):
    return mod_ref[0, pl.ds(row, 1), :], mod_ref[1, pl.ds(row, 1), :], mod_ref[2, pl.ds(row, 1), :]


def _ada_kernel(cond_ref, w_ref, b_ref, o_ref):
    a = _silu(cond_ref[...]).astype(BF16)
    o_ref[...] = jnp.dot(a, w_ref[...].astype(BF16), preferred_element_type=F32) + b_ref[...]


def _ada_all(cond, ada_w, ada_b):
    return pl.pallas_call(
        _ada_kernel,
        grid=(DEPTH, 3),
        in_specs=[
            pl.BlockSpec((COND_ROWS, D_MODEL), lambda l, p: (0, 0)),
            pl.BlockSpec((None, D_MODEL, D_MODEL), lambda l, p: (l, 0, p)),
            pl.BlockSpec((None, None, 1, D_MODEL), lambda l, p: (l, p, 0, 0)),
        ],
        out_specs=pl.BlockSpec((None, None, COND_ROWS, D_MODEL), lambda l, p: (l, p, 0, 0)),
        out_shape=jax.ShapeDtypeStruct((DEPTH, 3, COND_ROWS, D_MODEL), F32),
        compiler_params=pltpu.CompilerParams(
            dimension_semantics=("parallel", "parallel"), vmem_limit_bytes=VMEM_LIMIT),
        name="ada_mod",
    )(cond, ada_w, ada_b.reshape(DEPTH, 3, 1, D_MODEL))


def _fourier_rows(p_ref, ccs_ref, ucs_ref, rows):
    for grp in range(C_GROUPS):
        c0 = C_GROUP_DIM * grp
        ug = p_ref[:, c0:c0 + C_GROUP_DIM].astype(BF16)
        t = jnp.dot(ug, ccs_ref[...], preferred_element_type=F32)
        ucs_ref[0:rows, c0:c0 + C_GROUP_DIM] = t[:, :C_GROUP_DIM].astype(BF16)
        ucs_ref[rows:2 * rows, c0:c0 + C_GROUP_DIM] = t[:, C_GROUP_DIM:].astype(BF16)


def _ctx_even_layer(h, i, seq, sink_ref, win_ref, wout_ref, gq_ref, gk_ref, kv_refs, p_ref, o_ref):
    batch_rows = [slice(r0, r0 + seq) for r0 in range(0, h.shape[0], seq)]
    p_ref[...] = jnp.dot(h.astype(BF16), win_ref[i], preferred_element_type=F32)
    gq2 = gq_ref[i:i + 1, :] * (HEAD_DIM ** -0.5)
    gk2 = gk_ref[i:i + 1, :]
    for mixer, (q0, k0, v0, g0) in enumerate(((QA, KA, VA, GA), (QB, KB, VB, GB))):
        kv = []
        for bb, rows in enumerate(batch_rows):
            k2 = p_ref[rows, k0:k0 + KV_W]
            v2 = p_ref[rows, v0:v0 + KV_W]
            if mixer == 0:
                k2 = _rms_pair(k2, gk2)
            kv_refs[2 * mixer][bb, i] = k2.T.reshape(N_KV, HEAD_DIM, seq)
            kv_refs[2 * mixer + 1][bb, i] = v2.T.reshape(N_KV, HEAD_DIM, seq)
            kv.append((k2, v2))
        for j in range(N_KV):
            ops = [_kv_operands(k2, v2, j) for k2, v2 in kv]
            for cc in range(GROUP // 2):
                c0 = 128 * (j * (GROUP // 2) + cc)
                for rows, (k_cat, w) in zip(batch_rows, ops):
                    qc = p_ref[rows, q0 + c0:q0 + c0 + 128]
                    if mixer == 0:
                        qc = _rms_pair(qc, gq2)
                        sinks = None
                    else:
                        qc = qc * (HEAD_DIM ** -0.5)
                        head = c0 // HEAD_DIM
                        sinks = (sink_ref[i, head], sink_ref[i, head + 1])
                    o = _attend_pair(qc, k_cat, w, sinks)
                    gt = p_ref[rows, g0 + c0:g0 + c0 + 128]
                    o_ref[rows, mixer * MIX_W + c0:mixer * MIX_W + c0 + 128] = (o * _silu(gt)).astype(BF16)
    return jnp.dot(o_ref[...], wout_ref[i], preferred_element_type=F32)


def _ctx_odd_layer(h, i, seq, win_ref, wout_ref, ccs_ref, fs_ref, p_ref, o_ref, ucs_ref):
    n_b = h.shape[0] // seq
    p_ref[:, 0:ODD_IN] = jnp.dot(h.astype(BF16), win_ref[i], preferred_element_type=F32)
    for grp in range(C_GROUPS):
        c0 = C_GROUP_DIM * grp
        t = jnp.dot(p_ref[:, c0:c0 + C_GROUP_DIM].astype(BF16), ccs_ref[...], preferred_element_type=F32)
        for bb in range(n_b):
            tb = t[bb * seq:(bb + 1) * seq]
            ucs_ref[bb, 0:seq, c0:c0 + C_GROUP_DIM] = tb[:, :C_GROUP_DIM].astype(BF16)
            ucs_ref[bb, seq:2 * seq, c0:c0 + C_GROUP_DIM] = tb[:, C_GROUP_DIM:].astype(BF16)
    scale = float(1.0 / np.sqrt(float(seq * C_GROUP_DIM)))
    for bb in range(n_b):
        rows = slice(bb * seq, (bb + 1) * seq)
        r = jnp.dot(fs_ref[...], ucs_ref[bb], preferred_element_type=F32) * scale
        o_ref[rows, :] = (r * _silu(p_ref[rows, D_MODEL:2 * D_MODEL])).astype(BF16)
    return jnp.dot(o_ref[...], wout_ref[i], preferred_element_type=F32)


def _ctx_kernel(sink_ref, x_ref, mod_ref, g_ref, ewin_ref, ewout_ref, owin_ref, owout_ref, gq_ref, gk_ref,
                ccs_ref, fs_ref, fg_ref, xo_ref, ka_ref, va_ref, kb_ref, vb_ref,
                xs_ref, p_ref, o_ref, ucs_ref):
    n_b, seq, _ = x_ref.shape
    xs_ref[...] = x_ref[...].reshape(n_b * seq, D_MODEL)
    for l in range(DEPTH):
        i = l // 2
        shift, scale, gate = (mod_ref[l, part, 0:1, :] for part in range(3))
        h = _norm_mod(xs_ref[...], g_ref[l:l + 1, :], shift, scale)
        if l % 2 == 0:
            y = _ctx_even_layer(h, i, seq, sink_ref, ewin_ref, ewout_ref, gq_ref, gk_ref,
                                (ka_ref, va_ref, kb_ref, vb_ref), p_ref, o_ref)
        else:
            y = _ctx_odd_layer(h, i, seq, owin_ref, owout_ref, ccs_ref, fs_ref, p_ref, o_ref, ucs_ref)
        xn = xs_ref[...] + gate * y
        if l == DEPTH - 1:
            xo_ref[...] = _rms_head(xn, fg_ref[...]).reshape(n_b, seq, D_MODEL)
        else:
            xs_ref[...] = xn


def _resident(shape):
    return pl.BlockSpec(shape, lambda i: (0,) * len(shape), pipeline_mode=pl.Buffered(1))


def _ctx_path(x, mod, norm_g, ewin, ewout, owin, owout, gq2, gk2, sink, ccs, fs, fg):
    b, s, _ = x.shape
    n_even = ewin.shape[0]
    bb = CTX_BATCH_PER_STEP
    kv_shape = jax.ShapeDtypeStruct((b, n_even, N_KV, HEAD_DIM, s), F32)
    kv_spec = pl.BlockSpec((bb, n_even, N_KV, HEAD_DIM, s), lambda i: (i, 0, 0, 0, 0))
    x_spec = pl.BlockSpec((bb, s, D_MODEL), lambda i: (i, 0, 0))
    return pl.pallas_call(
        _ctx_kernel,
        grid=(b // bb,),
        in_specs=[
            pl.BlockSpec(memory_space=pltpu.SMEM),
            x_spec,
            _resident(mod.shape), _resident(norm_g.shape),
            _resident(ewin.shape), _resident(ewout.shape), _resident(owin.shape), _resident(owout.shape),
            _resident(gq2.shape), _resident(gk2.shape),
            _resident(ccs.shape), _resident(fs.shape), _resident(fg.shape),
        ],
        out_specs=[x_spec, kv_spec, kv_spec, kv_spec, kv_spec],
        out_shape=[jax.ShapeDtypeStruct(x.shape, F32), kv_shape, kv_shape, kv_shape, kv_shape],
        scratch_shapes=[pltpu.VMEM((bb * s, D_MODEL), F32), pltpu.VMEM((bb * s, EVEN_IN), F32),
                        pltpu.VMEM((bb * s, D_MODEL), BF16), pltpu.VMEM((bb, 2 * s, D_MODEL), BF16)],
        compiler_params=pltpu.CompilerParams(
            dimension_semantics=("parallel",), vmem_limit_bytes=CTX_VMEM_LIMIT),
        name="ctx_path",
    )(sink, x, mod, norm_g, ewin, ewout, owin, owout, gq2, gk2, ccs, fs, fg)


def _lat_even_proj_kernel(x_ref, mod_ref, g_ref, win_ref, gq_ref, gk_ref, cos_ref, sn_ref, sp_ref,
                          p_ref):
    shift, scale, _ = _mod_rows(mod_ref, 1 + pl.program_id(0))
    h = _norm_mod(x_ref[0], g_ref[...], shift, scale)
    p_ref[0] = jnp.dot(h.astype(BF16), win_ref[...], preferred_element_type=F32)
    cos, sn, sp = cos_ref[...], sn_ref[...], sp_ref[...]
    gq2 = gq_ref[...] * (HEAD_DIM ** -0.5)
    gk2 = gk_ref[...]

    def chunk(c0):
        return p_ref[0, :, c0:c0 + 128]

    for c0 in range(QA, QA + MIX_W, 128):
        p_ref[0, :, c0:c0 + 128] = _rope(_rms_pair(chunk(c0), gq2), cos, sn, sp)
    p_ref[0, :, KA:KA + 128] = _rope(_rms_pair(chunk(KA), gk2), cos, sn, sp)
    for c0 in range(QB, QB + MIX_W, 128):
        p_ref[0, :, c0:c0 + 128] = _rope(chunk(c0) * (HEAD_DIM ** -0.5), cos, sn, sp)
    p_ref[0, :, KB:KB + 128] = _rope(chunk(KB), cos, sn, sp)
    for g0 in (GA, GB):
        for c0 in range(g0, g0 + MIX_W, 128):
            p_ref[0, :, c0:c0 + 128] = _silu(chunk(c0))


def _layer_spec(tail, idx):
    return pl.BlockSpec((None,) + tuple(tail), lambda i, n: (idx,) + (0,) * len(tail))


def _lat_even_proj(x, mod, g3, ewin, gq3, gk3, cos, sn, sp, l):
    b, s, _ = x.shape
    rope_spec = pl.BlockSpec((ROWS, 128), lambda i, n: (n, 0))
    return pl.pallas_call(
        _lat_even_proj_kernel,
        grid=(b, s // ROWS),
        in_specs=[
            pl.BlockSpec((1, ROWS, D_MODEL), lambda i, n: (i, n, 0)),
            _layer_spec((3, COND_ROWS, D_MODEL), l),
            _layer_spec((1, D_MODEL), l),
            _layer_spec((D_MODEL, EVEN_IN), l // 2),
            _layer_spec((1, 128), l // 2),
            _layer_spec((1, 128), l // 2),
            rope_spec, rope_spec, rope_spec,
        ],
        out_specs=pl.BlockSpec((1, ROWS, EVEN_IN), lambda i, n: (i, n, 0)),
        out_shape=jax.ShapeDtypeStruct((b, s, EVEN_IN), F32),
        compiler_params=pltpu.CompilerParams(
            dimension_semantics=("parallel", "parallel"), vmem_limit_bytes=VMEM_LIMIT),
        name="lat_even_proj",
    )(x, mod, g3, ewin, gq3, gk3, cos, sn, sp)


def _lat_even_attn_kernel(sink_ref, x_ref, mod_ref, pq_ref, ka_ref, va_ref, kb_ref, vb_ref,
                          cka_ref, cva_ref, ckb_ref, cvb_ref, wout_ref, xo_ref, o_ref, *, layer_i):
    n = pl.program_id(1)
    seq = ka_ref.shape[1]
    _, _, gate = _mod_rows(mod_ref, 1 + pl.program_id(0))

    prev0 = pl.multiple_of(jnp.maximum(n * ROWS - WINDOW, 0), WINDOW)
    own0 = pl.multiple_of(n * ROWS, ROWS)
    next0 = pl.multiple_of(jnp.minimum(n * ROWS + ROWS, seq - WINDOW), WINDOW)
    win_len = ROWS + 2 * WINDOW
    ctx_len = cka_ref.shape[0]
    qi = lax.broadcasted_iota(jnp.int32, (ROWS, win_len + ctx_len), 0)
    cj = lax.broadcasted_iota(jnp.int32, (ROWS, win_len + ctx_len), 1)
    kpos = n * ROWS - WINDOW + cj
    in_win = (jnp.abs(cj - WINDOW - qi) <= WINDOW) & (kpos >= 0) & (kpos < seq)
    mask_b = in_win | (cj >= win_len)

    def window(ref, cache_ref):
        return jnp.concatenate([ref[0, pl.ds(prev0, WINDOW), :], ref[0, pl.ds(own0, ROWS), :],
                                ref[0, pl.ds(next0, WINDOW), :], cache_ref[...]], axis=0)

    k2a = jnp.concatenate([ka_ref[0], cka_ref[...]], axis=0)
    v2a = jnp.concatenate([va_ref[0], cva_ref[...]], axis=0)
    k2b, v2b = window(kb_ref, ckb_ref), window(vb_ref, cvb_ref)
    for mixer, (q0, g0, k2, v2) in enumerate(((QA, GA, k2a, v2a), (QB, GB, k2b, v2b))):
        for j in range(N_KV):
            k_cat, w = _kv_operands(k2, v2, j)
            for cc in range(GROUP // 2):
                c0 = 128 * (j * (GROUP // 2) + cc)
                if mixer == 0:
                    sinks, mask = None, None
                else:
                    head = c0 // HEAD_DIM
                    sinks, mask = (sink_ref[layer_i, head], sink_ref[layer_i, head + 1]), mask_b
                o = _attend_pair(pq_ref[0, :, q0 + c0:q0 + c0 + 128], k_cat, w, sinks, mask)
                gt = pq_ref[0, :, g0 + c0:g0 + c0 + 128]
                o_ref[:, mixer * MIX_W + c0:mixer * MIX_W + c0 + 128] = (o * gt).astype(BF16)
    y = jnp.dot(o_ref[...], wout_ref[...], preferred_element_type=F32)
    xo_ref[0] = x_ref[0] + gate * y


def _lat_even_attn(x, mod, p, cka, cva, ckb, cvb, ewout, sink, l):
    b, s, _ = x.shape
    past = cka.shape[2]
    layer_i = l // 2
    kv_spec = lambda blk: pl.BlockSpec((1, s, KV_W), lambda i, n, blk=blk: (i, 0, blk))
    cache_spec = pl.BlockSpec((None, None, past, KV_W), lambda i, n: (i, layer_i, 0, 0))
    return pl.pallas_call(
        functools.partial(_lat_even_attn_kernel, layer_i=layer_i),
        grid=(b, s // ROWS),
        in_specs=[
            pl.BlockSpec(memory_space=pltpu.SMEM),
            pl.BlockSpec((1, ROWS, D_MODEL), lambda i, n: (i, n, 0)),
            _layer_spec((3, COND_ROWS, D_MODEL), l),
            pl.BlockSpec((1, ROWS, EVEN_IN), lambda i, n: (i, n, 0)),
            kv_spec(KA // KV_W), kv_spec(VA // KV_W), kv_spec(KB // KV_W), kv_spec(VB // KV_W),
            cache_spec, cache_spec, cache_spec, cache_spec,
            _layer_spec((D_MODEL, D_MODEL), layer_i),
        ],
        out_specs=pl.BlockSpec((1, ROWS, D_MODEL), lambda i, n: (i, n, 0)),
        out_shape=jax.ShapeDtypeStruct(x.shape, F32),
        scratch_shapes=[pltpu.VMEM((ROWS, D_MODEL), BF16)],
        compiler_params=pltpu.CompilerParams(
            dimension_semantics=("parallel", "parallel"), vmem_limit_bytes=VMEM_LIMIT),
        name="lat_even_attn",
    )(sink, x, mod, p, p, p, p, p, cka, cva, ckb, cvb, ewout)


def _lat_odd_proj_kernel(x_ref, mod_ref, g_ref, win_ref, ccs_ref, ucs_ref, gate_ref, p_ref, t_ref):
    shift, scale, _ = _mod_rows(mod_ref, 1 + pl.program_id(0))
    h = _norm_mod(x_ref[0], g_ref[...], shift, scale)
    p_ref[...] = jnp.dot(h.astype(BF16), win_ref[...], preferred_element_type=F32)
    _fourier_rows(p_ref, ccs_ref, t_ref, ROWS)
    ucs_ref[0, 0] = t_ref[0:ROWS, :]
    ucs_ref[0, 1] = t_ref[ROWS:2 * ROWS, :]
    gate_ref[0] = _silu(p_ref[:, D_MODEL:2 * D_MODEL])


def _lat_odd_proj(x, mod, g3, owin, ccs, l):
    b, s, _ = x.shape
    return pl.pallas_call(
        _lat_odd_proj_kernel,
        grid=(b, s // ROWS),
        in_specs=[
            pl.BlockSpec((1, ROWS, D_MODEL), lambda i, n: (i, n, 0)),
            _layer_spec((3, COND_ROWS, D_MODEL), l),
            _layer_spec((1, D_MODEL), l),
            _layer_spec((D_MODEL, ODD_IN), l // 2),
            pl.BlockSpec((C_GROUP_DIM, 2 * C_GROUP_DIM), lambda i, n: (0, 0)),
        ],
        out_specs=[
            pl.BlockSpec((1, 2, ROWS, D_MODEL), lambda i, n: (i, 0, n, 0)),
            pl.BlockSpec((1, ROWS, D_MODEL), lambda i, n: (i, n, 0)),
        ],
        out_shape=[jax.ShapeDtypeStruct((b, 2, s, D_MODEL), BF16),
                   jax.ShapeDtypeStruct((b, s, D_MODEL), F32)],
        scratch_shapes=[pltpu.VMEM((ROWS, ODD_IN), F32), pltpu.VMEM((2 * ROWS, D_MODEL), BF16)],
        compiler_params=pltpu.CompilerParams(
            dimension_semantics=("parallel", "parallel"), vmem_limit_bytes=VMEM_LIMIT),
        name="lat_odd_proj",
    )(x, mod, g3, owin, ccs)


def _lat_odd_mix_kernel(x_ref, mod_ref, fs_ref, ucs_ref, gate_ref, wout_ref, fg_ref, xo_ref, *, final):
    seq = ucs_ref.shape[1] // 2
    _, _, gate = _mod_rows(mod_ref, 1 + pl.program_id(0))
    r = jnp.dot(fs_ref[...], ucs_ref[0], preferred_element_type=F32)
    r = r * float(1.0 / np.sqrt(float(seq * C_GROUP_DIM)))
    mix = r * gate_ref[0]
    y = jnp.dot(mix.astype(BF16), wout_ref[...], preferred_element_type=F32)
    xn = x_ref[0] + gate * y
    if final:
        xn = _rms_head(xn, fg_ref[...])
    xo_ref[0] = xn


def _lat_odd_mix(x, mod, fs, ucs, gsilu, owout, fg, l):
    b, s, _ = x.shape
    return pl.pallas_call(
        functools.partial(_lat_odd_mix_kernel, final=l == DEPTH - 1),
        grid=(b, s // ROWS),
        in_specs=[
            pl.BlockSpec((1, ROWS, D_MODEL), lambda i, n: (i, n, 0)),
            _layer_spec((3, COND_ROWS, D_MODEL), l),
            pl.BlockSpec((ROWS, 2 * s), lambda i, n: (n, 0)),
            pl.BlockSpec((1, 2 * s, D_MODEL), lambda i, n: (i, 0, 0)),
            pl.BlockSpec((1, ROWS, D_MODEL), lambda i, n: (i, n, 0)),
            _layer_spec((D_MODEL, D_MODEL), l // 2),
            pl.BlockSpec((1, D_MODEL), lambda i, n: (0, 0)),
        ],
        out_specs=pl.BlockSpec((1, ROWS, D_MODEL), lambda i, n: (i, n, 0)),
        out_shape=jax.ShapeDtypeStruct(x.shape, F32),
        compiler_params=pltpu.CompilerParams(
            dimension_semantics=("parallel", "parallel"), vmem_limit_bytes=VMEM_LIMIT),
        name="lat_odd_mix",
    )(x, mod, fs, ucs.reshape(b, 2 * s, D_MODEL), gsilu, owout, fg)


def kernel(x_prompt, x_sample, cache_k_a, cache_v_a, cache_k_b, cache_v_b, c, c_ctx, norm_g, ada_w, ada_b,
           even_w_in, even_w_out, qk_g_q, qk_g_k, sink_logit, odd_w_in, odd_w_out, final_g):
    batch, seq, _ = x_prompt.shape
    dec_batch, dec_seq, _ = x_sample.shape
    n_even = even_w_in.shape[0]
    past = cache_k_a.shape[2]

    cond = jnp.concatenate(
        [c_ctx[None, :], c, jnp.zeros((COND_ROWS - 1 - dec_batch, D_MODEL), F32)], axis=0)
    mod = _ada_all(cond, ada_w, ada_b)

    ccs = jnp.asarray(_channel_dft()).astype(BF16)
    fs_ctx = jnp.asarray(_position_dft(seq)).astype(BF16)
    fs_lat = jnp.asarray(_position_dft(dec_seq)).astype(BF16)
    cos, sn, sp = (jnp.asarray(t) for t in _rope_tables(dec_seq))

    caches = [a.reshape(dec_batch, n_even, past, KV_W) for a in (cache_k_a, cache_v_a, cache_k_b, cache_v_b)]
    fg = final_g.reshape(1, D_MODEL)
    ewin, ewout = even_w_in.astype(BF16), even_w_out.astype(BF16)
    owin, owout = odd_w_in.astype(BF16), odd_w_out.astype(BF16)
    gq2 = jnp.tile(qk_g_q, (1, 2))
    gk2 = jnp.tile(qk_g_k, (1, 2))

    xc, *new_kv = _ctx_path(x_prompt, mod, norm_g, ewin, ewout, owin, owout, gq2, gk2, sink_logit,
                            ccs, fs_ctx, fg)
    new_kv = [jnp.transpose(a, (0, 1, 4, 2, 3)) for a in new_kv]

    xl = x_sample
    g3 = norm_g.reshape(DEPTH, 1, D_MODEL)
    gq3, gk3 = gq2.reshape(n_even, 1, 128), gk2.reshape(n_even, 1, 128)
    for l in range(DEPTH):
        if l % 2 == 0:
            p = _lat_even_proj(xl, mod, g3, ewin, gq3, gk3, cos, sn, sp, l)
            xl = _lat_even_attn(xl, mod, p, *caches, ewout, sink_logit, l)
        else:
            ucs, gsilu = _lat_odd_proj(xl, mod, g3, owin, ccs, l)
            xl = _lat_odd_mix(xl, mod, fs_lat, ucs, gsilu, owout, fg, l)
    return (xc, xl, *new_kv)
```

```python
import functools

import numpy as np
import jax
import jax.numpy as jnp
from jax import lax
from jax.experimental import pallas as pl
from jax.experimental.pallas import tpu as pltpu

D_MODEL = 1024
DEPTH = 4
HEAD_DIM = 64
N_HEADS = 8
N_KV = 2
GROUP = N_HEADS // N_KV
MIX_W = N_HEADS * HEAD_DIM
KV_W = N_KV * HEAD_DIM
EVEN_IN = 2 * (2 * MIX_W + 2 * KV_W)
ODD_IN = 2 * D_MODEL
GRID_W = 64
WINDOW = 128
ROPE_BASE = 10000.0
C_GROUPS = 4
C_GROUP_DIM = D_MODEL // C_GROUPS
EPS = 1e-6
NEG_BIG = -1e30
ROWS = 256
COND_ROWS = 8
VMEM_LIMIT = 48 * 1024 * 1024
CTX_VMEM_LIMIT = 56 * 1024 * 1024
CTX_BATCH_PER_STEP = 2

QA, KA, VA, GA = 0, 512, 640, 768
QB, KB, VB, GB = 1280, 1792, 1920, 2048

F32 = jnp.float32
BF16 = jnp.bfloat16


def _dft_tables(n):
    k = np.arange(n, dtype=np.int64)
    ang = ((k[:, None] * k[None, :]) % n).astype(np.float64) * (2.0 * np.pi / n)
    return np.cos(ang).astype(np.float32), np.sin(ang).astype(np.float32)


def _channel_dft():
    c, s = _dft_tables(C_GROUP_DIM)
    return np.concatenate([c, s], axis=1)


def _position_dft(n):
    c, s = _dft_tables(n)
    return np.concatenate([c, -s], axis=1)


def _rope_tables(n_tok):
    rows = n_tok // GRID_W
    row = np.repeat(np.arange(rows), GRID_W).astype(np.float64)
    col = np.tile(np.arange(GRID_W), rows).astype(np.float64)
    half = HEAD_DIM // 2
    inv = 1.0 / (ROPE_BASE ** (np.arange(0, half, 2, dtype=np.float64) / half))
    ang_r = row[:, None] * inv
    ang_c = col[:, None] * inv
    zeros = np.zeros_like(ang_r)
    cos_h = np.concatenate([np.cos(ang_r), np.cos(ang_r), np.cos(ang_c), np.cos(ang_c)], axis=1)
    nxt_h = np.concatenate([-np.sin(ang_r), zeros, -np.sin(ang_c), zeros], axis=1)
    prv_h = np.concatenate([zeros, np.sin(ang_r), zeros, np.sin(ang_c)], axis=1)
    two = lambda t: np.concatenate([t, t], axis=1).astype(np.float32)
    return two(cos_h), two(nxt_h), two(prv_h)


def _silu(x):
    return x / (1.0 + jnp.exp(-x))


def _norm_mod(x, g, shift, scale):
    ms = jnp.mean(x * x, axis=-1, keepdims=True)
    return x * lax.rsqrt(ms + EPS) * (g * (1.0 + scale)) + shift


def _rms_head(xh, g):
    ms = jnp.mean(xh * xh, axis=-1, keepdims=True)
    return xh * lax.rsqrt(ms + EPS) * g


def _rms_pair(xc, g2):
    lo = lax.broadcasted_iota(jnp.int32, xc.shape, 1) < HEAD_DIM
    ss = xc * xc
    s_lo = jnp.sum(jnp.where(lo, ss, 0.0), axis=-1, keepdims=True)
    s_hi = jnp.sum(jnp.where(lo, 0.0, ss), axis=-1, keepdims=True)
    ms = jnp.where(lo, s_lo, s_hi) * (1.0 / HEAD_DIM)
    return xc * lax.rsqrt(ms + EPS) * g2


def _rope(xc, cos, sin_next, sin_prev):
    nxt = pltpu.roll(xc, 128 - 16, 1)
    prv = pltpu.roll(xc, 16, 1)
    return xc * cos + nxt * sin_next + prv * sin_prev


DEN_ROWS = 16
LOG2E = float(np.log2(np.e))
Q_SCALE = (HEAD_DIM ** -0.5) * LOG2E


def _sink_pair(sink_ref, i, head):
    return sink_ref[i, head] * LOG2E, sink_ref[i, head + 1] * LOG2E


def _kv_operands(k2, v2t, j):
    tk = k2.shape[0]
    low = lax.broadcasted_iota(jnp.int32, k2.shape, 1) < HEAD_DIM
    km = jnp.where(low if j == 0 else jnp.logical_not(low), k2, 0.0)
    kr = pltpu.roll(km, HEAD_DIM, 1)
    k_lo, k_hi = (km, kr) if j == 0 else (kr, km)
    k_cat = jnp.concatenate([k_lo, k_hi], axis=0).astype(BF16)
    vjt = v2t[HEAD_DIM * j:HEAD_DIM * (j + 1), :]
    zero = jnp.zeros_like(vjt)
    row = lax.broadcasted_iota(jnp.int32, (DEN_ROWS, 2 * tk), 0)
    col = lax.broadcasted_iota(jnp.int32, (DEN_ROWS, 2 * tk), 1)
    ones = jnp.where(((row == 0) & (col < tk)) | ((row == 1) & (col >= tk)), 1.0, 0.0)
    w_t = jnp.concatenate([jnp.concatenate([vjt, zero], axis=1),
                           jnp.concatenate([zero, vjt], axis=1), ones], axis=0).astype(BF16)
    return k_cat, w_t


def _scores_t(qc, k_cat):
    return lax.dot_general(k_cat, qc.astype(BF16), (((1,), (1,)), ((), ())), preferred_element_type=F32)


def _softmax_t(s_t, sinks=None, mask_t=None):
    tk = s_t.shape[0] // 2
    es, ms = [], []
    for hh in range(2):
        sh = s_t[hh * tk:(hh + 1) * tk]
        if mask_t is not None:
            sh = jnp.where(mask_t, sh, NEG_BIG)
        m = jnp.max(sh, axis=0, keepdims=True)
        if sinks is not None:
            m = jnp.maximum(m, sinks[hh])
        es.append(jnp.exp2(sh - m))
        ms.append(m)
    return jnp.concatenate(es, axis=0).astype(BF16), ms


def _values_t(e_t, w_t):
    return jnp.dot(w_t, e_t, preferred_element_type=F32)


def _normalise_t(nd, ms, sinks=None):
    dens = [nd[2 * HEAD_DIM + hh:2 * HEAD_DIM + hh + 1] for hh in range(2)]
    if sinks is not None:
        dens = [dens[hh] + jnp.exp2(sinks[hh] - ms[hh]) for hh in range(2)]
    o_t = jnp.concatenate([nd[0:HEAD_DIM] / dens[0], nd[HEAD_DIM:2 * HEAD_DIM] / dens[1]], axis=0)
    return o_t.T


def _run_skewed(items, stages):
    state = list(items)
    for t in range(len(items) + len(stages) - 1):
        for k, stage in enumerate(stages):
            if 0 <= t - k < len(items):
                state[t - k] = stage(state[t - k])


def _mod_rows(mod_ref, row):
    return mod_ref[0, pl.ds(row, 1), :], mod_ref[1, pl.ds(row, 1), :], mod_ref[2, pl.ds(row, 1), :]


def _ada_kernel(cond_ref, w_ref, b_ref, o_ref):
    a = _silu(cond_ref[...]).astype(BF16)
    o_ref[...] = jnp.dot(a, w_ref[...].astype(BF16), preferred_element_type=F32) + b_ref[...]


def _ada_all(cond, ada_w, ada_b):
    return pl.pallas_call(
        _ada_kernel,
        grid=(DEPTH, 3),
        in_specs=[
            pl.BlockSpec((COND_ROWS, D_MODEL), lambda l, p: (0, 0)),
            pl.BlockSpec((None, D_MODEL, D_MODEL), lambda l, p: (l, 0, p)),
            pl.BlockSpec((None, None, 1, D_MODEL), lambda l, p: (l, p, 0, 0)),
        ],
        out_specs=pl.BlockSpec((None, None, COND_ROWS, D_MODEL), lambda l, p: (l, p, 0, 0)),
        out_shape=jax.ShapeDtypeStruct((DEPTH, 3, COND_ROWS, D_MODEL), F32),
        compiler_params=pltpu.CompilerParams(
            dimension_semantics=("parallel", "parallel"), vmem_limit_bytes=VMEM_LIMIT),
        name="ada_mod",
    )(cond, ada_w, ada_b.reshape(DEPTH, 3, 1, D_MODEL))


def _fourier_rows(p_ref, ccs_ref, ucs_ref, rows):
    for grp in range(C_GROUPS):
        c0 = C_GROUP_DIM * grp
        ug = p_ref[:, c0:c0 + C_GROUP_DIM].astype(BF16)
        t = jnp.dot(ug, ccs_ref[...], preferred_element_type=F32)
        ucs_ref[0:rows, c0:c0 + C_GROUP_DIM] = t[:, :C_GROUP_DIM].astype(BF16)
        ucs_ref[rows:2 * rows, c0:c0 + C_GROUP_DIM] = t[:, C_GROUP_DIM:].astype(BF16)


def _ctx_even_mix(i, sink_ref, gq_ref, gk_ref, kv_refs, p_ref, o_ref):
    n_b, seq, _ = p_ref.shape
    gq2 = gq_ref[i:i + 1, :] * Q_SCALE
    gk2 = gk_ref[i:i + 1, :]
    items = []
    for mixer, (q0, k0, v0, g0) in enumerate(((QA, KA, VA, GA), (QB, KB, VB, GB))):
        kv = []
        for bb in range(n_b):
            k2 = p_ref[bb, :, k0:k0 + KV_W]
            v2 = p_ref[bb, :, v0:v0 + KV_W]
            if mixer == 0:
                k2 = _rms_pair(k2, gk2)
            v2t = v2.T
            kv_refs[2 * mixer][bb, i] = k2.T.reshape(N_KV, HEAD_DIM, seq)
            kv_refs[2 * mixer + 1][bb, i] = v2t.reshape(N_KV, HEAD_DIM, seq)
            kv.append((k2, v2t))
        for j in range(N_KV):
            ops = [_kv_operands(k2, v2t, j) for k2, v2t in kv]
            for cc in range(GROUP // 2):
                c0 = 128 * (j * (GROUP // 2) + cc)
                sinks = None if mixer == 0 else _sink_pair(sink_ref, i, c0 // HEAD_DIM)
                for bb, (k_cat, w_t) in enumerate(ops):
                    items.append(dict(rows=bb, q=q0 + c0, g=g0 + c0, o=mixer * MIX_W + c0,
                                      norm=mixer == 0, sinks=sinks, k_cat=k_cat, w_t=w_t))

    def scores(it):
        qc = p_ref[it["rows"], :, it["q"]:it["q"] + 128]
        qc = _rms_pair(qc, gq2) if it["norm"] else qc * Q_SCALE
        return dict(it, s_t=_scores_t(qc, it["k_cat"]))

    def softmax(it):
        e_t, ms = _softmax_t(it["s_t"], it["sinks"])
        return dict(it, e_t=e_t, ms=ms, s_t=None)

    def values(it):
        return dict(it, nd=_values_t(it["e_t"], it["w_t"]), e_t=None)

    def finish(it):
        o = _normalise_t(it["nd"], it["ms"], it["sinks"])
        gt = p_ref[it["rows"], :, it["g"]:it["g"] + 128]
        o_ref[it["rows"], :, it["o"]:it["o"] + 128] = (o * _silu(gt)).astype(BF16)
        return None

    _run_skewed(items, (scores, softmax, values, finish))


def _ctx_odd_mix(ccs_ref, fs_ref, p_ref, o_ref, ucs_ref):
    n_b, seq, _ = p_ref.shape
    for bb in range(n_b):
        for grp in range(C_GROUPS):
            c0 = C_GROUP_DIM * grp
            t = jnp.dot(p_ref[bb, :, c0:c0 + C_GROUP_DIM].astype(BF16), ccs_ref[...],
                        preferred_element_type=F32)
            ucs_ref[bb, 0:seq, c0:c0 + C_GROUP_DIM] = t[:, :C_GROUP_DIM].astype(BF16)
            ucs_ref[bb, seq:2 * seq, c0:c0 + C_GROUP_DIM] = t[:, C_GROUP_DIM:].astype(BF16)
    scale = float(1.0 / np.sqrt(float(seq * C_GROUP_DIM)))
    for bb in range(n_b):
        r = jnp.dot(fs_ref[...], ucs_ref[bb], preferred_element_type=F32) * scale
        o_ref[bb] = (r * _silu(p_ref[bb, :, D_MODEL:2 * D_MODEL])).astype(BF16)


def _ctx_kernel(sink_ref, x_ref, mod_ref, g_ref, ewin_ref, ewout_ref, owin_ref, owout_ref, gq_ref, gk_ref,
                ccs_ref, fs_ref, fg_ref, xo_ref, ka_ref, va_ref, kb_ref, vb_ref,
                xs_ref, p_ref, o_ref, ucs_ref):
    n_b = x_ref.shape[0]
    for l in range(DEPTH):
        i = l // 2
        even = l % 2 == 0
        win_ref, wout_ref, width = (ewin_ref, ewout_ref, EVEN_IN) if even else (owin_ref, owout_ref, ODD_IN)
        shift, scale, gate = (mod_ref[l, part, 0:1, :] for part in range(3))
        for bb in range(n_b):
            x = x_ref[bb] if l == 0 else xs_ref[bb]
            h = _norm_mod(x, g_ref[l:l + 1, :], shift, scale)
            p_ref[bb, :, 0:width] = jnp.dot(h.astype(BF16), win_ref[i], preferred_element_type=F32)
        if even:
            _ctx_even_mix(i, sink_ref, gq_ref, gk_ref, (ka_ref, va_ref, kb_ref, vb_ref), p_ref, o_ref)
        else:
            _ctx_odd_mix(ccs_ref, fs_ref, p_ref, o_ref, ucs_ref)
        for bb in range(n_b):
            y = jnp.dot(o_ref[bb], wout_ref[i], preferred_element_type=F32)
            xn = (x_ref[bb] if l == 0 else xs_ref[bb]) + gate * y
            if l == DEPTH - 1:
                xo_ref[bb] = _rms_head(xn, fg_ref[...])
            else:
                xs_ref[bb] = xn


def _resident(shape):
    return pl.BlockSpec(shape, lambda i: (0,) * len(shape), pipeline_mode=pl.Buffered(1))


def _ctx_path(x, mod, norm_g, ewin, ewout, owin, owout, gq2, gk2, sink, ccs, fs, fg):
    b, s, _ = x.shape
    n_even = ewin.shape[0]
    bb = CTX_BATCH_PER_STEP
    kv_shape = jax.ShapeDtypeStruct((b, n_even, N_KV, HEAD_DIM, s), F32)
    kv_spec = pl.BlockSpec((bb, n_even, N_KV, HEAD_DIM, s), lambda i: (i, 0, 0, 0, 0))
    x_spec = pl.BlockSpec((bb, s, D_MODEL), lambda i: (i, 0, 0))
    return pl.pallas_call(
        _ctx_kernel,
        grid=(b // bb,),
        in_specs=[
            pl.BlockSpec(memory_space=pltpu.SMEM),
            x_spec,
            _resident(mod.shape), _resident(norm_g.shape),
            _resident(ewin.shape), _resident(ewout.shape), _resident(owin.shape), _resident(owout.shape),
            _resident(gq2.shape), _resident(gk2.shape),
            _resident(ccs.shape), _resident(fs.shape), _resident(fg.shape),
        ],
        out_specs=[x_spec, kv_spec, kv_spec, kv_spec, kv_spec],
        out_shape=[jax.ShapeDtypeStruct(x.shape, F32), kv_shape, kv_shape, kv_shape, kv_shape],
        scratch_shapes=[pltpu.VMEM((bb, s, D_MODEL), F32), pltpu.VMEM((bb, s, EVEN_IN), F32),
                        pltpu.VMEM((bb, s, D_MODEL), BF16), pltpu.VMEM((bb, 2 * s, D_MODEL), BF16)],
        compiler_params=pltpu.CompilerParams(
            dimension_semantics=("parallel",), vmem_limit_bytes=CTX_VMEM_LIMIT),
        name="ctx_path",
    )(sink, x, mod, norm_g, ewin, ewout, owin, owout, gq2, gk2, ccs, fs, fg)


def _lat_even_proj_kernel(x_ref, mod_ref, g_ref, win_ref, gq_ref, gk_ref, cos_ref, sn_ref, sp_ref,
                          p_ref):
    shift, scale, _ = _mod_rows(mod_ref, 1 + pl.program_id(0))
    h = _norm_mod(x_ref[0], g_ref[...], shift, scale)
    p_ref[0] = jnp.dot(h.astype(BF16), win_ref[...], preferred_element_type=F32)
    cos, sn, sp = cos_ref[...], sn_ref[...], sp_ref[...]
    gq2 = gq_ref[...] * Q_SCALE
    gk2 = gk_ref[...]

    def chunk(c0):
        return p_ref[0, :, c0:c0 + 128]

    for c0 in range(QA, QA + MIX_W, 128):
        p_ref[0, :, c0:c0 + 128] = _rope(_rms_pair(chunk(c0), gq2), cos, sn, sp)
    p_ref[0, :, KA:KA + 128] = _rope(_rms_pair(chunk(KA), gk2), cos, sn, sp)
    for c0 in range(QB, QB + MIX_W, 128):
        p_ref[0, :, c0:c0 + 128] = _rope(chunk(c0) * Q_SCALE, cos, sn, sp)
    p_ref[0, :, KB:KB + 128] = _rope(chunk(KB), cos, sn, sp)
    for g0 in (GA, GB):
        for c0 in range(g0, g0 + MIX_W, 128):
            p_ref[0, :, c0:c0 + 128] = _silu(chunk(c0))


def _layer_spec(tail, idx):
    return pl.BlockSpec((None,) + tuple(tail), lambda i, n: (idx,) + (0,) * len(tail))


def _lat_even_proj(x, mod, g3, ewin, gq3, gk3, cos, sn, sp, l):
    b, s, _ = x.shape
    rope_spec = pl.BlockSpec((ROWS, 128), lambda i, n: (n, 0))
    return pl.pallas_call(
        _lat_even_proj_kernel,
        grid=(b, s // ROWS),
        in_specs=[
            pl.BlockSpec((1, ROWS, D_MODEL), lambda i, n: (i, n, 0)),
            _layer_spec((3, COND_ROWS, D_MODEL), l),
            _layer_spec((1, D_MODEL), l),
            _layer_spec((D_MODEL, EVEN_IN), l // 2),
            _layer_spec((1, 128), l // 2),
            _layer_spec((1, 128), l // 2),
            rope_spec, rope_spec, rope_spec,
        ],
        out_specs=pl.BlockSpec((1, ROWS, EVEN_IN), lambda i, n: (i, n, 0)),
        out_shape=jax.ShapeDtypeStruct((b, s, EVEN_IN), F32),
        compiler_params=pltpu.CompilerParams(
            dimension_semantics=("parallel", "parallel"), vmem_limit_bytes=VMEM_LIMIT),
        name="lat_even_proj",
    )(x, mod, g3, ewin, gq3, gk3, cos, sn, sp)


def _lat_even_attn_kernel(sink_ref, x_ref, mod_ref, pq_ref, ka_ref, va_ref, kb_ref, vb_ref,
                          cka_ref, cva_ref, ckb_ref, cvb_ref, wout_ref, xo_ref, o_ref, *, layer_i):
    n = pl.program_id(1)
    seq = ka_ref.shape[1]
    _, _, gate = _mod_rows(mod_ref, 1 + pl.program_id(0))

    prev0 = pl.multiple_of(jnp.maximum(n * ROWS - WINDOW, 0), WINDOW)
    own0 = pl.multiple_of(n * ROWS, ROWS)
    next0 = pl.multiple_of(jnp.minimum(n * ROWS + ROWS, seq - WINDOW), WINDOW)
    win_len = ROWS + 2 * WINDOW
    ctx_len = cka_ref.shape[0]
    cj = lax.broadcasted_iota(jnp.int32, (win_len + ctx_len, ROWS), 0)
    qi = lax.broadcasted_iota(jnp.int32, (win_len + ctx_len, ROWS), 1)
    kpos = n * ROWS - WINDOW + cj
    in_win = (jnp.abs(cj - WINDOW - qi) <= WINDOW) & (kpos >= 0) & (kpos < seq)
    mask_b = in_win | (cj >= win_len)

    def window(ref, cache_ref):
        return jnp.concatenate([ref[0, pl.ds(prev0, WINDOW), :], ref[0, pl.ds(own0, ROWS), :],
                                ref[0, pl.ds(next0, WINDOW), :], cache_ref[...]], axis=0)

    k2a = jnp.concatenate([ka_ref[0], cka_ref[...]], axis=0)
    v2a = jnp.concatenate([va_ref[0], cva_ref[...]], axis=0)
    k2b, v2b = window(kb_ref, ckb_ref), window(vb_ref, cvb_ref)
    items = []
    for mixer, (q0, g0, k2, v2) in enumerate(((QA, GA, k2a, v2a), (QB, GB, k2b, v2b))):
        v2t = v2.T
        for j in range(N_KV):
            k_cat, w_t = _kv_operands(k2, v2t, j)
            for cc in range(GROUP // 2):
                c0 = 128 * (j * (GROUP // 2) + cc)
                if mixer == 0:
                    sinks, mask = None, None
                else:
                    head = c0 // HEAD_DIM
                    sinks, mask = _sink_pair(sink_ref, layer_i, head), mask_b
                items.append(dict(q=q0 + c0, g=g0 + c0, o=mixer * MIX_W + c0, sinks=sinks, mask=mask,
                                  k_cat=k_cat, w_t=w_t))

    def scores(it):
        return dict(it, s_t=_scores_t(pq_ref[0, :, it["q"]:it["q"] + 128], it["k_cat"]))

    def softmax(it):
        e_t, ms = _softmax_t(it["s_t"], it["sinks"], it["mask"])
        return dict(it, e_t=e_t, ms=ms, s_t=None)

    def values(it):
        return dict(it, nd=_values_t(it["e_t"], it["w_t"]), e_t=None)

    def finish(it):
        o = _normalise_t(it["nd"], it["ms"], it["sinks"])
        gt = pq_ref[0, :, it["g"]:it["g"] + 128]
        o_ref[:, it["o"]:it["o"] + 128] = (o * gt).astype(BF16)
        return None

    _run_skewed(items, (scores, softmax, values, finish))
    y = jnp.dot(o_ref[...], wout_ref[...], preferred_element_type=F32)
    xo_ref[0] = x_ref[0] + gate * y


def _lat_even_attn(x, mod, p, cka, cva, ckb, cvb, ewout, sink, l):
    b, s, _ = x.shape
    past = cka.shape[2]
    layer_i = l // 2
    kv_spec = lambda blk: pl.BlockSpec((1, s, KV_W), lambda i, n, blk=blk: (i, 0, blk))
    cache_spec = pl.BlockSpec((None, None, past, KV_W), lambda i, n: (i, layer_i, 0, 0))
    return pl.pallas_call(
        functools.partial(_lat_even_attn_kernel, layer_i=layer_i),
        grid=(b, s // ROWS),
        in_specs=[
            pl.BlockSpec(memory_space=pltpu.SMEM),
            pl.BlockSpec((1, ROWS, D_MODEL), lambda i, n: (i, n, 0)),
            _layer_spec((3, COND_ROWS, D_MODEL), l),
            pl.BlockSpec((1, ROWS, EVEN_IN), lambda i, n: (i, n, 0)),
            kv_spec(KA // KV_W), kv_spec(VA // KV_W), kv_spec(KB // KV_W), kv_spec(VB // KV_W),
            cache_spec, cache_spec, cache_spec, cache_spec,
            _layer_spec((D_MODEL, D_MODEL), layer_i),
        ],
        out_specs=pl.BlockSpec((1, ROWS, D_MODEL), lambda i, n: (i, n, 0)),
        out_shape=jax.ShapeDtypeStruct(x.shape, F32),
        scratch_shapes=[pltpu.VMEM((ROWS, D_MODEL), BF16)],
        compiler_params=pltpu.CompilerParams(
            dimension_semantics=("parallel", "parallel"), vmem_limit_bytes=VMEM_LIMIT),
        name="lat_even_attn",
    )(sink, x, mod, p, p, p, p, p, cka, cva, ckb, cvb, ewout)


def _lat_odd_proj_kernel(x_ref, mod_ref, g_ref, win_ref, ccs_ref, ucs_ref, gate_ref, p_ref, t_ref):
    shift, scale, _ = _mod_rows(mod_ref, 1 + pl.program_id(0))
    h = _norm_mod(x_ref[0], g_ref[...], shift, scale)
    p_ref[...] = jnp.dot(h.astype(BF16), win_ref[...], preferred_element_type=F32)
    _fourier_rows(p_ref, ccs_ref, t_ref, ROWS)
    ucs_ref[0, 0] = t_ref[0:ROWS, :]
    ucs_ref[0, 1] = t_ref[ROWS:2 * ROWS, :]
    gate_ref[0] = _silu(p_ref[:, D_MODEL:2 * D_MODEL])


def _lat_odd_proj(x, mod, g3, owin, ccs, l):
    b, s, _ = x.shape
    return pl.pallas_call(
        _lat_odd_proj_kernel,
        grid=(b, s // ROWS),
        in_specs=[
            pl.BlockSpec((1, ROWS, D_MODEL), lambda i, n: (i, n, 0)),
            _layer_spec((3, COND_ROWS, D_MODEL), l),
            _layer_spec((1, D_MODEL), l),
            _layer_spec((D_MODEL, ODD_IN), l // 2),
            pl.BlockSpec((C_GROUP_DIM, 2 * C_GROUP_DIM), lambda i, n: (0, 0)),
        ],
        out_specs=[
            pl.BlockSpec((1, 2, ROWS, D_MODEL), lambda i, n: (i, 0, n, 0)),
            pl.BlockSpec((1, ROWS, D_MODEL), lambda i, n: (i, n, 0)),
        ],
        out_shape=[jax.ShapeDtypeStruct((b, 2, s, D_MODEL), BF16),
                   jax.ShapeDtypeStruct((b, s, D_MODEL), F32)],
        scratch_shapes=[pltpu.VMEM((ROWS, ODD_IN), F32), pltpu.VMEM((2 * ROWS, D_MODEL), BF16)],
        compiler_params=pltpu.CompilerParams(
            dimension_semantics=("parallel", "parallel"), vmem_limit_bytes=VMEM_LIMIT),
        name="lat_odd_proj",
    )(x, mod, g3, owin, ccs)


def _lat_odd_mix_kernel(x_ref, mod_ref, fs_ref, ucs_ref, gate_ref, wout_ref, fg_ref, xo_ref, *, final):
    seq = ucs_ref.shape[1] // 2
    _, _, gate = _mod_rows(mod_ref, 1 + pl.program_id(0))
    r = jnp.dot(fs_ref[...], ucs_ref[0], preferred_element_type=F32)
    r = r * float(1.0 / np.sqrt(float(seq * C_GROUP_DIM)))
    mix = r * gate_ref[0]
    y = jnp.dot(mix.astype(BF16), wout_ref[...], preferred_element_type=F32)
    xn = x_ref[0] + gate * y
    if final:
        xn = _rms_head(xn, fg_ref[...])
    xo_ref[0] = xn


def _lat_odd_mix(x, mod, fs, ucs, gsilu, owout, fg, l):
    b, s, _ = x.shape
    return pl.pallas_call(
        functools.partial(_lat_odd_mix_kernel, final=l == DEPTH - 1),
        grid=(b, s // ROWS),
        in_specs=[
            pl.BlockSpec((1, ROWS, D_MODEL), lambda i, n: (i, n, 0)),
            _layer_spec((3, COND_ROWS, D_MODEL), l),
            pl.BlockSpec((ROWS, 2 * s), lambda i, n: (n, 0)),
            pl.BlockSpec((1, 2 * s, D_MODEL), lambda i, n: (i, 0, 0)),
            pl.BlockSpec((1, ROWS, D_MODEL), lambda i, n: (i, n, 0)),
            _layer_spec((D_MODEL, D_MODEL), l // 2),
            pl.BlockSpec((1, D_MODEL), lambda i, n: (0, 0)),
        ],
        out_specs=pl.BlockSpec((1, ROWS, D_MODEL), lambda i, n: (i, n, 0)),
        out_shape=jax.ShapeDtypeStruct(x.shape, F32),
        compiler_params=pltpu.CompilerParams(
            dimension_semantics=("parallel", "parallel"), vmem_limit_bytes=VMEM_LIMIT),
        name="lat_odd_mix",
    )(x, mod, fs, ucs.reshape(b, 2 * s, D_MODEL), gsilu, owout, fg)


def kernel(x_prompt, x_sample, cache_k_a, cache_v_a, cache_k_b, cache_v_b, c, c_ctx, norm_g, ada_w, ada_b,
           even_w_in, even_w_out, qk_g_q, qk_g_k, sink_logit, odd_w_in, odd_w_out, final_g):
    batch, seq, _ = x_prompt.shape
    dec_batch, dec_seq, _ = x_sample.shape
    n_even = even_w_in.shape[0]
    past = cache_k_a.shape[2]

    cond = jnp.concatenate(
        [c_ctx[None, :], c, jnp.zeros((COND_ROWS - 1 - dec_batch, D_MODEL), F32)], axis=0)
    mod = _ada_all(cond, ada_w, ada_b)

    ccs = jnp.asarray(_channel_dft()).astype(BF16)
    fs_ctx = jnp.asarray(_position_dft(seq)).astype(BF16)
    fs_lat = jnp.asarray(_position_dft(dec_seq)).astype(BF16)
    cos, sn, sp = (jnp.asarray(t) for t in _rope_tables(dec_seq))

    caches = [a.reshape(dec_batch, n_even, past, KV_W) for a in (cache_k_a, cache_v_a, cache_k_b, cache_v_b)]
    fg = final_g.reshape(1, D_MODEL)
    ewin, ewout = even_w_in.astype(BF16), even_w_out.astype(BF16)
    owin, owout = odd_w_in.astype(BF16), odd_w_out.astype(BF16)
    gq2 = jnp.tile(qk_g_q, (1, 2))
    gk2 = jnp.tile(qk_g_k, (1, 2))

    xc, *new_kv = _ctx_path(x_prompt, mod, norm_g, ewin, ewout, owin, owout, gq2, gk2, sink_logit,
                            ccs, fs_ctx, fg)
    new_kv = [jnp.transpose(a, (0, 1, 4, 2, 3)) for a in new_kv]

    xl = x_sample
    g3 = norm_g.reshape(DEPTH, 1, D_MODEL)
    gq3, gk3 = gq2.reshape(n_even, 1, 128), gk2.reshape(n_even, 1, 128)
    for l in range(DEPTH):
        if l % 2 == 0:
            p = _lat_even_proj(xl, mod, g3, ewin, gq3, gk3, cos, sn, sp, l)
            xl = _lat_even_attn(xl, mod, p, *caches, ewout, sink_logit, l)
        else:
            ucs, gsilu = _lat_odd_proj(xl, mod, g3, owin, ccs, l)
            xl = _lat_odd_mix(xl, mod, fs_lat, ucs, gsilu, owout, fg, l)
    return (xc, xl, *new_kv)
```

```python
import functools

import numpy as np
import jax
import jax.numpy as jnp
from jax import lax
from jax.experimental import pallas as pl
from jax.experimental.pallas import tpu as pltpu

D_MODEL = 1024
DEPTH = 4
HEAD_DIM = 64
N_HEADS = 8
N_KV = 2
GROUP = N_HEADS // N_KV
MIX_W = N_HEADS * HEAD_DIM
KV_W = N_KV * HEAD_DIM
EVEN_IN = 2 * (2 * MIX_W + 2 * KV_W)
ODD_IN = 2 * D_MODEL
GRID_W = 64
WINDOW = 128
ROPE_BASE = 10000.0
C_GROUPS = 4
C_GROUP_DIM = D_MODEL // C_GROUPS
EPS = 1e-6
NEG_BIG = -1e30
ROWS = 256
COND_ROWS = 8
VMEM_LIMIT = 48 * 1024 * 1024
CTX_VMEM_LIMIT = 56 * 1024 * 1024
STAGE_ROWS = 64
CTX_BATCH_PER_STEP = 2

QA, KA, VA, GA = 0, 512, 640, 768
QB, KB, VB, GB = 1280, 1792, 1920, 2048

F32 = jnp.float32
BF16 = jnp.bfloat16


def _dft_tables(n):
    k = np.arange(n, dtype=np.int64)
    ang = ((k[:, None] * k[None, :]) % n).astype(np.float64) * (2.0 * np.pi / n)
    return np.cos(ang).astype(np.float32), np.sin(ang).astype(np.float32)


def _channel_dft():
    c, s = _dft_tables(C_GROUP_DIM)
    return np.concatenate([c, s], axis=1)


def _position_dft(n):
    c, s = _dft_tables(n)
    return np.concatenate([c, -s], axis=1)


def _rope_tables(n_tok):
    rows = n_tok // GRID_W
    row = np.repeat(np.arange(rows), GRID_W).astype(np.float64)
    col = np.tile(np.arange(GRID_W), rows).astype(np.float64)
    half = HEAD_DIM // 2
    inv = 1.0 / (ROPE_BASE ** (np.arange(0, half, 2, dtype=np.float64) / half))
    ang_r = row[:, None] * inv
    ang_c = col[:, None] * inv
    zeros = np.zeros_like(ang_r)
    cos_h = np.concatenate([np.cos(ang_r), np.cos(ang_r), np.cos(ang_c), np.cos(ang_c)], axis=1)
    nxt_h = np.concatenate([-np.sin(ang_r), zeros, -np.sin(ang_c), zeros], axis=1)
    prv_h = np.concatenate([zeros, np.sin(ang_r), zeros, np.sin(ang_c)], axis=1)
    two = lambda t: np.concatenate([t, t], axis=1).astype(np.float32)
    return two(cos_h), two(nxt_h), two(prv_h)


def _silu(x):
    return x / (1.0 + jnp.exp(-x))


def _norm_mod(x, g, shift, scale):
    ms = jnp.mean(x * x, axis=-1, keepdims=True)
    return x * lax.rsqrt(ms + EPS) * (g * (1.0 + scale)) + shift


def _rms_head(xh, g):
    ms = jnp.mean(xh * xh, axis=-1, keepdims=True)
    return xh * lax.rsqrt(ms + EPS) * g


def _rms_pair(xc, g2):
    lo = lax.broadcasted_iota(jnp.int32, xc.shape, 1) < HEAD_DIM
    ss = xc * xc
    s_lo = jnp.sum(jnp.where(lo, ss, 0.0), axis=-1, keepdims=True)
    s_hi = jnp.sum(jnp.where(lo, 0.0, ss), axis=-1, keepdims=True)
    ms = jnp.where(lo, s_lo, s_hi) * (1.0 / HEAD_DIM)
    return xc * lax.rsqrt(ms + EPS) * g2


def _rope(xc, cos, sin_next, sin_prev):
    nxt = pltpu.roll(xc, 128 - 16, 1)
    prv = pltpu.roll(xc, 16, 1)
    return xc * cos + nxt * sin_next + prv * sin_prev


DEN_ROWS = 16
LOG2E = float(np.log2(np.e))
Q_SCALE = (HEAD_DIM ** -0.5) * LOG2E


def _sink_pair(sink_ref, i, head):
    return sink_ref[i, head] * LOG2E, sink_ref[i, head + 1] * LOG2E


def _kv_operands(k2, v2t, j):
    tk = k2.shape[0]
    low = lax.broadcasted_iota(jnp.int32, k2.shape, 1) < HEAD_DIM
    km = jnp.where(low if j == 0 else jnp.logical_not(low), k2, 0.0)
    kr = pltpu.roll(km, HEAD_DIM, 1)
    k_lo, k_hi = (km, kr) if j == 0 else (kr, km)
    k_cat = jnp.concatenate([k_lo, k_hi], axis=0).astype(BF16)
    vjt = v2t[HEAD_DIM * j:HEAD_DIM * (j + 1), :]
    zero = jnp.zeros_like(vjt)
    row = lax.broadcasted_iota(jnp.int32, (DEN_ROWS, 2 * tk), 0)
    col = lax.broadcasted_iota(jnp.int32, (DEN_ROWS, 2 * tk), 1)
    ones = jnp.where(((row == 0) & (col < tk)) | ((row == 1) & (col >= tk)), 1.0, 0.0)
    w_t = jnp.concatenate([jnp.concatenate([vjt, zero], axis=1),
                           jnp.concatenate([zero, vjt], axis=1), ones], axis=0).astype(BF16)
    return k_cat, w_t


def _scores_t(qc, k_cat):
    return lax.dot_general(k_cat, qc.astype(BF16), (((1,), (1,)), ((), ())), preferred_element_type=F32)


def _softmax_t(s_t, sinks=None, mask_t=None):
    tk = s_t.shape[0] // 2
    es, ms = [], []
    for hh in range(2):
        sh = s_t[hh * tk:(hh + 1) * tk]
        if mask_t is not None:
            sh = jnp.where(mask_t, sh, NEG_BIG)
        m = jnp.max(sh, axis=0, keepdims=True)
        if sinks is not None:
            m = jnp.maximum(m, sinks[hh])
        es.append(jnp.exp2(sh - m))
        ms.append(m)
    return jnp.concatenate(es, axis=0).astype(BF16), ms


def _values_t(e_t, w_t):
    return jnp.dot(w_t, e_t, preferred_element_type=F32)


def _normalise_t(nd, ms, sinks=None):
    dens = [nd[2 * HEAD_DIM + hh:2 * HEAD_DIM + hh + 1] for hh in range(2)]
    if sinks is not None:
        dens = [dens[hh] + jnp.exp2(sinks[hh] - ms[hh]) for hh in range(2)]
    o_t = jnp.concatenate([nd[0:HEAD_DIM] / dens[0], nd[HEAD_DIM:2 * HEAD_DIM] / dens[1]], axis=0)
    return o_t.T


def _run_skewed(items, stages):
    state = list(items)
    for t in range(len(items) + len(stages) - 1):
        for k, stage in enumerate(stages):
            if 0 <= t - k < len(items):
                state[t - k] = stage(state[t - k])


def _mod_rows(mod_ref, row):
    return mod_ref[0, pl.ds(row, 1), :], mod_ref[1, pl.ds(row, 1), :], mod_ref[2, pl.ds(row, 1), :]


def _ada_kernel(cond_ref, w_ref, b_ref, o_ref):
    a = _silu(cond_ref[...]).astype(BF16)
    o_ref[...] = jnp.dot(a, w_ref[...].astype(BF16), preferred_element_type=F32) + b_ref[...]


def _ada_all(cond, ada_w, ada_b):
    return pl.pallas_call(
        _ada_kernel,
        grid=(DEPTH, 3),
        in_specs=[
            pl.BlockSpec((COND_ROWS, D_MODEL), lambda l, p: (0, 0)),
            pl.BlockSpec((None, D_MODEL, D_MODEL), lambda l, p: (l, 0, p)),
            pl.BlockSpec((None, None, 1, D_MODEL), lambda l, p: (l, p, 0, 0)),
        ],
        out_specs=pl.BlockSpec((None, None, COND_ROWS, D_MODEL), lambda l, p: (l, p, 0, 0)),
        out_shape=jax.ShapeDtypeStruct((DEPTH, 3, COND_ROWS, D_MODEL), F32),
        compiler_params=pltpu.CompilerParams(
            dimension_semantics=("parallel", "parallel"), vmem_limit_bytes=VMEM_LIMIT),
        name="ada_mod",
    )(cond, ada_w, ada_b.reshape(DEPTH, 3, 1, D_MODEL))


def _fourier_rows(p_ref, ccs_ref, ucs_ref, rows):
    for grp in range(C_GROUPS):
        c0 = C_GROUP_DIM * grp
        ug = p_ref[:, c0:c0 + C_GROUP_DIM].astype(BF16)
        t = jnp.dot(ug, ccs_ref[...], preferred_element_type=F32)
        ucs_ref[0:rows, c0:c0 + C_GROUP_DIM] = t[:, :C_GROUP_DIM].astype(BF16)
        ucs_ref[rows:2 * rows, c0:c0 + C_GROUP_DIM] = t[:, C_GROUP_DIM:].astype(BF16)


def _ctx_even_mix(i, sink_ref, gq_ref, gk_ref, kv_refs, p_ref, o_ref):
    n_b, seq, _ = p_ref.shape
    gq2 = gq_ref[i:i + 1, :] * Q_SCALE
    gk2 = gk_ref[i:i + 1, :]
    items = []
    for mixer, (q0, k0, v0, g0) in enumerate(((QA, KA, VA, GA), (QB, KB, VB, GB))):
        kv = []
        for bb in range(n_b):
            k2 = p_ref[bb, :, k0:k0 + KV_W]
            v2 = p_ref[bb, :, v0:v0 + KV_W]
            if mixer == 0:
                k2 = _rms_pair(k2, gk2)
            v2t = v2.T
            kv_refs[2 * mixer][bb, i] = k2.T.reshape(N_KV, HEAD_DIM, seq)
            kv_refs[2 * mixer + 1][bb, i] = v2t.reshape(N_KV, HEAD_DIM, seq)
            kv.append((k2, v2t))
        for j in range(N_KV):
            ops = [_kv_operands(k2, v2t, j) for k2, v2t in kv]
            for cc in range(GROUP // 2):
                c0 = 128 * (j * (GROUP // 2) + cc)
                sinks = None if mixer == 0 else _sink_pair(sink_ref, i, c0 // HEAD_DIM)
                for bb, (k_cat, w_t) in enumerate(ops):
                    items.append(dict(rows=bb, q=q0 + c0, g=g0 + c0, o=mixer * MIX_W + c0,
                                      norm=mixer == 0, sinks=sinks, k_cat=k_cat, w_t=w_t))

    def scores(it):
        qc = p_ref[it["rows"], :, it["q"]:it["q"] + 128]
        qc = _rms_pair(qc, gq2) if it["norm"] else qc * Q_SCALE
        return dict(it, s_t=_scores_t(qc, it["k_cat"]))

    def softmax(it):
        e_t, ms = _softmax_t(it["s_t"], it["sinks"])
        return dict(it, e_t=e_t, ms=ms, s_t=None)

    def values(it):
        return dict(it, nd=_values_t(it["e_t"], it["w_t"]), e_t=None)

    def finish(it):
        o = _normalise_t(it["nd"], it["ms"], it["sinks"])
        gt = p_ref[it["rows"], :, it["g"]:it["g"] + 128]
        o_ref[it["rows"], :, it["o"]:it["o"] + 128] = (o * _silu(gt)).astype(BF16)
        return None

    _run_skewed(items, (scores, softmax, values, finish))


def _ctx_odd_mix(ccs_ref, fs_ref, p_ref, o_ref, ucs_ref):
    n_b, seq, _ = p_ref.shape
    for bb in range(n_b):
        for grp in range(C_GROUPS):
            c0 = C_GROUP_DIM * grp
            t = jnp.dot(p_ref[bb, :, c0:c0 + C_GROUP_DIM].astype(BF16), ccs_ref[...],
                        preferred_element_type=F32)
            ucs_ref[bb, 0:seq, c0:c0 + C_GROUP_DIM] = t[:, :C_GROUP_DIM].astype(BF16)
            ucs_ref[bb, seq:2 * seq, c0:c0 + C_GROUP_DIM] = t[:, C_GROUP_DIM:].astype(BF16)
    scale = float(1.0 / np.sqrt(float(seq * C_GROUP_DIM)))
    for bb in range(n_b):
        r = jnp.dot(fs_ref[...], ucs_ref[bb], preferred_element_type=F32) * scale
        o_ref[bb] = (r * _silu(p_ref[bb, :, D_MODEL:2 * D_MODEL])).astype(BF16)


def _stage_copy(w_hbm, stage_ref, sem, layer, chunk, slot):
    rows = stage_ref.shape[1]
    return pltpu.make_async_copy(w_hbm.at[layer, pl.ds(chunk * rows, rows), :], stage_ref.at[slot], sem.at[slot])


def _convert_weights(w_hbm, w_bf16_hbm, w_ref, stage_ref, in_sem, out_sem, sem0):
    n_layers, n_rows, _ = w_hbm.shape
    rows = stage_ref.shape[1]
    n_chunks = n_rows // rows
    exports = []
    for layer in range(n_layers):
        _stage_copy(w_hbm, stage_ref, in_sem, layer, 0, 0).start()

        def body(c, carry, layer=layer):
            slot = c % 2

            @pl.when(c + 1 < n_chunks)
            def _():
                _stage_copy(w_hbm, stage_ref, in_sem, layer, c + 1, 1 - slot).start()

            _stage_copy(w_hbm, stage_ref, in_sem, layer, c, slot).wait()
            r0 = pl.multiple_of(c * rows, rows)
            w_ref[layer, pl.ds(r0, rows), :] = stage_ref[slot].astype(BF16)
            return carry

        lax.fori_loop(0, n_chunks, body, 0)
        export = pltpu.make_async_copy(w_ref.at[layer], w_bf16_hbm.at[layer], out_sem.at[sem0 + layer])
        export.start()
        exports.append(export)
    return exports


def _ctx_kernel(sink_ref, x_ref, mod_ref, g_ref, ewin_hbm, ewout_hbm, owin_hbm, owout_hbm, gq_ref, gk_ref,
                ccs_ref, fs_ref, fg_ref,
                xo_ref, ka_ref, va_ref, kb_ref, vb_ref, ewin_o, ewout_o, owin_o, owout_o,
                xs_ref, p_ref, o_ref, ucs_ref, ewin_ref, ewout_ref, owin_ref, owout_ref,
                stage_e_ref, stage_o_ref, stage_d_ref, in_sem, out_sem):
    @pl.when(pl.program_id(0) == 0)
    def _():
        exports = []
        for k, (src, dst, res, stage) in enumerate((
                (ewin_hbm, ewin_o, ewin_ref, stage_e_ref), (ewout_hbm, ewout_o, ewout_ref, stage_d_ref),
                (owin_hbm, owin_o, owin_ref, stage_o_ref), (owout_hbm, owout_o, owout_ref, stage_d_ref))):
            exports += _convert_weights(src, dst, res, stage, in_sem, out_sem, k * src.shape[0])
        for export in exports:
            export.wait()

    n_b = x_ref.shape[0]
    for l in range(DEPTH):
        i = l // 2
        even = l % 2 == 0
        win_ref, wout_ref, width = (ewin_ref, ewout_ref, EVEN_IN) if even else (owin_ref, owout_ref, ODD_IN)
        shift, scale, gate = (mod_ref[l, part, 0:1, :] for part in range(3))
        for bb in range(n_b):
            x = x_ref[bb] if l == 0 else xs_ref[bb]
            h = _norm_mod(x, g_ref[l:l + 1, :], shift, scale)
            p_ref[bb, :, 0:width] = jnp.dot(h.astype(BF16), win_ref[i], preferred_element_type=F32)
        if even:
            _ctx_even_mix(i, sink_ref, gq_ref, gk_ref, (ka_ref, va_ref, kb_ref, vb_ref), p_ref, o_ref)
        else:
            _ctx_odd_mix(ccs_ref, fs_ref, p_ref, o_ref, ucs_ref)
        for bb in range(n_b):
            y = jnp.dot(o_ref[bb], wout_ref[i], preferred_element_type=F32)
            xn = (x_ref[bb] if l == 0 else xs_ref[bb]) + gate * y
            if l == DEPTH - 1:
                xo_ref[bb] = _rms_head(xn, fg_ref[...])
            else:
                xs_ref[bb] = xn


def _resident(shape):
    return pl.BlockSpec(shape, lambda i: (0,) * len(shape), pipeline_mode=pl.Buffered(1))


def _ctx_path(x, mod, norm_g, ewin, ewout, owin, owout, gq2, gk2, sink, ccs, fs, fg):
    b, s, _ = x.shape
    n_even = ewin.shape[0]
    bb = CTX_BATCH_PER_STEP
    kv_shape = jax.ShapeDtypeStruct((b, n_even, N_KV, HEAD_DIM, s), F32)
    kv_spec = pl.BlockSpec((bb, n_even, N_KV, HEAD_DIM, s), lambda i: (i, 0, 0, 0, 0))
    x_spec = pl.BlockSpec((bb, s, D_MODEL), lambda i: (i, 0, 0))
    hbm = pl.BlockSpec(memory_space=pl.ANY)
    weights = (ewin, ewout, owin, owout)
    return pl.pallas_call(
        _ctx_kernel,
        grid=(b // bb,),
        in_specs=[
            pl.BlockSpec(memory_space=pltpu.SMEM),
            x_spec,
            _resident(mod.shape), _resident(norm_g.shape),
            hbm, hbm, hbm, hbm,
            _resident(gq2.shape), _resident(gk2.shape),
            _resident(ccs.shape), _resident(fs.shape), _resident(fg.shape),
        ],
        out_specs=[x_spec, kv_spec, kv_spec, kv_spec, kv_spec, hbm, hbm, hbm, hbm],
        out_shape=[jax.ShapeDtypeStruct(x.shape, F32), kv_shape, kv_shape, kv_shape, kv_shape]
                  + [jax.ShapeDtypeStruct(w.shape, BF16) for w in weights],
        scratch_shapes=[pltpu.VMEM((bb, s, D_MODEL), F32), pltpu.VMEM((bb, s, EVEN_IN), F32),
                        pltpu.VMEM((bb, s, D_MODEL), BF16), pltpu.VMEM((bb, 2 * s, D_MODEL), BF16)]
                       + [pltpu.VMEM(w.shape, BF16) for w in weights]
                       + [pltpu.VMEM((2, STAGE_ROWS, width), F32) for width in (EVEN_IN, ODD_IN, D_MODEL)]
                       + [pltpu.SemaphoreType.DMA((2,)), pltpu.SemaphoreType.DMA((len(weights) * n_even,))],
        compiler_params=pltpu.CompilerParams(
            dimension_semantics=("arbitrary",), vmem_limit_bytes=CTX_VMEM_LIMIT),
        name="ctx_path",
    )(sink, x, mod, norm_g, *weights, gq2, gk2, ccs, fs, fg)


def _lat_even_proj_kernel(x_ref, mod_ref, g_ref, win_ref, gq_ref, gk_ref, cos_ref, sn_ref, sp_ref,
                          p_ref):
    shift, scale, _ = _mod_rows(mod_ref, 1 + pl.program_id(0))
    h = _norm_mod(x_ref[0], g_ref[...], shift, scale)
    p_ref[0] = jnp.dot(h.astype(BF16), win_ref[...], preferred_element_type=F32)
    cos, sn, sp = cos_ref[...], sn_ref[...], sp_ref[...]
    gq2 = gq_ref[...] * Q_SCALE
    gk2 = gk_ref[...]

    def chunk(c0):
        return p_ref[0, :, c0:c0 + 128]

    for c0 in range(QA, QA + MIX_W, 128):
        p_ref[0, :, c0:c0 + 128] = _rope(_rms_pair(chunk(c0), gq2), cos, sn, sp)
    p_ref[0, :, KA:KA + 128] = _rope(_rms_pair(chunk(KA), gk2), cos, sn, sp)
    for c0 in range(QB, QB + MIX_W, 128):
        p_ref[0, :, c0:c0 + 128] = _rope(chunk(c0) * Q_SCALE, cos, sn, sp)
    p_ref[0, :, KB:KB + 128] = _rope(chunk(KB), cos, sn, sp)
    for g0 in (GA, GB):
        for c0 in range(g0, g0 + MIX_W, 128):
            p_ref[0, :, c0:c0 + 128] = _silu(chunk(c0))


def _layer_spec(tail, idx):
    return pl.BlockSpec((None,) + tuple(tail), lambda i, n: (idx,) + (0,) * len(tail))


def _lat_even_proj(x, mod, g3, ewin, gq3, gk3, cos, sn, sp, l):
    b, s, _ = x.shape
    rope_spec = pl.BlockSpec((ROWS, 128), lambda i, n: (n, 0))
    return pl.pallas_call(
        _lat_even_proj_kernel,
        grid=(b, s // ROWS),
        in_specs=[
            pl.BlockSpec((1, ROWS, D_MODEL), lambda i, n: (i, n, 0)),
            _layer_spec((3, COND_ROWS, D_MODEL), l),
            _layer_spec((1, D_MODEL), l),
            _layer_spec((D_MODEL, EVEN_IN), l // 2),
            _layer_spec((1, 128), l // 2),
            _layer_spec((1, 128), l // 2),
            rope_spec, rope_spec, rope_spec,
        ],
        out_specs=pl.BlockSpec((1, ROWS, EVEN_IN), lambda i, n: (i, n, 0)),
        out_shape=jax.ShapeDtypeStruct((b, s, EVEN_IN), F32),
        compiler_params=pltpu.CompilerParams(
            dimension_semantics=("parallel", "parallel"), vmem_limit_bytes=VMEM_LIMIT),
        name="lat_even_proj",
    )(x, mod, g3, ewin, gq3, gk3, cos, sn, sp)


def _lat_even_attn_kernel(sink_ref, x_ref, mod_ref, pq_ref, ka_ref, va_ref, kb_ref, vb_ref,
                          cka_ref, cva_ref, ckb_ref, cvb_ref, wout_ref, xo_ref, o_ref, *, layer_i):
    n = pl.program_id(1)
    seq = ka_ref.shape[1]
    _, _, gate = _mod_rows(mod_ref, 1 + pl.program_id(0))

    prev0 = pl.multiple_of(jnp.maximum(n * ROWS - WINDOW, 0), WINDOW)
    own0 = pl.multiple_of(n * ROWS, ROWS)
    next0 = pl.multiple_of(jnp.minimum(n * ROWS + ROWS, seq - WINDOW), WINDOW)
    win_len = ROWS + 2 * WINDOW
    ctx_len = cka_ref.shape[0]
    cj = lax.broadcasted_iota(jnp.int32, (win_len + ctx_len, ROWS), 0)
    qi = lax.broadcasted_iota(jnp.int32, (win_len + ctx_len, ROWS), 1)
    kpos = n * ROWS - WINDOW + cj
    in_win = (jnp.abs(cj - WINDOW - qi) <= WINDOW) & (kpos >= 0) & (kpos < seq)
    mask_b = in_win | (cj >= win_len)

    def window(ref, cache_ref):
        return jnp.concatenate([ref[0, pl.ds(prev0, WINDOW), :], ref[0, pl.ds(own0, ROWS), :],
                                ref[0, pl.ds(next0, WINDOW), :], cache_ref[...]], axis=0)

    k2a = jnp.concatenate([ka_ref[0], cka_ref[...]], axis=0)
    v2a = jnp.concatenate([va_ref[0], cva_ref[...]], axis=0)
    k2b, v2b = window(kb_ref, ckb_ref), window(vb_ref, cvb_ref)
    items = []
    for mixer, (q0, g0, k2, v2) in enumerate(((QA, GA, k2a, v2a), (QB, GB, k2b, v2b))):
        v2t = v2.T
        for j in range(N_KV):
            k_cat, w_t = _kv_operands(k2, v2t, j)
            for cc in range(GROUP // 2):
                c0 = 128 * (j * (GROUP // 2) + cc)
                if mixer == 0:
                    sinks, mask = None, None
                else:
                    head = c0 // HEAD_DIM
                    sinks, mask = _sink_pair(sink_ref, layer_i, head), mask_b
                items.append(dict(q=q0 + c0, g=g0 + c0, o=mixer * MIX_W + c0, sinks=sinks, mask=mask,
                                  k_cat=k_cat, w_t=w_t))

    def scores(it):
        return dict(it, s_t=_scores_t(pq_ref[0, :, it["q"]:it["q"] + 128], it["k_cat"]))

    def softmax(it):
        e_t, ms = _softmax_t(it["s_t"], it["sinks"], it["mask"])
        return dict(it, e_t=e_t, ms=ms, s_t=None)

    def values(it):
        return dict(it, nd=_values_t(it["e_t"], it["w_t"]), e_t=None)

    def finish(it):
        o = _normalise_t(it["nd"], it["ms"], it["sinks"])
        gt = pq_ref[0, :, it["g"]:it["g"] + 128]
        o_ref[:, it["o"]:it["o"] + 128] = (o * gt).astype(BF16)
        return None

    _run_skewed(items, (scores, softmax, values, finish))
    y = jnp.dot(o_ref[...], wout_ref[...], preferred_element_type=F32)
    xo_ref[0] = x_ref[0] + gate * y


def _lat_even_attn(x, mod, p, cka, cva, ckb, cvb, ewout, sink, l):
    b, s, _ = x.shape
    past = cka.shape[2]
    layer_i = l // 2
    kv_spec = lambda blk: pl.BlockSpec((1, s, KV_W), lambda i, n, blk=blk: (i, 0, blk))
    cache_spec = pl.BlockSpec((None, None, past, KV_W), lambda i, n: (i, layer_i, 0, 0))
    return pl.pallas_call(
        functools.partial(_lat_even_attn_kernel, layer_i=layer_i),
        grid=(b, s // ROWS),
        in_specs=[
            pl.BlockSpec(memory_space=pltpu.SMEM),
            pl.BlockSpec((1, ROWS, D_MODEL), lambda i, n: (i, n, 0)),
            _layer_spec((3, COND_ROWS, D_MODEL), l),
            pl.BlockSpec((1, ROWS, EVEN_IN), lambda i, n: (i, n, 0)),
            kv_spec(KA // KV_W), kv_spec(VA // KV_W), kv_spec(KB // KV_W), kv_spec(VB // KV_W),
            cache_spec, cache_spec, cache_spec, cache_spec,
            _layer_spec((D_MODEL, D_MODEL), layer_i),
        ],
        out_specs=pl.BlockSpec((1, ROWS, D_MODEL), lambda i, n: (i, n, 0)),
        out_shape=jax.ShapeDtypeStruct(x.shape, F32),
        scratch_shapes=[pltpu.VMEM((ROWS, D_MODEL), BF16)],
        compiler_params=pltpu.CompilerParams(
            dimension_semantics=("parallel", "parallel"), vmem_limit_bytes=VMEM_LIMIT),
        name="lat_even_attn",
    )(sink, x, mod, p, p, p, p, p, cka, cva, ckb, cvb, ewout)


def _lat_odd_proj_kernel(x_ref, mod_ref, g_ref, win_ref, ccs_ref, ucs_ref, gate_ref, p_ref, t_ref):
    shift, scale, _ = _mod_rows(mod_ref, 1 + pl.program_id(0))
    h = _norm_mod(x_ref[0], g_ref[...], shift, scale)
    p_ref[...] = jnp.dot(h.astype(BF16), win_ref[...], preferred_element_type=F32)
    _fourier_rows(p_ref, ccs_ref, t_ref, ROWS)
    ucs_ref[0, 0] = t_ref[0:ROWS, :]
    ucs_ref[0, 1] = t_ref[ROWS:2 * ROWS, :]
    gate_ref[0] = _silu(p_ref[:, D_MODEL:2 * D_MODEL])


def _lat_odd_proj(x, mod, g3, owin, ccs, l):
    b, s, _ = x.shape
    return pl.pallas_call(
        _lat_odd_proj_kernel,
        grid=(b, s // ROWS),
        in_specs=[
            pl.BlockSpec((1, ROWS, D_MODEL), lambda i, n: (i, n, 0)),
            _layer_spec((3, COND_ROWS, D_MODEL), l),
            _layer_spec((1, D_MODEL), l),
            _layer_spec((D_MODEL, ODD_IN), l // 2),
            pl.BlockSpec((C_GROUP_DIM, 2 * C_GROUP_DIM), lambda i, n: (0, 0)),
        ],
        out_specs=[
            pl.BlockSpec((1, 2, ROWS, D_MODEL), lambda i, n: (i, 0, n, 0)),
            pl.BlockSpec((1, ROWS, D_MODEL), lambda i, n: (i, n, 0)),
        ],
        out_shape=[jax.ShapeDtypeStruct((b, 2, s, D_MODEL), BF16),
                   jax.ShapeDtypeStruct((b, s, D_MODEL), F32)],
        scratch_shapes=[pltpu.VMEM((ROWS, ODD_IN), F32), pltpu.VMEM((2 * ROWS, D_MODEL), BF16)],
        compiler_params=pltpu.CompilerParams(
            dimension_semantics=("parallel", "parallel"), vmem_limit_bytes=VMEM_LIMIT),
        name="lat_odd_proj",
    )(x, mod, g3, owin, ccs)


def _lat_odd_mix_kernel(x_ref, mod_ref, fs_ref, ucs_ref, gate_ref, wout_ref, fg_ref, xo_ref, *, final):
    seq = ucs_ref.shape[1] // 2
    _, _, gate = _mod_rows(mod_ref, 1 + pl.program_id(0))
    r = jnp.dot(fs_ref[...], ucs_ref[0], preferred_element_type=F32)
    r = r * float(1.0 / np.sqrt(float(seq * C_GROUP_DIM)))
    mix = r * gate_ref[0]
    y = jnp.dot(mix.astype(BF16), wout_ref[...], preferred_element_type=F32)
    xn = x_ref[0] + gate * y
    if final:
        xn = _rms_head(xn, fg_ref[...])
    xo_ref[0] = xn


def _lat_odd_mix(x, mod, fs, ucs, gsilu, owout, fg, l):
    b, s, _ = x.shape
    return pl.pallas_call(
        functools.partial(_lat_odd_mix_kernel, final=l == DEPTH - 1),
        grid=(b, s // ROWS),
        in_specs=[
            pl.BlockSpec((1, ROWS, D_MODEL), lambda i, n: (i, n, 0)),
            _layer_spec((3, COND_ROWS, D_MODEL), l),
            pl.BlockSpec((ROWS, 2 * s), lambda i, n: (n, 0)),
            pl.BlockSpec((1, 2 * s, D_MODEL), lambda i, n: (i, 0, 0)),
            pl.BlockSpec((1, ROWS, D_MODEL), lambda i, n: (i, n, 0)),
            _layer_spec((D_MODEL, D_MODEL), l // 2),
            pl.BlockSpec((1, D_MODEL), lambda i, n: (0, 0)),
        ],
        out_specs=pl.BlockSpec((1, ROWS, D_MODEL), lambda i, n: (i, n, 0)),
        out_shape=jax.ShapeDtypeStruct(x.shape, F32),
        compiler_params=pltpu.CompilerParams(
            dimension_semantics=("parallel", "parallel"), vmem_limit_bytes=VMEM_LIMIT),
        name="lat_odd_mix",
    )(x, mod, fs, ucs.reshape(b, 2 * s, D_MODEL), gsilu, owout, fg)


def kernel(x_prompt, x_sample, cache_k_a, cache_v_a, cache_k_b, cache_v_b, c, c_ctx, norm_g, ada_w, ada_b,
           even_w_in, even_w_out, qk_g_q, qk_g_k, sink_logit, odd_w_in, odd_w_out, final_g):
    batch, seq, _ = x_prompt.shape
    dec_batch, dec_seq, _ = x_sample.shape
    n_even = even_w_in.shape[0]
    past = cache_k_a.shape[2]

    cond = jnp.concatenate(
        [c_ctx[None, :], c, jnp.zeros((COND_ROWS - 1 - dec_batch, D_MODEL), F32)], axis=0)
    mod = _ada_all(cond, ada_w, ada_b)

    ccs = jnp.asarray(_channel_dft()).astype(BF16)
    fs_ctx = jnp.asarray(_position_dft(seq)).astype(BF16)
    fs_lat = jnp.asarray(_position_dft(dec_seq)).astype(BF16)
    cos, sn, sp = (jnp.asarray(t) for t in _rope_tables(dec_seq))

    caches = [a.reshape(dec_batch, n_even, past, KV_W) for a in (cache_k_a, cache_v_a, cache_k_b, cache_v_b)]
    fg = final_g.reshape(1, D_MODEL)
    gq2 = jnp.tile(qk_g_q, (1, 2))
    gk2 = jnp.tile(qk_g_k, (1, 2))

    xc, *outs = _ctx_path(x_prompt, mod, norm_g, even_w_in, even_w_out, odd_w_in, odd_w_out, gq2, gk2,
                          sink_logit, ccs, fs_ctx, fg)
    new_kv = [jnp.transpose(a, (0, 1, 4, 2, 3)) for a in outs[:4]]
    ewin, ewout, owin, owout = outs[4:]

    xl = x_sample
    g3 = norm_g.reshape(DEPTH, 1, D_MODEL)
    gq3, gk3 = gq2.reshape(n_even, 1, 128), gk2.reshape(n_even, 1, 128)
    for l in range(DEPTH):
        if l % 2 == 0:
            p = _lat_even_proj(xl, mod, g3, ewin, gq3, gk3, cos, sn, sp, l)
            xl = _lat_even_attn(xl, mod, p, *caches, ewout, sink_logit, l)
        else:
            ucs, gsilu = _lat_odd_proj(xl, mod, g3, owin, ccs, l)
            xl = _lat_odd_mix(xl, mod, fs_lat, ucs, gsilu, owout, fg, l)
    return (xc, xl, *new_kv)
```

```python
import functools

import numpy as np
import jax
import jax.numpy as jnp
from jax import lax
from jax.experimental import pallas as pl
from jax.experimental.pallas import tpu as pltpu

D_MODEL = 1024
DEPTH = 4
HEAD_DIM = 64
N_HEADS = 8
N_KV = 2
GROUP = N_HEADS // N_KV
MIX_W = N_HEADS * HEAD_DIM
KV_W = N_KV * HEAD_DIM
EVEN_IN = 2 * (2 * MIX_W + 2 * KV_W)
ODD_IN = 2 * D_MODEL
GRID_W = 64
WINDOW = 128
ROPE_BASE = 10000.0
C_GROUPS = 4
C_GROUP_DIM = D_MODEL // C_GROUPS
EPS = 1e-6
NEG_BIG = -1e30
ROWS = 256
COND_ROWS = 8
VMEM_LIMIT = 48 * 1024 * 1024
CTX_VMEM_LIMIT = 56 * 1024 * 1024
STAGE_ROWS = 128
STAGE_SLOTS = 4
CTX_BATCH_PER_STEP = 2

QA, KA, VA, GA = 0, 512, 640, 768
QB, KB, VB, GB = 1280, 1792, 1920, 2048

F32 = jnp.float32
BF16 = jnp.bfloat16


def _dft_tables(n):
    k = np.arange(n, dtype=np.int64)
    ang = ((k[:, None] * k[None, :]) % n).astype(np.float64) * (2.0 * np.pi / n)
    return np.cos(ang).astype(np.float32), np.sin(ang).astype(np.float32)


def _channel_dft():
    c, s = _dft_tables(C_GROUP_DIM)
    return np.concatenate([c, s], axis=1)


def _position_dft(n):
    c, s = _dft_tables(n)
    return np.concatenate([c, -s], axis=1)


def _rope_tables(n_tok):
    rows = n_tok // GRID_W
    row = np.repeat(np.arange(rows), GRID_W).astype(np.float64)
    col = np.tile(np.arange(GRID_W), rows).astype(np.float64)
    half = HEAD_DIM // 2
    inv = 1.0 / (ROPE_BASE ** (np.arange(0, half, 2, dtype=np.float64) / half))
    ang_r = row[:, None] * inv
    ang_c = col[:, None] * inv
    zeros = np.zeros_like(ang_r)
    cos_h = np.concatenate([np.cos(ang_r), np.cos(ang_r), np.cos(ang_c), np.cos(ang_c)], axis=1)
    nxt_h = np.concatenate([-np.sin(ang_r), zeros, -np.sin(ang_c), zeros], axis=1)
    prv_h = np.concatenate([zeros, np.sin(ang_r), zeros, np.sin(ang_c)], axis=1)
    two = lambda t: np.concatenate([t, t], axis=1).astype(np.float32)
    return two(cos_h), two(nxt_h), two(prv_h)


def _silu(x):
    return x / (1.0 + jnp.exp(-x))


def _norm_mod(x, g, shift, scale):
    ms = jnp.mean(x * x, axis=-1, keepdims=True)
    return x * lax.rsqrt(ms + EPS) * (g * (1.0 + scale)) + shift


def _rms_head(xh, g):
    ms = jnp.mean(xh * xh, axis=-1, keepdims=True)
    return xh * lax.rsqrt(ms + EPS) * g


def _rms_pair(xc, g2):
    lo = lax.broadcasted_iota(jnp.int32, xc.shape, 1) < HEAD_DIM
    ss = xc * xc
    s_lo = jnp.sum(jnp.where(lo, ss, 0.0), axis=-1, keepdims=True)
    s_hi = jnp.sum(jnp.where(lo, 0.0, ss), axis=-1, keepdims=True)
    ms = jnp.where(lo, s_lo, s_hi) * (1.0 / HEAD_DIM)
    return xc * lax.rsqrt(ms + EPS) * g2


def _rope(xc, cos, sin_next, sin_prev):
    nxt = pltpu.roll(xc, 128 - 16, 1)
    prv = pltpu.roll(xc, 16, 1)
    return xc * cos + nxt * sin_next + prv * sin_prev


DEN_ROWS = 16
LOG2E = float(np.log2(np.e))
Q_SCALE = (HEAD_DIM ** -0.5) * LOG2E


def _sink_pair(sink_ref, i, head):
    return sink_ref[i, head] * LOG2E, sink_ref[i, head + 1] * LOG2E


def _kv_operands(k2, v2t, j):
    tk = k2.shape[0]
    low = lax.broadcasted_iota(jnp.int32, k2.shape, 1) < HEAD_DIM
    km = jnp.where(low if j == 0 else jnp.logical_not(low), k2, 0.0)
    kr = pltpu.roll(km, HEAD_DIM, 1)
    k_lo, k_hi = (km, kr) if j == 0 else (kr, km)
    k_cat = jnp.concatenate([k_lo, k_hi], axis=0).astype(BF16)
    vjt = v2t[HEAD_DIM * j:HEAD_DIM * (j + 1), :]
    zero = jnp.zeros_like(vjt)
    row = lax.broadcasted_iota(jnp.int32, (DEN_ROWS, 2 * tk), 0)
    col = lax.broadcasted_iota(jnp.int32, (DEN_ROWS, 2 * tk), 1)
    ones = jnp.where(((row == 0) & (col < tk)) | ((row == 1) & (col >= tk)), 1.0, 0.0)
    w_t = jnp.concatenate([jnp.concatenate([vjt, zero], axis=1),
                           jnp.concatenate([zero, vjt], axis=1), ones], axis=0).astype(BF16)
    return k_cat, w_t


def _scores_t(qc, k_cat):
    return lax.dot_general(k_cat, qc.astype(BF16), (((1,), (1,)), ((), ())), preferred_element_type=F32)


def _softmax_t(s_t, sinks=None, mask_t=None):
    tk = s_t.shape[0] // 2
    es, ms = [], []
    for hh in range(2):
        sh = s_t[hh * tk:(hh + 1) * tk]
        if mask_t is not None:
            sh = jnp.where(mask_t, sh, NEG_BIG)
        m = jnp.max(sh, axis=0, keepdims=True)
        if sinks is not None:
            m = jnp.maximum(m, sinks[hh])
        es.append(jnp.exp2(sh - m))
        ms.append(m)
    return jnp.concatenate(es, axis=0).astype(BF16), ms


def _values_t(e_t, w_t):
    return jnp.dot(w_t, e_t, preferred_element_type=F32)


def _normalise_t(nd, ms, sinks=None):
    dens = [nd[2 * HEAD_DIM + hh:2 * HEAD_DIM + hh + 1] for hh in range(2)]
    if sinks is not None:
        dens = [dens[hh] + jnp.exp2(sinks[hh] - ms[hh]) for hh in range(2)]
    o_t = jnp.concatenate([nd[0:HEAD_DIM] / dens[0], nd[HEAD_DIM:2 * HEAD_DIM] / dens[1]], axis=0)
    return o_t.T


def _run_skewed(items, stages):
    state = list(items)
    for t in range(len(items) + len(stages) - 1):
        for k, stage in enumerate(stages):
            if 0 <= t - k < len(items):
                state[t - k] = stage(state[t - k])


def _mod_rows(mod_ref, row):
    return mod_ref[0, pl.ds(row, 1), :], mod_ref[1, pl.ds(row, 1), :], mod_ref[2, pl.ds(row, 1), :]


def _ada_kernel(cond_ref, w_ref, b_ref, o_ref):
    a = _silu(cond_ref[...]).astype(BF16)
    o_ref[...] = jnp.dot(a, w_ref[...].astype(BF16), preferred_element_type=F32) + b_ref[...]


def _ada_all(cond, ada_w, ada_b):
    return pl.pallas_call(
        _ada_kernel,
        grid=(DEPTH, 3),
        in_specs=[
            pl.BlockSpec((COND_ROWS, D_MODEL), lambda l, p: (0, 0)),
            pl.BlockSpec((None, D_MODEL, D_MODEL), lambda l, p: (l, 0, p)),
            pl.BlockSpec((None, None, 1, D_MODEL), lambda l, p: (l, p, 0, 0)),
        ],
        out_specs=pl.BlockSpec((None, None, COND_ROWS, D_MODEL), lambda l, p: (l, p, 0, 0)),
        out_shape=jax.ShapeDtypeStruct((DEPTH, 3, COND_ROWS, D_MODEL), F32),
        compiler_params=pltpu.CompilerParams(
            dimension_semantics=("parallel", "parallel"), vmem_limit_bytes=VMEM_LIMIT),
        name="ada_mod",
    )(cond, ada_w, ada_b.reshape(DEPTH, 3, 1, D_MODEL))


def _fourier_rows(p_ref, ccs_ref, ucs_ref, rows):
    for grp in range(C_GROUPS):
        c0 = C_GROUP_DIM * grp
        ug = p_ref[:, c0:c0 + C_GROUP_DIM].astype(BF16)
        t = jnp.dot(ug, ccs_ref[...], preferred_element_type=F32)
        ucs_ref[0:rows, c0:c0 + C_GROUP_DIM] = t[:, :C_GROUP_DIM].astype(BF16)
        ucs_ref[rows:2 * rows, c0:c0 + C_GROUP_DIM] = t[:, C_GROUP_DIM:].astype(BF16)


def _ctx_even_mix(i, sink_ref, gq_ref, gk_ref, kv_refs, p_ref, o_ref):
    n_b, seq, _ = p_ref.shape
    gq2 = gq_ref[i:i + 1, :] * Q_SCALE
    gk2 = gk_ref[i:i + 1, :]
    items = []
    for mixer, (q0, k0, v0, g0) in enumerate(((QA, KA, VA, GA), (QB, KB, VB, GB))):
        kv = []
        for bb in range(n_b):
            k2 = p_ref[bb, :, k0:k0 + KV_W]
            v2 = p_ref[bb, :, v0:v0 + KV_W]
            if mixer == 0:
                k2 = _rms_pair(k2, gk2)
            v2t = v2.T
            kv_refs[2 * mixer][bb, i] = k2.T.reshape(N_KV, HEAD_DIM, seq)
            kv_refs[2 * mixer + 1][bb, i] = v2t.reshape(N_KV, HEAD_DIM, seq)
            kv.append((k2, v2t))
        for j in range(N_KV):
            ops = [_kv_operands(k2, v2t, j) for k2, v2t in kv]
            for cc in range(GROUP // 2):
                c0 = 128 * (j * (GROUP // 2) + cc)
                sinks = None if mixer == 0 else _sink_pair(sink_ref, i, c0 // HEAD_DIM)
                for bb, (k_cat, w_t) in enumerate(ops):
                    items.append(dict(rows=bb, q=q0 + c0, g=g0 + c0, o=mixer * MIX_W + c0,
                                      norm=mixer == 0, sinks=sinks, k_cat=k_cat, w_t=w_t))

    def scores(it):
        qc = p_ref[it["rows"], :, it["q"]:it["q"] + 128]
        qc = _rms_pair(qc, gq2) if it["norm"] else qc * Q_SCALE
        return dict(it, s_t=_scores_t(qc, it["k_cat"]))

    def softmax(it):
        e_t, ms = _softmax_t(it["s_t"], it["sinks"])
        return dict(it, e_t=e_t, ms=ms, s_t=None)

    def values(it):
        return dict(it, nd=_values_t(it["e_t"], it["w_t"]), e_t=None)

    def finish(it):
        o = _normalise_t(it["nd"], it["ms"], it["sinks"])
        gt = p_ref[it["rows"], :, it["g"]:it["g"] + 128]
        o_ref[it["rows"], :, it["o"]:it["o"] + 128] = (o * _silu(gt)).astype(BF16)
        return None

    _run_skewed(items, (scores, softmax, values, finish))


def _ctx_odd_mix(ccs_ref, fs_ref, p_ref, o_ref, ucs_ref):
    n_b, seq, _ = p_ref.shape
    for bb in range(n_b):
        for grp in range(C_GROUPS):
            c0 = C_GROUP_DIM * grp
            t = jnp.dot(p_ref[bb, :, c0:c0 + C_GROUP_DIM].astype(BF16), ccs_ref[...],
                        preferred_element_type=F32)
            ucs_ref[bb, 0:seq, c0:c0 + C_GROUP_DIM] = t[:, :C_GROUP_DIM].astype(BF16)
            ucs_ref[bb, seq:2 * seq, c0:c0 + C_GROUP_DIM] = t[:, C_GROUP_DIM:].astype(BF16)
    scale = float(1.0 / np.sqrt(float(seq * C_GROUP_DIM)))
    for bb in range(n_b):
        r = jnp.dot(fs_ref[...], ucs_ref[bb], preferred_element_type=F32) * scale
        o_ref[bb] = (r * _silu(p_ref[bb, :, D_MODEL:2 * D_MODEL])).astype(BF16)


def _stage_copy(w_hbm, stage_ref, sem, chunk, slot):
    _, n_rows, cols = w_hbm.shape
    rows = stage_ref.shape[1]
    per_layer = n_rows // rows
    src = w_hbm.at[chunk // per_layer, pl.ds((chunk % per_layer) * rows, rows), :]
    return pltpu.make_async_copy(src, stage_ref.at[slot, :, pl.ds(0, cols)], sem.at[slot])


def _convert_weights(w_hbm, w_bf16_hbm, w_ref, stage_ref, in_sem, out_sem):
    n_layers, n_rows, cols = w_hbm.shape
    n_slots, rows, _ = stage_ref.shape
    per_layer = n_rows // rows
    n_chunks = n_layers * per_layer
    for c in range(n_slots - 1):
        _stage_copy(w_hbm, stage_ref, in_sem, c, c).start()

    def body(c, carry):
        slot = c % n_slots
        ahead = c + n_slots - 1

        @pl.when(ahead < n_chunks)
        def _():
            _stage_copy(w_hbm, stage_ref, in_sem, ahead, ahead % n_slots).start()

        _stage_copy(w_hbm, stage_ref, in_sem, c, slot).wait()
        r0 = pl.multiple_of((c % per_layer) * rows, rows)
        w_ref[c // per_layer, pl.ds(r0, rows), :] = stage_ref[slot, :, 0:cols].astype(BF16)
        return carry

    lax.fori_loop(0, n_chunks, body, 0)
    export = pltpu.make_async_copy(w_ref, w_bf16_hbm, out_sem)
    export.start()
    return export


def _ctx_kernel(sink_ref, x_ref, mod_ref, g_ref, ewin_hbm, ewout_hbm, owin_hbm, owout_hbm, gq_ref, gk_ref,
                ccs_ref, fs_ref, fg_ref,
                xo_ref, ka_ref, va_ref, kb_ref, vb_ref, ewin_o, ewout_o, owin_o, owout_o,
                xs_ref, p_ref, o_ref, ucs_ref, ewin_ref, ewout_ref, owin_ref, owout_ref,
                stage_ref, in_sem, out_sem):
    @pl.when(pl.program_id(0) == 0)
    def _():
        exports = [_convert_weights(src, dst, res, stage_ref, in_sem, out_sem.at[k])
                   for k, (src, dst, res) in enumerate((
                       (ewin_hbm, ewin_o, ewin_ref), (ewout_hbm, ewout_o, ewout_ref),
                       (owin_hbm, owin_o, owin_ref), (owout_hbm, owout_o, owout_ref)))]
        for export in exports:
            export.wait()

    n_b = x_ref.shape[0]
    for l in range(DEPTH):
        i = l // 2
        even = l % 2 == 0
        win_ref, wout_ref, width = (ewin_ref, ewout_ref, EVEN_IN) if even else (owin_ref, owout_ref, ODD_IN)
        shift, scale, gate = (mod_ref[l, part, 0:1, :] for part in range(3))
        for bb in range(n_b):
            x = x_ref[bb] if l == 0 else xs_ref[bb]
            h = _norm_mod(x, g_ref[l:l + 1, :], shift, scale)
            p_ref[bb, :, 0:width] = jnp.dot(h.astype(BF16), win_ref[i], preferred_element_type=F32)
        if even:
            _ctx_even_mix(i, sink_ref, gq_ref, gk_ref, (ka_ref, va_ref, kb_ref, vb_ref), p_ref, o_ref)
        else:
            _ctx_odd_mix(ccs_ref, fs_ref, p_ref, o_ref, ucs_ref)
        for bb in range(n_b):
            y = jnp.dot(o_ref[bb], wout_ref[i], preferred_element_type=F32)
            xn = (x_ref[bb] if l == 0 else xs_ref[bb]) + gate * y
            if l == DEPTH - 1:
                xo_ref[bb] = _rms_head(xn, fg_ref[...])
            else:
                xs_ref[bb] = xn


def _resident(shape):
    return pl.BlockSpec(shape, lambda i: (0,) * len(shape), pipeline_mode=pl.Buffered(1))


def _ctx_path(x, mod, norm_g, ewin, ewout, owin, owout, gq2, gk2, sink, ccs, fs, fg):
    b, s, _ = x.shape
    n_even = ewin.shape[0]
    bb = CTX_BATCH_PER_STEP
    kv_shape = jax.ShapeDtypeStruct((b, n_even, N_KV, HEAD_DIM, s), F32)
    kv_spec = pl.BlockSpec((bb, n_even, N_KV, HEAD_DIM, s), lambda i: (i, 0, 0, 0, 0))
    x_spec = pl.BlockSpec((bb, s, D_MODEL), lambda i: (i, 0, 0))
    hbm = pl.BlockSpec(memory_space=pl.ANY)
    weights = (ewin, ewout, owin, owout)
    return pl.pallas_call(
        _ctx_kernel,
        grid=(b // bb,),
        in_specs=[
            pl.BlockSpec(memory_space=pltpu.SMEM),
            x_spec,
            _resident(mod.shape), _resident(norm_g.shape),
            hbm, hbm, hbm, hbm,
            _resident(gq2.shape), _resident(gk2.shape),
            _resident(ccs.shape), _resident(fs.shape), _resident(fg.shape),
        ],
        out_specs=[x_spec, kv_spec, kv_spec, kv_spec, kv_spec, hbm, hbm, hbm, hbm],
        out_shape=[jax.ShapeDtypeStruct(x.shape, F32), kv_shape, kv_shape, kv_shape, kv_shape]
                  + [jax.ShapeDtypeStruct(w.shape, BF16) for w in weights],
        scratch_shapes=[pltpu.VMEM((bb, s, D_MODEL), F32), pltpu.VMEM((bb, s, EVEN_IN), F32),
                        pltpu.VMEM((bb, s, D_MODEL), BF16), pltpu.VMEM((bb, 2 * s, D_MODEL), BF16)]
                       + [pltpu.VMEM(w.shape, BF16) for w in weights]
                       + [pltpu.VMEM((STAGE_SLOTS, STAGE_ROWS, max(w.shape[2] for w in weights)), F32),
                          pltpu.SemaphoreType.DMA((STAGE_SLOTS,)), pltpu.SemaphoreType.DMA((len(weights),))],
        compiler_params=pltpu.CompilerParams(
            dimension_semantics=("arbitrary",), vmem_limit_bytes=CTX_VMEM_LIMIT),
        name="ctx_path",
    )(sink, x, mod, norm_g, *weights, gq2, gk2, ccs, fs, fg)


def _lat_even_proj_kernel(x_ref, mod_ref, g_ref, win_ref, gq_ref, gk_ref, cos_ref, sn_ref, sp_ref,
                          p_ref):
    shift, scale, _ = _mod_rows(mod_ref, 1 + pl.program_id(0))
    h = _norm_mod(x_ref[0], g_ref[...], shift, scale)
    p_ref[0] = jnp.dot(h.astype(BF16), win_ref[...], preferred_element_type=F32)
    cos, sn, sp = cos_ref[...], sn_ref[...], sp_ref[...]
    gq2 = gq_ref[...] * Q_SCALE
    gk2 = gk_ref[...]

    def chunk(c0):
        return p_ref[0, :, c0:c0 + 128]

    for c0 in range(QA, QA + MIX_W, 128):
        p_ref[0, :, c0:c0 + 128] = _rope(_rms_pair(chunk(c0), gq2), cos, sn, sp)
    p_ref[0, :, KA:KA + 128] = _rope(_rms_pair(chunk(KA), gk2), cos, sn, sp)
    for c0 in range(QB, QB + MIX_W, 128):
        p_ref[0, :, c0:c0 + 128] = _rope(chunk(c0) * Q_SCALE, cos, sn, sp)
    p_ref[0, :, KB:KB + 128] = _rope(chunk(KB), cos, sn, sp)
    for g0 in (GA, GB):
        for c0 in range(g0, g0 + MIX_W, 128):
            p_ref[0, :, c0:c0 + 128] = _silu(chunk(c0))


def _layer_spec(tail, idx):
    return pl.BlockSpec((None,) + tuple(tail), lambda i, n: (idx,) + (0,) * len(tail))


def _lat_even_proj(x, mod, g3, ewin, gq3, gk3, cos, sn, sp, l):
    b, s, _ = x.shape
    rope_spec = pl.BlockSpec((ROWS, 128), lambda i, n: (n, 0))
    return pl.pallas_call(
        _lat_even_proj_kernel,
        grid=(b, s // ROWS),
        in_specs=[
            pl.BlockSpec((1, ROWS, D_MODEL), lambda i, n: (i, n, 0)),
            _layer_spec((3, COND_ROWS, D_MODEL), l),
            _layer_spec((1, D_MODEL), l),
            _layer_spec((D_MODEL, EVEN_IN), l // 2),
            _layer_spec((1, 128), l // 2),
            _layer_spec((1, 128), l // 2),
            rope_spec, rope_spec, rope_spec,
        ],
        out_specs=pl.BlockSpec((1, ROWS, EVEN_IN), lambda i, n: (i, n, 0)),
        out_shape=jax.ShapeDtypeStruct((b, s, EVEN_IN), F32),
        compiler_params=pltpu.CompilerParams(
            dimension_semantics=("parallel", "parallel"), vmem_limit_bytes=VMEM_LIMIT),
        name="lat_even_proj",
    )(x, mod, g3, ewin, gq3, gk3, cos, sn, sp)


def _lat_even_attn_kernel(sink_ref, x_ref, mod_ref, pq_ref, ka_ref, va_ref, kb_ref, vb_ref,
                          cka_ref, cva_ref, ckb_ref, cvb_ref, wout_ref, xo_ref, o_ref, *, layer_i):
    n = pl.program_id(1)
    seq = ka_ref.shape[1]
    _, _, gate = _mod_rows(mod_ref, 1 + pl.program_id(0))

    prev0 = pl.multiple_of(jnp.maximum(n * ROWS - WINDOW, 0), WINDOW)
    own0 = pl.multiple_of(n * ROWS, ROWS)
    next0 = pl.multiple_of(jnp.minimum(n * ROWS + ROWS, seq - WINDOW), WINDOW)
    win_len = ROWS + 2 * WINDOW
    ctx_len = cka_ref.shape[0]
    cj = lax.broadcasted_iota(jnp.int32, (win_len + ctx_len, ROWS), 0)
    qi = lax.broadcasted_iota(jnp.int32, (win_len + ctx_len, ROWS), 1)
    kpos = n * ROWS - WINDOW + cj
    in_win = (jnp.abs(cj - WINDOW - qi) <= WINDOW) & (kpos >= 0) & (kpos < seq)
    mask_b = in_win | (cj >= win_len)

    def window(ref, cache_ref):
        return jnp.concatenate([ref[0, pl.ds(prev0, WINDOW), :], ref[0, pl.ds(own0, ROWS), :],
                                ref[0, pl.ds(next0, WINDOW), :], cache_ref[...]], axis=0)

    k2a = jnp.concatenate([ka_ref[0], cka_ref[...]], axis=0)
    v2a = jnp.concatenate([va_ref[0], cva_ref[...]], axis=0)
    k2b, v2b = window(kb_ref, ckb_ref), window(vb_ref, cvb_ref)
    items = []
    for mixer, (q0, g0, k2, v2) in enumerate(((QA, GA, k2a, v2a), (QB, GB, k2b, v2b))):
        v2t = v2.T
        for j in range(N_KV):
            k_cat, w_t = _kv_operands(k2, v2t, j)
            for cc in range(GROUP // 2):
                c0 = 128 * (j * (GROUP // 2) + cc)
                if mixer == 0:
                    sinks, mask = None, None
                else:
                    head = c0 // HEAD_DIM
                    sinks, mask = _sink_pair(sink_ref, layer_i, head), mask_b
                items.append(dict(q=q0 + c0, g=g0 + c0, o=mixer * MIX_W + c0, sinks=sinks, mask=mask,
                                  k_cat=k_cat, w_t=w_t))

    def scores(it):
        return dict(it, s_t=_scores_t(pq_ref[0, :, it["q"]:it["q"] + 128], it["k_cat"]))

    def softmax(it):
        e_t, ms = _softmax_t(it["s_t"], it["sinks"], it["mask"])
        return dict(it, e_t=e_t, ms=ms, s_t=None)

    def values(it):
        return dict(it, nd=_values_t(it["e_t"], it["w_t"]), e_t=None)

    def finish(it):
        o = _normalise_t(it["nd"], it["ms"], it["sinks"])
        gt = pq_ref[0, :, it["g"]:it["g"] + 128]
        o_ref[:, it["o"]:it["o"] + 128] = (o * gt).astype(BF16)
        return None

    _run_skewed(items, (scores, softmax, values, finish))
    y = jnp.dot(o_ref[...], wout_ref[...], preferred_element_type=F32)
    xo_ref[0] = x_ref[0] + gate * y


def _lat_even_attn(x, mod, p, cka, cva, ckb, cvb, ewout, sink, l):
    b, s, _ = x.shape
    past = cka.shape[2]
    layer_i = l // 2
    kv_spec = lambda blk: pl.BlockSpec((1, s, KV_W), lambda i, n, blk=blk: (i, 0, blk))
    cache_spec = pl.BlockSpec((None, None, past, KV_W), lambda i, n: (i, layer_i, 0, 0))
    return pl.pallas_call(
        functools.partial(_lat_even_attn_kernel, layer_i=layer_i),
        grid=(b, s // ROWS),
        in_specs=[
            pl.BlockSpec(memory_space=pltpu.SMEM),
            pl.BlockSpec((1, ROWS, D_MODEL), lambda i, n: (i, n, 0)),
            _layer_spec((3, COND_ROWS, D_MODEL), l),
            pl.BlockSpec((1, ROWS, EVEN_IN), lambda i, n: (i, n, 0)),
            kv_spec(KA // KV_W), kv_spec(VA // KV_W), kv_spec(KB // KV_W), kv_spec(VB // KV_W),
            cache_spec, cache_spec, cache_spec, cache_spec,
            _layer_spec((D_MODEL, D_MODEL), layer_i),
        ],
        out_specs=pl.BlockSpec((1, ROWS, D_MODEL), lambda i, n: (i, n, 0)),
        out_shape=jax.ShapeDtypeStruct(x.shape, F32),
        scratch_shapes=[pltpu.VMEM((ROWS, D_MODEL), BF16)],
        compiler_params=pltpu.CompilerParams(
            dimension_semantics=("parallel", "parallel"), vmem_limit_bytes=VMEM_LIMIT),
        name="lat_even_attn",
    )(sink, x, mod, p, p, p, p, p, cka, cva, ckb, cvb, ewout)


def _lat_odd_proj_kernel(x_ref, mod_ref, g_ref, win_ref, ccs_ref, ucs_ref, gate_ref, p_ref, t_ref):
    shift, scale, _ = _mod_rows(mod_ref, 1 + pl.program_id(0))
    h = _norm_mod(x_ref[0], g_ref[...], shift, scale)
    p_ref[...] = jnp.dot(h.astype(BF16), win_ref[...], preferred_element_type=F32)
    _fourier_rows(p_ref, ccs_ref, t_ref, ROWS)
    ucs_ref[0, 0] = t_ref[0:ROWS, :]
    ucs_ref[0, 1] = t_ref[ROWS:2 * ROWS, :]
    gate_ref[0] = _silu(p_ref[:, D_MODEL:2 * D_MODEL])


def _lat_odd_proj(x, mod, g3, owin, ccs, l):
    b, s, _ = x.shape
    return pl.pallas_call(
        _lat_odd_proj_kernel,
        grid=(b, s // ROWS),
        in_specs=[
            pl.BlockSpec((1, ROWS, D_MODEL), lambda i, n: (i, n, 0)),
            _layer_spec((3, COND_ROWS, D_MODEL), l),
            _layer_spec((1, D_MODEL), l),
            _layer_spec((D_MODEL, ODD_IN), l // 2),
            pl.BlockSpec((C_GROUP_DIM, 2 * C_GROUP_DIM), lambda i, n: (0, 0)),
        ],
        out_specs=[
            pl.BlockSpec((1, 2, ROWS, D_MODEL), lambda i, n: (i, 0, n, 0)),
            pl.BlockSpec((1, ROWS, D_MODEL), lambda i, n: (i, n, 0)),
        ],
        out_shape=[jax.ShapeDtypeStruct((b, 2, s, D_MODEL), BF16),
                   jax.ShapeDtypeStruct((b, s, D_MODEL), F32)],
        scratch_shapes=[pltpu.VMEM((ROWS, ODD_IN), F32), pltpu.VMEM((2 * ROWS, D_MODEL), BF16)],
        compiler_params=pltpu.CompilerParams(
            dimension_semantics=("parallel", "parallel"), vmem_limit_bytes=VMEM_LIMIT),
        name="lat_odd_proj",
    )(x, mod, g3, owin, ccs)


def _lat_odd_mix_kernel(x_ref, mod_ref, fs_ref, ucs_ref, gate_ref, wout_ref, fg_ref, xo_ref, *, final):
    seq = ucs_ref.shape[1] // 2
    _, _, gate = _mod_rows(mod_ref, 1 + pl.program_id(0))
    r = jnp.dot(fs_ref[...], ucs_ref[0], preferred_element_type=F32)
    r = r * float(1.0 / np.sqrt(float(seq * C_GROUP_DIM)))
    mix = r * gate_ref[0]
    y = jnp.dot(mix.astype(BF16), wout_ref[...], preferred_element_type=F32)
    xn = x_ref[0] + gate * y
    if final:
        xn = _rms_head(xn, fg_ref[...])
    xo_ref[0] = xn


def _lat_odd_mix(x, mod, fs, ucs, gsilu, owout, fg, l):
    b, s, _ = x.shape
    return pl.pallas_call(
        functools.partial(_lat_odd_mix_kernel, final=l == DEPTH - 1),
        grid=(b, s // ROWS),
        in_specs=[
            pl.BlockSpec((1, ROWS, D_MODEL), lambda i, n: (i, n, 0)),
            _layer_spec((3, COND_ROWS, D_MODEL), l),
            pl.BlockSpec((ROWS, 2 * s), lambda i, n: (n, 0)),
            pl.BlockSpec((1, 2 * s, D_MODEL), lambda i, n: (i, 0, 0)),
            pl.BlockSpec((1, ROWS, D_MODEL), lambda i, n: (i, n, 0)),
            _layer_spec((D_MODEL, D_MODEL), l // 2),
            pl.BlockSpec((1, D_MODEL), lambda i, n: (0, 0)),
        ],
        out_specs=pl.BlockSpec((1, ROWS, D_MODEL), lambda i, n: (i, n, 0)),
        out_shape=jax.ShapeDtypeStruct(x.shape, F32),
        compiler_params=pltpu.CompilerParams(
            dimension_semantics=("parallel", "parallel"), vmem_limit_bytes=VMEM_LIMIT),
        name="lat_odd_mix",
    )(x, mod, fs, ucs.reshape(b, 2 * s, D_MODEL), gsilu, owout, fg)


def kernel(x_prompt, x_sample, cache_k_a, cache_v_a, cache_k_b, cache_v_b, c, c_ctx, norm_g, ada_w, ada_b,
           even_w_in, even_w_out, qk_g_q, qk_g_k, sink_logit, odd_w_in, odd_w_out, final_g):
    batch, seq, _ = x_prompt.shape
    dec_batch, dec_seq, _ = x_sample.shape
    n_even = even_w_in.shape[0]
    past = cache_k_a.shape[2]

    cond = jnp.concatenate(
        [c_ctx[None, :], c, jnp.zeros((COND_ROWS - 1 - dec_batch, D_MODEL), F32)], axis=0)
    mod = _ada_all(cond, ada_w, ada_b)

    ccs = jnp.asarray(_channel_dft()).astype(BF16)
    fs_ctx = jnp.asarray(_position_dft(seq)).astype(BF16)
    fs_lat = jnp.asarray(_position_dft(dec_seq)).astype(BF16)
    cos, sn, sp = (jnp.asarray(t) for t in _rope_tables(dec_seq))

    caches = [a.reshape(dec_batch, n_even, past, KV_W) for a in (cache_k_a, cache_v_a, cache_k_b, cache_v_b)]
    fg = final_g.reshape(1, D_MODEL)
    gq2 = jnp.tile(qk_g_q, (1, 2))
    gk2 = jnp.tile(qk_g_k, (1, 2))

    xc, *outs = _ctx_path(x_prompt, mod, norm_g, even_w_in, even_w_out, odd_w_in, odd_w_out, gq2, gk2,
                          sink_logit, ccs, fs_ctx, fg)
    new_kv = [jnp.transpose(a, (0, 1, 4, 2, 3)) for a in outs[:4]]
    ewin, ewout, owin, owout = outs[4:]

    xl = x_sample
    g3 = norm_g.reshape(DEPTH, 1, D_MODEL)
    gq3, gk3 = gq2.reshape(n_even, 1, 128), gk2.reshape(n_even, 1, 128)
    for l in range(DEPTH):
        if l % 2 == 0:
            p = _lat_even_proj(xl, mod, g3, ewin, gq3, gk3, cos, sn, sp, l)
            xl = _lat_even_attn(xl, mod, p, *caches, ewout, sink_logit, l)
        else:
            ucs, gsilu = _lat_odd_proj(xl, mod, g3, owin, ccs, l)
            xl = _lat_odd_mix(xl, mod, fs_lat, ucs, gsilu, owout, fg, l)
    return (xc, xl, *new_kv)
```

```python
import functools

import numpy as np
import jax
import jax.numpy as jnp
from jax import lax
from jax.experimental import pallas as pl
from jax.experimental.pallas import tpu as pltpu

D_MODEL = 1024
DEPTH = 4
HEAD_DIM = 64
N_HEADS = 8
N_KV = 2
GROUP = N_HEADS // N_KV
MIX_W = N_HEADS * HEAD_DIM
KV_W = N_KV * HEAD_DIM
EVEN_IN = 2 * (2 * MIX_W + 2 * KV_W)
ODD_IN = 2 * D_MODEL
GRID_W = 64
WINDOW = 128
ROPE_BASE = 10000.0
C_GROUPS = 4
C_GROUP_DIM = D_MODEL // C_GROUPS
EPS = 1e-6
NEG_BIG = -1e30
ROWS = 256
COND_ROWS = 8
VMEM_LIMIT = 48 * 1024 * 1024
CTX_VMEM_LIMIT = 56 * 1024 * 1024
STAGE_ROWS = 128
STAGE_SLOTS = 4
CTX_BATCH_PER_STEP = 2

QA, KA, VA, GA = 0, 512, 640, 768
QB, KB, VB, GB = 1280, 1792, 1920, 2048

F32 = jnp.float32
BF16 = jnp.bfloat16


def _dft_tables(n):
    k = np.arange(n, dtype=np.int64)
    ang = ((k[:, None] * k[None, :]) % n).astype(np.float64) * (2.0 * np.pi / n)
    return np.cos(ang).astype(np.float32), np.sin(ang).astype(np.float32)


def _channel_dft():
    c, s = _dft_tables(C_GROUP_DIM)
    return np.concatenate([c, s], axis=1)


def _position_dft(n):
    c, s = _dft_tables(n)
    return np.concatenate([c, -s], axis=1)


def _rope_tables(n_tok):
    rows = n_tok // GRID_W
    row = np.repeat(np.arange(rows), GRID_W).astype(np.float64)
    col = np.tile(np.arange(GRID_W), rows).astype(np.float64)
    half = HEAD_DIM // 2
    inv = 1.0 / (ROPE_BASE ** (np.arange(0, half, 2, dtype=np.float64) / half))
    ang_r = row[:, None] * inv
    ang_c = col[:, None] * inv
    zeros = np.zeros_like(ang_r)
    cos_h = np.concatenate([np.cos(ang_r), np.cos(ang_r), np.cos(ang_c), np.cos(ang_c)], axis=1)
    nxt_h = np.concatenate([-np.sin(ang_r), zeros, -np.sin(ang_c), zeros], axis=1)
    prv_h = np.concatenate([zeros, np.sin(ang_r), zeros, np.sin(ang_c)], axis=1)
    two = lambda t: np.concatenate([t, t], axis=1).astype(np.float32)
    return two(cos_h), two(nxt_h), two(prv_h)


def _silu(x):
    return x / (1.0 + jnp.exp(-x))


def _norm_mod(x, g, shift, scale):
    ms = jnp.mean(x * x, axis=-1, keepdims=True)
    return x * lax.rsqrt(ms + EPS) * (g * (1.0 + scale)) + shift


def _rms_head(xh, g):
    ms = jnp.mean(xh * xh, axis=-1, keepdims=True)
    return xh * lax.rsqrt(ms + EPS) * g


def _rms_pair(xc, g2):
    lo = lax.broadcasted_iota(jnp.int32, xc.shape, 1) < HEAD_DIM
    ss = xc * xc
    s_lo = jnp.sum(jnp.where(lo, ss, 0.0), axis=-1, keepdims=True)
    s_hi = jnp.sum(jnp.where(lo, 0.0, ss), axis=-1, keepdims=True)
    ms = jnp.where(lo, s_lo, s_hi) * (1.0 / HEAD_DIM)
    return xc * lax.rsqrt(ms + EPS) * g2


def _rope(xc, cos, sin_next, sin_prev):
    nxt = pltpu.roll(xc, 128 - 16, 1)
    prv = pltpu.roll(xc, 16, 1)
    return xc * cos + nxt * sin_next + prv * sin_prev


DEN_ROWS = 16
LOG2E = float(np.log2(np.e))
Q_SCALE = (HEAD_DIM ** -0.5) * LOG2E


def _sink_pair(sink_ref, i, head):
    return sink_ref[i, head] * LOG2E, sink_ref[i, head + 1] * LOG2E


def _kv_operands(k2, v2t, j):
    tk = k2.shape[0]
    low = lax.broadcasted_iota(jnp.int32, k2.shape, 1) < HEAD_DIM
    km = jnp.where(low if j == 0 else jnp.logical_not(low), k2, 0.0)
    kr = pltpu.roll(km, HEAD_DIM, 1)
    k_lo, k_hi = (km, kr) if j == 0 else (kr, km)
    k_cat = jnp.concatenate([k_lo, k_hi], axis=0).astype(BF16)
    vjt = v2t[HEAD_DIM * j:HEAD_DIM * (j + 1), :]
    zero = jnp.zeros_like(vjt)
    row = lax.broadcasted_iota(jnp.int32, (DEN_ROWS, 2 * tk), 0)
    col = lax.broadcasted_iota(jnp.int32, (DEN_ROWS, 2 * tk), 1)
    ones = jnp.where(((row == 0) & (col < tk)) | ((row == 1) & (col >= tk)), 1.0, 0.0)
    w_t = jnp.concatenate([jnp.concatenate([vjt, zero], axis=1),
                           jnp.concatenate([zero, vjt], axis=1), ones], axis=0).astype(BF16)
    return k_cat, w_t


def _scores_t(qc, k_cat):
    return lax.dot_general(k_cat, qc.astype(BF16), (((1,), (1,)), ((), ())), preferred_element_type=F32)


def _softmax_t(s_t, sinks=None, mask_t=None):
    tk = s_t.shape[0] // 2
    es, ms = [], []
    for hh in range(2):
        sh = s_t[hh * tk:(hh + 1) * tk]
        if mask_t is not None:
            sh = jnp.where(mask_t, sh, NEG_BIG)
        m = jnp.max(sh, axis=0, keepdims=True)
        if sinks is not None:
            m = jnp.maximum(m, sinks[hh])
        es.append(jnp.exp2(sh - m))
        ms.append(m)
    return jnp.concatenate(es, axis=0).astype(BF16), ms


def _values_t(e_t, w_t):
    return jnp.dot(w_t, e_t, preferred_element_type=F32)


def _normalise_t(nd, ms, sinks=None):
    dens = [nd[2 * HEAD_DIM + hh:2 * HEAD_DIM + hh + 1] for hh in range(2)]
    if sinks is not None:
        dens = [dens[hh] + jnp.exp2(sinks[hh] - ms[hh]) for hh in range(2)]
    o_t = jnp.concatenate([nd[0:HEAD_DIM] / dens[0], nd[HEAD_DIM:2 * HEAD_DIM] / dens[1]], axis=0)
    return o_t.T


def _run_skewed(items, stages):
    state = list(items)
    for t in range(len(items) + len(stages) - 1):
        for k, stage in enumerate(stages):
            if 0 <= t - k < len(items):
                state[t - k] = stage(state[t - k])


def _mod_rows(mod_ref, row):
    return mod_ref[0, pl.ds(row, 1), :], mod_ref[1, pl.ds(row, 1), :], mod_ref[2, pl.ds(row, 1), :]


def _ada_kernel(cond_ref, w_ref, b_ref, o_ref):
    a = _silu(cond_ref[...]).astype(BF16)
    o_ref[...] = jnp.dot(a, w_ref[...].astype(BF16), preferred_element_type=F32) + b_ref[...]


def _ada_all(cond, ada_w, ada_b):
    return pl.pallas_call(
        _ada_kernel,
        grid=(DEPTH, 3),
        in_specs=[
            pl.BlockSpec((COND_ROWS, D_MODEL), lambda l, p: (0, 0)),
            pl.BlockSpec((None, D_MODEL, D_MODEL), lambda l, p: (l, 0, p)),
            pl.BlockSpec((None, None, 1, D_MODEL), lambda l, p: (l, p, 0, 0)),
        ],
        out_specs=pl.BlockSpec((None, None, COND_ROWS, D_MODEL), lambda l, p: (l, p, 0, 0)),
        out_shape=jax.ShapeDtypeStruct((DEPTH, 3, COND_ROWS, D_MODEL), F32),
        compiler_params=pltpu.CompilerParams(
            dimension_semantics=("parallel", "parallel"), vmem_limit_bytes=VMEM_LIMIT),
        name="ada_mod",
    )(cond, ada_w, ada_b.reshape(DEPTH, 3, 1, D_MODEL))


def _fourier_rows(p_ref, ccs_ref, ucs_ref, rows):
    for grp in range(C_GROUPS):
        c0 = C_GROUP_DIM * grp
        ug = p_ref[:, c0:c0 + C_GROUP_DIM].astype(BF16)
        t = jnp.dot(ug, ccs_ref[...], preferred_element_type=F32)
        ucs_ref[0:rows, c0:c0 + C_GROUP_DIM] = t[:, :C_GROUP_DIM].astype(BF16)
        ucs_ref[rows:2 * rows, c0:c0 + C_GROUP_DIM] = t[:, C_GROUP_DIM:].astype(BF16)


def _ctx_even_mix(i, sink_ref, gq_ref, gk_ref, kv_refs, p_ref, o_ref):
    n_b, seq, _ = p_ref.shape
    gq2 = gq_ref[i:i + 1, :] * Q_SCALE
    gk2 = gk_ref[i:i + 1, :]
    items = []
    for mixer, (q0, k0, v0, g0) in enumerate(((QA, KA, VA, GA), (QB, KB, VB, GB))):
        kv = []
        for bb in range(n_b):
            k2 = p_ref[bb, :, k0:k0 + KV_W]
            v2 = p_ref[bb, :, v0:v0 + KV_W]
            if mixer == 0:
                k2 = _rms_pair(k2, gk2)
            v2t = v2.T
            kv_refs[2 * mixer][bb, i] = k2.T.reshape(N_KV, HEAD_DIM, seq)
            kv_refs[2 * mixer + 1][bb, i] = v2t.reshape(N_KV, HEAD_DIM, seq)
            kv.append((k2, v2t))
        for j in range(N_KV):
            ops = [_kv_operands(k2, v2t, j) for k2, v2t in kv]
            for cc in range(GROUP // 2):
                c0 = 128 * (j * (GROUP // 2) + cc)
                sinks = None if mixer == 0 else _sink_pair(sink_ref, i, c0 // HEAD_DIM)
                for bb, (k_cat, w_t) in enumerate(ops):
                    items.append(dict(rows=bb, q=q0 + c0, g=g0 + c0, o=mixer * MIX_W + c0,
                                      norm=mixer == 0, sinks=sinks, k_cat=k_cat, w_t=w_t))

    def scores(it):
        qc = p_ref[it["rows"], :, it["q"]:it["q"] + 128]
        qc = _rms_pair(qc, gq2) if it["norm"] else qc * Q_SCALE
        return dict(it, s_t=_scores_t(qc, it["k_cat"]))

    def softmax(it):
        e_t, ms = _softmax_t(it["s_t"], it["sinks"])
        return dict(it, e_t=e_t, ms=ms, s_t=None)

    def values(it):
        return dict(it, nd=_values_t(it["e_t"], it["w_t"]), e_t=None)

    def finish(it):
        o = _normalise_t(it["nd"], it["ms"], it["sinks"])
        gt = p_ref[it["rows"], :, it["g"]:it["g"] + 128]
        o_ref[it["rows"], :, it["o"]:it["o"] + 128] = (o * _silu(gt)).astype(BF16)
        return None

    _run_skewed(items, (scores, softmax, values, finish))


def _ctx_odd_mix(ccs_ref, fs_ref, p_ref, o_ref, ucs_ref):
    n_b, seq, _ = p_ref.shape
    for bb in range(n_b):
        for grp in range(C_GROUPS):
            c0 = C_GROUP_DIM * grp
            t = jnp.dot(p_ref[bb, :, c0:c0 + C_GROUP_DIM].astype(BF16), ccs_ref[...],
                        preferred_element_type=F32)
            ucs_ref[bb, 0:seq, c0:c0 + C_GROUP_DIM] = t[:, :C_GROUP_DIM].astype(BF16)
            ucs_ref[bb, seq:2 * seq, c0:c0 + C_GROUP_DIM] = t[:, C_GROUP_DIM:].astype(BF16)
    scale = float(1.0 / np.sqrt(float(seq * C_GROUP_DIM)))
    for bb in range(n_b):
        r = jnp.dot(fs_ref[...], ucs_ref[bb], preferred_element_type=F32) * scale
        o_ref[bb] = (r * _silu(p_ref[bb, :, D_MODEL:2 * D_MODEL])).astype(BF16)


def _stage_copy(w_hbm, stage_ref, sem, chunk, slot):
    _, n_rows, cols = w_hbm.shape
    rows = stage_ref.shape[1]
    per_layer = n_rows // rows
    src = w_hbm.at[chunk // per_layer, pl.ds((chunk % per_layer) * rows, rows), :]
    return pltpu.make_async_copy(src, stage_ref.at[slot, :, pl.ds(0, cols)], sem.at[slot])


def _convert_weights(w_hbm, w_ref, stage_ref, in_sem):
    n_layers, n_rows, cols = w_hbm.shape
    n_slots, rows, _ = stage_ref.shape
    per_layer = n_rows // rows
    n_chunks = n_layers * per_layer
    for c in range(n_slots - 1):
        _stage_copy(w_hbm, stage_ref, in_sem, c, c).start()

    def body(c, carry):
        slot = c % n_slots
        ahead = c + n_slots - 1

        @pl.when(ahead < n_chunks)
        def _():
            _stage_copy(w_hbm, stage_ref, in_sem, ahead, ahead % n_slots).start()

        _stage_copy(w_hbm, stage_ref, in_sem, c, slot).wait()
        r0 = pl.multiple_of((c % per_layer) * rows, rows)
        w_ref[c // per_layer, pl.ds(r0, rows), :] = stage_ref[slot, :, 0:cols].astype(BF16)
        return carry

    lax.fori_loop(0, n_chunks, body, 0)


def _ctx_kernel(sink_ref, x_ref, mod_ref, g_ref, ewin_hbm, ewout_hbm, owin_hbm, owout_hbm, gq_ref, gk_ref,
                ccs_ref, fs_ref, fg_ref,
                xo_ref, ka_ref, va_ref, kb_ref, vb_ref, ewin_o, ewout_o, owin_o, owout_o,
                xs_ref, p_ref, o_ref, ucs_ref, ewin_ref, ewout_ref, owin_ref, owout_ref,
                stage_ref, in_sem, out_sem):
    weights = ((ewin_hbm, ewin_o, ewin_ref), (ewout_hbm, ewout_o, ewout_ref),
               (owin_hbm, owin_o, owin_ref), (owout_hbm, owout_o, owout_ref))
    exports = [pltpu.make_async_copy(res, dst, out_sem.at[k]) for k, (_, dst, res) in enumerate(weights)]

    @pl.when(pl.program_id(0) == 0)
    def _():
        for src, _, res in weights:
            _convert_weights(src, res, stage_ref, in_sem)
        for export in exports:
            export.start()

    @pl.when(pl.program_id(0) == pl.num_programs(0) - 1)
    def _():
        for export in exports:
            export.wait()

    n_b = x_ref.shape[0]
    for l in range(DEPTH):
        i = l // 2
        even = l % 2 == 0
        win_ref, wout_ref, width = (ewin_ref, ewout_ref, EVEN_IN) if even else (owin_ref, owout_ref, ODD_IN)
        shift, scale, gate = (mod_ref[l, part, 0:1, :] for part in range(3))
        for bb in range(n_b):
            x = x_ref[bb] if l == 0 else xs_ref[bb]
            h = _norm_mod(x, g_ref[l:l + 1, :], shift, scale)
            p_ref[bb, :, 0:width] = jnp.dot(h.astype(BF16), win_ref[i], preferred_element_type=F32)
        if even:
            _ctx_even_mix(i, sink_ref, gq_ref, gk_ref, (ka_ref, va_ref, kb_ref, vb_ref), p_ref, o_ref)
        else:
            _ctx_odd_mix(ccs_ref, fs_ref, p_ref, o_ref, ucs_ref)
        for bb in range(n_b):
            y = jnp.dot(o_ref[bb], wout_ref[i], preferred_element_type=F32)
            xn = (x_ref[bb] if l == 0 else xs_ref[bb]) + gate * y
            if l == DEPTH - 1:
                xo_ref[bb] = _rms_head(xn, fg_ref[...])
            else:
                xs_ref[bb] = xn


def _resident(shape):
    return pl.BlockSpec(shape, lambda i: (0,) * len(shape), pipeline_mode=pl.Buffered(1))


def _ctx_path(x, mod, norm_g, ewin, ewout, owin, owout, gq2, gk2, sink, ccs, fs, fg):
    b, s, _ = x.shape
    n_even = ewin.shape[0]
    bb = CTX_BATCH_PER_STEP
    kv_shape = jax.ShapeDtypeStruct((b, n_even, N_KV, HEAD_DIM, s), F32)
    kv_spec = pl.BlockSpec((bb, n_even, N_KV, HEAD_DIM, s), lambda i: (i, 0, 0, 0, 0))
    x_spec = pl.BlockSpec((bb, s, D_MODEL), lambda i: (i, 0, 0))
    hbm = pl.BlockSpec(memory_space=pl.ANY)
    weights = (ewin, ewout, owin, owout)
    return pl.pallas_call(
        _ctx_kernel,
        grid=(b // bb,),
        in_specs=[
            pl.BlockSpec(memory_space=pltpu.SMEM),
            x_spec,
            _resident(mod.shape), _resident(norm_g.shape),
            hbm, hbm, hbm, hbm,
            _resident(gq2.shape), _resident(gk2.shape),
            _resident(ccs.shape), _resident(fs.shape), _resident(fg.shape),
        ],
        out_specs=[x_spec, kv_spec, kv_spec, kv_spec, kv_spec, hbm, hbm, hbm, hbm],
        out_shape=[jax.ShapeDtypeStruct(x.shape, F32), kv_shape, kv_shape, kv_shape, kv_shape]
                  + [jax.ShapeDtypeStruct(w.shape, BF16) for w in weights],
        scratch_shapes=[pltpu.VMEM((bb, s, D_MODEL), F32), pltpu.VMEM((bb, s, EVEN_IN), F32),
                        pltpu.VMEM((bb, s, D_MODEL), BF16), pltpu.VMEM((bb, 2 * s, D_MODEL), BF16)]
                       + [pltpu.VMEM(w.shape, BF16) for w in weights]
                       + [pltpu.VMEM((STAGE_SLOTS, STAGE_ROWS, max(w.shape[2] for w in weights)), F32),
                          pltpu.SemaphoreType.DMA((STAGE_SLOTS,)), pltpu.SemaphoreType.DMA((len(weights),))],
        compiler_params=pltpu.CompilerParams(
            dimension_semantics=("arbitrary",), vmem_limit_bytes=CTX_VMEM_LIMIT),
        name="ctx_path",
    )(sink, x, mod, norm_g, *weights, gq2, gk2, ccs, fs, fg)


def _lat_even_proj_kernel(x_ref, mod_ref, g_ref, win_ref, gq_ref, gk_ref, cos_ref, sn_ref, sp_ref,
                          p_ref):
    shift, scale, _ = _mod_rows(mod_ref, 1 + pl.program_id(0))
    h = _norm_mod(x_ref[0], g_ref[...], shift, scale)
    p_ref[0] = jnp.dot(h.astype(BF16), win_ref[...], preferred_element_type=F32)
    cos, sn, sp = cos_ref[...], sn_ref[...], sp_ref[...]
    gq2 = gq_ref[...] * Q_SCALE
    gk2 = gk_ref[...]

    def chunk(c0):
        return p_ref[0, :, c0:c0 + 128]

    for c0 in range(QA, QA + MIX_W, 128):
        p_ref[0, :, c0:c0 + 128] = _rope(_rms_pair(chunk(c0), gq2), cos, sn, sp)
    p_ref[0, :, KA:KA + 128] = _rope(_rms_pair(chunk(KA), gk2), cos, sn, sp)
    for c0 in range(QB, QB + MIX_W, 128):
        p_ref[0, :, c0:c0 + 128] = _rope(chunk(c0) * Q_SCALE, cos, sn, sp)
    p_ref[0, :, KB:KB + 128] = _rope(chunk(KB), cos, sn, sp)
    for g0 in (GA, GB):
        for c0 in range(g0, g0 + MIX_W, 128):
            p_ref[0, :, c0:c0 + 128] = _silu(chunk(c0))


def _layer_spec(tail, idx):
    return pl.BlockSpec((None,) + tuple(tail), lambda i, n: (idx,) + (0,) * len(tail))


def _lat_even_proj(x, mod, g3, ewin, gq3, gk3, cos, sn, sp, l):
    b, s, _ = x.shape
    rope_spec = pl.BlockSpec((ROWS, 128), lambda i, n: (n, 0))
    return pl.pallas_call(
        _lat_even_proj_kernel,
        grid=(b, s // ROWS),
        in_specs=[
            pl.BlockSpec((1, ROWS, D_MODEL), lambda i, n: (i, n, 0)),
            _layer_spec((3, COND_ROWS, D_MODEL), l),
            _layer_spec((1, D_MODEL), l),
            _layer_spec((D_MODEL, EVEN_IN), l // 2),
            _layer_spec((1, 128), l // 2),
            _layer_spec((1, 128), l // 2),
            rope_spec, rope_spec, rope_spec,
        ],
        out_specs=pl.BlockSpec((1, ROWS, EVEN_IN), lambda i, n: (i, n, 0)),
        out_shape=jax.ShapeDtypeStruct((b, s, EVEN_IN), F32),
        compiler_params=pltpu.CompilerParams(
            dimension_semantics=("parallel", "parallel"), vmem_limit_bytes=VMEM_LIMIT),
        name="lat_even_proj",
    )(x, mod, g3, ewin, gq3, gk3, cos, sn, sp)


def _lat_even_attn_kernel(sink_ref, x_ref, mod_ref, pq_ref, ka_ref, va_ref, kb_ref, vb_ref,
                          cka_ref, cva_ref, ckb_ref, cvb_ref, wout_ref, xo_ref, o_ref, *, layer_i):
    n = pl.program_id(1)
    seq = ka_ref.shape[1]
    _, _, gate = _mod_rows(mod_ref, 1 + pl.program_id(0))

    prev0 = pl.multiple_of(jnp.maximum(n * ROWS - WINDOW, 0), WINDOW)
    own0 = pl.multiple_of(n * ROWS, ROWS)
    next0 = pl.multiple_of(jnp.minimum(n * ROWS + ROWS, seq - WINDOW), WINDOW)
    win_len = ROWS + 2 * WINDOW
    ctx_len = cka_ref.shape[0]
    cj = lax.broadcasted_iota(jnp.int32, (win_len + ctx_len, ROWS), 0)
    qi = lax.broadcasted_iota(jnp.int32, (win_len + ctx_len, ROWS), 1)
    kpos = n * ROWS - WINDOW + cj
    in_win = (jnp.abs(cj - WINDOW - qi) <= WINDOW) & (kpos >= 0) & (kpos < seq)
    mask_b = in_win | (cj >= win_len)

    def window(ref, cache_ref):
        return jnp.concatenate([ref[0, pl.ds(prev0, WINDOW), :], ref[0, pl.ds(own0, ROWS), :],
                                ref[0, pl.ds(next0, WINDOW), :], cache_ref[...]], axis=0)

    k2a = jnp.concatenate([ka_ref[0], cka_ref[...]], axis=0)
    v2a = jnp.concatenate([va_ref[0], cva_ref[...]], axis=0)
    k2b, v2b = window(kb_ref, ckb_ref), window(vb_ref, cvb_ref)
    items = []
    for mixer, (q0, g0, k2, v2) in enumerate(((QA, GA, k2a, v2a), (QB, GB, k2b, v2b))):
        v2t = v2.T
        for j in range(N_KV):
            k_cat, w_t = _kv_operands(k2, v2t, j)
            for cc in range(GROUP // 2):
                c0 = 128 * (j * (GROUP // 2) + cc)
                if mixer == 0:
                    sinks, mask = None, None
                else:
                    head = c0 // HEAD_DIM
                    sinks, mask = _sink_pair(sink_ref, layer_i, head), mask_b
                items.append(dict(q=q0 + c0, g=g0 + c0, o=mixer * MIX_W + c0, sinks=sinks, mask=mask,
                                  k_cat=k_cat, w_t=w_t))

    def scores(it):
        return dict(it, s_t=_scores_t(pq_ref[0, :, it["q"]:it["q"] + 128], it["k_cat"]))

    def softmax(it):
        e_t, ms = _softmax_t(it["s_t"], it["sinks"], it["mask"])
        return dict(it, e_t=e_t, ms=ms, s_t=None)

    def values(it):
        return dict(it, nd=_values_t(it["e_t"], it["w_t"]), e_t=None)

    def finish(it):
        o = _normalise_t(it["nd"], it["ms"], it["sinks"])
        gt = pq_ref[0, :, it["g"]:it["g"] + 128]
        o_ref[:, it["o"]:it["o"] + 128] = (o * gt).astype(BF16)
        return None

    _run_skewed(items, (scores, softmax, values, finish))
    y = jnp.dot(o_ref[...], wout_ref[...], preferred_element_type=F32)
    xo_ref[0] = x_ref[0] + gate * y


def _lat_even_attn(x, mod, p, cka, cva, ckb, cvb, ewout, sink, l):
    b, s, _ = x.shape
    past = cka.shape[2]
    layer_i = l // 2
    kv_spec = lambda blk: pl.BlockSpec((1, s, KV_W), lambda i, n, blk=blk: (i, 0, blk))
    cache_spec = pl.BlockSpec((None, None, past, KV_W), lambda i, n: (i, layer_i, 0, 0))
    return pl.pallas_call(
        functools.partial(_lat_even_attn_kernel, layer_i=layer_i),
        grid=(b, s // ROWS),
        in_specs=[
            pl.BlockSpec(memory_space=pltpu.SMEM),
            pl.BlockSpec((1, ROWS, D_MODEL), lambda i, n: (i, n, 0)),
            _layer_spec((3, COND_ROWS, D_MODEL), l),
            pl.BlockSpec((1, ROWS, EVEN_IN), lambda i, n: (i, n, 0)),
            kv_spec(KA // KV_W), kv_spec(VA // KV_W), kv_spec(KB // KV_W), kv_spec(VB // KV_W),
            cache_spec, cache_spec, cache_spec, cache_spec,
            _layer_spec((D_MODEL, D_MODEL), layer_i),
        ],
        out_specs=pl.BlockSpec((1, ROWS, D_MODEL), lambda i, n: (i, n, 0)),
        out_shape=jax.ShapeDtypeStruct(x.shape, F32),
        scratch_shapes=[pltpu.VMEM((ROWS, D_MODEL), BF16)],
        compiler_params=pltpu.CompilerParams(
            dimension_semantics=("parallel", "parallel"), vmem_limit_bytes=VMEM_LIMIT),
        name="lat_even_attn",
    )(sink, x, mod, p, p, p, p, p, cka, cva, ckb, cvb, ewout)


def _lat_odd_proj_kernel(x_ref, mod_ref, g_ref, win_ref, ccs_ref, ucs_ref, gate_ref, p_ref, t_ref):
    shift, scale, _ = _mod_rows(mod_ref, 1 + pl.program_id(0))
    h = _norm_mod(x_ref[0], g_ref[...], shift, scale)
    p_ref[...] = jnp.dot(h.astype(BF16), win_ref[...], preferred_element_type=F32)
    _fourier_rows(p_ref, ccs_ref, t_ref, ROWS)
    ucs_ref[0, 0] = t_ref[0:ROWS, :]
    ucs_ref[0, 1] = t_ref[ROWS:2 * ROWS, :]
    gate_ref[0] = _silu(p_ref[:, D_MODEL:2 * D_MODEL])


def _lat_odd_proj(x, mod, g3, owin, ccs, l):
    b, s, _ = x.shape
    return pl.pallas_call(
        _lat_odd_proj_kernel,
        grid=(b, s // ROWS),
        in_specs=[
            pl.BlockSpec((1, ROWS, D_MODEL), lambda i, n: (i, n, 0)),
            _layer_spec((3, COND_ROWS, D_MODEL), l),
            _layer_spec((1, D_MODEL), l),
            _layer_spec((D_MODEL, ODD_IN), l // 2),
            pl.BlockSpec((C_GROUP_DIM, 2 * C_GROUP_DIM), lambda i, n: (0, 0)),
        ],
        out_specs=[
            pl.BlockSpec((1, 2, ROWS, D_MODEL), lambda i, n: (i, 0, n, 0)),
            pl.BlockSpec((1, ROWS, D_MODEL), lambda i, n: (i, n, 0)),
        ],
        out_shape=[jax.ShapeDtypeStruct((b, 2, s, D_MODEL), BF16),
                   jax.ShapeDtypeStruct((b, s, D_MODEL), F32)],
        scratch_shapes=[pltpu.VMEM((ROWS, ODD_IN), F32), pltpu.VMEM((2 * ROWS, D_MODEL), BF16)],
        compiler_params=pltpu.CompilerParams(
            dimension_semantics=("parallel", "parallel"), vmem_limit_bytes=VMEM_LIMIT),
        name="lat_odd_proj",
    )(x, mod, g3, owin, ccs)


def _lat_odd_mix_kernel(x_ref, mod_ref, fs_ref, ucs_ref, gate_ref, wout_ref, fg_ref, xo_ref, *, final):
    seq = ucs_ref.shape[1] // 2
    _, _, gate = _mod_rows(mod_ref, 1 + pl.program_id(0))
    r = jnp.dot(fs_ref[...], ucs_ref[0], preferred_element_type=F32)
    r = r * float(1.0 / np.sqrt(float(seq * C_GROUP_DIM)))
    mix = r * gate_ref[0]
    y = jnp.dot(mix.astype(BF16), wout_ref[...], preferred_element_type=F32)
    xn = x_ref[0] + gate * y
    if final:
        xn = _rms_head(xn, fg_ref[...])
    xo_ref[0] = xn


def _lat_odd_mix(x, mod, fs, ucs, gsilu, owout, fg, l):
    b, s, _ = x.shape
    return pl.pallas_call(
        functools.partial(_lat_odd_mix_kernel, final=l == DEPTH - 1),
        grid=(b, s // ROWS),
        in_specs=[
            pl.BlockSpec((1, ROWS, D_MODEL), lambda i, n: (i, n, 0)),
            _layer_spec((3, COND_ROWS, D_MODEL), l),
            pl.BlockSpec((ROWS, 2 * s), lambda i, n: (n, 0)),
            pl.BlockSpec((1, 2 * s, D_MODEL), lambda i, n: (i, 0, 0)),
            pl.BlockSpec((1, ROWS, D_MODEL), lambda i, n: (i, n, 0)),
            _layer_spec((D_MODEL, D_MODEL), l // 2),
            pl.BlockSpec((1, D_MODEL), lambda i, n: (0, 0)),
        ],
        out_specs=pl.BlockSpec((1, ROWS, D_MODEL), lambda i, n: (i, n, 0)),
        out_shape=jax.ShapeDtypeStruct(x.shape, F32),
        compiler_params=pltpu.CompilerParams(
            dimension_semantics=("parallel", "parallel"), vmem_limit_bytes=VMEM_LIMIT),
        name="lat_odd_mix",
    )(x, mod, fs, ucs.reshape(b, 2 * s, D_MODEL), gsilu, owout, fg)


def kernel(x_prompt, x_sample, cache_k_a, cache_v_a, cache_k_b, cache_v_b, c, c_ctx, norm_g, ada_w, ada_b,
           even_w_in, even_w_out, qk_g_q, qk_g_k, sink_logit, odd_w_in, odd_w_out, final_g):
    batch, seq, _ = x_prompt.shape
    dec_batch, dec_seq, _ = x_sample.shape
    n_even = even_w_in.shape[0]
    past = cache_k_a.shape[2]

    cond = jnp.concatenate(
        [c_ctx[None, :], c, jnp.zeros((COND_ROWS - 1 - dec_batch, D_MODEL), F32)], axis=0)
    mod = _ada_all(cond, ada_w, ada_b)

    ccs = jnp.asarray(_channel_dft()).astype(BF16)
    fs_ctx = jnp.asarray(_position_dft(seq)).astype(BF16)
    fs_lat = jnp.asarray(_position_dft(dec_seq)).astype(BF16)
    cos, sn, sp = (jnp.asarray(t) for t in _rope_tables(dec_seq))

    caches = [a.reshape(dec_batch, n_even, past, KV_W) for a in (cache_k_a, cache_v_a, cache_k_b, cache_v_b)]
    fg = final_g.reshape(1, D_MODEL)
    gq2 = jnp.tile(qk_g_q, (1, 2))
    gk2 = jnp.tile(qk_g_k, (1, 2))

    xc, *outs = _ctx_path(x_prompt, mod, norm_g, even_w_in, even_w_out, odd_w_in, odd_w_out, gq2, gk2,
                          sink_logit, ccs, fs_ctx, fg)
    new_kv = [jnp.transpose(a, (0, 1, 4, 2, 3)) for a in outs[:4]]
    ewin, ewout, owin, owout = outs[4:]

    xl = x_sample
    g3 = norm_g.reshape(DEPTH, 1, D_MODEL)
    gq3, gk3 = gq2.reshape(n_even, 1, 128), gk2.reshape(n_even, 1, 128)
    for l in range(DEPTH):
        if l % 2 == 0:
            p = _lat_even_proj(xl, mod, g3, ewin, gq3, gk3, cos, sn, sp, l)
            xl = _lat_even_attn(xl, mod, p, *caches, ewout, sink_logit, l)
        else:
            ucs, gsilu = _lat_odd_proj(xl, mod, g3, owin, ccs, l)
            xl = _lat_odd_mix(xl, mod, fs_lat, ucs, gsilu, owout, fg, l)
    return (xc, xl, *new_kv)
```

```python
import numpy as np
import jax
import jax.numpy as jnp
from jax import lax
from jax.experimental import pallas as pl
from jax.experimental.pallas import tpu as pltpu

D_MODEL = 1024
DEPTH = 4
HEAD_DIM = 64
N_HEADS = 8
N_KV = 2
GROUP = N_HEADS // N_KV
MIX_W = N_HEADS * HEAD_DIM
KV_W = N_KV * HEAD_DIM
EVEN_IN = 2 * (2 * MIX_W + 2 * KV_W)
ODD_IN = 2 * D_MODEL
GRID_W = 64
WINDOW = 128
ROPE_BASE = 10000.0
C_GROUPS = 4
C_GROUP_DIM = D_MODEL // C_GROUPS
EPS = 1e-6
NEG_BIG = -1e30
ROWS = 256
COND_ROWS = 8
VMEM_LIMIT = 48 * 1024 * 1024
CTX_VMEM_LIMIT = 56 * 1024 * 1024
LAT_VMEM_LIMIT = 56 * 1024 * 1024
STAGE_ROWS = 128
STAGE_SLOTS = 4
CTX_BATCH_PER_STEP = 2
LAT_UNROLL = 2

QA, KA, VA, GA = 0, 512, 640, 768
QB, KB, VB, GB = 1280, 1792, 1920, 2048
LQA, LGA, LQB, LGB = 0, 512, 1024, 1536

F32 = jnp.float32
BF16 = jnp.bfloat16


def _dft_tables(n):
    k = np.arange(n, dtype=np.int64)
    ang = ((k[:, None] * k[None, :]) % n).astype(np.float64) * (2.0 * np.pi / n)
    return np.cos(ang).astype(np.float32), np.sin(ang).astype(np.float32)


def _channel_dft():
    c, s = _dft_tables(C_GROUP_DIM)
    return np.concatenate([c, s], axis=1)


def _position_dft(n):
    c, s = _dft_tables(n)
    return np.concatenate([c, -s], axis=1)


def _rope_tables(n_tok):
    rows = n_tok // GRID_W
    row = np.repeat(np.arange(rows), GRID_W).astype(np.float64)
    col = np.tile(np.arange(GRID_W), rows).astype(np.float64)
    half = HEAD_DIM // 2
    inv = 1.0 / (ROPE_BASE ** (np.arange(0, half, 2, dtype=np.float64) / half))
    ang_r = row[:, None] * inv
    ang_c = col[:, None] * inv
    zeros = np.zeros_like(ang_r)
    cos_h = np.concatenate([np.cos(ang_r), np.cos(ang_r), np.cos(ang_c), np.cos(ang_c)], axis=1)
    nxt_h = np.concatenate([-np.sin(ang_r), zeros, -np.sin(ang_c), zeros], axis=1)
    prv_h = np.concatenate([zeros, np.sin(ang_r), zeros, np.sin(ang_c)], axis=1)
    two = lambda t: np.concatenate([t, t], axis=1).astype(np.float32)
    return two(cos_h), two(nxt_h), two(prv_h)


def _silu(x):
    return x / (1.0 + jnp.exp(-x))


def _norm_mod(x, g, shift, scale):
    ms = jnp.mean(x * x, axis=-1, keepdims=True)
    return x * lax.rsqrt(ms + EPS) * (g * (1.0 + scale)) + shift


def _rms_head(xh, g):
    ms = jnp.mean(xh * xh, axis=-1, keepdims=True)
    return xh * lax.rsqrt(ms + EPS) * g


def _rms_pair(xc, g2):
    lo = lax.broadcasted_iota(jnp.int32, xc.shape, 1) < HEAD_DIM
    ss = xc * xc
    s_lo = jnp.sum(jnp.where(lo, ss, 0.0), axis=-1, keepdims=True)
    s_hi = jnp.sum(jnp.where(lo, 0.0, ss), axis=-1, keepdims=True)
    ms = jnp.where(lo, s_lo, s_hi) * (1.0 / HEAD_DIM)
    return xc * lax.rsqrt(ms + EPS) * g2


def _rope(xc, cos, sin_next, sin_prev):
    nxt = pltpu.roll(xc, 128 - 16, 1)
    prv = pltpu.roll(xc, 16, 1)
    return xc * cos + nxt * sin_next + prv * sin_prev


DEN_ROWS = 16
LOG2E = float(np.log2(np.e))
Q_SCALE = (HEAD_DIM ** -0.5) * LOG2E


def _sink_pair(sink_ref, i, head):
    return sink_ref[i, head] * LOG2E, sink_ref[i, head + 1] * LOG2E


def _kv_operands(k2, v2t, j):
    tk = k2.shape[0]
    low = lax.broadcasted_iota(jnp.int32, k2.shape, 1) < HEAD_DIM
    km = jnp.where(low if j == 0 else jnp.logical_not(low), k2, 0.0)
    kr = pltpu.roll(km, HEAD_DIM, 1)
    k_lo, k_hi = (km, kr) if j == 0 else (kr, km)
    k_cat = jnp.concatenate([k_lo, k_hi], axis=0).astype(BF16)
    vjt = v2t[HEAD_DIM * j:HEAD_DIM * (j + 1), :]
    zero = jnp.zeros_like(vjt)
    row = lax.broadcasted_iota(jnp.int32, (DEN_ROWS, 2 * tk), 0)
    col = lax.broadcasted_iota(jnp.int32, (DEN_ROWS, 2 * tk), 1)
    ones = jnp.where(((row == 0) & (col < tk)) | ((row == 1) & (col >= tk)), 1.0, 0.0)
    w_t = jnp.concatenate([jnp.concatenate([vjt, zero], axis=1),
                           jnp.concatenate([zero, vjt], axis=1), ones], axis=0).astype(BF16)
    return k_cat, w_t


def _scores_t(qc, k_cat):
    return lax.dot_general(k_cat, qc.astype(BF16), (((1,), (1,)), ((), ())), preferred_element_type=F32)


def _softmax_t(s_t, sinks=None, mask_t=None):
    tk = s_t.shape[0] // 2
    es, ms = [], []
    for hh in range(2):
        sh = s_t[hh * tk:(hh + 1) * tk]
        if mask_t is not None:
            sh = jnp.where(mask_t, sh, NEG_BIG)
        m = jnp.max(sh, axis=0, keepdims=True)
        if sinks is not None:
            m = jnp.maximum(m, sinks[hh])
        es.append(jnp.exp2(sh - m))
        ms.append(m)
    return jnp.concatenate(es, axis=0).astype(BF16), ms


def _values_t(e_t, w_t):
    return jnp.dot(w_t, e_t, preferred_element_type=F32)


def _normalise_t(nd, ms, sinks=None):
    dens = [nd[2 * HEAD_DIM + hh:2 * HEAD_DIM + hh + 1] for hh in range(2)]
    if sinks is not None:
        dens = [dens[hh] + jnp.exp2(sinks[hh] - ms[hh]) for hh in range(2)]
    o_t = jnp.concatenate([nd[0:HEAD_DIM] / dens[0], nd[HEAD_DIM:2 * HEAD_DIM] / dens[1]], axis=0)
    return o_t.T


def _run_skewed(items, stages):
    state = list(items)
    for t in range(len(items) + len(stages) - 1):
        for k, stage in enumerate(stages):
            if 0 <= t - k < len(items):
                state[t - k] = stage(state[t - k])


def _attention_chunks(items, read_q, read_gate, write_out):
    def scores(it):
        return dict(it, s_t=_scores_t(read_q(it), it["k_cat"]))

    def softmax(it):
        e_t, ms = _softmax_t(it["s_t"], it["sinks"], it["mask"])
        return dict(it, e_t=e_t, ms=ms, s_t=None)

    def values(it):
        return dict(it, nd=_values_t(it["e_t"], it["w_t"]), e_t=None)

    def finish(it):
        o = _normalise_t(it["nd"], it["ms"], it["sinks"])
        write_out(it, (o * read_gate(it)).astype(BF16))
        return None

    _run_skewed(items, (scores, softmax, values, finish))


def _ada_kernel(cond_ref, w_ref, b_ref, o_ref):
    a = _silu(cond_ref[...]).astype(BF16)
    o_ref[...] = jnp.dot(a, w_ref[...].astype(BF16), preferred_element_type=F32) + b_ref[...]


def _ada_all(cond, ada_w, ada_b):
    return pl.pallas_call(
        _ada_kernel,
        grid=(DEPTH, 3),
        in_specs=[
            pl.BlockSpec((COND_ROWS, D_MODEL), lambda l, p: (0, 0)),
            pl.BlockSpec((None, D_MODEL, D_MODEL), lambda l, p: (l, 0, p)),
            pl.BlockSpec((None, None, 1, D_MODEL), lambda l, p: (l, p, 0, 0)),
        ],
        out_specs=pl.BlockSpec((None, None, COND_ROWS, D_MODEL), lambda l, p: (l, p, 0, 0)),
        out_shape=jax.ShapeDtypeStruct((DEPTH, 3, COND_ROWS, D_MODEL), F32),
        compiler_params=pltpu.CompilerParams(
            dimension_semantics=("parallel", "parallel"), vmem_limit_bytes=VMEM_LIMIT),
        name="ada_mod",
    )(cond, ada_w, ada_b.reshape(DEPTH, 3, 1, D_MODEL))


def _ctx_even_mix(i, sink_ref, gq_ref, gk_ref, kv_refs, p_ref, o_ref):
    n_b, seq, _ = p_ref.shape
    gq2 = gq_ref[i:i + 1, :] * Q_SCALE
    gk2 = gk_ref[i:i + 1, :]
    items = []
    for mixer, (q0, k0, v0, g0) in enumerate(((QA, KA, VA, GA), (QB, KB, VB, GB))):
        kv = []
        for bb in range(n_b):
            k2 = p_ref[bb, :, k0:k0 + KV_W]
            v2 = p_ref[bb, :, v0:v0 + KV_W]
            if mixer == 0:
                k2 = _rms_pair(k2, gk2)
            v2t = v2.T
            kv_refs[2 * mixer][bb, i] = k2.T.reshape(N_KV, HEAD_DIM, seq)
            kv_refs[2 * mixer + 1][bb, i] = v2t.reshape(N_KV, HEAD_DIM, seq)
            kv.append((k2, v2t))
        for j in range(N_KV):
            ops = [_kv_operands(k2, v2t, j) for k2, v2t in kv]
            for cc in range(GROUP // 2):
                c0 = 128 * (j * (GROUP // 2) + cc)
                sinks = None if mixer == 0 else _sink_pair(sink_ref, i, c0 // HEAD_DIM)
                for bb, (k_cat, w_t) in enumerate(ops):
                    items.append(dict(rows=bb, q=q0 + c0, g=g0 + c0, o=mixer * MIX_W + c0,
                                      norm=mixer == 0, sinks=sinks, mask=None, k_cat=k_cat, w_t=w_t))

    def read_q(it):
        qc = p_ref[it["rows"], :, it["q"]:it["q"] + 128]
        return _rms_pair(qc, gq2) if it["norm"] else qc * Q_SCALE

    def read_gate(it):
        return _silu(p_ref[it["rows"], :, it["g"]:it["g"] + 128])

    def write_out(it, value):
        o_ref[it["rows"], :, it["o"]:it["o"] + 128] = value

    _attention_chunks(items, read_q, read_gate, write_out)


def _ctx_odd_mix(ccs_ref, fs_ref, p_ref, o_ref, ucs_ref):
    n_b, seq, _ = p_ref.shape
    for bb in range(n_b):
        for grp in range(C_GROUPS):
            c0 = C_GROUP_DIM * grp
            t = jnp.dot(p_ref[bb, :, c0:c0 + C_GROUP_DIM].astype(BF16), ccs_ref[...],
                        preferred_element_type=F32)
            ucs_ref[bb, 0:seq, c0:c0 + C_GROUP_DIM] = t[:, :C_GROUP_DIM].astype(BF16)
            ucs_ref[bb, seq:2 * seq, c0:c0 + C_GROUP_DIM] = t[:, C_GROUP_DIM:].astype(BF16)
    scale = float(1.0 / np.sqrt(float(seq * C_GROUP_DIM)))
    for bb in range(n_b):
        r = jnp.dot(fs_ref[...], ucs_ref[bb], preferred_element_type=F32) * scale
        o_ref[bb] = (r * _silu(p_ref[bb, :, D_MODEL:2 * D_MODEL])).astype(BF16)


def _stage_copy(w_hbm, stage_ref, sem, chunk, slot):
    _, n_rows, cols = w_hbm.shape
    rows = stage_ref.shape[1]
    per_layer = n_rows // rows
    src = w_hbm.at[chunk // per_layer, pl.ds((chunk % per_layer) * rows, rows), :]
    return pltpu.make_async_copy(src, stage_ref.at[slot, :, pl.ds(0, cols)], sem.at[slot])


def _convert_weights(w_hbm, w_ref, stage_ref, in_sem):
    n_layers, n_rows, cols = w_hbm.shape
    n_slots, rows, _ = stage_ref.shape
    per_layer = n_rows // rows
    n_chunks = n_layers * per_layer
    for c in range(n_slots - 1):
        _stage_copy(w_hbm, stage_ref, in_sem, c, c).start()

    def body(c, carry):
        slot = c % n_slots
        ahead = c + n_slots - 1

        @pl.when(ahead < n_chunks)
        def _():
            _stage_copy(w_hbm, stage_ref, in_sem, ahead, ahead % n_slots).start()

        _stage_copy(w_hbm, stage_ref, in_sem, c, slot).wait()
        r0 = pl.multiple_of((c % per_layer) * rows, rows)
        w_ref[c // per_layer, pl.ds(r0, rows), :] = stage_ref[slot, :, 0:cols].astype(BF16)
        return carry

    lax.fori_loop(0, n_chunks, body, 0)


def _ctx_kernel(sink_ref, x_ref, mod_ref, g_ref, ewin_hbm, ewout_hbm, owin_hbm, owout_hbm, gq_ref, gk_ref,
                ccs_ref, fs_ref, fg_ref,
                xo_ref, ka_ref, va_ref, kb_ref, vb_ref, ewin_o, ewout_o, owin_o, owout_o,
                xs_ref, p_ref, o_ref, ucs_ref, ewin_ref, ewout_ref, owin_ref, owout_ref,
                stage_ref, in_sem, out_sem):
    weights = ((ewin_hbm, ewin_o, ewin_ref), (ewout_hbm, ewout_o, ewout_ref),
               (owin_hbm, owin_o, owin_ref), (owout_hbm, owout_o, owout_ref))
    exports = [pltpu.make_async_copy(res, dst, out_sem.at[k]) for k, (_, dst, res) in enumerate(weights)]

    @pl.when(pl.program_id(0) == 0)
    def _():
        for src, _, res in weights:
            _convert_weights(src, res, stage_ref, in_sem)
        for export in exports:
            export.start()

    @pl.when(pl.program_id(0) == pl.num_programs(0) - 1)
    def _():
        for export in exports:
            export.wait()

    n_b = x_ref.shape[0]
    for l in range(DEPTH):
        i = l // 2
        even = l % 2 == 0
        win_ref, wout_ref, width = (ewin_ref, ewout_ref, EVEN_IN) if even else (owin_ref, owout_ref, ODD_IN)
        shift, scale, gate = (mod_ref[l, part, 0:1, :] for part in range(3))
        for bb in range(n_b):
            x = x_ref[bb] if l == 0 else xs_ref[bb]
            h = _norm_mod(x, g_ref[l:l + 1, :], shift, scale)
            p_ref[bb, :, 0:width] = jnp.dot(h.astype(BF16), win_ref[i], preferred_element_type=F32)
        if even:
            _ctx_even_mix(i, sink_ref, gq_ref, gk_ref, (ka_ref, va_ref, kb_ref, vb_ref), p_ref, o_ref)
        else:
            _ctx_odd_mix(ccs_ref, fs_ref, p_ref, o_ref, ucs_ref)
        for bb in range(n_b):
            y = jnp.dot(o_ref[bb], wout_ref[i], preferred_element_type=F32)
            xn = (x_ref[bb] if l == 0 else xs_ref[bb]) + gate * y
            if l == DEPTH - 1:
                xo_ref[bb] = _rms_head(xn, fg_ref[...])
            else:
                xs_ref[bb] = xn


def _resident(shape):
    return pl.BlockSpec(shape, lambda i: (0,) * len(shape), pipeline_mode=pl.Buffered(1))


def _ctx_path(x, mod, norm_g, ewin, ewout, owin, owout, gq2, gk2, sink, ccs, fs, fg):
    b, s, _ = x.shape
    n_even = ewin.shape[0]
    bb = CTX_BATCH_PER_STEP
    kv_shape = jax.ShapeDtypeStruct((b, n_even, N_KV, HEAD_DIM, s), F32)
    kv_spec = pl.BlockSpec((bb, n_even, N_KV, HEAD_DIM, s), lambda i: (i, 0, 0, 0, 0))
    x_spec = pl.BlockSpec((bb, s, D_MODEL), lambda i: (i, 0, 0))
    hbm = pl.BlockSpec(memory_space=pl.ANY)
    weights = (ewin, ewout, owin, owout)
    return pl.pallas_call(
        _ctx_kernel,
        grid=(b // bb,),
        in_specs=[
            pl.BlockSpec(memory_space=pltpu.SMEM),
            x_spec,
            _resident(mod.shape), _resident(norm_g.shape),
            hbm, hbm, hbm, hbm,
            _resident(gq2.shape), _resident(gk2.shape),
            _resident(ccs.shape), _resident(fs.shape), _resident(fg.shape),
        ],
        out_specs=[x_spec, kv_spec, kv_spec, kv_spec, kv_spec, hbm, hbm, hbm, hbm],
        out_shape=[jax.ShapeDtypeStruct(x.shape, F32), kv_shape, kv_shape, kv_shape, kv_shape]
                  + [jax.ShapeDtypeStruct(w.shape, BF16) for w in weights],
        scratch_shapes=[pltpu.VMEM((bb, s, D_MODEL), F32), pltpu.VMEM((bb, s, EVEN_IN), F32),
                        pltpu.VMEM((bb, s, D_MODEL), BF16), pltpu.VMEM((bb, 2 * s, D_MODEL), BF16)]
                       + [pltpu.VMEM(w.shape, BF16) for w in weights]
                       + [pltpu.VMEM((STAGE_SLOTS, STAGE_ROWS, max(w.shape[2] for w in weights)), F32),
                          pltpu.SemaphoreType.DMA((STAGE_SLOTS,)), pltpu.SemaphoreType.DMA((len(weights),))],
        compiler_params=pltpu.CompilerParams(
            dimension_semantics=("arbitrary",), vmem_limit_bytes=CTX_VMEM_LIMIT),
        name="ctx_path",
    )(sink, x, mod, norm_g, *weights, gq2, gk2, ccs, fs, fg)


def _lat_weight_copies(l, w_hbms, win_buf, wout_buf, w_sem):
    ewin_hbm, ewout_hbm, owin_hbm, owout_hbm = w_hbms
    i, slot = l // 2, l % 2
    w_in, w_out = (ewin_hbm, ewout_hbm) if l % 2 == 0 else (owin_hbm, owout_hbm)
    return (pltpu.make_async_copy(w_in.at[i], win_buf.at[slot, :, pl.ds(0, w_in.shape[2])], w_sem.at[2 * slot]),
            pltpu.make_async_copy(w_out.at[i], wout_buf.at[slot], w_sem.at[2 * slot + 1]))


def _lat_attention_items(n, u, layer_i, sink_ref, kv_ref, kcat_ref, wt_ref, cache_refs):
    seq = kv_ref.shape[1]
    ckb, cvb = cache_refs[2][layer_i], cache_refs[3][layer_i]
    prev0 = pl.multiple_of(jnp.maximum(n * ROWS - WINDOW, 0), WINDOW)
    own0 = pl.multiple_of(n * ROWS, ROWS)
    next0 = pl.multiple_of(jnp.minimum(n * ROWS + ROWS, seq - WINDOW), WINDOW)
    win_len = ROWS + 2 * WINDOW
    ctx_len = ckb.shape[0]
    cj = lax.broadcasted_iota(jnp.int32, (win_len + ctx_len, ROWS), 0)
    qi = lax.broadcasted_iota(jnp.int32, (win_len + ctx_len, ROWS), 1)
    kpos = n * ROWS - WINDOW + cj
    in_win = (jnp.abs(cj - WINDOW - qi) <= WINDOW) & (kpos >= 0) & (kpos < seq)
    mask_b = in_win | (cj >= win_len)

    def window(idx, cache):
        return jnp.concatenate([kv_ref[idx, pl.ds(prev0, WINDOW), :], kv_ref[idx, pl.ds(own0, ROWS), :],
                                kv_ref[idx, pl.ds(next0, WINDOW), :], cache], axis=0)

    k2b, v2bt = window(2, ckb), window(3, cvb).T
    items = []
    for mixer, (q0, g0) in enumerate(((LQA, LGA), (LQB, LGB))):
        for j in range(N_KV):
            k_cat, w_t = (kcat_ref[j], wt_ref[j]) if mixer == 0 else _kv_operands(k2b, v2bt, j)
            for cc in range(GROUP // 2):
                c0 = 128 * (j * (GROUP // 2) + cc)
                if mixer == 0:
                    sinks, mask = None, None
                else:
                    sinks, mask = _sink_pair(sink_ref, layer_i, c0 // HEAD_DIM), mask_b
                items.append(dict(u=u, q=q0 + c0, g=g0 + c0, o=mixer * MIX_W + c0, sinks=sinks, mask=mask,
                                  k_cat=k_cat, w_t=w_t))
    return items


def _lat_kernel(sink_ref, x_hbm, mod_ref, g_ref, ewin_hbm, ewout_hbm, owin_hbm, owout_hbm, gq_ref, gk_ref,
                cos_ref, sn_ref, sp_ref, cka_ref, cva_ref, ckb_ref, cvb_ref, ccs_ref, fs_ref, fg_ref,
                y_hbm,
                xw_ref, win_buf, wout_buf, kv_ref, kcat_ref, wt_ref, p_ref, o_ref, ucs_ref, gate_ref,
                x_sem, w_sem):
    b = pl.program_id(0)
    seq = xw_ref.shape[0]
    n_blocks = seq // ROWS
    w_hbms = (ewin_hbm, ewout_hbm, owin_hbm, owout_hbm)

    x_in = pltpu.make_async_copy(x_hbm.at[b], xw_ref, x_sem.at[0])
    x_in.start()
    for copy in _lat_weight_copies(0, w_hbms, win_buf, wout_buf, w_sem):
        copy.start()
    x_in.wait()

    for l in range(DEPTH):
        i, slot = l // 2, l % 2
        for copy in _lat_weight_copies(l, w_hbms, win_buf, wout_buf, w_sem):
            copy.wait()
        if l + 1 < DEPTH:
            for copy in _lat_weight_copies(l + 1, w_hbms, win_buf, wout_buf, w_sem):
                copy.start()
        shift, scale, gate = (mod_ref[l, part, pl.ds(1 + b, 1), :] for part in range(3))
        g = g_ref[l:l + 1, :]

        def normed(rows):
            return _norm_mod(xw_ref[rows, :], g, shift, scale).astype(BF16)

        def w_in(c0, width):
            return win_buf[slot, :, c0:c0 + width]

        if l % 2 == 0:
            gq2 = gq_ref[i:i + 1, :] * Q_SCALE
            gk2 = gk_ref[i:i + 1, :]

            def project_kv(n, carry):
                rows = pl.ds(pl.multiple_of(n * ROWS, ROWS), ROWS)
                h = normed(rows)
                cos, sn, sp = cos_ref[rows, :], sn_ref[rows, :], sp_ref[rows, :]
                for mixer, c0 in enumerate((KA, KB)):
                    kv = jnp.dot(h, w_in(c0, 2 * KV_W), preferred_element_type=F32)
                    k2 = kv[:, :KV_W]
                    if mixer == 0:
                        k2 = _rms_pair(k2, gk2)
                    kv_ref[2 * mixer, rows, :] = _rope(k2, cos, sn, sp)
                    kv_ref[2 * mixer + 1, rows, :] = kv[:, KV_W:]
                return carry

            lax.fori_loop(0, n_blocks, project_kv, 0, unroll=LAT_UNROLL)
            k2a = jnp.concatenate([kv_ref[0], cka_ref[i]], axis=0)
            v2at = jnp.concatenate([kv_ref[1], cva_ref[i]], axis=0).T
            for j in range(N_KV):
                kcat_ref[j], wt_ref[j] = _kv_operands(k2a, v2at, j)

            def attend(m, carry):
                blocks = [m * LAT_UNROLL + u for u in range(LAT_UNROLL)]
                block_rows = [pl.ds(pl.multiple_of(n * ROWS, ROWS), ROWS) for n in blocks]
                for u, rows in enumerate(block_rows):
                    h = normed(rows)
                    cos, sn, sp = cos_ref[rows, :], sn_ref[rows, :], sp_ref[rows, :]
                    qa = jnp.dot(h, w_in(QA, MIX_W), preferred_element_type=F32)
                    for c0 in range(0, MIX_W, 128):
                        p_ref[u, :, LQA + c0:LQA + c0 + 128] = _rope(_rms_pair(qa[:, c0:c0 + 128], gq2),
                                                                     cos, sn, sp)
                    mid = jnp.dot(h, w_in(GA, 2 * MIX_W), preferred_element_type=F32)
                    p_ref[u, :, LGA:LGA + MIX_W] = _silu(mid[:, :MIX_W])
                    for c0 in range(0, MIX_W, 128):
                        qb = mid[:, MIX_W + c0:MIX_W + c0 + 128] * Q_SCALE
                        p_ref[u, :, LQB + c0:LQB + c0 + 128] = _rope(qb, cos, sn, sp)
                    p_ref[u, :, LGB:LGB + MIX_W] = _silu(
                        jnp.dot(h, w_in(GB, MIX_W), preferred_element_type=F32))
                per_block = [_lat_attention_items(n, u, i, sink_ref, kv_ref, kcat_ref, wt_ref,
                                                  (cka_ref, cva_ref, ckb_ref, cvb_ref))
                             for u, n in enumerate(blocks)]
                items = [it for group in zip(*per_block) for it in group]

                def write_out(it, value):
                    o_ref[it["u"], :, it["o"]:it["o"] + 128] = value

                _attention_chunks(items, lambda it: p_ref[it["u"], :, it["q"]:it["q"] + 128],
                                  lambda it: p_ref[it["u"], :, it["g"]:it["g"] + 128], write_out)
                for u, rows in enumerate(block_rows):
                    y = jnp.dot(o_ref[u], wout_buf[slot], preferred_element_type=F32)
                    xw_ref[rows, :] = xw_ref[rows, :] + gate * y
                return carry

            lax.fori_loop(0, n_blocks // LAT_UNROLL, attend, 0)
        else:
            def project(n, carry):
                start = pl.multiple_of(n * ROWS, ROWS)
                rows = pl.ds(start, ROWS)
                sin_rows = pl.ds(pl.multiple_of(seq + start, ROWS), ROWS)
                p = jnp.dot(normed(rows), w_in(0, ODD_IN), preferred_element_type=F32)
                for grp in range(C_GROUPS):
                    c0 = C_GROUP_DIM * grp
                    t = jnp.dot(p[:, c0:c0 + C_GROUP_DIM].astype(BF16), ccs_ref[...],
                                preferred_element_type=F32)
                    ucs_ref[rows, c0:c0 + C_GROUP_DIM] = t[:, :C_GROUP_DIM].astype(BF16)
                    ucs_ref[sin_rows, c0:c0 + C_GROUP_DIM] = t[:, C_GROUP_DIM:].astype(BF16)
                gate_ref[rows, :] = _silu(p[:, D_MODEL:ODD_IN])
                return carry

            lax.fori_loop(0, n_blocks, project, 0, unroll=LAT_UNROLL)
            dft_scale = float(1.0 / np.sqrt(float(seq * C_GROUP_DIM)))

            def mix(n, carry):
                rows = pl.ds(pl.multiple_of(n * ROWS, ROWS), ROWS)
                r = jnp.dot(fs_ref[rows, :], ucs_ref[...], preferred_element_type=F32) * dft_scale
                m = (r * gate_ref[rows, :]).astype(BF16)
                y = jnp.dot(m, wout_buf[slot], preferred_element_type=F32)
                xn = xw_ref[rows, :] + gate * y
                if l == DEPTH - 1:
                    xn = _rms_head(xn, fg_ref[...])
                xw_ref[rows, :] = xn
                return carry

            lax.fori_loop(0, n_blocks, mix, 0, unroll=LAT_UNROLL)

    y_out = pltpu.make_async_copy(xw_ref, y_hbm.at[b], x_sem.at[1])
    y_out.start()
    y_out.wait()


def _lat_path(x, mod, norm_g, ewin, ewout, owin, owout, gq2, gk2, sink, cos, sn, sp, caches, ccs, fs, fg):
    b, s, _ = x.shape
    n_even, past = caches[0].shape[1], caches[0].shape[2]
    hbm = pl.BlockSpec(memory_space=pl.ANY)
    cache_spec = pl.BlockSpec((None, n_even, past, KV_W), lambda i: (i, 0, 0, 0))
    return pl.pallas_call(
        _lat_kernel,
        grid=(b,),
        in_specs=[
            pl.BlockSpec(memory_space=pltpu.SMEM),
            hbm,
            _resident(mod.shape), _resident(norm_g.shape),
            hbm, hbm, hbm, hbm,
            _resident(gq2.shape), _resident(gk2.shape),
            _resident(cos.shape), _resident(sn.shape), _resident(sp.shape),
            cache_spec, cache_spec, cache_spec, cache_spec,
            _resident(ccs.shape), _resident(fs.shape), _resident(fg.shape),
        ],
        out_specs=hbm,
        out_shape=jax.ShapeDtypeStruct(x.shape, F32),
        scratch_shapes=[
            pltpu.VMEM((s, D_MODEL), F32),
            pltpu.VMEM((2, D_MODEL, EVEN_IN), BF16),
            pltpu.VMEM((2, D_MODEL, D_MODEL), BF16),
            pltpu.VMEM((4, s, KV_W), F32),
            pltpu.VMEM((N_KV, 2 * (s + past), KV_W), BF16),
            pltpu.VMEM((N_KV, 2 * HEAD_DIM + DEN_ROWS, 2 * (s + past)), BF16),
            pltpu.VMEM((LAT_UNROLL, ROWS, 4 * MIX_W), F32),
            pltpu.VMEM((LAT_UNROLL, ROWS, D_MODEL), BF16),
            pltpu.VMEM((2 * s, D_MODEL), BF16),
            pltpu.VMEM((s, D_MODEL), F32),
            pltpu.SemaphoreType.DMA((2,)), pltpu.SemaphoreType.DMA((4,)),
        ],
        compiler_params=pltpu.CompilerParams(
            dimension_semantics=("arbitrary",), vmem_limit_bytes=LAT_VMEM_LIMIT),
        name="lat_path",
    )(sink, x, mod, norm_g, ewin, ewout, owin, owout, gq2, gk2, cos, sn, sp, *caches, ccs, fs, fg)


def kernel(x_prompt, x_sample, cache_k_a, cache_v_a, cache_k_b, cache_v_b, c, c_ctx, norm_g, ada_w, ada_b,
           even_w_in, even_w_out, qk_g_q, qk_g_k, sink_logit, odd_w_in, odd_w_out, final_g):
    batch, seq, _ = x_prompt.shape
    dec_batch, dec_seq, _ = x_sample.shape
    n_even = even_w_in.shape[0]
    past = cache_k_a.shape[2]

    cond = jnp.concatenate(
        [c_ctx[None, :], c, jnp.zeros((COND_ROWS - 1 - dec_batch, D_MODEL), F32)], axis=0)
    mod = _ada_all(cond, ada_w, ada_b)

    ccs = jnp.asarray(_channel_dft()).astype(BF16)
    fs_ctx = jnp.asarray(_position_dft(seq)).astype(BF16)
    fs_lat = jnp.asarray(_position_dft(dec_seq)).astype(BF16)
    cos, sn, sp = (jnp.asarray(t) for t in _rope_tables(dec_seq))

    caches = [a.reshape(dec_batch, n_even, past, KV_W) for a in (cache_k_a, cache_v_a, cache_k_b, cache_v_b)]
    fg = final_g.reshape(1, D_MODEL)
    gq2 = jnp.tile(qk_g_q, (1, 2))
    gk2 = jnp.tile(qk_g_k, (1, 2))

    xc, *outs = _ctx_path(x_prompt, mod, norm_g, even_w_in, even_w_out, odd_w_in, odd_w_out, gq2, gk2,
                          sink_logit, ccs, fs_ctx, fg)
    new_kv = [jnp.transpose(a, (0, 1, 4, 2, 3)) for a in outs[:4]]
    ewin, ewout, owin, owout = outs[4:]

    xl = _lat_path(x_sample, mod, norm_g, ewin, ewout, owin, owout, gq2, gk2, sink_logit, cos, sn, sp,
                   caches, ccs, fs_lat, fg)
    return (xc, xl, *new_kv)
```

```python
import numpy as np
import jax
import jax.numpy as jnp
from jax import lax
from jax.experimental import pallas as pl
from jax.experimental.pallas import tpu as pltpu

D_MODEL = 1024
DEPTH = 4
HEAD_DIM = 64
N_HEADS = 8
N_KV = 2
GROUP = N_HEADS // N_KV
MIX_W = N_HEADS * HEAD_DIM
KV_W = N_KV * HEAD_DIM
EVEN_IN = 2 * (2 * MIX_W + 2 * KV_W)
ODD_IN = 2 * D_MODEL
GRID_W = 64
WINDOW = 128
ROPE_BASE = 10000.0
C_GROUPS = 4
C_GROUP_DIM = D_MODEL // C_GROUPS
EPS = 1e-6
NEG_BIG = -1e30
ROWS = 256
COND_ROWS = 8
VMEM_LIMIT = 48 * 1024 * 1024
CTX_VMEM_LIMIT = 56 * 1024 * 1024
LAT_VMEM_LIMIT = 56 * 1024 * 1024
STAGE_ROWS = 128
STAGE_SLOTS = 4
CTX_BATCH_PER_STEP = 2
LAT_UNROLL = 1

QA, KA, VA, GA = 0, 512, 640, 768
QB, KB, VB, GB = 1280, 1792, 1920, 2048
LQA, LGA, LQB, LGB = 0, 512, 1024, 1536

F32 = jnp.float32
BF16 = jnp.bfloat16


def _dft_tables(n):
    k = np.arange(n, dtype=np.int64)
    ang = ((k[:, None] * k[None, :]) % n).astype(np.float64) * (2.0 * np.pi / n)
    return np.cos(ang).astype(np.float32), np.sin(ang).astype(np.float32)


HALF_GROUP = C_GROUP_DIM // 2


def _packed_channel_dft():
    c, s = _dft_tables(C_GROUP_DIM)
    return np.concatenate([c[:, :HALF_GROUP], c[:, HALF_GROUP:HALF_GROUP + 1], s[:, 1:HALF_GROUP]], axis=1)


def _mirror_perm():
    t = np.array([0] + [HALF_GROUP - m for m in range(1, HALF_GROUP)])
    p = np.zeros((HALF_GROUP, HALF_GROUP), np.float32)
    p[t, np.arange(HALF_GROUP)] = 1.0
    return p


def _rope_tables(n_tok):
    rows = n_tok // GRID_W
    row = np.repeat(np.arange(rows), GRID_W).astype(np.float64)
    col = np.tile(np.arange(GRID_W), rows).astype(np.float64)
    half = HEAD_DIM // 2
    inv = 1.0 / (ROPE_BASE ** (np.arange(0, half, 2, dtype=np.float64) / half))
    ang_r = row[:, None] * inv
    ang_c = col[:, None] * inv
    zeros = np.zeros_like(ang_r)
    cos_h = np.concatenate([np.cos(ang_r), np.cos(ang_r), np.cos(ang_c), np.cos(ang_c)], axis=1)
    nxt_h = np.concatenate([-np.sin(ang_r), zeros, -np.sin(ang_c), zeros], axis=1)
    prv_h = np.concatenate([zeros, np.sin(ang_r), zeros, np.sin(ang_c)], axis=1)
    two = lambda t: np.concatenate([t, t], axis=1).astype(np.float32)
    return two(cos_h), two(nxt_h), two(prv_h)


def _silu(x):
    return x / (1.0 + jnp.exp(-x))


def _norm_mod(x, g, shift, scale):
    ms = jnp.mean(x * x, axis=-1, keepdims=True)
    return x * lax.rsqrt(ms + EPS) * (g * (1.0 + scale)) + shift


def _rms_head(xh, g):
    ms = jnp.mean(xh * xh, axis=-1, keepdims=True)
    return xh * lax.rsqrt(ms + EPS) * g


def _rms_pair(xc, g2):
    lo = lax.broadcasted_iota(jnp.int32, xc.shape, 1) < HEAD_DIM
    ss = xc * xc
    s_lo = jnp.sum(jnp.where(lo, ss, 0.0), axis=-1, keepdims=True)
    s_hi = jnp.sum(jnp.where(lo, 0.0, ss), axis=-1, keepdims=True)
    ms = jnp.where(lo, s_lo, s_hi) * (1.0 / HEAD_DIM)
    return xc * lax.rsqrt(ms + EPS) * g2


def _rope(xc, cos, sin_next, sin_prev):
    nxt = pltpu.roll(xc, 128 - 16, 1)
    prv = pltpu.roll(xc, 16, 1)
    return xc * cos + nxt * sin_next + prv * sin_prev


DEN_ROWS = 16
LOG2E = float(np.log2(np.e))
Q_SCALE = (HEAD_DIM ** -0.5) * LOG2E


def _sink_pair(sink_ref, i, head):
    return sink_ref[i, head] * LOG2E, sink_ref[i, head + 1] * LOG2E


def _kv_operands(k2, v2t, j):
    tk = k2.shape[0]
    low = lax.broadcasted_iota(jnp.int32, k2.shape, 1) < HEAD_DIM
    km = jnp.where(low if j == 0 else jnp.logical_not(low), k2, 0.0)
    kr = pltpu.roll(km, HEAD_DIM, 1)
    k_lo, k_hi = (km, kr) if j == 0 else (kr, km)
    k_cat = jnp.concatenate([k_lo, k_hi], axis=0).astype(BF16)
    vjt = v2t[HEAD_DIM * j:HEAD_DIM * (j + 1), :]
    zero = jnp.zeros_like(vjt)
    row = lax.broadcasted_iota(jnp.int32, (DEN_ROWS, 2 * tk), 0)
    col = lax.broadcasted_iota(jnp.int32, (DEN_ROWS, 2 * tk), 1)
    ones = jnp.where(((row == 0) & (col < tk)) | ((row == 1) & (col >= tk)), 1.0, 0.0)
    w_t = jnp.concatenate([jnp.concatenate([vjt, zero], axis=1),
                           jnp.concatenate([zero, vjt], axis=1), ones], axis=0).astype(BF16)
    return k_cat, w_t


def _scores_t(qc, k_cat):
    return lax.dot_general(k_cat, qc.astype(BF16), (((1,), (1,)), ((), ())), preferred_element_type=F32)


def _softmax_t(s_t, sinks=None, mask_t=None):
    tk = s_t.shape[0] // 2
    es, ms = [], []
    for hh in range(2):
        sh = s_t[hh * tk:(hh + 1) * tk]
        if mask_t is not None:
            sh = jnp.where(mask_t, sh, NEG_BIG)
        m = jnp.max(sh, axis=0, keepdims=True)
        if sinks is not None:
            m = jnp.maximum(m, sinks[hh])
        es.append(jnp.exp2(sh - m))
        ms.append(m)
    return jnp.concatenate(es, axis=0).astype(BF16), ms


def _values_t(e_t, w_t):
    return jnp.dot(w_t, e_t, preferred_element_type=F32)


def _normalise_t(nd, ms, sinks=None):
    dens = [nd[2 * HEAD_DIM + hh:2 * HEAD_DIM + hh + 1] for hh in range(2)]
    if sinks is not None:
        dens = [dens[hh] + jnp.exp2(sinks[hh] - ms[hh]) for hh in range(2)]
    o_t = jnp.concatenate([nd[0:HEAD_DIM] / dens[0], nd[HEAD_DIM:2 * HEAD_DIM] / dens[1]], axis=0)
    return o_t.T


def _run_skewed(items, stages):
    state = list(items)
    for t in range(len(items) + len(stages) - 1):
        for k, stage in enumerate(stages):
            if 0 <= t - k < len(items):
                state[t - k] = stage(state[t - k])


def _attention_chunks(items, read_q, read_gate, write_out):
    def scores(it):
        return dict(it, s_t=_scores_t(read_q(it), it["k_cat"]))

    def softmax(it):
        e_t, ms = _softmax_t(it["s_t"], it["sinks"], it["mask"])
        return dict(it, e_t=e_t, ms=ms, s_t=None)

    def values(it):
        return dict(it, nd=_values_t(it["e_t"], it["w_t"]), e_t=None)

    def finish(it):
        o = _normalise_t(it["nd"], it["ms"], it["sinks"])
        write_out(it, (o * read_gate(it)).astype(BF16))
        return None

    _run_skewed(items, (scores, softmax, values, finish))


def _ada_kernel(cond_ref, w_ref, b_ref, o_ref):
    a = _silu(cond_ref[...]).astype(BF16)
    o_ref[...] = jnp.dot(a, w_ref[...].astype(BF16), preferred_element_type=F32) + b_ref[...]


def _ada_all(cond, ada_w, ada_b):
    return pl.pallas_call(
        _ada_kernel,
        grid=(DEPTH, 3),
        in_specs=[
            pl.BlockSpec((COND_ROWS, D_MODEL), lambda l, p: (0, 0)),
            pl.BlockSpec((None, D_MODEL, D_MODEL), lambda l, p: (l, 0, p)),
            pl.BlockSpec((None, None, 1, D_MODEL), lambda l, p: (l, p, 0, 0)),
        ],
        out_specs=pl.BlockSpec((None, None, COND_ROWS, D_MODEL), lambda l, p: (l, p, 0, 0)),
        out_shape=jax.ShapeDtypeStruct((DEPTH, 3, COND_ROWS, D_MODEL), F32),
        compiler_params=pltpu.CompilerParams(
            dimension_semantics=("parallel", "parallel"), vmem_limit_bytes=VMEM_LIMIT),
        name="ada_mod",
    )(cond, ada_w, ada_b.reshape(DEPTH, 3, 1, D_MODEL))


def _ctx_even_mix(i, sink_ref, gq_ref, gk_ref, kv_refs, p_ref, o_ref):
    n_b, seq, _ = p_ref.shape
    gq2 = gq_ref[i:i + 1, :] * Q_SCALE
    gk2 = gk_ref[i:i + 1, :]
    items = []
    for mixer, (q0, k0, v0, g0) in enumerate(((QA, KA, VA, GA), (QB, KB, VB, GB))):
        kv = []
        for bb in range(n_b):
            k2 = p_ref[bb, :, k0:k0 + KV_W]
            v2 = p_ref[bb, :, v0:v0 + KV_W]
            if mixer == 0:
                k2 = _rms_pair(k2, gk2)
            v2t = v2.T
            kv_refs[2 * mixer][bb, i] = k2.T.reshape(N_KV, HEAD_DIM, seq)
            kv_refs[2 * mixer + 1][bb, i] = v2t.reshape(N_KV, HEAD_DIM, seq)
            kv.append((k2, v2t))
        for j in range(N_KV):
            ops = [_kv_operands(k2, v2t, j) for k2, v2t in kv]
            for cc in range(GROUP // 2):
                c0 = 128 * (j * (GROUP // 2) + cc)
                sinks = None if mixer == 0 else _sink_pair(sink_ref, i, c0 // HEAD_DIM)
                for bb, (k_cat, w_t) in enumerate(ops):
                    items.append(dict(rows=bb, q=q0 + c0, g=g0 + c0, o=mixer * MIX_W + c0,
                                      norm=mixer == 0, sinks=sinks, mask=None, k_cat=k_cat, w_t=w_t))

    def read_q(it):
        qc = p_ref[it["rows"], :, it["q"]:it["q"] + 128]
        return _rms_pair(qc, gq2) if it["norm"] else qc * Q_SCALE

    def read_gate(it):
        return _silu(p_ref[it["rows"], :, it["g"]:it["g"] + 128])

    def write_out(it, value):
        o_ref[it["rows"], :, it["o"]:it["o"] + 128] = value

    _attention_chunks(items, read_q, read_gate, write_out)


def _spectrum_operands(t):
    first = lax.broadcasted_iota(jnp.int32, (t.shape[0], HALF_GROUP), 1) == 0
    lower = [t[:, C_GROUP_DIM * g:C_GROUP_DIM * g + HALF_GROUP] for g in range(C_GROUPS)]
    upper = [t[:, C_GROUP_DIM * g + HALF_GROUP:C_GROUP_DIM * (g + 1)] for g in range(C_GROUPS)]
    for_cos = jnp.concatenate(lower + [jnp.where(first, u, 0.0) for u in upper], axis=1).astype(BF16)
    for_sin = jnp.concatenate([jnp.where(first, 0.0, u) for u in upper], axis=1).astype(BF16)
    return for_cos, for_sin


def _spectrum_combine(m_cos, m_sin, scale, read_gate, write_out):
    first = lax.broadcasted_iota(jnp.int32, (m_cos.shape[0], HALF_GROUP), 1) == 0
    for g in range(C_GROUPS):
        p = m_cos[:, HALF_GROUP * g:HALF_GROUP * (g + 1)]
        p_mid = m_cos[:, D_MODEL // 2 + HALF_GROUP * g:D_MODEL // 2 + HALF_GROUP * (g + 1)]
        q = m_sin[:, HALF_GROUP * g:HALF_GROUP * (g + 1)]
        c0 = C_GROUP_DIM * g
        write_out(c0, ((p - q) * scale * read_gate(c0)).astype(BF16))
        upper = jnp.where(first, p_mid, p + q) * scale
        write_out(c0 + HALF_GROUP, (upper * read_gate(c0 + HALF_GROUP)).astype(BF16))


def _ctx_odd_mix(cs_ref, ss_ref, p_ref, o_ref):
    n_b, seq, _ = p_ref.shape
    scale = float(1.0 / np.sqrt(float(seq * C_GROUP_DIM)))
    for bb in range(n_b):
        for_cos, for_sin = _spectrum_operands(p_ref[bb, :, 0:D_MODEL])
        m_cos = jnp.dot(cs_ref[...], for_cos, preferred_element_type=F32)
        m_sin = jnp.dot(ss_ref[...], for_sin, preferred_element_type=F32)

        def read_gate(c0, bb=bb):
            return _silu(p_ref[bb, :, D_MODEL + c0:D_MODEL + c0 + HALF_GROUP])

        def write_out(c0, value, bb=bb):
            o_ref[bb, :, c0:c0 + HALF_GROUP] = value

        _spectrum_combine(m_cos, m_sin, scale, read_gate, write_out)


def _fold_odd_weights(owin_ref, owout_ref, epack_ref, perm_ref):
    for layer in range(owin_ref.shape[0]):
        for g in range(C_GROUPS):
            c0 = C_GROUP_DIM * g
            owin_ref[layer, :, c0:c0 + C_GROUP_DIM] = jnp.dot(
                owin_ref[layer, :, c0:c0 + C_GROUP_DIM], epack_ref[...],
                preferred_element_type=F32).astype(BF16)
            g0 = D_MODEL + c0 + HALF_GROUP
            owin_ref[layer, :, g0:g0 + HALF_GROUP] = jnp.dot(
                owin_ref[layer, :, g0:g0 + HALF_GROUP], perm_ref[...], preferred_element_type=F32).astype(BF16)
            r0 = c0 + HALF_GROUP
            owout_ref[layer, r0:r0 + HALF_GROUP, :] = jnp.dot(
                perm_ref[...], owout_ref[layer, r0:r0 + HALF_GROUP, :], preferred_element_type=F32).astype(BF16)


def _stage_copy(w_hbm, stage_ref, sem, chunk, slot):
    _, n_rows, cols = w_hbm.shape
    rows = stage_ref.shape[1]
    per_layer = n_rows // rows
    src = w_hbm.at[chunk // per_layer, pl.ds((chunk % per_layer) * rows, rows), :]
    return pltpu.make_async_copy(src, stage_ref.at[slot, :, pl.ds(0, cols)], sem.at[slot])


def _convert_weights(w_hbm, w_ref, stage_ref, in_sem):
    n_layers, n_rows, cols = w_hbm.shape
    n_slots, rows, _ = stage_ref.shape
    per_layer = n_rows // rows
    n_chunks = n_layers * per_layer
    for c in range(n_slots - 1):
        _stage_copy(w_hbm, stage_ref, in_sem, c, c).start()

    def body(c, carry):
        slot = c % n_slots
        ahead = c + n_slots - 1

        @pl.when(ahead < n_chunks)
        def _():
            _stage_copy(w_hbm, stage_ref, in_sem, ahead, ahead % n_slots).start()

        _stage_copy(w_hbm, stage_ref, in_sem, c, slot).wait()
        r0 = pl.multiple_of((c % per_layer) * rows, rows)
        w_ref[c // per_layer, pl.ds(r0, rows), :] = stage_ref[slot, :, 0:cols].astype(BF16)
        return carry

    lax.fori_loop(0, n_chunks, body, 0)


def _ctx_kernel(sink_ref, x_ref, mod_ref, g_ref, ewin_hbm, ewout_hbm, owin_hbm, owout_hbm, gq_ref, gk_ref,
                epack_ref, perm_ref, cs_ref, ss_ref, fg_ref,
                xo_ref, ka_ref, va_ref, kb_ref, vb_ref, ewin_o, ewout_o, owin_o, owout_o,
                xs_ref, p_ref, o_ref, ewin_ref, ewout_ref, owin_ref, owout_ref,
                stage_ref, in_sem, out_sem):
    weights = ((ewin_hbm, ewin_o, ewin_ref), (ewout_hbm, ewout_o, ewout_ref),
               (owin_hbm, owin_o, owin_ref), (owout_hbm, owout_o, owout_ref))
    exports = [pltpu.make_async_copy(res, dst, out_sem.at[k]) for k, (_, dst, res) in enumerate(weights)]

    @pl.when(pl.program_id(0) == 0)
    def _():
        for src, _, res in weights:
            _convert_weights(src, res, stage_ref, in_sem)
        _fold_odd_weights(owin_ref, owout_ref, epack_ref, perm_ref)
        for export in exports:
            export.start()

    @pl.when(pl.program_id(0) == pl.num_programs(0) - 1)
    def _():
        for export in exports:
            export.wait()

    n_b = x_ref.shape[0]
    for l in range(DEPTH):
        i = l // 2
        even = l % 2 == 0
        win_ref, wout_ref, width = (ewin_ref, ewout_ref, EVEN_IN) if even else (owin_ref, owout_ref, ODD_IN)
        shift, scale, gate = (mod_ref[l, part, 0:1, :] for part in range(3))
        for bb in range(n_b):
            x = x_ref[bb] if l == 0 else xs_ref[bb]
            h = _norm_mod(x, g_ref[l:l + 1, :], shift, scale)
            p_ref[bb, :, 0:width] = jnp.dot(h.astype(BF16), win_ref[i], preferred_element_type=F32)
        if even:
            _ctx_even_mix(i, sink_ref, gq_ref, gk_ref, (ka_ref, va_ref, kb_ref, vb_ref), p_ref, o_ref)
        else:
            _ctx_odd_mix(cs_ref, ss_ref, p_ref, o_ref)
        for bb in range(n_b):
            y = jnp.dot(o_ref[bb], wout_ref[i], preferred_element_type=F32)
            xn = (x_ref[bb] if l == 0 else xs_ref[bb]) + gate * y
            if l == DEPTH - 1:
                xo_ref[bb] = _rms_head(xn, fg_ref[...])
            else:
                xs_ref[bb] = xn


def _resident(shape):
    return pl.BlockSpec(shape, lambda i: (0,) * len(shape), pipeline_mode=pl.Buffered(1))


def _ctx_path(x, mod, norm_g, ewin, ewout, owin, owout, gq2, gk2, sink, epack, perm, cs, ss, fg):
    b, s, _ = x.shape
    n_even = ewin.shape[0]
    bb = CTX_BATCH_PER_STEP
    kv_shape = jax.ShapeDtypeStruct((b, n_even, N_KV, HEAD_DIM, s), F32)
    kv_spec = pl.BlockSpec((bb, n_even, N_KV, HEAD_DIM, s), lambda i: (i, 0, 0, 0, 0))
    x_spec = pl.BlockSpec((bb, s, D_MODEL), lambda i: (i, 0, 0))
    hbm = pl.BlockSpec(memory_space=pl.ANY)
    weights = (ewin, ewout, owin, owout)
    return pl.pallas_call(
        _ctx_kernel,
        grid=(b // bb,),
        in_specs=[
            pl.BlockSpec(memory_space=pltpu.SMEM),
            x_spec,
            _resident(mod.shape), _resident(norm_g.shape),
            hbm, hbm, hbm, hbm,
            _resident(gq2.shape), _resident(gk2.shape),
            _resident(epack.shape), _resident(perm.shape), _resident(cs.shape), _resident(ss.shape),
            _resident(fg.shape),
        ],
        out_specs=[x_spec, kv_spec, kv_spec, kv_spec, kv_spec, hbm, hbm, hbm, hbm],
        out_shape=[jax.ShapeDtypeStruct(x.shape, F32), kv_shape, kv_shape, kv_shape, kv_shape]
                  + [jax.ShapeDtypeStruct(w.shape, BF16) for w in weights],
        scratch_shapes=[pltpu.VMEM((bb, s, D_MODEL), F32), pltpu.VMEM((bb, s, EVEN_IN), F32),
                        pltpu.VMEM((bb, s, D_MODEL), BF16)]
                       + [pltpu.VMEM(w.shape, BF16) for w in weights]
                       + [pltpu.VMEM((STAGE_SLOTS, STAGE_ROWS, max(w.shape[2] for w in weights)), F32),
                          pltpu.SemaphoreType.DMA((STAGE_SLOTS,)), pltpu.SemaphoreType.DMA((len(weights),))],
        compiler_params=pltpu.CompilerParams(
            dimension_semantics=("arbitrary",), vmem_limit_bytes=CTX_VMEM_LIMIT),
        name="ctx_path",
    )(sink, x, mod, norm_g, *weights, gq2, gk2, epack, perm, cs, ss, fg)


def _lat_weight_copies(l, w_hbms, win_buf, wout_buf, w_sem):
    ewin_hbm, ewout_hbm, owin_hbm, owout_hbm = w_hbms
    i, slot = l // 2, l % 2
    w_in, w_out = (ewin_hbm, ewout_hbm) if l % 2 == 0 else (owin_hbm, owout_hbm)
    return (pltpu.make_async_copy(w_in.at[i], win_buf.at[slot, :, pl.ds(0, w_in.shape[2])], w_sem.at[2 * slot]),
            pltpu.make_async_copy(w_out.at[i], wout_buf.at[slot], w_sem.at[2 * slot + 1]))


def _lat_attention_items(n, u, layer_i, sink_ref, kv_ref, kcat_ref, wt_ref, cache_refs):
    seq = kv_ref.shape[1]
    ckb, cvb = cache_refs[2][layer_i], cache_refs[3][layer_i]
    prev0 = pl.multiple_of(jnp.maximum(n * ROWS - WINDOW, 0), WINDOW)
    own0 = pl.multiple_of(n * ROWS, ROWS)
    next0 = pl.multiple_of(jnp.minimum(n * ROWS + ROWS, seq - WINDOW), WINDOW)
    win_len = ROWS + 2 * WINDOW
    ctx_len = ckb.shape[0]
    cj = lax.broadcasted_iota(jnp.int32, (win_len + ctx_len, ROWS), 0)
    qi = lax.broadcasted_iota(jnp.int32, (win_len + ctx_len, ROWS), 1)
    kpos = n * ROWS - WINDOW + cj
    in_win = (jnp.abs(cj - WINDOW - qi) <= WINDOW) & (kpos >= 0) & (kpos < seq)
    mask_b = in_win | (cj >= win_len)

    def window(idx, cache):
        return jnp.concatenate([kv_ref[idx, pl.ds(prev0, WINDOW), :], kv_ref[idx, pl.ds(own0, ROWS), :],
                                kv_ref[idx, pl.ds(next0, WINDOW), :], cache], axis=0)

    k2b, v2bt = window(2, ckb), window(3, cvb).T
    items = []
    for mixer, (q0, g0) in enumerate(((LQA, LGA), (LQB, LGB))):
        for j in range(N_KV):
            k_cat, w_t = (kcat_ref[j], wt_ref[j]) if mixer == 0 else _kv_operands(k2b, v2bt, j)
            for cc in range(GROUP // 2):
                c0 = 128 * (j * (GROUP // 2) + cc)
                if mixer == 0:
                    sinks, mask = None, None
                else:
                    sinks, mask = _sink_pair(sink_ref, layer_i, c0 // HEAD_DIM), mask_b
                items.append(dict(u=u, q=q0 + c0, g=g0 + c0, o=mixer * MIX_W + c0, sinks=sinks, mask=mask,
                                  k_cat=k_cat, w_t=w_t))
    return items


def _lat_kernel(sink_ref, x_hbm, mod_ref, g_ref, ewin_hbm, ewout_hbm, owin_hbm, owout_hbm, gq_ref, gk_ref,
                cos_ref, sn_ref, sp_ref, cka_ref, cva_ref, ckb_ref, cvb_ref, cs_ref, ss_ref, fg_ref,
                y_hbm,
                xw_ref, win_buf, wout_buf, kv_ref, kcat_ref, wt_ref, p_ref, o_ref, fcos_ref, fsin_ref, gate_ref,
                x_sem, w_sem):
    b = pl.program_id(0)
    seq = xw_ref.shape[0]
    n_blocks = seq // ROWS
    w_hbms = (ewin_hbm, ewout_hbm, owin_hbm, owout_hbm)

    x_in = pltpu.make_async_copy(x_hbm.at[b], xw_ref, x_sem.at[0])
    x_in.start()
    for copy in _lat_weight_copies(0, w_hbms, win_buf, wout_buf, w_sem):
        copy.start()
    x_in.wait()

    for l in range(DEPTH):
        i, slot = l // 2, l % 2
        for copy in _lat_weight_copies(l, w_hbms, win_buf, wout_buf, w_sem):
            copy.wait()
        if l + 1 < DEPTH:
            for copy in _lat_weight_copies(l + 1, w_hbms, win_buf, wout_buf, w_sem):
                copy.start()
        shift, scale, gate = (mod_ref[l, part, pl.ds(1 + b, 1), :] for part in range(3))
        g = g_ref[l:l + 1, :]

        def normed(rows):
            return _norm_mod(xw_ref[rows, :], g, shift, scale).astype(BF16)

        def w_in(c0, width):
            return win_buf[slot, :, c0:c0 + width]

        if l % 2 == 0:
            gq2 = gq_ref[i:i + 1, :] * Q_SCALE
            gk2 = gk_ref[i:i + 1, :]

            def project_kv(n, carry):
                rows = pl.ds(pl.multiple_of(n * ROWS, ROWS), ROWS)
                h = normed(rows)
                cos, sn, sp = cos_ref[rows, :], sn_ref[rows, :], sp_ref[rows, :]
                for mixer, c0 in enumerate((KA, KB)):
                    kv = jnp.dot(h, w_in(c0, 2 * KV_W), preferred_element_type=F32)
                    k2 = kv[:, :KV_W]
                    if mixer == 0:
                        k2 = _rms_pair(k2, gk2)
                    kv_ref[2 * mixer, rows, :] = _rope(k2, cos, sn, sp)
                    kv_ref[2 * mixer + 1, rows, :] = kv[:, KV_W:]
                return carry

            lax.fori_loop(0, n_blocks, project_kv, 0, unroll=LAT_UNROLL)
            k2a = jnp.concatenate([kv_ref[0], cka_ref[i]], axis=0)
            v2at = jnp.concatenate([kv_ref[1], cva_ref[i]], axis=0).T
            for j in range(N_KV):
                kcat_ref[j], wt_ref[j] = _kv_operands(k2a, v2at, j)

            def attend(m, carry):
                blocks = [m * LAT_UNROLL + u for u in range(LAT_UNROLL)]
                block_rows = [pl.ds(pl.multiple_of(n * ROWS, ROWS), ROWS) for n in blocks]
                for u, rows in enumerate(block_rows):
                    h = normed(rows)
                    cos, sn, sp = cos_ref[rows, :], sn_ref[rows, :], sp_ref[rows, :]
                    qa = jnp.dot(h, w_in(QA, MIX_W), preferred_element_type=F32)
                    for c0 in range(0, MIX_W, 128):
                        p_ref[u, :, LQA + c0:LQA + c0 + 128] = _rope(_rms_pair(qa[:, c0:c0 + 128], gq2),
                                                                     cos, sn, sp)
                    mid = jnp.dot(h, w_in(GA, 2 * MIX_W), preferred_element_type=F32)
                    p_ref[u, :, LGA:LGA + MIX_W] = _silu(mid[:, :MIX_W])
                    for c0 in range(0, MIX_W, 128):
                        qb = mid[:, MIX_W + c0:MIX_W + c0 + 128] * Q_SCALE
                        p_ref[u, :, LQB + c0:LQB + c0 + 128] = _rope(qb, cos, sn, sp)
                    p_ref[u, :, LGB:LGB + MIX_W] = _silu(
                        jnp.dot(h, w_in(GB, MIX_W), preferred_element_type=F32))
                per_block = [_lat_attention_items(n, u, i, sink_ref, kv_ref, kcat_ref, wt_ref,
                                                  (cka_ref, cva_ref, ckb_ref, cvb_ref))
                             for u, n in enumerate(blocks)]
                items = [it for group in zip(*per_block) for it in group]

                def write_out(it, value):
                    o_ref[it["u"], :, it["o"]:it["o"] + 128] = value

                _attention_chunks(items, lambda it: p_ref[it["u"], :, it["q"]:it["q"] + 128],
                                  lambda it: p_ref[it["u"], :, it["g"]:it["g"] + 128], write_out)
                for u, rows in enumerate(block_rows):
                    y = jnp.dot(o_ref[u], wout_buf[slot], preferred_element_type=F32)
                    xw_ref[rows, :] = xw_ref[rows, :] + gate * y
                return carry

            lax.fori_loop(0, n_blocks // LAT_UNROLL, attend, 0)
        else:
            def project(n, carry):
                rows = pl.ds(pl.multiple_of(n * ROWS, ROWS), ROWS)
                p = jnp.dot(normed(rows), w_in(0, ODD_IN), preferred_element_type=F32)
                fcos_ref[rows, :], fsin_ref[rows, :] = _spectrum_operands(p[:, 0:D_MODEL])
                gate_ref[rows, :] = _silu(p[:, D_MODEL:ODD_IN])
                return carry

            lax.fori_loop(0, n_blocks, project, 0, unroll=LAT_UNROLL)
            dft_scale = float(1.0 / np.sqrt(float(seq * C_GROUP_DIM)))

            def mix(n, carry):
                rows = pl.ds(pl.multiple_of(n * ROWS, ROWS), ROWS)
                m_cos = jnp.dot(cs_ref[rows, :], fcos_ref[...], preferred_element_type=F32)
                m_sin = jnp.dot(ss_ref[rows, :], fsin_ref[...], preferred_element_type=F32)

                def write_out(c0, value):
                    o_ref[0, :, c0:c0 + HALF_GROUP] = value

                _spectrum_combine(m_cos, m_sin, dft_scale,
                                  lambda c0: gate_ref[rows, c0:c0 + HALF_GROUP], write_out)
                y = jnp.dot(o_ref[0], wout_buf[slot], preferred_element_type=F32)
                xn = xw_ref[rows, :] + gate * y
                if l == DEPTH - 1:
                    xn = _rms_head(xn, fg_ref[...])
                xw_ref[rows, :] = xn
                return carry

            lax.fori_loop(0, n_blocks, mix, 0, unroll=LAT_UNROLL)

    y_out = pltpu.make_async_copy(xw_ref, y_hbm.at[b], x_sem.at[1])
    y_out.start()
    y_out.wait()


def _lat_path(x, mod, norm_g, ewin, ewout, owin, owout, gq2, gk2, sink, cos, sn, sp, caches, cs, ss, fg):
    b, s, _ = x.shape
    n_even, past = caches[0].shape[1], caches[0].shape[2]
    hbm = pl.BlockSpec(memory_space=pl.ANY)
    cache_spec = pl.BlockSpec((None, n_even, past, KV_W), lambda i: (i, 0, 0, 0))
    return pl.pallas_call(
        _lat_kernel,
        grid=(b,),
        in_specs=[
            pl.BlockSpec(memory_space=pltpu.SMEM),
            hbm,
            _resident(mod.shape), _resident(norm_g.shape),
            hbm, hbm, hbm, hbm,
            _resident(gq2.shape), _resident(gk2.shape),
            _resident(cos.shape), _resident(sn.shape), _resident(sp.shape),
            cache_spec, cache_spec, cache_spec, cache_spec,
            _resident(cs.shape), _resident(ss.shape), _resident(fg.shape),
        ],
        out_specs=hbm,
        out_shape=jax.ShapeDtypeStruct(x.shape, F32),
        scratch_shapes=[
            pltpu.VMEM((s, D_MODEL), F32),
            pltpu.VMEM((2, D_MODEL, EVEN_IN), BF16),
            pltpu.VMEM((2, D_MODEL, D_MODEL), BF16),
            pltpu.VMEM((4, s, KV_W), F32),
            pltpu.VMEM((N_KV, 2 * (s + past), KV_W), BF16),
            pltpu.VMEM((N_KV, 2 * HEAD_DIM + DEN_ROWS, 2 * (s + past)), BF16),
            pltpu.VMEM((LAT_UNROLL, ROWS, 4 * MIX_W), F32),
            pltpu.VMEM((LAT_UNROLL, ROWS, D_MODEL), BF16),
            pltpu.VMEM((s, D_MODEL), BF16),
            pltpu.VMEM((s, D_MODEL // 2), BF16),
            pltpu.VMEM((s, D_MODEL), F32),
            pltpu.SemaphoreType.DMA((2,)), pltpu.SemaphoreType.DMA((4,)),
        ],
        compiler_params=pltpu.CompilerParams(
            dimension_semantics=("arbitrary",), vmem_limit_bytes=LAT_VMEM_LIMIT),
        name="lat_path",
    )(sink, x, mod, norm_g, ewin, ewout, owin, owout, gq2, gk2, cos, sn, sp, *caches, cs, ss, fg)


def kernel(x_prompt, x_sample, cache_k_a, cache_v_a, cache_k_b, cache_v_b, c, c_ctx, norm_g, ada_w, ada_b,
           even_w_in, even_w_out, qk_g_q, qk_g_k, sink_logit, odd_w_in, odd_w_out, final_g):
    batch, seq, _ = x_prompt.shape
    dec_batch, dec_seq, _ = x_sample.shape
    n_even = even_w_in.shape[0]
    past = cache_k_a.shape[2]

    cond = jnp.concatenate(
        [c_ctx[None, :], c, jnp.zeros((COND_ROWS - 1 - dec_batch, D_MODEL), F32)], axis=0)
    mod = _ada_all(cond, ada_w, ada_b)

    epack = jnp.asarray(_packed_channel_dft()).astype(BF16)
    perm = jnp.asarray(_mirror_perm()).astype(BF16)
    cs_ctx, ss_ctx = (jnp.asarray(t).astype(BF16) for t in _dft_tables(seq))
    cs_lat, ss_lat = (jnp.asarray(t).astype(BF16) for t in _dft_tables(dec_seq))
    cos, sn, sp = (jnp.asarray(t) for t in _rope_tables(dec_seq))

    caches = [a.reshape(dec_batch, n_even, past, KV_W) for a in (cache_k_a, cache_v_a, cache_k_b, cache_v_b)]
    fg = final_g.reshape(1, D_MODEL)
    gq2 = jnp.tile(qk_g_q, (1, 2))
    gk2 = jnp.tile(qk_g_k, (1, 2))

    xc, *outs = _ctx_path(x_prompt, mod, norm_g, even_w_in, even_w_out, odd_w_in, odd_w_out, gq2, gk2,
                          sink_logit, epack, perm, cs_ctx, ss_ctx, fg)
    new_kv = [jnp.transpose(a, (0, 1, 4, 2, 3)) for a in outs[:4]]
    ewin, ewout, owin, owout = outs[4:]

    xl = _lat_path(x_sample, mod, norm_g, ewin, ewout, owin, owout, gq2, gk2, sink_logit, cos, sn, sp,
                   caches, cs_lat, ss_lat, fg)
    return (xc, xl, *new_kv)
```

```python
import numpy as np
import jax
import jax.numpy as jnp
from jax import lax
from jax.experimental import pallas as pl
from jax.experimental.pallas import tpu as pltpu

D_MODEL = 1024
DEPTH = 4
HEAD_DIM = 64
N_HEADS = 8
N_KV = 2
GROUP = N_HEADS // N_KV
MIX_W = N_HEADS * HEAD_DIM
KV_W = N_KV * HEAD_DIM
EVEN_IN = 2 * (2 * MIX_W + 2 * KV_W)
ODD_IN = 2 * D_MODEL
GRID_W = 64
WINDOW = 128
ROPE_BASE = 10000.0
C_GROUPS = 4
C_GROUP_DIM = D_MODEL // C_GROUPS
EPS = 1e-6
NEG_BIG = -1e30
ROWS = 256
COND_ROWS = 8
VMEM_LIMIT = 48 * 1024 * 1024
CTX_VMEM_LIMIT = 56 * 1024 * 1024
LAT_VMEM_LIMIT = 56 * 1024 * 1024
STAGE_ROWS = 128
STAGE_SLOTS = 4
CTX_BATCH_PER_STEP = 2
LAT_UNROLL = 1

QA, KA, VA, GA = 0, 512, 640, 768
QB, KB, VB, GB = 1280, 1792, 1920, 2048
LQA, LGA, LQB, LGB = 0, 512, 1024, 1536

F32 = jnp.float32
BF16 = jnp.bfloat16


def _dft_tables(n):
    k = np.arange(n, dtype=np.int64)
    ang = ((k[:, None] * k[None, :]) % n).astype(np.float64) * (2.0 * np.pi / n)
    return np.cos(ang).astype(np.float32), np.sin(ang).astype(np.float32)


HALF_GROUP = C_GROUP_DIM // 2


def _packed_channel_dft():
    c, s = _dft_tables(C_GROUP_DIM)
    return np.concatenate([c[:, :HALF_GROUP], c[:, HALF_GROUP:HALF_GROUP + 1], s[:, 1:HALF_GROUP]], axis=1)


def _mirror_perm():
    t = np.array([0] + [HALF_GROUP - m for m in range(1, HALF_GROUP)])
    p = np.zeros((HALF_GROUP, HALF_GROUP), np.float32)
    p[t, np.arange(HALF_GROUP)] = 1.0
    return p


def _rope_tables(n_tok):
    rows = n_tok // GRID_W
    row = np.repeat(np.arange(rows), GRID_W).astype(np.float64)
    col = np.tile(np.arange(GRID_W), rows).astype(np.float64)
    half = HEAD_DIM // 2
    inv = 1.0 / (ROPE_BASE ** (np.arange(0, half, 2, dtype=np.float64) / half))
    ang_r = row[:, None] * inv
    ang_c = col[:, None] * inv
    zeros = np.zeros_like(ang_r)
    cos_h = np.concatenate([np.cos(ang_r), np.cos(ang_r), np.cos(ang_c), np.cos(ang_c)], axis=1)
    nxt_h = np.concatenate([-np.sin(ang_r), zeros, -np.sin(ang_c), zeros], axis=1)
    prv_h = np.concatenate([zeros, np.sin(ang_r), zeros, np.sin(ang_c)], axis=1)
    two = lambda t: np.concatenate([t, t], axis=1).astype(np.float32)
    return two(cos_h), two(nxt_h), two(prv_h)


def _silu(x):
    return x * jax.nn.sigmoid(x)


def _norm_mod(x, g, shift, scale):
    ms = jnp.mean(x * x, axis=-1, keepdims=True)
    return x * lax.rsqrt(ms + EPS) * (g * (1.0 + scale)) + shift


def _rms_head(xh, g):
    ms = jnp.mean(xh * xh, axis=-1, keepdims=True)
    return xh * lax.rsqrt(ms + EPS) * g


def _rms_pair(xc, g2):
    lo = lax.broadcasted_iota(jnp.int32, xc.shape, 1) < HEAD_DIM
    ss = xc * xc
    s_lo = jnp.sum(jnp.where(lo, ss, 0.0), axis=-1, keepdims=True)
    s_hi = jnp.sum(jnp.where(lo, 0.0, ss), axis=-1, keepdims=True)
    ms = jnp.where(lo, s_lo, s_hi) * (1.0 / HEAD_DIM)
    return xc * lax.rsqrt(ms + EPS) * g2


def _rope(xc, cos, sin_next, sin_prev):
    nxt = pltpu.roll(xc, 128 - 16, 1)
    prv = pltpu.roll(xc, 16, 1)
    return xc * cos + nxt * sin_next + prv * sin_prev


DEN_ROWS = 16
LOG2E = float(np.log2(np.e))
Q_SCALE = (HEAD_DIM ** -0.5) * LOG2E


def _sink_pair(sink_ref, i, head):
    return sink_ref[i, head] * LOG2E, sink_ref[i, head + 1] * LOG2E


def _kv_operands(k2, v2t, j):
    tk = k2.shape[0]
    low = lax.broadcasted_iota(jnp.int32, k2.shape, 1) < HEAD_DIM
    km = jnp.where(low if j == 0 else jnp.logical_not(low), k2, 0.0)
    kr = pltpu.roll(km, HEAD_DIM, 1)
    k_lo, k_hi = (km, kr) if j == 0 else (kr, km)
    k_cat = jnp.concatenate([k_lo, k_hi], axis=0).astype(BF16)
    vjt = v2t[HEAD_DIM * j:HEAD_DIM * (j + 1), :]
    zero = jnp.zeros_like(vjt)
    row = lax.broadcasted_iota(jnp.int32, (DEN_ROWS, 2 * tk), 0)
    col = lax.broadcasted_iota(jnp.int32, (DEN_ROWS, 2 * tk), 1)
    ones = jnp.where(((row == 0) & (col < tk)) | ((row == 1) & (col >= tk)), 1.0, 0.0)
    w_t = jnp.concatenate([jnp.concatenate([vjt, zero], axis=1),
                           jnp.concatenate([zero, vjt], axis=1), ones], axis=0).astype(BF16)
    return k_cat, w_t


def _scores_t(qc, k_cat):
    return lax.dot_general(k_cat, qc.astype(BF16), (((1,), (1,)), ((), ())), preferred_element_type=F32)


def _softmax_t(s_t, sinks=None, mask_t=None):
    tk = s_t.shape[0] // 2
    es, ms = [], []
    for hh in range(2):
        sh = s_t[hh * tk:(hh + 1) * tk]
        if mask_t is not None:
            sh = jnp.where(mask_t, sh, NEG_BIG)
        m = jnp.max(sh, axis=0, keepdims=True)
        if sinks is not None:
            m = jnp.maximum(m, sinks[hh])
        es.append(jnp.exp2(sh - m))
        ms.append(m)
    return jnp.concatenate(es, axis=0).astype(BF16), ms


def _values_t(e_t, w_t):
    return jnp.dot(w_t, e_t, preferred_element_type=F32)


def _normalise_t(nd, ms, sinks=None):
    dens = [nd[2 * HEAD_DIM + hh:2 * HEAD_DIM + hh + 1] for hh in range(2)]
    if sinks is not None:
        dens = [dens[hh] + jnp.exp2(sinks[hh] - ms[hh]) for hh in range(2)]
    inv = [1.0 / d for d in dens]
    o_t = jnp.concatenate([nd[0:HEAD_DIM] * inv[0], nd[HEAD_DIM:2 * HEAD_DIM] * inv[1]], axis=0)
    return o_t.T


def _run_skewed(items, stages):
    state = list(items)
    for t in range(len(items) + len(stages) - 1):
        for k, stage in enumerate(stages):
            if 0 <= t - k < len(items):
                state[t - k] = stage(state[t - k])


def _attention_chunks(items, read_q, read_gate, write_out):
    def scores(it):
        return dict(it, s_t=_scores_t(read_q(it), it["k_cat"]))

    def softmax(it):
        e_t, ms = _softmax_t(it["s_t"], it["sinks"], it["mask"])
        return dict(it, e_t=e_t, ms=ms, s_t=None)

    def values(it):
        return dict(it, nd=_values_t(it["e_t"], it["w_t"]), e_t=None)

    def finish(it):
        o = _normalise_t(it["nd"], it["ms"], it["sinks"])
        write_out(it, (o * read_gate(it)).astype(BF16))
        return None

    _run_skewed(items, (scores, softmax, values, finish))


def _ada_kernel(cond_ref, w_ref, b_ref, o_ref):
    a = _silu(cond_ref[...]).astype(BF16)
    o_ref[...] = jnp.dot(a, w_ref[...].astype(BF16), preferred_element_type=F32) + b_ref[...]


def _ada_all(cond, ada_w, ada_b):
    return pl.pallas_call(
        _ada_kernel,
        grid=(DEPTH, 3),
        in_specs=[
            pl.BlockSpec((COND_ROWS, D_MODEL), lambda l, p: (0, 0)),
            pl.BlockSpec((None, D_MODEL, D_MODEL), lambda l, p: (l, 0, p)),
            pl.BlockSpec((None, None, 1, D_MODEL), lambda l, p: (l, p, 0, 0)),
        ],
        out_specs=pl.BlockSpec((None, None, COND_ROWS, D_MODEL), lambda l, p: (l, p, 0, 0)),
        out_shape=jax.ShapeDtypeStruct((DEPTH, 3, COND_ROWS, D_MODEL), F32),
        compiler_params=pltpu.CompilerParams(
            dimension_semantics=("parallel", "parallel"), vmem_limit_bytes=VMEM_LIMIT),
        name="ada_mod",
    )(cond, ada_w, ada_b.reshape(DEPTH, 3, 1, D_MODEL))


def _ctx_even_mix(i, sink_ref, gq_ref, gk_ref, kv_refs, p_ref, o_ref):
    n_b, seq, _ = p_ref.shape
    gq2 = gq_ref[i:i + 1, :] * Q_SCALE
    gk2 = gk_ref[i:i + 1, :]
    items = []
    for mixer, (q0, k0, v0, g0) in enumerate(((QA, KA, VA, GA), (QB, KB, VB, GB))):
        kv = []
        for bb in range(n_b):
            k2 = p_ref[bb, :, k0:k0 + KV_W]
            v2 = p_ref[bb, :, v0:v0 + KV_W]
            if mixer == 0:
                k2 = _rms_pair(k2, gk2)
            v2t = v2.T
            kv_refs[2 * mixer][bb, i] = k2.T.reshape(N_KV, HEAD_DIM, seq)
            kv_refs[2 * mixer + 1][bb, i] = v2t.reshape(N_KV, HEAD_DIM, seq)
            kv.append((k2, v2t))
        for j in range(N_KV):
            ops = [_kv_operands(k2, v2t, j) for k2, v2t in kv]
            for cc in range(GROUP // 2):
                c0 = 128 * (j * (GROUP // 2) + cc)
                sinks = None if mixer == 0 else _sink_pair(sink_ref, i, c0 // HEAD_DIM)
                for bb, (k_cat, w_t) in enumerate(ops):
                    items.append(dict(rows=bb, q=q0 + c0, g=g0 + c0, o=mixer * MIX_W + c0,
                                      norm=mixer == 0, sinks=sinks, mask=None, k_cat=k_cat, w_t=w_t))

    def read_q(it):
        qc = p_ref[it["rows"], :, it["q"]:it["q"] + 128]
        return _rms_pair(qc, gq2) if it["norm"] else qc * Q_SCALE

    def read_gate(it):
        return _silu(p_ref[it["rows"], :, it["g"]:it["g"] + 128])

    def write_out(it, value):
        o_ref[it["rows"], :, it["o"]:it["o"] + 128] = value

    _attention_chunks(items, read_q, read_gate, write_out)


def _spectrum_operands(t):
    first = lax.broadcasted_iota(jnp.int32, (t.shape[0], HALF_GROUP), 1) == 0
    lower = [t[:, C_GROUP_DIM * g:C_GROUP_DIM * g + HALF_GROUP] for g in range(C_GROUPS)]
    upper = [t[:, C_GROUP_DIM * g + HALF_GROUP:C_GROUP_DIM * (g + 1)] for g in range(C_GROUPS)]
    for_cos = jnp.concatenate(lower + [jnp.where(first, u, 0.0) for u in upper], axis=1).astype(BF16)
    for_sin = jnp.concatenate([jnp.where(first, 0.0, u) for u in upper], axis=1).astype(BF16)
    return for_cos, for_sin


def _spectrum_combine(m_cos, m_sin, scale, read_gate, write_out):
    first = lax.broadcasted_iota(jnp.int32, (m_cos.shape[0], HALF_GROUP), 1) == 0
    for g in range(C_GROUPS):
        p = m_cos[:, HALF_GROUP * g:HALF_GROUP * (g + 1)]
        p_mid = m_cos[:, D_MODEL // 2 + HALF_GROUP * g:D_MODEL // 2 + HALF_GROUP * (g + 1)]
        q = m_sin[:, HALF_GROUP * g:HALF_GROUP * (g + 1)]
        c0 = C_GROUP_DIM * g
        write_out(c0, ((p - q) * scale * read_gate(c0)).astype(BF16))
        upper = jnp.where(first, p_mid, p + q) * scale
        write_out(c0 + HALF_GROUP, (upper * read_gate(c0 + HALF_GROUP)).astype(BF16))


def _ctx_odd_mix(cs_ref, ss_ref, p_ref, o_ref):
    n_b, seq, _ = p_ref.shape
    scale = float(1.0 / np.sqrt(float(seq * C_GROUP_DIM)))
    for bb in range(n_b):
        for_cos, for_sin = _spectrum_operands(p_ref[bb, :, 0:D_MODEL])
        m_cos = jnp.dot(cs_ref[...], for_cos, preferred_element_type=F32)
        m_sin = jnp.dot(ss_ref[...], for_sin, preferred_element_type=F32)

        def read_gate(c0, bb=bb):
            return _silu(p_ref[bb, :, D_MODEL + c0:D_MODEL + c0 + HALF_GROUP])

        def write_out(c0, value, bb=bb):
            o_ref[bb, :, c0:c0 + HALF_GROUP] = value

        _spectrum_combine(m_cos, m_sin, scale, read_gate, write_out)


def _fold_odd_weights(owin_ref, owout_ref, epack_ref, perm_ref):
    for layer in range(owin_ref.shape[0]):
        for g in range(C_GROUPS):
            c0 = C_GROUP_DIM * g
            owin_ref[layer, :, c0:c0 + C_GROUP_DIM] = jnp.dot(
                owin_ref[layer, :, c0:c0 + C_GROUP_DIM], epack_ref[...],
                preferred_element_type=F32).astype(BF16)
            g0 = D_MODEL + c0 + HALF_GROUP
            owin_ref[layer, :, g0:g0 + HALF_GROUP] = jnp.dot(
                owin_ref[layer, :, g0:g0 + HALF_GROUP], perm_ref[...], preferred_element_type=F32).astype(BF16)
            r0 = c0 + HALF_GROUP
            owout_ref[layer, r0:r0 + HALF_GROUP, :] = jnp.dot(
                perm_ref[...], owout_ref[layer, r0:r0 + HALF_GROUP, :], preferred_element_type=F32).astype(BF16)


def _stage_copy(w_hbm, stage_ref, sem, chunk, slot):
    _, n_rows, cols = w_hbm.shape
    rows = stage_ref.shape[1]
    per_layer = n_rows // rows
    src = w_hbm.at[chunk // per_layer, pl.ds((chunk % per_layer) * rows, rows), :]
    return pltpu.make_async_copy(src, stage_ref.at[slot, :, pl.ds(0, cols)], sem.at[slot])


def _convert_weights(w_hbm, w_ref, stage_ref, in_sem):
    n_layers, n_rows, cols = w_hbm.shape
    n_slots, rows, _ = stage_ref.shape
    per_layer = n_rows // rows
    n_chunks = n_layers * per_layer
    for c in range(n_slots - 1):
        _stage_copy(w_hbm, stage_ref, in_sem, c, c).start()

    def body(c, carry):
        slot = c % n_slots
        ahead = c + n_slots - 1

        @pl.when(ahead < n_chunks)
        def _():
            _stage_copy(w_hbm, stage_ref, in_sem, ahead, ahead % n_slots).start()

        _stage_copy(w_hbm, stage_ref, in_sem, c, slot).wait()
        r0 = pl.multiple_of((c % per_layer) * rows, rows)
        w_ref[c // per_layer, pl.ds(r0, rows), :] = stage_ref[slot, :, 0:cols].astype(BF16)
        return carry

    lax.fori_loop(0, n_chunks, body, 0)


def _ctx_kernel(sink_ref, x_ref, mod_ref, g_ref, ewin_hbm, ewout_hbm, owin_hbm, owout_hbm, gq_ref, gk_ref,
                epack_ref, perm_ref, cs_ref, ss_ref, fg_ref,
                xo_ref, ka_ref, va_ref, kb_ref, vb_ref, ewin_o, ewout_o, owin_o, owout_o,
                xs_ref, p_ref, o_ref, ewin_ref, ewout_ref, owin_ref, owout_ref,
                stage_ref, in_sem, out_sem):
    weights = ((ewin_hbm, ewin_o, ewin_ref), (ewout_hbm, ewout_o, ewout_ref),
               (owin_hbm, owin_o, owin_ref), (owout_hbm, owout_o, owout_ref))
    exports = [pltpu.make_async_copy(res, dst, out_sem.at[k]) for k, (_, dst, res) in enumerate(weights)]

    @pl.when(pl.program_id(0) == 0)
    def _():
        for src, _, res in weights:
            _convert_weights(src, res, stage_ref, in_sem)
        _fold_odd_weights(owin_ref, owout_ref, epack_ref, perm_ref)
        for export in exports:
            export.start()

    @pl.when(pl.program_id(0) == pl.num_programs(0) - 1)
    def _():
        for export in exports:
            export.wait()

    n_b = x_ref.shape[0]
    for l in range(DEPTH):
        i = l // 2
        even = l % 2 == 0
        win_ref, wout_ref, width = (ewin_ref, ewout_ref, EVEN_IN) if even else (owin_ref, owout_ref, ODD_IN)
        shift, scale, gate = (mod_ref[l, part, 0:1, :] for part in range(3))
        for bb in range(n_b):
            x = x_ref[bb] if l == 0 else xs_ref[bb]
            h = _norm_mod(x, g_ref[l:l + 1, :], shift, scale)
            p_ref[bb, :, 0:width] = jnp.dot(h.astype(BF16), win_ref[i], preferred_element_type=F32)
        if even:
            _ctx_even_mix(i, sink_ref, gq_ref, gk_ref, (ka_ref, va_ref, kb_ref, vb_ref), p_ref, o_ref)
        else:
            _ctx_odd_mix(cs_ref, ss_ref, p_ref, o_ref)
        for bb in range(n_b):
            y = jnp.dot(o_ref[bb], wout_ref[i], preferred_element_type=F32)
            xn = (x_ref[bb] if l == 0 else xs_ref[bb]) + gate * y
            if l == DEPTH - 1:
                xo_ref[bb] = _rms_head(xn, fg_ref[...])
            else:
                xs_ref[bb] = xn


def _resident(shape):
    return pl.BlockSpec(shape, lambda i: (0,) * len(shape), pipeline_mode=pl.Buffered(1))


def _ctx_path(x, mod, norm_g, ewin, ewout, owin, owout, gq2, gk2, sink, epack, perm, cs, ss, fg):
    b, s, _ = x.shape
    n_even = ewin.shape[0]
    bb = CTX_BATCH_PER_STEP
    kv_shape = jax.ShapeDtypeStruct((b, n_even, N_KV, HEAD_DIM, s), F32)
    kv_spec = pl.BlockSpec((bb, n_even, N_KV, HEAD_DIM, s), lambda i: (i, 0, 0, 0, 0))
    x_spec = pl.BlockSpec((bb, s, D_MODEL), lambda i: (i, 0, 0))
    hbm = pl.BlockSpec(memory_space=pl.ANY)
    weights = (ewin, ewout, owin, owout)
    return pl.pallas_call(
        _ctx_kernel,
        grid=(b // bb,),
        in_specs=[
            pl.BlockSpec(memory_space=pltpu.SMEM),
            x_spec,
            _resident(mod.shape), _resident(norm_g.shape),
            hbm, hbm, hbm, hbm,
            _resident(gq2.shape), _resident(gk2.shape),
            _resident(epack.shape), _resident(perm.shape), _resident(cs.shape), _resident(ss.shape),
            _resident(fg.shape),
        ],
        out_specs=[x_spec, kv_spec, kv_spec, kv_spec, kv_spec, hbm, hbm, hbm, hbm],
        out_shape=[jax.ShapeDtypeStruct(x.shape, F32), kv_shape, kv_shape, kv_shape, kv_shape]
                  + [jax.ShapeDtypeStruct(w.shape, BF16) for w in weights],
        scratch_shapes=[pltpu.VMEM((bb, s, D_MODEL), F32), pltpu.VMEM((bb, s, EVEN_IN), F32),
                        pltpu.VMEM((bb, s, D_MODEL), BF16)]
                       + [pltpu.VMEM(w.shape, BF16) for w in weights]
                       + [pltpu.VMEM((STAGE_SLOTS, STAGE_ROWS, max(w.shape[2] for w in weights)), F32),
                          pltpu.SemaphoreType.DMA((STAGE_SLOTS,)), pltpu.SemaphoreType.DMA((len(weights),))],
        compiler_params=pltpu.CompilerParams(
            dimension_semantics=("arbitrary",), vmem_limit_bytes=CTX_VMEM_LIMIT),
        name="ctx_path",
    )(sink, x, mod, norm_g, *weights, gq2, gk2, epack, perm, cs, ss, fg)


def _lat_weight_copies(l, w_hbms, win_buf, wout_buf, w_sem):
    ewin_hbm, ewout_hbm, owin_hbm, owout_hbm = w_hbms
    i, slot = l // 2, l % 2
    w_in, w_out = (ewin_hbm, ewout_hbm) if l % 2 == 0 else (owin_hbm, owout_hbm)
    return (pltpu.make_async_copy(w_in.at[i], win_buf.at[slot, :, pl.ds(0, w_in.shape[2])], w_sem.at[2 * slot]),
            pltpu.make_async_copy(w_out.at[i], wout_buf.at[slot], w_sem.at[2 * slot + 1]))


def _lat_attention_items(n, u, layer_i, sink_ref, kv_ref, kcat_ref, wt_ref, cache_refs):
    seq = kv_ref.shape[1]
    ckb_t, cvb_t = cache_refs[2][layer_i], cache_refs[3][layer_i]
    prev0 = pl.multiple_of(jnp.maximum(n * ROWS - WINDOW, 0), WINDOW)
    own0 = pl.multiple_of(n * ROWS, ROWS)
    next0 = pl.multiple_of(jnp.minimum(n * ROWS + ROWS, seq - WINDOW), WINDOW)
    win_len = ROWS + 2 * WINDOW
    ctx_len = ckb_t.shape[1]
    cj = lax.broadcasted_iota(jnp.int32, (win_len + ctx_len, ROWS), 0)
    qi = lax.broadcasted_iota(jnp.int32, (win_len + ctx_len, ROWS), 1)
    kpos = n * ROWS - WINDOW + cj
    in_win = (jnp.abs(cj - WINDOW - qi) <= WINDOW) & (kpos >= 0) & (kpos < seq)
    mask_b = in_win | (cj >= win_len)

    def window(idx):
        return jnp.concatenate([kv_ref[idx, pl.ds(prev0, WINDOW), :], kv_ref[idx, pl.ds(own0, ROWS), :],
                                kv_ref[idx, pl.ds(next0, WINDOW), :]], axis=0)

    k2b = jnp.concatenate([window(2), ckb_t.T], axis=0)
    v2bt = jnp.concatenate([window(3).T, cvb_t], axis=1)
    items = []
    for mixer, (q0, g0) in enumerate(((LQA, LGA), (LQB, LGB))):
        for j in range(N_KV):
            k_cat, w_t = (kcat_ref[j], wt_ref[j]) if mixer == 0 else _kv_operands(k2b, v2bt, j)
            for cc in range(GROUP // 2):
                c0 = 128 * (j * (GROUP // 2) + cc)
                if mixer == 0:
                    sinks, mask = None, None
                else:
                    sinks, mask = _sink_pair(sink_ref, layer_i, c0 // HEAD_DIM), mask_b
                items.append(dict(u=u, q=q0 + c0, g=g0 + c0, o=mixer * MIX_W + c0, sinks=sinks, mask=mask,
                                  k_cat=k_cat, w_t=w_t))
    return items


def _lat_kernel(sink_ref, x_hbm, mod_ref, g_ref, ewin_hbm, ewout_hbm, owin_hbm, owout_hbm, gq_ref, gk_ref,
                cos_ref, sn_ref, sp_ref, cka_ref, cva_ref, ckb_ref, cvb_ref, cs_ref, ss_ref, fg_ref,
                y_hbm,
                xw_ref, win_buf, wout_buf, kv_ref, kcat_ref, wt_ref, p_ref, o_ref, fcos_ref, fsin_ref, gate_ref,
                x_sem, w_sem):
    b = pl.program_id(0)
    seq = xw_ref.shape[0]
    n_blocks = seq // ROWS
    w_hbms = (ewin_hbm, ewout_hbm, owin_hbm, owout_hbm)

    x_in = pltpu.make_async_copy(x_hbm.at[b], xw_ref, x_sem.at[0])
    x_in.start()
    for copy in _lat_weight_copies(0, w_hbms, win_buf, wout_buf, w_sem):
        copy.start()
    x_in.wait()

    for l in range(DEPTH):
        i, slot = l // 2, l % 2
        for copy in _lat_weight_copies(l, w_hbms, win_buf, wout_buf, w_sem):
            copy.wait()
        if l + 1 < DEPTH:
            for copy in _lat_weight_copies(l + 1, w_hbms, win_buf, wout_buf, w_sem):
                copy.start()
        shift, scale, gate = (mod_ref[l, part, pl.ds(1 + b, 1), :] for part in range(3))
        g = g_ref[l:l + 1, :]

        def normed(rows):
            return _norm_mod(xw_ref[rows, :], g, shift, scale).astype(BF16)

        def w_in(c0, width):
            return win_buf[slot, :, c0:c0 + width]

        if l % 2 == 0:
            gq2 = gq_ref[i:i + 1, :] * Q_SCALE
            gk2 = gk_ref[i:i + 1, :]

            def project_kv(n, carry):
                rows = pl.ds(pl.multiple_of(n * ROWS, ROWS), ROWS)
                h = normed(rows)
                cos, sn, sp = cos_ref[rows, :], sn_ref[rows, :], sp_ref[rows, :]
                for mixer, c0 in enumerate((KA, KB)):
                    kv = jnp.dot(h, w_in(c0, 2 * KV_W), preferred_element_type=F32)
                    k2 = kv[:, :KV_W]
                    if mixer == 0:
                        k2 = _rms_pair(k2, gk2)
                    kv_ref[2 * mixer, rows, :] = _rope(k2, cos, sn, sp)
                    kv_ref[2 * mixer + 1, rows, :] = kv[:, KV_W:]
                return carry

            lax.fori_loop(0, n_blocks, project_kv, 0, unroll=LAT_UNROLL)
            k2a = jnp.concatenate([kv_ref[0], cka_ref[i].T], axis=0)
            v2at = jnp.concatenate([kv_ref[1].T, cva_ref[i]], axis=1)
            for j in range(N_KV):
                kcat_ref[j], wt_ref[j] = _kv_operands(k2a, v2at, j)

            def attend(m, carry):
                blocks = [m * LAT_UNROLL + u for u in range(LAT_UNROLL)]
                block_rows = [pl.ds(pl.multiple_of(n * ROWS, ROWS), ROWS) for n in blocks]
                for u, rows in enumerate(block_rows):
                    h = normed(rows)
                    cos, sn, sp = cos_ref[rows, :], sn_ref[rows, :], sp_ref[rows, :]
                    qa = jnp.dot(h, w_in(QA, MIX_W), preferred_element_type=F32)
                    for c0 in range(0, MIX_W, 128):
                        p_ref[u, :, LQA + c0:LQA + c0 + 128] = _rope(_rms_pair(qa[:, c0:c0 + 128], gq2),
                                                                     cos, sn, sp)
                    mid = jnp.dot(h, w_in(GA, 2 * MIX_W), preferred_element_type=F32)
                    p_ref[u, :, LGA:LGA + MIX_W] = _silu(mid[:, :MIX_W])
                    for c0 in range(0, MIX_W, 128):
                        qb = mid[:, MIX_W + c0:MIX_W + c0 + 128] * Q_SCALE
                        p_ref[u, :, LQB + c0:LQB + c0 + 128] = _rope(qb, cos, sn, sp)
                    p_ref[u, :, LGB:LGB + MIX_W] = _silu(
                        jnp.dot(h, w_in(GB, MIX_W), preferred_element_type=F32))
                per_block = [_lat_attention_items(n, u, i, sink_ref, kv_ref, kcat_ref, wt_ref,
                                                  (cka_ref, cva_ref, ckb_ref, cvb_ref))
                             for u, n in enumerate(blocks)]
                items = [it for group in zip(*per_block) for it in group]

                def write_out(it, value):
                    o_ref[it["u"], :, it["o"]:it["o"] + 128] = value

                _attention_chunks(items, lambda it: p_ref[it["u"], :, it["q"]:it["q"] + 128],
                                  lambda it: p_ref[it["u"], :, it["g"]:it["g"] + 128], write_out)
                for u, rows in enumerate(block_rows):
                    y = jnp.dot(o_ref[u], wout_buf[slot], preferred_element_type=F32)
                    xw_ref[rows, :] = xw_ref[rows, :] + gate * y
                return carry

            lax.fori_loop(0, n_blocks // LAT_UNROLL, attend, 0)
        else:
            def project(n, carry):
                rows = pl.ds(pl.multiple_of(n * ROWS, ROWS), ROWS)
                p = jnp.dot(normed(rows), w_in(0, ODD_IN), preferred_element_type=F32)
                fcos_ref[rows, :], fsin_ref[rows, :] = _spectrum_operands(p[:, 0:D_MODEL])
                gate_ref[rows, :] = _silu(p[:, D_MODEL:ODD_IN])
                return carry

            lax.fori_loop(0, n_blocks, project, 0, unroll=LAT_UNROLL)
            dft_scale = float(1.0 / np.sqrt(float(seq * C_GROUP_DIM)))

            def mix(n, carry):
                rows = pl.ds(pl.multiple_of(n * ROWS, ROWS), ROWS)
                m_cos = jnp.dot(cs_ref[rows, :], fcos_ref[...], preferred_element_type=F32)
                m_sin = jnp.dot(ss_ref[rows, :], fsin_ref[...], preferred_element_type=F32)

                def write_out(c0, value):
                    o_ref[0, :, c0:c0 + HALF_GROUP] = value

                _spectrum_combine(m_cos, m_sin, dft_scale,
                                  lambda c0: gate_ref[rows, c0:c0 + HALF_GROUP], write_out)
                y = jnp.dot(o_ref[0], wout_buf[slot], preferred_element_type=F32)
                xn = xw_ref[rows, :] + gate * y
                if l == DEPTH - 1:
                    xn = _rms_head(xn, fg_ref[...])
                xw_ref[rows, :] = xn
                return carry

            lax.fori_loop(0, n_blocks, mix, 0, unroll=LAT_UNROLL)

    y_out = pltpu.make_async_copy(xw_ref, y_hbm.at[b], x_sem.at[1])
    y_out.start()
    y_out.wait()


def _lat_path(x, mod, norm_g, ewin, ewout, owin, owout, gq2, gk2, sink, cos, sn, sp, caches, cs, ss, fg):
    b, s, _ = x.shape
    n_even, past = caches[0].shape[1], caches[0].shape[3]
    hbm = pl.BlockSpec(memory_space=pl.ANY)
    cache_spec = pl.BlockSpec((None, n_even, KV_W, past), lambda i: (i, 0, 0, 0))
    return pl.pallas_call(
        _lat_kernel,
        grid=(b,),
        in_specs=[
            pl.BlockSpec(memory_space=pltpu.SMEM),
            hbm,
            _resident(mod.shape), _resident(norm_g.shape),
            hbm, hbm, hbm, hbm,
            _resident(gq2.shape), _resident(gk2.shape),
            _resident(cos.shape), _resident(sn.shape), _resident(sp.shape),
            cache_spec, cache_spec, cache_spec, cache_spec,
            _resident(cs.shape), _resident(ss.shape), _resident(fg.shape),
        ],
        out_specs=hbm,
        out_shape=jax.ShapeDtypeStruct(x.shape, F32),
        scratch_shapes=[
            pltpu.VMEM((s, D_MODEL), F32),
            pltpu.VMEM((2, D_MODEL, EVEN_IN), BF16),
            pltpu.VMEM((2, D_MODEL, D_MODEL), BF16),
            pltpu.VMEM((4, s, KV_W), F32),
            pltpu.VMEM((N_KV, 2 * (s + past), KV_W), BF16),
            pltpu.VMEM((N_KV, 2 * HEAD_DIM + DEN_ROWS, 2 * (s + past)), BF16),
            pltpu.VMEM((LAT_UNROLL, ROWS, 4 * MIX_W), F32),
            pltpu.VMEM((LAT_UNROLL, ROWS, D_MODEL), BF16),
            pltpu.VMEM((s, D_MODEL), BF16),
            pltpu.VMEM((s, D_MODEL // 2), BF16),
            pltpu.VMEM((s, D_MODEL), F32),
            pltpu.SemaphoreType.DMA((2,)), pltpu.SemaphoreType.DMA((4,)),
        ],
        compiler_params=pltpu.CompilerParams(
            dimension_semantics=("arbitrary",), vmem_limit_bytes=LAT_VMEM_LIMIT),
        name="lat_path",
    )(sink, x, mod, norm_g, ewin, ewout, owin, owout, gq2, gk2, cos, sn, sp, *caches, cs, ss, fg)


def kernel(x_prompt, x_sample, cache_k_a, cache_v_a, cache_k_b, cache_v_b, c, c_ctx, norm_g, ada_w, ada_b,
           even_w_in, even_w_out, qk_g_q, qk_g_k, sink_logit, odd_w_in, odd_w_out, final_g):
    batch, seq, _ = x_prompt.shape
    dec_batch, dec_seq, _ = x_sample.shape
    n_even = even_w_in.shape[0]
    past = cache_k_a.shape[2]

    cond = jnp.concatenate(
        [c_ctx[None, :], c, jnp.zeros((COND_ROWS - 1 - dec_batch, D_MODEL), F32)], axis=0)
    mod = _ada_all(cond, ada_w, ada_b)

    epack = jnp.asarray(_packed_channel_dft()).astype(BF16)
    perm = jnp.asarray(_mirror_perm()).astype(BF16)
    cs_ctx, ss_ctx = (jnp.asarray(t).astype(BF16) for t in _dft_tables(seq))
    cs_lat, ss_lat = (jnp.asarray(t).astype(BF16) for t in _dft_tables(dec_seq))
    cos, sn, sp = (jnp.asarray(t) for t in _rope_tables(dec_seq))

    caches = [jnp.transpose(a, (0, 1, 3, 4, 2)).reshape(dec_batch, n_even, KV_W, past)
              for a in (cache_k_a, cache_v_a, cache_k_b, cache_v_b)]
    fg = final_g.reshape(1, D_MODEL)
    gq2 = jnp.tile(qk_g_q, (1, 2))
    gk2 = jnp.tile(qk_g_k, (1, 2))

    xc, *outs = _ctx_path(x_prompt, mod, norm_g, even_w_in, even_w_out, odd_w_in, odd_w_out, gq2, gk2,
                          sink_logit, epack, perm, cs_ctx, ss_ctx, fg)
    new_kv = [jnp.transpose(a, (0, 1, 4, 2, 3)) for a in outs[:4]]
    ewin, ewout, owin, owout = outs[4:]

    xl = _lat_path(x_sample, mod, norm_g, ewin, ewout, owin, owout, gq2, gk2, sink_logit, cos, sn, sp,
                   caches, cs_lat, ss_lat, fg)
    return (xc, xl, *new_kv)
```

```python
import numpy as np
import jax
import jax.numpy as jnp
from jax import lax
from jax.experimental import pallas as pl
from jax.experimental.pallas import tpu as pltpu

D_MODEL = 1024
DEPTH = 4
HEAD_DIM = 64
N_HEADS = 8
N_KV = 2
GROUP = N_HEADS // N_KV
MIX_W = N_HEADS * HEAD_DIM
KV_W = N_KV * HEAD_DIM
EVEN_IN = 2 * (2 * MIX_W + 2 * KV_W)
ODD_IN = 2 * D_MODEL
GRID_W = 64
WINDOW = 128
ROPE_BASE = 10000.0
C_GROUPS = 4
C_GROUP_DIM = D_MODEL // C_GROUPS
EPS = 1e-6
NEG_BIG = -1e30
ROWS = 256
COND_ROWS = 8
VMEM_LIMIT = 48 * 1024 * 1024
CTX_VMEM_LIMIT = 56 * 1024 * 1024
LAT_VMEM_LIMIT = 56 * 1024 * 1024
STAGE_ROWS = 128
STAGE_SLOTS = 4
CTX_BATCH_PER_STEP = 2
LAT_UNROLL = 1

QA, KA, VA, GA = 0, 512, 640, 768
QB, KB, VB, GB = 1280, 1792, 1920, 2048
LQA, LGA, LQB, LGB = 0, 512, 1024, 1536

F32 = jnp.float32
BF16 = jnp.bfloat16


def _dft_tables(n):
    k = np.arange(n, dtype=np.int64)
    ang = ((k[:, None] * k[None, :]) % n).astype(np.float64) * (2.0 * np.pi / n)
    return np.cos(ang).astype(np.float32), np.sin(ang).astype(np.float32)


HALF_GROUP = C_GROUP_DIM // 2


def _packed_channel_dft():
    c, s = _dft_tables(C_GROUP_DIM)
    return np.concatenate([c[:, :HALF_GROUP], c[:, HALF_GROUP:HALF_GROUP + 1], s[:, 1:HALF_GROUP]], axis=1)


def _mirror_perm():
    t = np.array([0] + [HALF_GROUP - m for m in range(1, HALF_GROUP)])
    p = np.zeros((HALF_GROUP, HALF_GROUP), np.float32)
    p[t, np.arange(HALF_GROUP)] = 1.0
    return p


def _rope_tables(n_tok):
    rows = n_tok // GRID_W
    row = np.repeat(np.arange(rows), GRID_W).astype(np.float64)
    col = np.tile(np.arange(GRID_W), rows).astype(np.float64)
    half = HEAD_DIM // 2
    inv = 1.0 / (ROPE_BASE ** (np.arange(0, half, 2, dtype=np.float64) / half))
    ang_r = row[:, None] * inv
    ang_c = col[:, None] * inv
    zeros = np.zeros_like(ang_r)
    cos_h = np.concatenate([np.cos(ang_r), np.cos(ang_r), np.cos(ang_c), np.cos(ang_c)], axis=1)
    nxt_h = np.concatenate([-np.sin(ang_r), zeros, -np.sin(ang_c), zeros], axis=1)
    prv_h = np.concatenate([zeros, np.sin(ang_r), zeros, np.sin(ang_c)], axis=1)
    two = lambda t: np.concatenate([t, t], axis=1).astype(np.float32)
    return two(cos_h), two(nxt_h), two(prv_h)


def _silu(x):
    return x * jax.nn.sigmoid(x)


def _norm_mod(x, g, shift, scale):
    ms = jnp.mean(x * x, axis=-1, keepdims=True)
    return x * lax.rsqrt(ms + EPS) * (g * (1.0 + scale)) + shift


def _rms_head(xh, g):
    ms = jnp.mean(xh * xh, axis=-1, keepdims=True)
    return xh * lax.rsqrt(ms + EPS) * g


def _rms_pair(xc, g2):
    lo = lax.broadcasted_iota(jnp.int32, xc.shape, 1) < HEAD_DIM
    ss = xc * xc
    s_lo = jnp.sum(jnp.where(lo, ss, 0.0), axis=-1, keepdims=True)
    s_hi = jnp.sum(jnp.where(lo, 0.0, ss), axis=-1, keepdims=True)
    ms = jnp.where(lo, s_lo, s_hi) * (1.0 / HEAD_DIM)
    return xc * lax.rsqrt(ms + EPS) * g2


def _rope(xc, cos, sin_next, sin_prev):
    nxt = pltpu.roll(xc, 128 - 16, 1)
    prv = pltpu.roll(xc, 16, 1)
    return xc * cos + nxt * sin_next + prv * sin_prev


DEN_ROWS = 16
LOG2E = float(np.log2(np.e))
Q_SCALE = (HEAD_DIM ** -0.5) * LOG2E


def _sink_pair(sink_ref, i, head):
    return sink_ref[i, head] * LOG2E, sink_ref[i, head + 1] * LOG2E


def _kv_operands(k2, v2t, j):
    tk = k2.shape[0]
    low = lax.broadcasted_iota(jnp.int32, k2.shape, 1) < HEAD_DIM
    km = jnp.where(low if j == 0 else jnp.logical_not(low), k2, 0.0)
    kr = pltpu.roll(km, HEAD_DIM, 1)
    k_lo, k_hi = (km, kr) if j == 0 else (kr, km)
    k_cat = jnp.concatenate([k_lo, k_hi], axis=0).astype(BF16)
    vjt = v2t[HEAD_DIM * j:HEAD_DIM * (j + 1), :]
    zero = jnp.zeros_like(vjt)
    row = lax.broadcasted_iota(jnp.int32, (DEN_ROWS, 2 * tk), 0)
    col = lax.broadcasted_iota(jnp.int32, (DEN_ROWS, 2 * tk), 1)
    ones = jnp.where(((row == 0) & (col < tk)) | ((row == 1) & (col >= tk)), 1.0, 0.0)
    w_t = jnp.concatenate([jnp.concatenate([vjt, zero], axis=1),
                           jnp.concatenate([zero, vjt], axis=1), ones], axis=0).astype(BF16)
    return k_cat, w_t


def _scores_t(qc, k_cat):
    return lax.dot_general(k_cat, qc.astype(BF16), (((1,), (1,)), ((), ())), preferred_element_type=F32)


def _softmax_t(s_t, sinks=None, mask_t=None):
    tk = s_t.shape[0] // 2
    es, ms = [], []
    for hh in range(2):
        sh = s_t[hh * tk:(hh + 1) * tk]
        if mask_t is not None:
            sh = jnp.where(mask_t, sh, NEG_BIG)
        m = jnp.max(sh, axis=0, keepdims=True)
        if sinks is not None:
            m = jnp.maximum(m, sinks[hh])
        es.append(jnp.exp2(sh - m))
        ms.append(m)
    return jnp.concatenate(es, axis=0).astype(BF16), ms


def _values_t(e_t, w_t):
    return jnp.dot(w_t, e_t, preferred_element_type=F32)


def _normalise_t(nd, ms, sinks=None):
    dens = [nd[2 * HEAD_DIM + hh:2 * HEAD_DIM + hh + 1] for hh in range(2)]
    if sinks is not None:
        dens = [dens[hh] + jnp.exp2(sinks[hh] - ms[hh]) for hh in range(2)]
    inv = [1.0 / d for d in dens]
    o_t = jnp.concatenate([nd[0:HEAD_DIM] * inv[0], nd[HEAD_DIM:2 * HEAD_DIM] * inv[1]], axis=0)
    return o_t.T


def _run_skewed(items, stages):
    state = list(items)
    for t in range(len(items) + len(stages) - 1):
        for k, stage in enumerate(stages):
            if 0 <= t - k < len(items):
                state[t - k] = stage(state[t - k])


def _attention_chunks(items, read_q, read_gate, write_out):
    def scores(it):
        return dict(it, s_t=_scores_t(read_q(it), it["k_cat"]))

    def softmax(it):
        e_t, ms = _softmax_t(it["s_t"], it["sinks"], it["mask"])
        return dict(it, e_t=e_t, ms=ms, s_t=None)

    def values(it):
        return dict(it, nd=_values_t(it["e_t"], it["w_t"]), e_t=None)

    def finish(it):
        o = _normalise_t(it["nd"], it["ms"], it["sinks"])
        write_out(it, (o * read_gate(it)).astype(BF16))
        return None

    _run_skewed(items, (scores, lambda it: it, softmax, values, lambda it: it, finish))


def _ada_kernel(cond_ref, w_ref, b_ref, o_ref):
    a = _silu(cond_ref[...]).astype(BF16)
    o_ref[...] = jnp.dot(a, w_ref[...].astype(BF16), preferred_element_type=F32) + b_ref[...]


def _ada_all(cond, ada_w, ada_b):
    return pl.pallas_call(
        _ada_kernel,
        grid=(DEPTH, 3),
        in_specs=[
            pl.BlockSpec((COND_ROWS, D_MODEL), lambda l, p: (0, 0)),
            pl.BlockSpec((None, D_MODEL, D_MODEL), lambda l, p: (l, 0, p)),
            pl.BlockSpec((None, None, 1, D_MODEL), lambda l, p: (l, p, 0, 0)),
        ],
        out_specs=pl.BlockSpec((None, None, COND_ROWS, D_MODEL), lambda l, p: (l, p, 0, 0)),
        out_shape=jax.ShapeDtypeStruct((DEPTH, 3, COND_ROWS, D_MODEL), F32),
        compiler_params=pltpu.CompilerParams(
            dimension_semantics=("parallel", "parallel"), vmem_limit_bytes=VMEM_LIMIT),
        name="ada_mod",
    )(cond, ada_w, ada_b.reshape(DEPTH, 3, 1, D_MODEL))


def _ctx_even_mix(i, sink_ref, gq_ref, gk_ref, kv_refs, p_ref, o_ref):
    n_b, seq, _ = p_ref.shape
    gq2 = gq_ref[i:i + 1, :] * Q_SCALE
    gk2 = gk_ref[i:i + 1, :]
    items = []
    for mixer, (q0, k0, v0, g0) in enumerate(((QA, KA, VA, GA), (QB, KB, VB, GB))):
        kv = []
        for bb in range(n_b):
            k2 = p_ref[bb, :, k0:k0 + KV_W]
            v2 = p_ref[bb, :, v0:v0 + KV_W]
            if mixer == 0:
                k2 = _rms_pair(k2, gk2)
            v2t = v2.T
            kv_refs[2 * mixer][bb, i] = k2.T.reshape(N_KV, HEAD_DIM, seq)
            kv_refs[2 * mixer + 1][bb, i] = v2t.reshape(N_KV, HEAD_DIM, seq)
            kv.append((k2, v2t))
        for j in range(N_KV):
            ops = [_kv_operands(k2, v2t, j) for k2, v2t in kv]
            for cc in range(GROUP // 2):
                c0 = 128 * (j * (GROUP // 2) + cc)
                sinks = None if mixer == 0 else _sink_pair(sink_ref, i, c0 // HEAD_DIM)
                for bb, (k_cat, w_t) in enumerate(ops):
                    items.append(dict(rows=bb, q=q0 + c0, g=g0 + c0, o=mixer * MIX_W + c0,
                                      norm=mixer == 0, sinks=sinks, mask=None, k_cat=k_cat, w_t=w_t))

    def read_q(it):
        qc = p_ref[it["rows"], :, it["q"]:it["q"] + 128]
        return _rms_pair(qc, gq2) if it["norm"] else qc * Q_SCALE

    def read_gate(it):
        return _silu(p_ref[it["rows"], :, it["g"]:it["g"] + 128])

    def write_out(it, value):
        o_ref[it["rows"], :, it["o"]:it["o"] + 128] = value

    _attention_chunks(items, read_q, read_gate, write_out)


def _spectrum_operands(t):
    first = lax.broadcasted_iota(jnp.int32, (t.shape[0], HALF_GROUP), 1) == 0
    lower = [t[:, C_GROUP_DIM * g:C_GROUP_DIM * g + HALF_GROUP] for g in range(C_GROUPS)]
    upper = [t[:, C_GROUP_DIM * g + HALF_GROUP:C_GROUP_DIM * (g + 1)] for g in range(C_GROUPS)]
    for_cos = jnp.concatenate(lower + [jnp.where(first, u, 0.0) for u in upper], axis=1).astype(BF16)
    for_sin = jnp.concatenate([jnp.where(first, 0.0, u) for u in upper], axis=1).astype(BF16)
    return for_cos, for_sin


def _spectrum_combine(m_cos, m_sin, scale, read_gate, write_out):
    first = lax.broadcasted_iota(jnp.int32, (m_cos.shape[0], HALF_GROUP), 1) == 0
    for g in range(C_GROUPS):
        p = m_cos[:, HALF_GROUP * g:HALF_GROUP * (g + 1)]
        p_mid = m_cos[:, D_MODEL // 2 + HALF_GROUP * g:D_MODEL // 2 + HALF_GROUP * (g + 1)]
        q = m_sin[:, HALF_GROUP * g:HALF_GROUP * (g + 1)]
        c0 = C_GROUP_DIM * g
        write_out(c0, ((p - q) * scale * read_gate(c0)).astype(BF16))
        upper = jnp.where(first, p_mid, p + q) * scale
        write_out(c0 + HALF_GROUP, (upper * read_gate(c0 + HALF_GROUP)).astype(BF16))


def _ctx_odd_mix(cs_ref, ss_ref, p_ref, o_ref):
    n_b, seq, _ = p_ref.shape
    scale = float(1.0 / np.sqrt(float(seq * C_GROUP_DIM)))
    for bb in range(n_b):
        for_cos, for_sin = _spectrum_operands(p_ref[bb, :, 0:D_MODEL])
        m_cos = jnp.dot(cs_ref[...], for_cos, preferred_element_type=F32)
        m_sin = jnp.dot(ss_ref[...], for_sin, preferred_element_type=F32)

        def read_gate(c0, bb=bb):
            return _silu(p_ref[bb, :, D_MODEL + c0:D_MODEL + c0 + HALF_GROUP])

        def write_out(c0, value, bb=bb):
            o_ref[bb, :, c0:c0 + HALF_GROUP] = value

        _spectrum_combine(m_cos, m_sin, scale, read_gate, write_out)


def _fold_odd_weights(owin_ref, owout_ref, epack_ref, perm_ref):
    for layer in range(owin_ref.shape[0]):
        for g in range(C_GROUPS):
            c0 = C_GROUP_DIM * g
            owin_ref[layer, :, c0:c0 + C_GROUP_DIM] = jnp.dot(
                owin_ref[layer, :, c0:c0 + C_GROUP_DIM], epack_ref[...],
                preferred_element_type=F32).astype(BF16)
            g0 = D_MODEL + c0 + HALF_GROUP
            owin_ref[layer, :, g0:g0 + HALF_GROUP] = jnp.dot(
                owin_ref[layer, :, g0:g0 + HALF_GROUP], perm_ref[...], preferred_element_type=F32).astype(BF16)
            r0 = c0 + HALF_GROUP
            owout_ref[layer, r0:r0 + HALF_GROUP, :] = jnp.dot(
                perm_ref[...], owout_ref[layer, r0:r0 + HALF_GROUP, :], preferred_element_type=F32).astype(BF16)


def _stage_copy(w_hbm, stage_ref, sem, chunk, slot):
    _, n_rows, cols = w_hbm.shape
    rows = stage_ref.shape[1]
    per_layer = n_rows // rows
    src = w_hbm.at[chunk // per_layer, pl.ds((chunk % per_layer) * rows, rows), :]
    return pltpu.make_async_copy(src, stage_ref.at[slot, :, pl.ds(0, cols)], sem.at[slot])


def _convert_weights(w_hbm, w_ref, stage_ref, in_sem):
    n_layers, n_rows, cols = w_hbm.shape
    n_slots, rows, _ = stage_ref.shape
    per_layer = n_rows // rows
    n_chunks = n_layers * per_layer
    for c in range(n_slots - 1):
        _stage_copy(w_hbm, stage_ref, in_sem, c, c).start()

    def body(c, carry):
        slot = c % n_slots
        ahead = c + n_slots - 1

        @pl.when(ahead < n_chunks)
        def _():
            _stage_copy(w_hbm, stage_ref, in_sem, ahead, ahead % n_slots).start()

        _stage_copy(w_hbm, stage_ref, in_sem, c, slot).wait()
        r0 = pl.multiple_of((c % per_layer) * rows, rows)
        w_ref[c // per_layer, pl.ds(r0, rows), :] = stage_ref[slot, :, 0:cols].astype(BF16)
        return carry

    lax.fori_loop(0, n_chunks, body, 0)


def _ctx_kernel(sink_ref, x_ref, mod_ref, g_ref, ewin_hbm, ewout_hbm, owin_hbm, owout_hbm, gq_ref, gk_ref,
                epack_ref, perm_ref, cs_ref, ss_ref, fg_ref,
                xo_ref, ka_ref, va_ref, kb_ref, vb_ref, ewin_o, ewout_o, owin_o, owout_o,
                xs_ref, p_ref, o_ref, ewin_ref, ewout_ref, owin_ref, owout_ref,
                stage_ref, in_sem, out_sem):
    weights = ((ewin_hbm, ewin_o, ewin_ref), (ewout_hbm, ewout_o, ewout_ref),
               (owin_hbm, owin_o, owin_ref), (owout_hbm, owout_o, owout_ref))
    exports = [pltpu.make_async_copy(res, dst, out_sem.at[k]) for k, (_, dst, res) in enumerate(weights)]

    @pl.when(pl.program_id(0) == 0)
    def _():
        for src, _, res in weights:
            _convert_weights(src, res, stage_ref, in_sem)
        _fold_odd_weights(owin_ref, owout_ref, epack_ref, perm_ref)
        for export in exports:
            export.start()

    @pl.when(pl.program_id(0) == pl.num_programs(0) - 1)
    def _():
        for export in exports:
            export.wait()

    n_b = x_ref.shape[0]
    for l in range(DEPTH):
        i = l // 2
        even = l % 2 == 0
        win_ref, wout_ref, width = (ewin_ref, ewout_ref, EVEN_IN) if even else (owin_ref, owout_ref, ODD_IN)
        shift, scale, gate = (mod_ref[l, part, 0:1, :] for part in range(3))
        for bb in range(n_b):
            x = x_ref[bb] if l == 0 else xs_ref[bb]
            h = _norm_mod(x, g_ref[l:l + 1, :], shift, scale)
            p_ref[bb, :, 0:width] = jnp.dot(h.astype(BF16), win_ref[i], preferred_element_type=F32)
        if even:
            _ctx_even_mix(i, sink_ref, gq_ref, gk_ref, (ka_ref, va_ref, kb_ref, vb_ref), p_ref, o_ref)
        else:
            _ctx_odd_mix(cs_ref, ss_ref, p_ref, o_ref)
        for bb in range(n_b):
            y = jnp.dot(o_ref[bb], wout_ref[i], preferred_element_type=F32)
            xn = (x_ref[bb] if l == 0 else xs_ref[bb]) + gate * y
            if l == DEPTH - 1:
                xo_ref[bb] = _rms_head(xn, fg_ref[...])
            else:
                xs_ref[bb] = xn


def _resident(shape):
    return pl.BlockSpec(shape, lambda i: (0,) * len(shape), pipeline_mode=pl.Buffered(1))


def _ctx_path(x, mod, norm_g, ewin, ewout, owin, owout, gq2, gk2, sink, epack, perm, cs, ss, fg):
    b, s, _ = x.shape
    n_even = ewin.shape[0]
    bb = CTX_BATCH_PER_STEP
    kv_shape = jax.ShapeDtypeStruct((b, n_even, N_KV, HEAD_DIM, s), F32)
    kv_spec = pl.BlockSpec((bb, n_even, N_KV, HEAD_DIM, s), lambda i: (i, 0, 0, 0, 0))
    x_spec = pl.BlockSpec((bb, s, D_MODEL), lambda i: (i, 0, 0))
    hbm = pl.BlockSpec(memory_space=pl.ANY)
    weights = (ewin, ewout, owin, owout)
    return pl.pallas_call(
        _ctx_kernel,
        grid=(b // bb,),
        in_specs=[
            pl.BlockSpec(memory_space=pltpu.SMEM),
            x_spec,
            _resident(mod.shape), _resident(norm_g.shape),
            hbm, hbm, hbm, hbm,
            _resident(gq2.shape), _resident(gk2.shape),
            _resident(epack.shape), _resident(perm.shape), _resident(cs.shape), _resident(ss.shape),
            _resident(fg.shape),
        ],
        out_specs=[x_spec, kv_spec, kv_spec, kv_spec, kv_spec, hbm, hbm, hbm, hbm],
        out_shape=[jax.ShapeDtypeStruct(x.shape, F32), kv_shape, kv_shape, kv_shape, kv_shape]
                  + [jax.ShapeDtypeStruct(w.shape, BF16) for w in weights],
        scratch_shapes=[pltpu.VMEM((bb, s, D_MODEL), F32), pltpu.VMEM((bb, s, EVEN_IN), F32),
                        pltpu.VMEM((bb, s, D_MODEL), BF16)]
                       + [pltpu.VMEM(w.shape, BF16) for w in weights]
                       + [pltpu.VMEM((STAGE_SLOTS, STAGE_ROWS, max(w.shape[2] for w in weights)), F32),
                          pltpu.SemaphoreType.DMA((STAGE_SLOTS,)), pltpu.SemaphoreType.DMA((len(weights),))],
        compiler_params=pltpu.CompilerParams(
            dimension_semantics=("arbitrary",), vmem_limit_bytes=CTX_VMEM_LIMIT),
        name="ctx_path",
    )(sink, x, mod, norm_g, *weights, gq2, gk2, epack, perm, cs, ss, fg)


def _lat_weight_copies(l, w_hbms, win_buf, wout_buf, w_sem):
    ewin_hbm, ewout_hbm, owin_hbm, owout_hbm = w_hbms
    i, slot = l // 2, l % 2
    w_in, w_out = (ewin_hbm, ewout_hbm) if l % 2 == 0 else (owin_hbm, owout_hbm)
    return (pltpu.make_async_copy(w_in.at[i], win_buf.at[slot, :, pl.ds(0, w_in.shape[2])], w_sem.at[2 * slot]),
            pltpu.make_async_copy(w_out.at[i], wout_buf.at[slot], w_sem.at[2 * slot + 1]))


def _lat_attention_items(n, u, layer_i, sink_ref, kv_ref, kcat_ref, wt_ref, cache_refs):
    seq = kv_ref.shape[1]
    ckb_t, cvb_t = cache_refs[2][layer_i], cache_refs[3][layer_i]
    prev0 = pl.multiple_of(jnp.maximum(n * ROWS - WINDOW, 0), WINDOW)
    own0 = pl.multiple_of(n * ROWS, ROWS)
    next0 = pl.multiple_of(jnp.minimum(n * ROWS + ROWS, seq - WINDOW), WINDOW)
    win_len = ROWS + 2 * WINDOW
    ctx_len = ckb_t.shape[1]
    cj = lax.broadcasted_iota(jnp.int32, (win_len + ctx_len, ROWS), 0)
    qi = lax.broadcasted_iota(jnp.int32, (win_len + ctx_len, ROWS), 1)
    kpos = n * ROWS - WINDOW + cj
    in_win = (jnp.abs(cj - WINDOW - qi) <= WINDOW) & (kpos >= 0) & (kpos < seq)
    mask_b = in_win | (cj >= win_len)

    def window(idx):
        return jnp.concatenate([kv_ref[idx, pl.ds(prev0, WINDOW), :], kv_ref[idx, pl.ds(own0, ROWS), :],
                                kv_ref[idx, pl.ds(next0, WINDOW), :]], axis=0)

    k2b = jnp.concatenate([window(2), ckb_t.T], axis=0)
    v2bt = jnp.concatenate([window(3).T, cvb_t], axis=1)
    items = []
    for mixer, (q0, g0) in enumerate(((LQA, LGA), (LQB, LGB))):
        for j in range(N_KV):
            k_cat, w_t = (kcat_ref[j], wt_ref[j]) if mixer == 0 else _kv_operands(k2b, v2bt, j)
            for cc in range(GROUP // 2):
                c0 = 128 * (j * (GROUP // 2) + cc)
                if mixer == 0:
                    sinks, mask = None, None
                else:
                    sinks, mask = _sink_pair(sink_ref, layer_i, c0 // HEAD_DIM), mask_b
                items.append(dict(u=u, q=q0 + c0, g=g0 + c0, o=mixer * MIX_W + c0, sinks=sinks, mask=mask,
                                  k_cat=k_cat, w_t=w_t))
    return items


def _lat_kernel(sink_ref, x_hbm, mod_ref, g_ref, ewin_hbm, ewout_hbm, owin_hbm, owout_hbm, gq_ref, gk_ref,
                cos_ref, sn_ref, sp_ref, cka_ref, cva_ref, ckb_ref, cvb_ref, cs_ref, ss_ref, fg_ref,
                y_hbm,
                xw_ref, win_buf, wout_buf, kv_ref, kcat_ref, wt_ref, p_ref, o_ref, fcos_ref, fsin_ref, gate_ref,
                x_sem, w_sem):
    b = pl.program_id(0)
    seq = xw_ref.shape[0]
    n_blocks = seq // ROWS
    w_hbms = (ewin_hbm, ewout_hbm, owin_hbm, owout_hbm)

    x_in = pltpu.make_async_copy(x_hbm.at[b], xw_ref, x_sem.at[0])
    x_in.start()
    for copy in _lat_weight_copies(0, w_hbms, win_buf, wout_buf, w_sem):
        copy.start()
    x_in.wait()

    for l in range(DEPTH):
        i, slot = l // 2, l % 2
        for copy in _lat_weight_copies(l, w_hbms, win_buf, wout_buf, w_sem):
            copy.wait()
        if l + 1 < DEPTH:
            for copy in _lat_weight_copies(l + 1, w_hbms, win_buf, wout_buf, w_sem):
                copy.start()
        shift, scale, gate = (mod_ref[l, part, pl.ds(1 + b, 1), :] for part in range(3))
        g = g_ref[l:l + 1, :]

        def normed(rows):
            return _norm_mod(xw_ref[rows, :], g, shift, scale).astype(BF16)

        def w_in(c0, width):
            return win_buf[slot, :, c0:c0 + width]

        if l % 2 == 0:
            gq2 = gq_ref[i:i + 1, :] * Q_SCALE
            gk2 = gk_ref[i:i + 1, :]

            def project_kv(n, carry):
                rows = pl.ds(pl.multiple_of(n * ROWS, ROWS), ROWS)
                h = normed(rows)
                cos, sn, sp = cos_ref[rows, :], sn_ref[rows, :], sp_ref[rows, :]
                for mixer, c0 in enumerate((KA, KB)):
                    kv = jnp.dot(h, w_in(c0, 2 * KV_W), preferred_element_type=F32)
                    k2 = kv[:, :KV_W]
                    if mixer == 0:
                        k2 = _rms_pair(k2, gk2)
                    kv_ref[2 * mixer, rows, :] = _rope(k2, cos, sn, sp)
                    kv_ref[2 * mixer + 1, rows, :] = kv[:, KV_W:]
                return carry

            lax.fori_loop(0, n_blocks, project_kv, 0, unroll=LAT_UNROLL)
            k2a = jnp.concatenate([kv_ref[0], cka_ref[i].T], axis=0)
            v2at = jnp.concatenate([kv_ref[1].T, cva_ref[i]], axis=1)
            for j in range(N_KV):
                kcat_ref[j], wt_ref[j] = _kv_operands(k2a, v2at, j)

            def attend(m, carry):
                blocks = [m * LAT_UNROLL + u for u in range(LAT_UNROLL)]
                block_rows = [pl.ds(pl.multiple_of(n * ROWS, ROWS), ROWS) for n in blocks]
                for u, rows in enumerate(block_rows):
                    h = normed(rows)
                    cos, sn, sp = cos_ref[rows, :], sn_ref[rows, :], sp_ref[rows, :]
                    qa = jnp.dot(h, w_in(QA, MIX_W), preferred_element_type=F32)
                    for c0 in range(0, MIX_W, 128):
                        p_ref[u, :, LQA + c0:LQA + c0 + 128] = _rope(_rms_pair(qa[:, c0:c0 + 128], gq2),
                                                                     cos, sn, sp)
                    mid = jnp.dot(h, w_in(GA, 2 * MIX_W), preferred_element_type=F32)
                    p_ref[u, :, LGA:LGA + MIX_W] = _silu(mid[:, :MIX_W])
                    for c0 in range(0, MIX_W, 128):
                        qb = mid[:, MIX_W + c0:MIX_W + c0 + 128] * Q_SCALE
                        p_ref[u, :, LQB + c0:LQB + c0 + 128] = _rope(qb, cos, sn, sp)
                    p_ref[u, :, LGB:LGB + MIX_W] = _silu(
                        jnp.dot(h, w_in(GB, MIX_W), preferred_element_type=F32))
                per_block = [_lat_attention_items(n, u, i, sink_ref, kv_ref, kcat_ref, wt_ref,
                                                  (cka_ref, cva_ref, ckb_ref, cvb_ref))
                             for u, n in enumerate(blocks)]
                items = [it for group in zip(*per_block) for it in group]

                def write_out(it, value):
                    o_ref[it["u"], :, it["o"]:it["o"] + 128] = value

                _attention_chunks(items, lambda it: p_ref[it["u"], :, it["q"]:it["q"] + 128],
                                  lambda it: p_ref[it["u"], :, it["g"]:it["g"] + 128], write_out)
                for u, rows in enumerate(block_rows):
                    y = jnp.dot(o_ref[u], wout_buf[slot], preferred_element_type=F32)
                    xw_ref[rows, :] = xw_ref[rows, :] + gate * y
                return carry

            lax.fori_loop(0, n_blocks // LAT_UNROLL, attend, 0)
        else:
            def project(n, carry):
                rows = pl.ds(pl.multiple_of(n * ROWS, ROWS), ROWS)
                p = jnp.dot(normed(rows), w_in(0, ODD_IN), preferred_element_type=F32)
                fcos_ref[rows, :], fsin_ref[rows, :] = _spectrum_operands(p[:, 0:D_MODEL])
                gate_ref[rows, :] = _silu(p[:, D_MODEL:ODD_IN])
                return carry

            lax.fori_loop(0, n_blocks, project, 0, unroll=LAT_UNROLL)
            dft_scale = float(1.0 / np.sqrt(float(seq * C_GROUP_DIM)))

            def mix(n, carry):
                rows = pl.ds(pl.multiple_of(n * ROWS, ROWS), ROWS)
                m_cos = jnp.dot(cs_ref[rows, :], fcos_ref[...], preferred_element_type=F32)
                m_sin = jnp.dot(ss_ref[rows, :], fsin_ref[...], preferred_element_type=F32)

                def write_out(c0, value):
                    o_ref[0, :, c0:c0 + HALF_GROUP] = value

                _spectrum_combine(m_cos, m_sin, dft_scale,
                                  lambda c0: gate_ref[rows, c0:c0 + HALF_GROUP], write_out)
                y = jnp.dot(o_ref[0], wout_buf[slot], preferred_element_type=F32)
                xn = xw_ref[rows, :] + gate * y
                if l == DEPTH - 1:
                    xn = _rms_head(xn, fg_ref[...])
                xw_ref[rows, :] = xn
                return carry

            lax.fori_loop(0, n_blocks, mix, 0, unroll=LAT_UNROLL)

    y_out = pltpu.make_async_copy(xw_ref, y_hbm.at[b], x_sem.at[1])
    y_out.start()
    y_out.wait()


def _lat_path(x, mod, norm_g, ewin, ewout, owin, owout, gq2, gk2, sink, cos, sn, sp, caches, cs, ss, fg):
    b, s, _ = x.shape
    n_even, past = caches[0].shape[1], caches[0].shape[3]
    hbm = pl.BlockSpec(memory_space=pl.ANY)
    cache_spec = pl.BlockSpec((None, n_even, KV_W, past), lambda i: (i, 0, 0, 0))
    return pl.pallas_call(
        _lat_kernel,
        grid=(b,),
        in_specs=[
            pl.BlockSpec(memory_space=pltpu.SMEM),
            hbm,
            _resident(mod.shape), _resident(norm_g.shape),
            hbm, hbm, hbm, hbm,
            _resident(gq2.shape), _resident(gk2.shape),
            _resident(cos.shape), _resident(sn.shape), _resident(sp.shape),
            cache_spec, cache_spec, cache_spec, cache_spec,
            _resident(cs.shape), _resident(ss.shape), _resident(fg.shape),
        ],
        out_specs=hbm,
        out_shape=jax.ShapeDtypeStruct(x.shape, F32),
        scratch_shapes=[
            pltpu.VMEM((s, D_MODEL), F32),
            pltpu.VMEM((2, D_MODEL, EVEN_IN), BF16),
            pltpu.VMEM((2, D_MODEL, D_MODEL), BF16),
            pltpu.VMEM((4, s, KV_W), F32),
            pltpu.VMEM((N_KV, 2 * (s + past), KV_W), BF16),
            pltpu.VMEM((N_KV, 2 * HEAD_DIM + DEN_ROWS, 2 * (s + past)), BF16),
            pltpu.VMEM((LAT_UNROLL, ROWS, 4 * MIX_W), F32),
            pltpu.VMEM((LAT_UNROLL, ROWS, D_MODEL), BF16),
            pltpu.VMEM((s, D_MODEL), BF16),
            pltpu.VMEM((s, D_MODEL // 2), BF16),
            pltpu.VMEM((s, D_MODEL), F32),
            pltpu.SemaphoreType.DMA((2,)), pltpu.SemaphoreType.DMA((4,)),
        ],
        compiler_params=pltpu.CompilerParams(
            dimension_semantics=("arbitrary",), vmem_limit_bytes=LAT_VMEM_LIMIT),
        name="lat_path",
    )(sink, x, mod, norm_g, ewin, ewout, owin, owout, gq2, gk2, cos, sn, sp, *caches, cs, ss, fg)


def kernel(x_prompt, x_sample, cache_k_a, cache_v_a, cache_k_b, cache_v_b, c, c_ctx, norm_g, ada_w, ada_b,
           even_w_in, even_w_out, qk_g_q, qk_g_k, sink_logit, odd_w_in, odd_w_out, final_g):
    batch, seq, _ = x_prompt.shape
    dec_batch, dec_seq, _ = x_sample.shape
    n_even = even_w_in.shape[0]
    past = cache_k_a.shape[2]

    cond = jnp.concatenate(
        [c_ctx[None, :], c, jnp.zeros((COND_ROWS - 1 - dec_batch, D_MODEL), F32)], axis=0)
    mod = _ada_all(cond, ada_w, ada_b)

    epack = jnp.asarray(_packed_channel_dft()).astype(BF16)
    perm = jnp.asarray(_mirror_perm()).astype(BF16)
    cs_ctx, ss_ctx = (jnp.asarray(t).astype(BF16) for t in _dft_tables(seq))
    cs_lat, ss_lat = (jnp.asarray(t).astype(BF16) for t in _dft_tables(dec_seq))
    cos, sn, sp = (jnp.asarray(t) for t in _rope_tables(dec_seq))

    caches = [jnp.transpose(a, (0, 1, 3, 4, 2)).reshape(dec_batch, n_even, KV_W, past)
              for a in (cache_k_a, cache_v_a, cache_k_b, cache_v_b)]
    fg = final_g.reshape(1, D_MODEL)
    gq2 = jnp.tile(qk_g_q, (1, 2))
    gk2 = jnp.tile(qk_g_k, (1, 2))

    xc, *outs = _ctx_path(x_prompt, mod, norm_g, even_w_in, even_w_out, odd_w_in, odd_w_out, gq2, gk2,
                          sink_logit, epack, perm, cs_ctx, ss_ctx, fg)
    new_kv = [jnp.transpose(a, (0, 1, 4, 2, 3)) for a in outs[:4]]
    ewin, ewout, owin, owout = outs[4:]

    xl = _lat_path(x_sample, mod, norm_g, ewin, ewout, owin, owout, gq2, gk2, sink_logit, cos, sn, sp,
                   caches, cs_lat, ss_lat, fg)
    return (xc, xl, *new_kv)
```

```python
import numpy as np
import jax
import jax.numpy as jnp
from jax import lax
from jax.experimental import pallas as pl
from jax.experimental.pallas import tpu as pltpu

D_MODEL = 1024
DEPTH = 4
HEAD_DIM = 64
N_HEADS = 8
N_KV = 2
GROUP = N_HEADS // N_KV
MIX_W = N_HEADS * HEAD_DIM
KV_W = N_KV * HEAD_DIM
EVEN_IN = 2 * (2 * MIX_W + 2 * KV_W)
ODD_IN = 2 * D_MODEL
GRID_W = 64
WINDOW = 128
ROPE_BASE = 10000.0
C_GROUPS = 4
C_GROUP_DIM = D_MODEL // C_GROUPS
EPS = 1e-6
NEG_BIG = -1e30
ROWS = 256
COND_ROWS = 8
VMEM_LIMIT = 48 * 1024 * 1024
CTX_VMEM_LIMIT = 56 * 1024 * 1024
LAT_VMEM_LIMIT = 56 * 1024 * 1024
STAGE_ROWS = 128
STAGE_SLOTS = 4
CTX_BATCH_PER_STEP = 2
SKEW_GAPS = (1, 0, 1)
LAT_UNROLL = 1

QA, KA, VA, GA = 0, 512, 640, 768
QB, KB, VB, GB = 1280, 1792, 1920, 2048
LQA, LGA, LQB, LGB = 0, 512, 1024, 1536

F32 = jnp.float32
BF16 = jnp.bfloat16


def _dft_tables(n):
    k = np.arange(n, dtype=np.int64)
    ang = ((k[:, None] * k[None, :]) % n).astype(np.float64) * (2.0 * np.pi / n)
    return np.cos(ang).astype(np.float32), np.sin(ang).astype(np.float32)


HALF_GROUP = C_GROUP_DIM // 2


def _packed_channel_dft():
    c, s = _dft_tables(C_GROUP_DIM)
    return np.concatenate([c[:, :HALF_GROUP], c[:, HALF_GROUP:HALF_GROUP + 1], s[:, 1:HALF_GROUP]], axis=1)


def _position_dft(n):
    scale = np.float32(1.0 / np.sqrt(float(n * C_GROUP_DIM)))
    c, s = _dft_tables(n)
    return c * scale, s * scale


def _mirror_perm():
    t = np.array([0] + [HALF_GROUP - m for m in range(1, HALF_GROUP)])
    p = np.zeros((HALF_GROUP, HALF_GROUP), np.float32)
    p[t, np.arange(HALF_GROUP)] = 1.0
    return p


def _rope_tables(n_tok):
    rows = n_tok // GRID_W
    row = np.repeat(np.arange(rows), GRID_W).astype(np.float64)
    col = np.tile(np.arange(GRID_W), rows).astype(np.float64)
    half = HEAD_DIM // 2
    inv = 1.0 / (ROPE_BASE ** (np.arange(0, half, 2, dtype=np.float64) / half))
    ang_r = row[:, None] * inv
    ang_c = col[:, None] * inv
    zeros = np.zeros_like(ang_r)
    cos_h = np.concatenate([np.cos(ang_r), np.cos(ang_r), np.cos(ang_c), np.cos(ang_c)], axis=1)
    nxt_h = np.concatenate([-np.sin(ang_r), zeros, -np.sin(ang_c), zeros], axis=1)
    prv_h = np.concatenate([zeros, np.sin(ang_r), zeros, np.sin(ang_c)], axis=1)
    two = lambda t: np.concatenate([t, t], axis=1).astype(np.float32)
    return two(cos_h), two(nxt_h), two(prv_h)


def _silu(x):
    return x * jax.nn.sigmoid(x)


def _norm_mod(x, g, shift, scale):
    ms = jnp.mean(x * x, axis=-1, keepdims=True)
    return x * lax.rsqrt(ms + EPS) * (g * (1.0 + scale)) + shift


def _rms_head(xh, g):
    ms = jnp.mean(xh * xh, axis=-1, keepdims=True)
    return xh * lax.rsqrt(ms + EPS) * g


def _rms_pair(xc, g2):
    lo = lax.broadcasted_iota(jnp.int32, xc.shape, 1) < HEAD_DIM
    ss = xc * xc
    s_lo = jnp.sum(jnp.where(lo, ss, 0.0), axis=-1, keepdims=True)
    s_hi = jnp.sum(jnp.where(lo, 0.0, ss), axis=-1, keepdims=True)
    ms = jnp.where(lo, s_lo, s_hi) * (1.0 / HEAD_DIM)
    return xc * lax.rsqrt(ms + EPS) * g2


def _rope(xc, cos, sin_next, sin_prev):
    nxt = pltpu.roll(xc, 128 - 16, 1)
    prv = pltpu.roll(xc, 16, 1)
    return xc * cos + nxt * sin_next + prv * sin_prev


DEN_ROWS = 16
LOG2E = float(np.log2(np.e))
Q_SCALE = (HEAD_DIM ** -0.5) * LOG2E


def _sink_pair(sink_ref, i, head):
    return sink_ref[i, head] * LOG2E, sink_ref[i, head + 1] * LOG2E


def _kv_operands(k2, v2t, j):
    tk = k2.shape[0]
    low = lax.broadcasted_iota(jnp.int32, k2.shape, 1) < HEAD_DIM
    km = jnp.where(low if j == 0 else jnp.logical_not(low), k2, 0.0)
    kr = pltpu.roll(km, HEAD_DIM, 1)
    k_lo, k_hi = (km, kr) if j == 0 else (kr, km)
    k_cat = jnp.concatenate([k_lo, k_hi], axis=0).astype(BF16)
    vjt = v2t[HEAD_DIM * j:HEAD_DIM * (j + 1), :]
    zero = jnp.zeros_like(vjt)
    row = lax.broadcasted_iota(jnp.int32, (DEN_ROWS, 2 * tk), 0)
    col = lax.broadcasted_iota(jnp.int32, (DEN_ROWS, 2 * tk), 1)
    ones = jnp.where(((row == 0) & (col < tk)) | ((row == 1) & (col >= tk)), 1.0, 0.0)
    w_t = jnp.concatenate([jnp.concatenate([vjt, zero], axis=1),
                           jnp.concatenate([zero, vjt], axis=1), ones], axis=0).astype(BF16)
    return k_cat, w_t


def _scores_t(qc, k_cat):
    return lax.dot_general(k_cat, qc.astype(BF16), (((1,), (1,)), ((), ())), preferred_element_type=F32)


def _softmax_t(s_t, sinks=None, mask_t=None):
    tk = s_t.shape[0] // 2
    es, ms = [], []
    for hh in range(2):
        sh = s_t[hh * tk:(hh + 1) * tk]
        if mask_t is not None:
            sh = jnp.where(mask_t, sh, NEG_BIG)
        m = jnp.max(sh, axis=0, keepdims=True)
        if sinks is not None:
            m = jnp.maximum(m, sinks[hh])
        es.append(jnp.exp2(sh - m))
        ms.append(m)
    return jnp.concatenate(es, axis=0).astype(BF16), ms


def _values_t(e_t, w_t):
    return jnp.dot(w_t, e_t, preferred_element_type=F32)


def _normalise_t(nd, ms, sinks=None):
    dens = [nd[2 * HEAD_DIM + hh:2 * HEAD_DIM + hh + 1] for hh in range(2)]
    if sinks is not None:
        dens = [dens[hh] + jnp.exp2(sinks[hh] - ms[hh]) for hh in range(2)]
    inv = [1.0 / d for d in dens]
    o_t = jnp.concatenate([nd[0:HEAD_DIM] * inv[0], nd[HEAD_DIM:2 * HEAD_DIM] * inv[1]], axis=0)
    return o_t.T


def _run_skewed(items, stages):
    state = list(items)
    for t in range(len(items) + len(stages) - 1):
        for k, stage in enumerate(stages):
            if 0 <= t - k < len(items):
                state[t - k] = stage(state[t - k])


def _attention_chunks(items, read_q, read_gate, write_out):
    def scores(it):
        return dict(it, s_t=_scores_t(read_q(it), it["k_cat"]))

    def softmax(it):
        e_t, ms = _softmax_t(it["s_t"], it["sinks"], it["mask"])
        return dict(it, e_t=e_t, ms=ms, s_t=None)

    def values(it):
        return dict(it, nd=_values_t(it["e_t"], it["w_t"]), e_t=None)

    def finish(it):
        o = _normalise_t(it["nd"], it["ms"], it["sinks"])
        write_out(it, (o * read_gate(it)).astype(BF16))
        return None

    def hold(it):
        return it

    stages = [scores]
    for gap, stage in zip(SKEW_GAPS, (softmax, values, finish)):
        stages += [hold] * gap + [stage]
    _run_skewed(items, stages)


def _ada_kernel(cond_ref, w_ref, b_ref, o_ref):
    a = _silu(cond_ref[...]).astype(BF16)
    o_ref[...] = jnp.dot(a, w_ref[...].astype(BF16), preferred_element_type=F32) + b_ref[...]


def _ada_all(cond, ada_w, ada_b):
    return pl.pallas_call(
        _ada_kernel,
        grid=(DEPTH, 3),
        in_specs=[
            pl.BlockSpec((COND_ROWS, D_MODEL), lambda l, p: (0, 0)),
            pl.BlockSpec((None, D_MODEL, D_MODEL), lambda l, p: (l, 0, p)),
            pl.BlockSpec((None, None, 1, D_MODEL), lambda l, p: (l, p, 0, 0)),
        ],
        out_specs=pl.BlockSpec((None, None, COND_ROWS, D_MODEL), lambda l, p: (l, p, 0, 0)),
        out_shape=jax.ShapeDtypeStruct((DEPTH, 3, COND_ROWS, D_MODEL), F32),
        compiler_params=pltpu.CompilerParams(
            dimension_semantics=("parallel", "parallel"), vmem_limit_bytes=VMEM_LIMIT),
        name="ada_mod",
    )(cond, ada_w, ada_b.reshape(DEPTH, 3, 1, D_MODEL))


def _ctx_even_mix(i, sink_ref, gq_ref, gk_ref, kv_refs, p_ref, o_ref):
    n_b, seq, _ = p_ref.shape
    gq2 = gq_ref[i:i + 1, :] * Q_SCALE
    gk2 = gk_ref[i:i + 1, :]
    items = []
    for mixer, (q0, k0, v0, g0) in enumerate(((QA, KA, VA, GA), (QB, KB, VB, GB))):
        kv = []
        for bb in range(n_b):
            k2 = p_ref[bb, :, k0:k0 + KV_W]
            v2 = p_ref[bb, :, v0:v0 + KV_W]
            if mixer == 0:
                k2 = _rms_pair(k2, gk2)
            v2t = v2.T
            kv_refs[2 * mixer][bb, i] = k2.T.reshape(N_KV, HEAD_DIM, seq)
            kv_refs[2 * mixer + 1][bb, i] = v2t.reshape(N_KV, HEAD_DIM, seq)
            kv.append((k2, v2t))
        for j in range(N_KV):
            ops = [_kv_operands(k2, v2t, j) for k2, v2t in kv]
            for cc in range(GROUP // 2):
                c0 = 128 * (j * (GROUP // 2) + cc)
                sinks = None if mixer == 0 else _sink_pair(sink_ref, i, c0 // HEAD_DIM)
                for bb, (k_cat, w_t) in enumerate(ops):
                    items.append(dict(rows=bb, q=q0 + c0, g=g0 + c0, o=mixer * MIX_W + c0,
                                      norm=mixer == 0, sinks=sinks, mask=None, k_cat=k_cat, w_t=w_t))

    def read_q(it):
        qc = p_ref[it["rows"], :, it["q"]:it["q"] + 128]
        return _rms_pair(qc, gq2) if it["norm"] else qc * Q_SCALE

    def read_gate(it):
        return _silu(p_ref[it["rows"], :, it["g"]:it["g"] + 128])

    def write_out(it, value):
        o_ref[it["rows"], :, it["o"]:it["o"] + 128] = value

    _attention_chunks(items, read_q, read_gate, write_out)


def _spectrum_operands(t):
    upper = [t[:, C_GROUP_DIM * g + HALF_GROUP:C_GROUP_DIM * (g + 1)] for g in range(C_GROUPS)]
    return t.astype(BF16), jnp.concatenate(upper, axis=1).astype(BF16)


def _spectrum_combine(m_cos, m_sin, read_gate, write_out):
    first = lax.broadcasted_iota(jnp.int32, (m_cos.shape[0], HALF_GROUP), 1) == 0
    for g in range(C_GROUPS):
        c0 = C_GROUP_DIM * g
        p = m_cos[:, c0:c0 + HALF_GROUP]
        p_mid = m_cos[:, c0 + HALF_GROUP:c0 + C_GROUP_DIM]
        q = jnp.where(first, 0.0, m_sin[:, HALF_GROUP * g:HALF_GROUP * (g + 1)])
        write_out(c0, ((p - q) * read_gate(c0)).astype(BF16))
        upper = jnp.where(first, p_mid, p + q)
        write_out(c0 + HALF_GROUP, (upper * read_gate(c0 + HALF_GROUP)).astype(BF16))


def _ctx_odd_mix(cs_ref, ss_ref, p_ref, o_ref):
    n_b = p_ref.shape[0]
    for bb in range(n_b):
        for_cos, for_sin = _spectrum_operands(p_ref[bb, :, 0:D_MODEL])
        m_cos = jnp.dot(cs_ref[...], for_cos, preferred_element_type=F32)
        m_sin = jnp.dot(ss_ref[...], for_sin, preferred_element_type=F32)

        def read_gate(c0, bb=bb):
            return _silu(p_ref[bb, :, D_MODEL + c0:D_MODEL + c0 + HALF_GROUP])

        def write_out(c0, value, bb=bb):
            o_ref[bb, :, c0:c0 + HALF_GROUP] = value

        _spectrum_combine(m_cos, m_sin, read_gate, write_out)


def _fold_odd_weights(owin_ref, owout_ref, epack_ref, perm_ref):
    for layer in range(owin_ref.shape[0]):
        for g in range(C_GROUPS):
            c0 = C_GROUP_DIM * g
            owin_ref[layer, :, c0:c0 + C_GROUP_DIM] = jnp.dot(
                owin_ref[layer, :, c0:c0 + C_GROUP_DIM], epack_ref[...],
                preferred_element_type=F32).astype(BF16)
            g0 = D_MODEL + c0 + HALF_GROUP
            owin_ref[layer, :, g0:g0 + HALF_GROUP] = jnp.dot(
                owin_ref[layer, :, g0:g0 + HALF_GROUP], perm_ref[...], preferred_element_type=F32).astype(BF16)
            r0 = c0 + HALF_GROUP
            owout_ref[layer, r0:r0 + HALF_GROUP, :] = jnp.dot(
                perm_ref[...], owout_ref[layer, r0:r0 + HALF_GROUP, :], preferred_element_type=F32).astype(BF16)


def _stage_copy(w_hbm, stage_ref, sem, chunk, slot):
    _, n_rows, cols = w_hbm.shape
    rows = stage_ref.shape[1]
    per_layer = n_rows // rows
    src = w_hbm.at[chunk // per_layer, pl.ds((chunk % per_layer) * rows, rows), :]
    return pltpu.make_async_copy(src, stage_ref.at[slot, :, pl.ds(0, cols)], sem.at[slot])


def _convert_weights(w_hbm, w_ref, stage_ref, in_sem):
    n_layers, n_rows, cols = w_hbm.shape
    n_slots, rows, _ = stage_ref.shape
    per_layer = n_rows // rows
    n_chunks = n_layers * per_layer
    for c in range(n_slots - 1):
        _stage_copy(w_hbm, stage_ref, in_sem, c, c).start()

    def body(c, carry):
        slot = c % n_slots
        ahead = c + n_slots - 1

        @pl.when(ahead < n_chunks)
        def _():
            _stage_copy(w_hbm, stage_ref, in_sem, ahead, ahead % n_slots).start()

        _stage_copy(w_hbm, stage_ref, in_sem, c, slot).wait()
        r0 = pl.multiple_of((c % per_layer) * rows, rows)
        w_ref[c // per_layer, pl.ds(r0, rows), :] = stage_ref[slot, :, 0:cols].astype(BF16)
        return carry

    lax.fori_loop(0, n_chunks, body, 0)


def _ctx_kernel(sink_ref, x_ref, mod_ref, g_ref, ewin_hbm, ewout_hbm, owin_hbm, owout_hbm, gq_ref, gk_ref,
                epack_ref, perm_ref, cs_ref, ss_ref, fg_ref,
                xo_ref, ka_ref, va_ref, kb_ref, vb_ref, ewin_o, ewout_o, owin_o, owout_o,
                xs_ref, p_ref, o_ref, ewin_ref, ewout_ref, owin_ref, owout_ref,
                stage_ref, in_sem, out_sem):
    weights = ((ewin_hbm, ewin_o, ewin_ref), (ewout_hbm, ewout_o, ewout_ref),
               (owin_hbm, owin_o, owin_ref), (owout_hbm, owout_o, owout_ref))
    exports = [pltpu.make_async_copy(res, dst, out_sem.at[k]) for k, (_, dst, res) in enumerate(weights)]

    @pl.when(pl.program_id(0) == 0)
    def _():
        for src, _, res in weights:
            _convert_weights(src, res, stage_ref, in_sem)
        _fold_odd_weights(owin_ref, owout_ref, epack_ref, perm_ref)
        for export in exports:
            export.start()

    @pl.when(pl.program_id(0) == pl.num_programs(0) - 1)
    def _():
        for export in exports:
            export.wait()

    n_b = x_ref.shape[0]
    for l in range(DEPTH):
        i = l // 2
        even = l % 2 == 0
        win_ref, wout_ref, width = (ewin_ref, ewout_ref, EVEN_IN) if even else (owin_ref, owout_ref, ODD_IN)
        shift, scale, gate = (mod_ref[l, part, 0:1, :] for part in range(3))
        for bb in range(n_b):
            x = x_ref[bb] if l == 0 else xs_ref[bb]
            h = _norm_mod(x, g_ref[l:l + 1, :], shift, scale)
            p_ref[bb, :, 0:width] = jnp.dot(h.astype(BF16), win_ref[i], preferred_element_type=F32)
        if even:
            _ctx_even_mix(i, sink_ref, gq_ref, gk_ref, (ka_ref, va_ref, kb_ref, vb_ref), p_ref, o_ref)
        else:
            _ctx_odd_mix(cs_ref, ss_ref, p_ref, o_ref)
        for bb in range(n_b):
            y = jnp.dot(o_ref[bb], wout_ref[i], preferred_element_type=F32)
            xn = (x_ref[bb] if l == 0 else xs_ref[bb]) + gate * y
            if l == DEPTH - 1:
                xo_ref[bb] = _rms_head(xn, fg_ref[...])
            else:
                xs_ref[bb] = xn


def _resident(shape):
    return pl.BlockSpec(shape, lambda i: (0,) * len(shape), pipeline_mode=pl.Buffered(1))


def _ctx_path(x, mod, norm_g, ewin, ewout, owin, owout, gq2, gk2, sink, epack, perm, cs, ss, fg):
    b, s, _ = x.shape
    n_even = ewin.shape[0]
    bb = CTX_BATCH_PER_STEP
    kv_shape = jax.ShapeDtypeStruct((b, n_even, N_KV, HEAD_DIM, s), F32)
    kv_spec = pl.BlockSpec((bb, n_even, N_KV, HEAD_DIM, s), lambda i: (i, 0, 0, 0, 0))
    x_spec = pl.BlockSpec((bb, s, D_MODEL), lambda i: (i, 0, 0))
    hbm = pl.BlockSpec(memory_space=pl.ANY)
    weights = (ewin, ewout, owin, owout)
    return pl.pallas_call(
        _ctx_kernel,
        grid=(b // bb,),
        in_specs=[
            pl.BlockSpec(memory_space=pltpu.SMEM),
            x_spec,
            _resident(mod.shape), _resident(norm_g.shape),
            hbm, hbm, hbm, hbm,
            _resident(gq2.shape), _resident(gk2.shape),
            _resident(epack.shape), _resident(perm.shape), _resident(cs.shape), _resident(ss.shape),
            _resident(fg.shape),
        ],
        out_specs=[x_spec, kv_spec, kv_spec, kv_spec, kv_spec, hbm, hbm, hbm, hbm],
        out_shape=[jax.ShapeDtypeStruct(x.shape, F32), kv_shape, kv_shape, kv_shape, kv_shape]
                  + [jax.ShapeDtypeStruct(w.shape, BF16) for w in weights],
        scratch_shapes=[pltpu.VMEM((bb, s, D_MODEL), F32), pltpu.VMEM((bb, s, EVEN_IN), F32),
                        pltpu.VMEM((bb, s, D_MODEL), BF16)]
                       + [pltpu.VMEM(w.shape, BF16) for w in weights]
                       + [pltpu.VMEM((STAGE_SLOTS, STAGE_ROWS, max(w.shape[2] for w in weights)), F32),
                          pltpu.SemaphoreType.DMA((STAGE_SLOTS,)), pltpu.SemaphoreType.DMA((len(weights),))],
        compiler_params=pltpu.CompilerParams(
            dimension_semantics=("arbitrary",), vmem_limit_bytes=CTX_VMEM_LIMIT),
        name="ctx_path",
    )(sink, x, mod, norm_g, *weights, gq2, gk2, epack, perm, cs, ss, fg)


def _lat_weight_copies(l, w_hbms, win_buf, wout_buf, w_sem):
    ewin_hbm, ewout_hbm, owin_hbm, owout_hbm = w_hbms
    i, slot = l // 2, l % 2
    w_in, w_out = (ewin_hbm, ewout_hbm) if l % 2 == 0 else (owin_hbm, owout_hbm)
    return (pltpu.make_async_copy(w_in.at[i], win_buf.at[slot, :, pl.ds(0, w_in.shape[2])], w_sem.at[2 * slot]),
            pltpu.make_async_copy(w_out.at[i], wout_buf.at[slot], w_sem.at[2 * slot + 1]))


def _lat_attention_items(n, u, layer_i, sink_ref, kv_ref, kcat_ref, wt_ref, cache_refs):
    seq = kv_ref.shape[1]
    ckb_t, cvb_t = cache_refs[2][layer_i], cache_refs[3][layer_i]
    prev0 = pl.multiple_of(jnp.maximum(n * ROWS - WINDOW, 0), WINDOW)
    own0 = pl.multiple_of(n * ROWS, ROWS)
    next0 = pl.multiple_of(jnp.minimum(n * ROWS + ROWS, seq - WINDOW), WINDOW)
    win_len = ROWS + 2 * WINDOW
    ctx_len = ckb_t.shape[1]
    cj = lax.broadcasted_iota(jnp.int32, (win_len + ctx_len, ROWS), 0)
    qi = lax.broadcasted_iota(jnp.int32, (win_len + ctx_len, ROWS), 1)
    kpos = n * ROWS - WINDOW + cj
    in_win = (jnp.abs(cj - WINDOW - qi) <= WINDOW) & (kpos >= 0) & (kpos < seq)
    mask_b = in_win | (cj >= win_len)

    def window(idx):
        return jnp.concatenate([kv_ref[idx, pl.ds(prev0, WINDOW), :], kv_ref[idx, pl.ds(own0, ROWS), :],
                                kv_ref[idx, pl.ds(next0, WINDOW), :]], axis=0)

    k2b = jnp.concatenate([window(2), ckb_t.T], axis=0)
    v2bt = jnp.concatenate([window(3).T, cvb_t], axis=1)
    items = []
    for mixer, (q0, g0) in enumerate(((LQA, LGA), (LQB, LGB))):
        for j in range(N_KV):
            k_cat, w_t = (kcat_ref[j], wt_ref[j]) if mixer == 0 else _kv_operands(k2b, v2bt, j)
            for cc in range(GROUP // 2):
                c0 = 128 * (j * (GROUP // 2) + cc)
                if mixer == 0:
                    sinks, mask = None, None
                else:
                    sinks, mask = _sink_pair(sink_ref, layer_i, c0 // HEAD_DIM), mask_b
                items.append(dict(u=u, q=q0 + c0, g=g0 + c0, o=mixer * MIX_W + c0, sinks=sinks, mask=mask,
                                  k_cat=k_cat, w_t=w_t))
    return items


def _lat_kernel(sink_ref, x_hbm, mod_ref, g_ref, ewin_hbm, ewout_hbm, owin_hbm, owout_hbm, gq_ref, gk_ref,
                cos_ref, sn_ref, sp_ref, cka_ref, cva_ref, ckb_ref, cvb_ref, cs_ref, ss_ref, fg_ref,
                y_hbm,
                xw_ref, win_buf, wout_buf, kv_ref, kcat_ref, wt_ref, p_ref, o_ref, fcos_ref, fsin_ref, gate_ref,
                x_sem, w_sem):
    b = pl.program_id(0)
    seq = xw_ref.shape[0]
    n_blocks = seq // ROWS
    w_hbms = (ewin_hbm, ewout_hbm, owin_hbm, owout_hbm)

    x_in = pltpu.make_async_copy(x_hbm.at[b], xw_ref, x_sem.at[0])
    x_in.start()
    for copy in _lat_weight_copies(0, w_hbms, win_buf, wout_buf, w_sem):
        copy.start()
    x_in.wait()

    for l in range(DEPTH):
        i, slot = l // 2, l % 2
        for copy in _lat_weight_copies(l, w_hbms, win_buf, wout_buf, w_sem):
            copy.wait()
        if l + 1 < DEPTH:
            for copy in _lat_weight_copies(l + 1, w_hbms, win_buf, wout_buf, w_sem):
                copy.start()
        shift, scale, gate = (mod_ref[l, part, pl.ds(1 + b, 1), :] for part in range(3))
        g = g_ref[l:l + 1, :]

        def normed(rows):
            return _norm_mod(xw_ref[rows, :], g, shift, scale).astype(BF16)

        def w_in(c0, width):
            return win_buf[slot, :, c0:c0 + width]

        if l % 2 == 0:
            gq2 = gq_ref[i:i + 1, :] * Q_SCALE
            gk2 = gk_ref[i:i + 1, :]

            def project_kv(n, carry):
                rows = pl.ds(pl.multiple_of(n * ROWS, ROWS), ROWS)
                h = normed(rows)
                cos, sn, sp = cos_ref[rows, :], sn_ref[rows, :], sp_ref[rows, :]
                for mixer, c0 in enumerate((KA, KB)):
                    kv = jnp.dot(h, w_in(c0, 2 * KV_W), preferred_element_type=F32)
                    k2 = kv[:, :KV_W]
                    if mixer == 0:
                        k2 = _rms_pair(k2, gk2)
                    kv_ref[2 * mixer, rows, :] = _rope(k2, cos, sn, sp)
                    kv_ref[2 * mixer + 1, rows, :] = kv[:, KV_W:]
                return carry

            lax.fori_loop(0, n_blocks, project_kv, 0, unroll=LAT_UNROLL)
            k2a = jnp.concatenate([kv_ref[0], cka_ref[i].T], axis=0)
            v2at = jnp.concatenate([kv_ref[1].T, cva_ref[i]], axis=1)
            for j in range(N_KV):
                kcat_ref[j], wt_ref[j] = _kv_operands(k2a, v2at, j)

            def attend(m, carry):
                blocks = [m * LAT_UNROLL + u for u in range(LAT_UNROLL)]
                block_rows = [pl.ds(pl.multiple_of(n * ROWS, ROWS), ROWS) for n in blocks]
                for u, rows in enumerate(block_rows):
                    h = normed(rows)
                    cos, sn, sp = cos_ref[rows, :], sn_ref[rows, :], sp_ref[rows, :]
                    qa = jnp.dot(h, w_in(QA, MIX_W), preferred_element_type=F32)
                    for c0 in range(0, MIX_W, 128):
                        p_ref[u, :, LQA + c0:LQA + c0 + 128] = _rope(_rms_pair(qa[:, c0:c0 + 128], gq2),
                                                                     cos, sn, sp)
                    mid = jnp.dot(h, w_in(GA, 2 * MIX_W), preferred_element_type=F32)
                    p_ref[u, :, LGA:LGA + MIX_W] = _silu(mid[:, :MIX_W])
                    for c0 in range(0, MIX_W, 128):
                        qb = mid[:, MIX_W + c0:MIX_W + c0 + 128] * Q_SCALE
                        p_ref[u, :, LQB + c0:LQB + c0 + 128] = _rope(qb, cos, sn, sp)
                    p_ref[u, :, LGB:LGB + MIX_W] = _silu(
                        jnp.dot(h, w_in(GB, MIX_W), preferred_element_type=F32))
                per_block = [_lat_attention_items(n, u, i, sink_ref, kv_ref, kcat_ref, wt_ref,
                                                  (cka_ref, cva_ref, ckb_ref, cvb_ref))
                             for u, n in enumerate(blocks)]
                items = [it for group in zip(*per_block) for it in group]

                def write_out(it, value):
                    o_ref[it["u"], :, it["o"]:it["o"] + 128] = value

                _attention_chunks(items, lambda it: p_ref[it["u"], :, it["q"]:it["q"] + 128],
                                  lambda it: p_ref[it["u"], :, it["g"]:it["g"] + 128], write_out)
                for u, rows in enumerate(block_rows):
                    y = jnp.dot(o_ref[u], wout_buf[slot], preferred_element_type=F32)
                    xw_ref[rows, :] = xw_ref[rows, :] + gate * y
                return carry

            lax.fori_loop(0, n_blocks // LAT_UNROLL, attend, 0)
        else:
            def project(n, carry):
                rows = pl.ds(pl.multiple_of(n * ROWS, ROWS), ROWS)
                p = jnp.dot(normed(rows), w_in(0, ODD_IN), preferred_element_type=F32)
                fcos_ref[rows, :], fsin_ref[rows, :] = _spectrum_operands(p[:, 0:D_MODEL])
                gate_ref[rows, :] = _silu(p[:, D_MODEL:ODD_IN])
                return carry

            lax.fori_loop(0, n_blocks, project, 0, unroll=LAT_UNROLL)

            def mix(n, carry):
                rows = pl.ds(pl.multiple_of(n * ROWS, ROWS), ROWS)
                m_cos = jnp.dot(cs_ref[rows, :], fcos_ref[...], preferred_element_type=F32)
                m_sin = jnp.dot(ss_ref[rows, :], fsin_ref[...], preferred_element_type=F32)

                def write_out(c0, value):
                    o_ref[0, :, c0:c0 + HALF_GROUP] = value

                _spectrum_combine(m_cos, m_sin, lambda c0: gate_ref[rows, c0:c0 + HALF_GROUP], write_out)
                y = jnp.dot(o_ref[0], wout_buf[slot], preferred_element_type=F32)
                xn = xw_ref[rows, :] + gate * y
                if l == DEPTH - 1:
                    xn = _rms_head(xn, fg_ref[...])
                xw_ref[rows, :] = xn
                return carry

            lax.fori_loop(0, n_blocks, mix, 0, unroll=LAT_UNROLL)

    y_out = pltpu.make_async_copy(xw_ref, y_hbm.at[b], x_sem.at[1])
    y_out.start()
    y_out.wait()


def _lat_path(x, mod, norm_g, ewin, ewout, owin, owout, gq2, gk2, sink, cos, sn, sp, caches, cs, ss, fg):
    b, s, _ = x.shape
    n_even, past = caches[0].shape[1], caches[0].shape[3]
    hbm = pl.BlockSpec(memory_space=pl.ANY)
    cache_spec = pl.BlockSpec((None, n_even, KV_W, past), lambda i: (i, 0, 0, 0))
    return pl.pallas_call(
        _lat_kernel,
        grid=(b,),
        in_specs=[
            pl.BlockSpec(memory_space=pltpu.SMEM),
            hbm,
            _resident(mod.shape), _resident(norm_g.shape),
            hbm, hbm, hbm, hbm,
            _resident(gq2.shape), _resident(gk2.shape),
            _resident(cos.shape), _resident(sn.shape), _resident(sp.shape),
            cache_spec, cache_spec, cache_spec, cache_spec,
            _resident(cs.shape), _resident(ss.shape), _resident(fg.shape),
        ],
        out_specs=hbm,
        out_shape=jax.ShapeDtypeStruct(x.shape, F32),
        scratch_shapes=[
            pltpu.VMEM((s, D_MODEL), F32),
            pltpu.VMEM((2, D_MODEL, EVEN_IN), BF16),
            pltpu.VMEM((2, D_MODEL, D_MODEL), BF16),
            pltpu.VMEM((4, s, KV_W), F32),
            pltpu.VMEM((N_KV, 2 * (s + past), KV_W), BF16),
            pltpu.VMEM((N_KV, 2 * HEAD_DIM + DEN_ROWS, 2 * (s + past)), BF16),
            pltpu.VMEM((LAT_UNROLL, ROWS, 4 * MIX_W), F32),
            pltpu.VMEM((LAT_UNROLL, ROWS, D_MODEL), BF16),
            pltpu.VMEM((s, D_MODEL), BF16),
            pltpu.VMEM((s, D_MODEL // 2), BF16),
            pltpu.VMEM((s, D_MODEL), F32),
            pltpu.SemaphoreType.DMA((2,)), pltpu.SemaphoreType.DMA((4,)),
        ],
        compiler_params=pltpu.CompilerParams(
            dimension_semantics=("arbitrary",), vmem_limit_bytes=LAT_VMEM_LIMIT),
        name="lat_path",
    )(sink, x, mod, norm_g, ewin, ewout, owin, owout, gq2, gk2, cos, sn, sp, *caches, cs, ss, fg)


def kernel(x_prompt, x_sample, cache_k_a, cache_v_a, cache_k_b, cache_v_b, c, c_ctx, norm_g, ada_w, ada_b,
           even_w_in, even_w_out, qk_g_q, qk_g_k, sink_logit, odd_w_in, odd_w_out, final_g):
    batch, seq, _ = x_prompt.shape
    dec_batch, dec_seq, _ = x_sample.shape
    n_even = even_w_in.shape[0]
    past = cache_k_a.shape[2]

    cond = jnp.concatenate(
        [c_ctx[None, :], c, jnp.zeros((COND_ROWS - 1 - dec_batch, D_MODEL), F32)], axis=0)
    mod = _ada_all(cond, ada_w, ada_b)

    epack = jnp.asarray(_packed_channel_dft()).astype(BF16)
    perm = jnp.asarray(_mirror_perm()).astype(BF16)
    cs_ctx, ss_ctx = (jnp.asarray(t).astype(BF16) for t in _position_dft(seq))
    cs_lat, ss_lat = (jnp.asarray(t).astype(BF16) for t in _position_dft(dec_seq))
    cos, sn, sp = (jnp.asarray(t) for t in _rope_tables(dec_seq))

    caches = [jnp.transpose(a, (0, 1, 3, 4, 2)).reshape(dec_batch, n_even, KV_W, past)
              for a in (cache_k_a, cache_v_a, cache_k_b, cache_v_b)]
    fg = final_g.reshape(1, D_MODEL)
    gq2 = jnp.tile(qk_g_q, (1, 2))
    gk2 = jnp.tile(qk_g_k, (1, 2))

    xc, *outs = _ctx_path(x_prompt, mod, norm_g, even_w_in, even_w_out, odd_w_in, odd_w_out, gq2, gk2,
                          sink_logit, epack, perm, cs_ctx, ss_ctx, fg)
    new_kv = [jnp.transpose(a, (0, 1, 4, 2, 3)) for a in outs[:4]]
    ewin, ewout, owin, owout = outs[4:]

    xl = _lat_path(x_sample, mod, norm_g, ewin, ewout, owin, owout, gq2, gk2, sink_logit, cos, sn, sp,
                   caches, cs_lat, ss_lat, fg)
    return (xc, xl, *new_kv)
```

```python
import numpy as np
import jax
import jax.numpy as jnp
from jax import lax
from jax.experimental import pallas as pl
from jax.experimental.pallas import tpu as pltpu

D_MODEL = 1024
DEPTH = 4
HEAD_DIM = 64
N_HEADS = 8
N_KV = 2
GROUP = N_HEADS // N_KV
MIX_W = N_HEADS * HEAD_DIM
KV_W = N_KV * HEAD_DIM
EVEN_IN = 2 * (2 * MIX_W + 2 * KV_W)
ODD_IN = 2 * D_MODEL
GRID_W = 64
WINDOW = 128
ROPE_BASE = 10000.0
C_GROUPS = 4
C_GROUP_DIM = D_MODEL // C_GROUPS
EPS = 1e-6
NEG_BIG = -1e30
ROWS = 256
COND_ROWS = 8
VMEM_LIMIT = 48 * 1024 * 1024
CTX_VMEM_LIMIT = 56 * 1024 * 1024
LAT_VMEM_LIMIT = 56 * 1024 * 1024
STAGE_ROWS = 64
STAGE_SLOTS = 8
CTX_BATCH_PER_STEP = 2
SKEW_GAPS = (1, 0, 1)
LAT_UNROLL = 1

QA, KA, VA, GA = 0, 512, 640, 768
QB, KB, VB, GB = 1280, 1792, 1920, 2048
LQA, LGA, LQB, LGB = 0, 512, 1024, 1536

F32 = jnp.float32
BF16 = jnp.bfloat16


def _dft_tables(n):
    k = np.arange(n, dtype=np.int64)
    ang = ((k[:, None] * k[None, :]) % n).astype(np.float64) * (2.0 * np.pi / n)
    return np.cos(ang).astype(np.float32), np.sin(ang).astype(np.float32)


HALF_GROUP = C_GROUP_DIM // 2


def _packed_channel_dft():
    c, s = _dft_tables(C_GROUP_DIM)
    return np.concatenate([c[:, :HALF_GROUP], c[:, HALF_GROUP:HALF_GROUP + 1], s[:, 1:HALF_GROUP]], axis=1)


def _position_dft(n):
    scale = np.float32(1.0 / np.sqrt(float(n * C_GROUP_DIM)))
    c, s = _dft_tables(n)
    return c * scale, s * scale


def _mirror_perm():
    t = np.array([0] + [HALF_GROUP - m for m in range(1, HALF_GROUP)])
    p = np.zeros((HALF_GROUP, HALF_GROUP), np.float32)
    p[t, np.arange(HALF_GROUP)] = 1.0
    return p


def _rope_tables(n_tok):
    rows = n_tok // GRID_W
    row = np.repeat(np.arange(rows), GRID_W).astype(np.float64)
    col = np.tile(np.arange(GRID_W), rows).astype(np.float64)
    half = HEAD_DIM // 2
    inv = 1.0 / (ROPE_BASE ** (np.arange(0, half, 2, dtype=np.float64) / half))
    ang_r = row[:, None] * inv
    ang_c = col[:, None] * inv
    zeros = np.zeros_like(ang_r)
    cos_h = np.concatenate([np.cos(ang_r), np.cos(ang_r), np.cos(ang_c), np.cos(ang_c)], axis=1)
    nxt_h = np.concatenate([-np.sin(ang_r), zeros, -np.sin(ang_c), zeros], axis=1)
    prv_h = np.concatenate([zeros, np.sin(ang_r), zeros, np.sin(ang_c)], axis=1)
    two = lambda t: np.concatenate([t, t], axis=1).astype(np.float32)
    return two(cos_h), two(nxt_h), two(prv_h)


def _silu(x):
    return x * jax.nn.sigmoid(x)


def _norm_mod(x, g, shift, scale):
    ms = jnp.mean(x * x, axis=-1, keepdims=True)
    return x * lax.rsqrt(ms + EPS) * (g * (1.0 + scale)) + shift


def _rms_head(xh, g):
    ms = jnp.mean(xh * xh, axis=-1, keepdims=True)
    return xh * lax.rsqrt(ms + EPS) * g


def _rms_pair(xc, g2):
    lo = lax.broadcasted_iota(jnp.int32, xc.shape, 1) < HEAD_DIM
    ss = xc * xc
    s_lo = jnp.sum(jnp.where(lo, ss, 0.0), axis=-1, keepdims=True)
    s_hi = jnp.sum(jnp.where(lo, 0.0, ss), axis=-1, keepdims=True)
    ms = jnp.where(lo, s_lo, s_hi) * (1.0 / HEAD_DIM)
    return xc * lax.rsqrt(ms + EPS) * g2


def _rope(xc, cos, sin_next, sin_prev):
    nxt = pltpu.roll(xc, 128 - 16, 1)
    prv = pltpu.roll(xc, 16, 1)
    return xc * cos + nxt * sin_next + prv * sin_prev


DEN_ROWS = 16
LOG2E = float(np.log2(np.e))
Q_SCALE = (HEAD_DIM ** -0.5) * LOG2E


def _sink_pair(sink_ref, i, head):
    return sink_ref[i, head] * LOG2E, sink_ref[i, head + 1] * LOG2E


def _kv_operands(k2, v2t, j):
    tk = k2.shape[0]
    low = lax.broadcasted_iota(jnp.int32, k2.shape, 1) < HEAD_DIM
    km = jnp.where(low if j == 0 else jnp.logical_not(low), k2, 0.0)
    kr = pltpu.roll(km, HEAD_DIM, 1)
    k_lo, k_hi = (km, kr) if j == 0 else (kr, km)
    k_cat = jnp.concatenate([k_lo, k_hi], axis=0).astype(BF16)
    vjt = v2t[HEAD_DIM * j:HEAD_DIM * (j + 1), :]
    zero = jnp.zeros_like(vjt)
    row = lax.broadcasted_iota(jnp.int32, (DEN_ROWS, 2 * tk), 0)
    col = lax.broadcasted_iota(jnp.int32, (DEN_ROWS, 2 * tk), 1)
    ones = jnp.where(((row == 0) & (col < tk)) | ((row == 1) & (col >= tk)), 1.0, 0.0)
    w_t = jnp.concatenate([jnp.concatenate([vjt, zero], axis=1),
                           jnp.concatenate([zero, vjt], axis=1), ones], axis=0).astype(BF16)
    return k_cat, w_t


def _scores_t(qc, k_cat):
    return lax.dot_general(k_cat, qc.astype(BF16), (((1,), (1,)), ((), ())), preferred_element_type=F32)


def _softmax_t(s_t, sinks=None, mask_t=None):
    tk = s_t.shape[0] // 2
    es, ms = [], []
    for hh in range(2):
        sh = s_t[hh * tk:(hh + 1) * tk]
        if mask_t is not None:
            sh = jnp.where(mask_t, sh, NEG_BIG)
        m = jnp.max(sh, axis=0, keepdims=True)
        if sinks is not None:
            m = jnp.maximum(m, sinks[hh])
        es.append(jnp.exp2(sh - m))
        ms.append(m)
    return jnp.concatenate(es, axis=0).astype(BF16), ms


def _values_t(e_t, w_t):
    return jnp.dot(w_t, e_t, preferred_element_type=F32)


def _normalise_t(nd, ms, sinks=None):
    dens = [nd[2 * HEAD_DIM + hh:2 * HEAD_DIM + hh + 1] for hh in range(2)]
    if sinks is not None:
        dens = [dens[hh] + jnp.exp2(sinks[hh] - ms[hh]) for hh in range(2)]
    inv = [1.0 / d for d in dens]
    o_t = jnp.concatenate([nd[0:HEAD_DIM] * inv[0], nd[HEAD_DIM:2 * HEAD_DIM] * inv[1]], axis=0)
    return o_t.T


def _run_skewed(items, stages):
    state = list(items)
    for t in range(len(items) + len(stages) - 1):
        for k, stage in enumerate(stages):
            if 0 <= t - k < len(items):
                state[t - k] = stage(state[t - k])


def _attention_chunks(items, read_q, read_gate, write_out):
    def scores(it):
        return dict(it, s_t=_scores_t(read_q(it), it["k_cat"]))

    def softmax(it):
        e_t, ms = _softmax_t(it["s_t"], it["sinks"], it["mask"])
        return dict(it, e_t=e_t, ms=ms, s_t=None)

    def values(it):
        return dict(it, nd=_values_t(it["e_t"], it["w_t"]), e_t=None)

    def finish(it):
        o = _normalise_t(it["nd"], it["ms"], it["sinks"])
        write_out(it, (o * read_gate(it)).astype(BF16))
        return None

    def hold(it):
        return it

    stages = [scores]
    for gap, stage in zip(SKEW_GAPS, (softmax, values, finish)):
        stages += [hold] * gap + [stage]
    _run_skewed(items, stages)


def _ada_kernel(cond_ref, w_ref, b_ref, o_ref):
    a = _silu(cond_ref[...]).astype(BF16)
    o_ref[...] = jnp.dot(a, w_ref[...].astype(BF16), preferred_element_type=F32) + b_ref[...]


def _ada_all(cond, ada_w, ada_b):
    return pl.pallas_call(
        _ada_kernel,
        grid=(DEPTH, 3),
        in_specs=[
            pl.BlockSpec((COND_ROWS, D_MODEL), lambda l, p: (0, 0)),
            pl.BlockSpec((None, D_MODEL, D_MODEL), lambda l, p: (l, 0, p)),
            pl.BlockSpec((None, None, 1, D_MODEL), lambda l, p: (l, p, 0, 0)),
        ],
        out_specs=pl.BlockSpec((None, None, COND_ROWS, D_MODEL), lambda l, p: (l, p, 0, 0)),
        out_shape=jax.ShapeDtypeStruct((DEPTH, 3, COND_ROWS, D_MODEL), F32),
        compiler_params=pltpu.CompilerParams(
            dimension_semantics=("parallel", "parallel"), vmem_limit_bytes=VMEM_LIMIT),
        name="ada_mod",
    )(cond, ada_w, ada_b.reshape(DEPTH, 3, 1, D_MODEL))


def _ctx_even_mix(i, sink_ref, gq_ref, gk_ref, kv_refs, p_ref, o_ref):
    n_b, seq, _ = p_ref.shape
    gq2 = gq_ref[i:i + 1, :] * Q_SCALE
    gk2 = gk_ref[i:i + 1, :]
    items = []
    for mixer, (q0, k0, v0, g0) in enumerate(((QA, KA, VA, GA), (QB, KB, VB, GB))):
        kv = []
        for bb in range(n_b):
            k2 = p_ref[bb, :, k0:k0 + KV_W]
            v2 = p_ref[bb, :, v0:v0 + KV_W]
            if mixer == 0:
                k2 = _rms_pair(k2, gk2)
            v2t = v2.T
            kv_refs[2 * mixer][bb, i] = k2.T.reshape(N_KV, HEAD_DIM, seq)
            kv_refs[2 * mixer + 1][bb, i] = v2t.reshape(N_KV, HEAD_DIM, seq)
            kv.append((k2, v2t))
        for j in range(N_KV):
            ops = [_kv_operands(k2, v2t, j) for k2, v2t in kv]
            for cc in range(GROUP // 2):
                c0 = 128 * (j * (GROUP // 2) + cc)
                sinks = None if mixer == 0 else _sink_pair(sink_ref, i, c0 // HEAD_DIM)
                for bb, (k_cat, w_t) in enumerate(ops):
                    items.append(dict(rows=bb, q=q0 + c0, g=g0 + c0, o=mixer * MIX_W + c0,
                                      norm=mixer == 0, sinks=sinks, mask=None, k_cat=k_cat, w_t=w_t))

    def read_q(it):
        qc = p_ref[it["rows"], :, it["q"]:it["q"] + 128]
        return _rms_pair(qc, gq2) if it["norm"] else qc * Q_SCALE

    def read_gate(it):
        return _silu(p_ref[it["rows"], :, it["g"]:it["g"] + 128])

    def write_out(it, value):
        o_ref[it["rows"], :, it["o"]:it["o"] + 128] = value

    _attention_chunks(items, read_q, read_gate, write_out)


def _spectrum_operands(t):
    upper = [t[:, C_GROUP_DIM * g + HALF_GROUP:C_GROUP_DIM * (g + 1)] for g in range(C_GROUPS)]
    return t.astype(BF16), jnp.concatenate(upper, axis=1).astype(BF16)


def _spectrum_combine(m_cos, m_sin, read_gate, write_out):
    first = lax.broadcasted_iota(jnp.int32, (m_cos.shape[0], HALF_GROUP), 1) == 0
    for g in range(C_GROUPS):
        c0 = C_GROUP_DIM * g
        p = m_cos[:, c0:c0 + HALF_GROUP]
        p_mid = m_cos[:, c0 + HALF_GROUP:c0 + C_GROUP_DIM]
        q = jnp.where(first, 0.0, m_sin[:, HALF_GROUP * g:HALF_GROUP * (g + 1)])
        write_out(c0, ((p - q) * read_gate(c0)).astype(BF16))
        upper = jnp.where(first, p_mid, p + q)
        write_out(c0 + HALF_GROUP, (upper * read_gate(c0 + HALF_GROUP)).astype(BF16))


def _ctx_odd_mix(cs_ref, ss_ref, p_ref, o_ref):
    n_b = p_ref.shape[0]
    for bb in range(n_b):
        for_cos, for_sin = _spectrum_operands(p_ref[bb, :, 0:D_MODEL])
        m_cos = jnp.dot(cs_ref[...], for_cos, preferred_element_type=F32)
        m_sin = jnp.dot(ss_ref[...], for_sin, preferred_element_type=F32)

        def read_gate(c0, bb=bb):
            return _silu(p_ref[bb, :, D_MODEL + c0:D_MODEL + c0 + HALF_GROUP])

        def write_out(c0, value, bb=bb):
            o_ref[bb, :, c0:c0 + HALF_GROUP] = value

        _spectrum_combine(m_cos, m_sin, read_gate, write_out)


def _fold_odd_weights(owin_ref, owout_ref, epack_ref, perm_ref):
    for layer in range(owin_ref.shape[0]):
        for g in range(C_GROUPS):
            c0 = C_GROUP_DIM * g
            owin_ref[layer, :, c0:c0 + C_GROUP_DIM] = jnp.dot(
                owin_ref[layer, :, c0:c0 + C_GROUP_DIM], epack_ref[...],
                preferred_element_type=F32).astype(BF16)
            g0 = D_MODEL + c0 + HALF_GROUP
            owin_ref[layer, :, g0:g0 + HALF_GROUP] = jnp.dot(
                owin_ref[layer, :, g0:g0 + HALF_GROUP], perm_ref[...], preferred_element_type=F32).astype(BF16)
            r0 = c0 + HALF_GROUP
            owout_ref[layer, r0:r0 + HALF_GROUP, :] = jnp.dot(
                perm_ref[...], owout_ref[layer, r0:r0 + HALF_GROUP, :], preferred_element_type=F32).astype(BF16)


def _stage_copy(w_hbm, stage_ref, sem, chunk, slot):
    _, n_rows, cols = w_hbm.shape
    rows = stage_ref.shape[1]
    per_layer = n_rows // rows
    src = w_hbm.at[chunk // per_layer, pl.ds((chunk % per_layer) * rows, rows), :]
    return pltpu.make_async_copy(src, stage_ref.at[slot, :, pl.ds(0, cols)], sem.at[slot])


def _convert_weights(w_hbm, w_ref, stage_ref, in_sem):
    n_layers, n_rows, cols = w_hbm.shape
    n_slots, rows, _ = stage_ref.shape
    per_layer = n_rows // rows
    n_chunks = n_layers * per_layer
    for c in range(n_slots - 1):
        _stage_copy(w_hbm, stage_ref, in_sem, c, c).start()

    def body(c, carry):
        slot = c % n_slots
        ahead = c + n_slots - 1

        @pl.when(ahead < n_chunks)
        def _():
            _stage_copy(w_hbm, stage_ref, in_sem, ahead, ahead % n_slots).start()

        _stage_copy(w_hbm, stage_ref, in_sem, c, slot).wait()
        r0 = pl.multiple_of((c % per_layer) * rows, rows)
        w_ref[c // per_layer, pl.ds(r0, rows), :] = stage_ref[slot, :, 0:cols].astype(BF16)
        return carry

    lax.fori_loop(0, n_chunks, body, 0)


def _ctx_kernel(sink_ref, x_ref, mod_ref, g_ref, ewin_hbm, ewout_hbm, owin_hbm, owout_hbm, gq_ref, gk_ref,
                epack_ref, perm_ref, cs_ref, ss_ref, fg_ref,
                xo_ref, ka_ref, va_ref, kb_ref, vb_ref, ewin_o, ewout_o, owin_o, owout_o,
                xs_ref, p_ref, o_ref, ewin_ref, ewout_ref, owin_ref, owout_ref,
                stage_ref, in_sem, out_sem):
    weights = ((ewin_hbm, ewin_o, ewin_ref), (ewout_hbm, ewout_o, ewout_ref),
               (owin_hbm, owin_o, owin_ref), (owout_hbm, owout_o, owout_ref))
    exports = [pltpu.make_async_copy(res, dst, out_sem.at[k]) for k, (_, dst, res) in enumerate(weights)]

    @pl.when(pl.program_id(0) == 0)
    def _():
        for src, _, res in weights:
            _convert_weights(src, res, stage_ref, in_sem)
        _fold_odd_weights(owin_ref, owout_ref, epack_ref, perm_ref)
        for export in exports:
            export.start()

    @pl.when(pl.program_id(0) == pl.num_programs(0) - 1)
    def _():
        for export in exports:
            export.wait()

    n_b = x_ref.shape[0]
    for l in range(DEPTH):
        i = l // 2
        even = l % 2 == 0
        win_ref, wout_ref, width = (ewin_ref, ewout_ref, EVEN_IN) if even else (owin_ref, owout_ref, ODD_IN)
        shift, scale, gate = (mod_ref[l, part, 0:1, :] for part in range(3))
        for bb in range(n_b):
            x = x_ref[bb] if l == 0 else xs_ref[bb]
            h = _norm_mod(x, g_ref[l:l + 1, :], shift, scale)
            p_ref[bb, :, 0:width] = jnp.dot(h.astype(BF16), win_ref[i], preferred_element_type=F32)
        if even:
            _ctx_even_mix(i, sink_ref, gq_ref, gk_ref, (ka_ref, va_ref, kb_ref, vb_ref), p_ref, o_ref)
        else:
            _ctx_odd_mix(cs_ref, ss_ref, p_ref, o_ref)
        for bb in range(n_b):
            y = jnp.dot(o_ref[bb], wout_ref[i], preferred_element_type=F32)
            xn = (x_ref[bb] if l == 0 else xs_ref[bb]) + gate * y
            if l == DEPTH - 1:
                xo_ref[bb] = _rms_head(xn, fg_ref[...])
            else:
                xs_ref[bb] = xn


def _resident(shape):
    return pl.BlockSpec(shape, lambda i: (0,) * len(shape), pipeline_mode=pl.Buffered(1))


def _ctx_path(x, mod, norm_g, ewin, ewout, owin, owout, gq2, gk2, sink, epack, perm, cs, ss, fg):
    b, s, _ = x.shape
    n_even = ewin.shape[0]
    bb = CTX_BATCH_PER_STEP
    kv_shape = jax.ShapeDtypeStruct((b, n_even, N_KV, HEAD_DIM, s), F32)
    kv_spec = pl.BlockSpec((bb, n_even, N_KV, HEAD_DIM, s), lambda i: (i, 0, 0, 0, 0))
    x_spec = pl.BlockSpec((bb, s, D_MODEL), lambda i: (i, 0, 0))
    hbm = pl.BlockSpec(memory_space=pl.ANY)
    weights = (ewin, ewout, owin, owout)
    return pl.pallas_call(
        _ctx_kernel,
        grid=(b // bb,),
        in_specs=[
            pl.BlockSpec(memory_space=pltpu.SMEM),
            x_spec,
            _resident(mod.shape), _resident(norm_g.shape),
            hbm, hbm, hbm, hbm,
            _resident(gq2.shape), _resident(gk2.shape),
            _resident(epack.shape), _resident(perm.shape), _resident(cs.shape), _resident(ss.shape),
            _resident(fg.shape),
        ],
        out_specs=[x_spec, kv_spec, kv_spec, kv_spec, kv_spec, hbm, hbm, hbm, hbm],
        out_shape=[jax.ShapeDtypeStruct(x.shape, F32), kv_shape, kv_shape, kv_shape, kv_shape]
                  + [jax.ShapeDtypeStruct(w.shape, BF16) for w in weights],
        scratch_shapes=[pltpu.VMEM((bb, s, D_MODEL), F32), pltpu.VMEM((bb, s, EVEN_IN), F32),
                        pltpu.VMEM((bb, s, D_MODEL), BF16)]
                       + [pltpu.VMEM(w.shape, BF16) for w in weights]
                       + [pltpu.VMEM((STAGE_SLOTS, STAGE_ROWS, max(w.shape[2] for w in weights)), F32),
                          pltpu.SemaphoreType.DMA((STAGE_SLOTS,)), pltpu.SemaphoreType.DMA((len(weights),))],
        compiler_params=pltpu.CompilerParams(
            dimension_semantics=("arbitrary",), vmem_limit_bytes=CTX_VMEM_LIMIT),
        name="ctx_path",
    )(sink, x, mod, norm_g, *weights, gq2, gk2, epack, perm, cs, ss, fg)


def _lat_weight_copies(l, w_hbms, win_buf, wout_buf, w_sem):
    ewin_hbm, ewout_hbm, owin_hbm, owout_hbm = w_hbms
    i, slot = l // 2, l % 2
    w_in, w_out = (ewin_hbm, ewout_hbm) if l % 2 == 0 else (owin_hbm, owout_hbm)
    return (pltpu.make_async_copy(w_in.at[i], win_buf.at[slot, :, pl.ds(0, w_in.shape[2])], w_sem.at[2 * slot]),
            pltpu.make_async_copy(w_out.at[i], wout_buf.at[slot], w_sem.at[2 * slot + 1]))


def _lat_attention_items(n, u, layer_i, sink_ref, kv_ref, kcat_ref, wt_ref, cache_refs):
    seq = kv_ref.shape[1]
    ckb_t, cvb_t = cache_refs[2][layer_i], cache_refs[3][layer_i]
    prev0 = pl.multiple_of(jnp.maximum(n * ROWS - WINDOW, 0), WINDOW)
    own0 = pl.multiple_of(n * ROWS, ROWS)
    next0 = pl.multiple_of(jnp.minimum(n * ROWS + ROWS, seq - WINDOW), WINDOW)
    win_len = ROWS + 2 * WINDOW
    ctx_len = ckb_t.shape[1]
    cj = lax.broadcasted_iota(jnp.int32, (win_len + ctx_len, ROWS), 0)
    qi = lax.broadcasted_iota(jnp.int32, (win_len + ctx_len, ROWS), 1)
    kpos = n * ROWS - WINDOW + cj
    in_win = (jnp.abs(cj - WINDOW - qi) <= WINDOW) & (kpos >= 0) & (kpos < seq)
    mask_b = in_win | (cj >= win_len)

    def window(idx):
        return jnp.concatenate([kv_ref[idx, pl.ds(prev0, WINDOW), :], kv_ref[idx, pl.ds(own0, ROWS), :],
                                kv_ref[idx, pl.ds(next0, WINDOW), :]], axis=0)

    k2b = jnp.concatenate([window(2), ckb_t.T], axis=0)
    v2bt = jnp.concatenate([window(3).T, cvb_t], axis=1)
    items = []
    for mixer, (q0, g0) in enumerate(((LQA, LGA), (LQB, LGB))):
        for j in range(N_KV):
            k_cat, w_t = (kcat_ref[j], wt_ref[j]) if mixer == 0 else _kv_operands(k2b, v2bt, j)
            for cc in range(GROUP // 2):
                c0 = 128 * (j * (GROUP // 2) + cc)
                if mixer == 0:
                    sinks, mask = None, None
                else:
                    sinks, mask = _sink_pair(sink_ref, layer_i, c0 // HEAD_DIM), mask_b
                items.append(dict(u=u, q=q0 + c0, g=g0 + c0, o=mixer * MIX_W + c0, sinks=sinks, mask=mask,
                                  k_cat=k_cat, w_t=w_t))
    return items


def _lat_kernel(sink_ref, x_hbm, mod_ref, g_ref, ewin_hbm, ewout_hbm, owin_hbm, owout_hbm, gq_ref, gk_ref,
                cos_ref, sn_ref, sp_ref, cka_ref, cva_ref, ckb_ref, cvb_ref, cs_ref, ss_ref, fg_ref,
                y_hbm,
                xw_ref, win_buf, wout_buf, kv_ref, kcat_ref, wt_ref, p_ref, o_ref, fcos_ref, fsin_ref, gate_ref,
                x_sem, w_sem):
    b = pl.program_id(0)
    seq = xw_ref.shape[0]
    n_blocks = seq // ROWS
    w_hbms = (ewin_hbm, ewout_hbm, owin_hbm, owout_hbm)

    x_in = pltpu.make_async_copy(x_hbm.at[b], xw_ref, x_sem.at[0])
    x_in.start()
    for copy in _lat_weight_copies(0, w_hbms, win_buf, wout_buf, w_sem):
        copy.start()
    x_in.wait()

    for l in range(DEPTH):
        i, slot = l // 2, l % 2
        for copy in _lat_weight_copies(l, w_hbms, win_buf, wout_buf, w_sem):
            copy.wait()
        if l + 1 < DEPTH:
            for copy in _lat_weight_copies(l + 1, w_hbms, win_buf, wout_buf, w_sem):
                copy.start()
        shift, scale, gate = (mod_ref[l, part, pl.ds(1 + b, 1), :] for part in range(3))
        g = g_ref[l:l + 1, :]

        def normed(rows):
            return _norm_mod(xw_ref[rows, :], g, shift, scale).astype(BF16)

        def w_in(c0, width):
            return win_buf[slot, :, c0:c0 + width]

        if l % 2 == 0:
            gq2 = gq_ref[i:i + 1, :] * Q_SCALE
            gk2 = gk_ref[i:i + 1, :]

            def project_kv(n, carry):
                rows = pl.ds(pl.multiple_of(n * ROWS, ROWS), ROWS)
                h = normed(rows)
                cos, sn, sp = cos_ref[rows, :], sn_ref[rows, :], sp_ref[rows, :]
                for mixer, c0 in enumerate((KA, KB)):
                    kv = jnp.dot(h, w_in(c0, 2 * KV_W), preferred_element_type=F32)
                    k2 = kv[:, :KV_W]
                    if mixer == 0:
                        k2 = _rms_pair(k2, gk2)
                    kv_ref[2 * mixer, rows, :] = _rope(k2, cos, sn, sp)
                    kv_ref[2 * mixer + 1, rows, :] = kv[:, KV_W:]
                return carry

            lax.fori_loop(0, n_blocks, project_kv, 0, unroll=LAT_UNROLL)
            k2a = jnp.concatenate([kv_ref[0], cka_ref[i].T], axis=0)
            v2at = jnp.concatenate([kv_ref[1].T, cva_ref[i]], axis=1)
            for j in range(N_KV):
                kcat_ref[j], wt_ref[j] = _kv_operands(k2a, v2at, j)

            def attend(m, carry):
                blocks = [m * LAT_UNROLL + u for u in range(LAT_UNROLL)]
                block_rows = [pl.ds(pl.multiple_of(n * ROWS, ROWS), ROWS) for n in blocks]
                for u, rows in enumerate(block_rows):
                    h = normed(rows)
                    cos, sn, sp = cos_ref[rows, :], sn_ref[rows, :], sp_ref[rows, :]
                    qa = jnp.dot(h, w_in(QA, MIX_W), preferred_element_type=F32)
                    for c0 in range(0, MIX_W, 128):
                        p_ref[u, :, LQA + c0:LQA + c0 + 128] = _rope(_rms_pair(qa[:, c0:c0 + 128], gq2),
                                                                     cos, sn, sp)
                    mid = jnp.dot(h, w_in(GA, 2 * MIX_W), preferred_element_type=F32)
                    p_ref[u, :, LGA:LGA + MIX_W] = _silu(mid[:, :MIX_W])
                    for c0 in range(0, MIX_W, 128):
                        qb = mid[:, MIX_W + c0:MIX_W + c0 + 128] * Q_SCALE
                        p_ref[u, :, LQB + c0:LQB + c0 + 128] = _rope(qb, cos, sn, sp)
                    p_ref[u, :, LGB:LGB + MIX_W] = _silu(
                        jnp.dot(h, w_in(GB, MIX_W), preferred_element_type=F32))
                per_block = [_lat_attention_items(n, u, i, sink_ref, kv_ref, kcat_ref, wt_ref,
                                                  (cka_ref, cva_ref, ckb_ref, cvb_ref))
                             for u, n in enumerate(blocks)]
                items = [it for group in zip(*per_block) for it in group]

                def write_out(it, value):
                    o_ref[it["u"], :, it["o"]:it["o"] + 128] = value

                _attention_chunks(items, lambda it: p_ref[it["u"], :, it["q"]:it["q"] + 128],
                                  lambda it: p_ref[it["u"], :, it["g"]:it["g"] + 128], write_out)
                for u, rows in enumerate(block_rows):
                    y = jnp.dot(o_ref[u], wout_buf[slot], preferred_element_type=F32)
                    xw_ref[rows, :] = xw_ref[rows, :] + gate * y
                return carry

            lax.fori_loop(0, n_blocks // LAT_UNROLL, attend, 0)
        else:
            def project(n, carry):
                rows = pl.ds(pl.multiple_of(n * ROWS, ROWS), ROWS)
                p = jnp.dot(normed(rows), w_in(0, ODD_IN), preferred_element_type=F32)
                fcos_ref[rows, :], fsin_ref[rows, :] = _spectrum_operands(p[:, 0:D_MODEL])
                gate_ref[rows, :] = _silu(p[:, D_MODEL:ODD_IN])
                return carry

            lax.fori_loop(0, n_blocks, project, 0, unroll=LAT_UNROLL)

            def mix(n, carry):
                rows = pl.ds(pl.multiple_of(n * ROWS, ROWS), ROWS)
                m_cos = jnp.dot(cs_ref[rows, :], fcos_ref[...], preferred_element_type=F32)
                m_sin = jnp.dot(ss_ref[rows, :], fsin_ref[...], preferred_element_type=F32)

                def write_out(c0, value):
                    o_ref[0, :, c0:c0 + HALF_GROUP] = value

                _spectrum_combine(m_cos, m_sin, lambda c0: gate_ref[rows, c0:c0 + HALF_GROUP], write_out)
                y = jnp.dot(o_ref[0], wout_buf[slot], preferred_element_type=F32)
                xn = xw_ref[rows, :] + gate * y
                if l == DEPTH - 1:
                    xn = _rms_head(xn, fg_ref[...])
                xw_ref[rows, :] = xn
                return carry

            lax.fori_loop(0, n_blocks, mix, 0, unroll=LAT_UNROLL)

    y_out = pltpu.make_async_copy(xw_ref, y_hbm.at[b], x_sem.at[1])
    y_out.start()
    y_out.wait()


def _lat_path(x, mod, norm_g, ewin, ewout, owin, owout, gq2, gk2, sink, cos, sn, sp, caches, cs, ss, fg):
    b, s, _ = x.shape
    n_even, past = caches[0].shape[1], caches[0].shape[3]
    hbm = pl.BlockSpec(memory_space=pl.ANY)
    cache_spec = pl.BlockSpec((None, n_even, KV_W, past), lambda i: (i, 0, 0, 0))
    return pl.pallas_call(
        _lat_kernel,
        grid=(b,),
        in_specs=[
            pl.BlockSpec(memory_space=pltpu.SMEM),
            hbm,
            _resident(mod.shape), _resident(norm_g.shape),
            hbm, hbm, hbm, hbm,
            _resident(gq2.shape), _resident(gk2.shape),
            _resident(cos.shape), _resident(sn.shape), _resident(sp.shape),
            cache_spec, cache_spec, cache_spec, cache_spec,
            _resident(cs.shape), _resident(ss.shape), _resident(fg.shape),
        ],
        out_specs=hbm,
        out_shape=jax.ShapeDtypeStruct(x.shape, F32),
        scratch_shapes=[
            pltpu.VMEM((s, D_MODEL), F32),
            pltpu.VMEM((2, D_MODEL, EVEN_IN), BF16),
            pltpu.VMEM((2, D_MODEL, D_MODEL), BF16),
            pltpu.VMEM((4, s, KV_W), F32),
            pltpu.VMEM((N_KV, 2 * (s + past), KV_W), BF16),
            pltpu.VMEM((N_KV, 2 * HEAD_DIM + DEN_ROWS, 2 * (s + past)), BF16),
            pltpu.VMEM((LAT_UNROLL, ROWS, 4 * MIX_W), F32),
            pltpu.VMEM((LAT_UNROLL, ROWS, D_MODEL), BF16),
            pltpu.VMEM((s, D_MODEL), BF16),
            pltpu.VMEM((s, D_MODEL // 2), BF16),
            pltpu.VMEM((s, D_MODEL), F32),
            pltpu.SemaphoreType.DMA((2,)), pltpu.SemaphoreType.DMA((4,)),
        ],
        compiler_params=pltpu.CompilerParams(
            dimension_semantics=("arbitrary",), vmem_limit_bytes=LAT_VMEM_LIMIT),
        name="lat_path",
    )(sink, x, mod, norm_g, ewin, ewout, owin, owout, gq2, gk2, cos, sn, sp, *caches, cs, ss, fg)


def kernel(x_prompt, x_sample, cache_k_a, cache_v_a, cache_k_b, cache_v_b, c, c_ctx, norm_g, ada_w, ada_b,
           even_w_in, even_w_out, qk_g_q, qk_g_k, sink_logit, odd_w_in, odd_w_out, final_g):
    batch, seq, _ = x_prompt.shape
    dec_batch, dec_seq, _ = x_sample.shape
    n_even = even_w_in.shape[0]
    past = cache_k_a.shape[2]

    cond = jnp.concatenate(
        [c_ctx[None, :], c, jnp.zeros((COND_ROWS - 1 - dec_batch, D_MODEL), F32)], axis=0)
    mod = _ada_all(cond, ada_w, ada_b)

    epack = jnp.asarray(_packed_channel_dft()).astype(BF16)
    perm = jnp.asarray(_mirror_perm()).astype(BF16)
    cs_ctx, ss_ctx = (jnp.asarray(t).astype(BF16) for t in _position_dft(seq))
    cs_lat, ss_lat = (jnp.asarray(t).astype(BF16) for t in _position_dft(dec_seq))
    cos, sn, sp = (jnp.asarray(t) for t in _rope_tables(dec_seq))

    caches = [jnp.transpose(a, (0, 1, 3, 4, 2)).reshape(dec_batch, n_even, KV_W, past)
              for a in (cache_k_a, cache_v_a, cache_k_b, cache_v_b)]
    fg = final_g.reshape(1, D_MODEL)
    gq2 = jnp.tile(qk_g_q, (1, 2))
    gk2 = jnp.tile(qk_g_k, (1, 2))

    xc, *outs = _ctx_path(x_prompt, mod, norm_g, even_w_in, even_w_out, odd_w_in, odd_w_out, gq2, gk2,
                          sink_logit, epack, perm, cs_ctx, ss_ctx, fg)
    new_kv = [jnp.transpose(a, (0, 1, 4, 2, 3)) for a in outs[:4]]
    ewin, ewout, owin, owout = outs[4:]

    xl = _lat_path(x_sample, mod, norm_g, ewin, ewout, owin, owout, gq2, gk2, sink_logit, cos, sn, sp,
                   caches, cs_lat, ss_lat, fg)
    return (xc, xl, *new_kv)
```

```python
import numpy as np
import jax
import jax.numpy as jnp
from jax import lax
from jax.experimental import pallas as pl
from jax.experimental.pallas import tpu as pltpu

D_MODEL = 1024
DEPTH = 4
HEAD_DIM = 64
N_HEADS = 8
N_KV = 2
GROUP = N_HEADS // N_KV
MIX_W = N_HEADS * HEAD_DIM
KV_W = N_KV * HEAD_DIM
CHUNK = 2 * HEAD_DIM
EVEN_IN = 2 * (2 * MIX_W + 2 * KV_W)
ODD_IN = 2 * D_MODEL
GRID_W = 64
WINDOW = 128
ROPE_BASE = 10000.0
C_GROUPS = 4
C_GROUP_DIM = D_MODEL // C_GROUPS
EPS = 1e-6
NEG_BIG = -1e30
ROWS = 256
COND_ROWS = 8
VMEM_LIMIT = 48 * 1024 * 1024
CTX_VMEM_LIMIT = 56 * 1024 * 1024
LAT_VMEM_LIMIT = 56 * 1024 * 1024
STAGE_ROWS = 64
STAGE_SLOTS = 8
CTX_BATCH_PER_STEP = 2
SKEW_GAPS = (2, 0, 1)

QA, KA, VA, GA = 0, 512, 640, 768
QB, KB, VB, GB = 1280, 1792, 1920, 2048
LQA, LGA, LQB, LGB = 0, 512, 1024, 1536

F32 = jnp.float32
BF16 = jnp.bfloat16


def _dft_tables(n):
    k = np.arange(n, dtype=np.int64)
    ang = ((k[:, None] * k[None, :]) % n).astype(np.float64) * (2.0 * np.pi / n)
    return np.cos(ang).astype(np.float32), np.sin(ang).astype(np.float32)


HALF_GROUP = C_GROUP_DIM // 2


def _packed_channel_dft():
    c, s = _dft_tables(C_GROUP_DIM)
    return np.concatenate([c[:, :HALF_GROUP], c[:, HALF_GROUP:HALF_GROUP + 1], s[:, 1:HALF_GROUP]], axis=1)


def _position_dft(n):
    scale = np.float32(1.0 / np.sqrt(float(n * C_GROUP_DIM)))
    c, s = _dft_tables(n)
    return c * scale, s * scale


def _mirror_perm():
    t = np.array([0] + [HALF_GROUP - m for m in range(1, HALF_GROUP)])
    p = np.zeros((HALF_GROUP, HALF_GROUP), np.float32)
    p[t, np.arange(HALF_GROUP)] = 1.0
    return p


def _rope_tables(n_tok):
    rows = n_tok // GRID_W
    row = np.repeat(np.arange(rows), GRID_W).astype(np.float64)
    col = np.tile(np.arange(GRID_W), rows).astype(np.float64)
    half = HEAD_DIM // 2
    inv = 1.0 / (ROPE_BASE ** (np.arange(0, half, 2, dtype=np.float64) / half))
    ang_r = row[:, None] * inv
    ang_c = col[:, None] * inv
    zeros = np.zeros_like(ang_r)
    cos_h = np.concatenate([np.cos(ang_r), np.cos(ang_r), np.cos(ang_c), np.cos(ang_c)], axis=1)
    nxt_h = np.concatenate([-np.sin(ang_r), zeros, -np.sin(ang_c), zeros], axis=1)
    prv_h = np.concatenate([zeros, np.sin(ang_r), zeros, np.sin(ang_c)], axis=1)
    two = lambda t: np.concatenate([t, t], axis=1).astype(np.float32)
    return two(cos_h), two(nxt_h), two(prv_h)


def _silu(x):
    return x * jax.nn.sigmoid(x)


def _norm_mod(x, g, shift, scale):
    ms = jnp.mean(x * x, axis=-1, keepdims=True)
    return x * lax.rsqrt(ms + EPS) * (g * (1.0 + scale)) + shift


def _rms_norm(x, g):
    ms = jnp.mean(x * x, axis=-1, keepdims=True)
    return x * lax.rsqrt(ms + EPS) * g


def _rms_pair(xc, g2):
    lo = lax.broadcasted_iota(jnp.int32, xc.shape, 1) < HEAD_DIM
    ss = xc * xc
    s_lo = jnp.sum(jnp.where(lo, ss, 0.0), axis=-1, keepdims=True)
    s_hi = jnp.sum(jnp.where(lo, 0.0, ss), axis=-1, keepdims=True)
    ms = jnp.where(lo, s_lo, s_hi) * (1.0 / HEAD_DIM)
    return xc * lax.rsqrt(ms + EPS) * g2


def _rope(xc, cos, sin_next, sin_prev):
    quarter = HEAD_DIM // 4
    nxt = pltpu.roll(xc, CHUNK - quarter, 1)
    prv = pltpu.roll(xc, quarter, 1)
    return xc * cos + nxt * sin_next + prv * sin_prev


DEN_ROWS = 16
LOG2E = float(np.log2(np.e))
Q_SCALE = (HEAD_DIM ** -0.5) * LOG2E


def _sink_pair(sink_ref, i, head):
    return sink_ref[i, head] * LOG2E, sink_ref[i, head + 1] * LOG2E


def _kv_operands(k2, v2t, j):
    tk = k2.shape[0]
    low = lax.broadcasted_iota(jnp.int32, k2.shape, 1) < HEAD_DIM
    km = jnp.where(low if j == 0 else jnp.logical_not(low), k2, 0.0)
    kr = pltpu.roll(km, HEAD_DIM, 1)
    k_lo, k_hi = (km, kr) if j == 0 else (kr, km)
    k_cat = jnp.concatenate([k_lo, k_hi], axis=0).astype(BF16)
    vjt = v2t[HEAD_DIM * j:HEAD_DIM * (j + 1), :]
    zero = jnp.zeros_like(vjt)
    row = lax.broadcasted_iota(jnp.int32, (DEN_ROWS, 2 * tk), 0)
    col = lax.broadcasted_iota(jnp.int32, (DEN_ROWS, 2 * tk), 1)
    ones = jnp.where(((row == 0) & (col < tk)) | ((row == 1) & (col >= tk)), 1.0, 0.0)
    w_t = jnp.concatenate([jnp.concatenate([vjt, zero], axis=1),
                           jnp.concatenate([zero, vjt], axis=1), ones], axis=0).astype(BF16)
    return k_cat, w_t


def _scores_t(qc, k_cat):
    return lax.dot_general(k_cat, qc.astype(BF16), (((1,), (1,)), ((), ())), preferred_element_type=F32)


def _softmax_t(s_t, sinks=None, mask_t=None):
    tk = s_t.shape[0] // 2
    es, ms = [], []
    for hh in range(2):
        sh = s_t[hh * tk:(hh + 1) * tk]
        if mask_t is not None:
            sh = jnp.where(mask_t, sh, NEG_BIG)
        m = jnp.max(sh, axis=0, keepdims=True)
        if sinks is not None:
            m = jnp.maximum(m, sinks[hh])
        es.append(jnp.exp2(sh - m))
        ms.append(m)
    return jnp.concatenate(es, axis=0).astype(BF16), ms


def _values_t(e_t, w_t):
    return jnp.dot(w_t, e_t, preferred_element_type=F32)


def _normalise_t(nd, ms, sinks=None):
    dens = [nd[2 * HEAD_DIM + hh:2 * HEAD_DIM + hh + 1] for hh in range(2)]
    if sinks is not None:
        dens = [dens[hh] + jnp.exp2(sinks[hh] - ms[hh]) for hh in range(2)]
    inv = [1.0 / d for d in dens]
    o_t = jnp.concatenate([nd[0:HEAD_DIM] * inv[0], nd[HEAD_DIM:2 * HEAD_DIM] * inv[1]], axis=0)
    return o_t.T


def _run_skewed(items, stages):
    state = list(items)
    for t in range(len(items) + len(stages) - 1):
        for k, stage in enumerate(stages):
            if 0 <= t - k < len(items):
                state[t - k] = stage(state[t - k])


def _attention_chunks(items, read_q, read_gate, write_out):
    def scores(it):
        return dict(it, s_t=_scores_t(read_q(it), it["k_cat"]))

    def softmax(it):
        e_t, ms = _softmax_t(it["s_t"], it["sinks"], it["mask"])
        return dict(it, e_t=e_t, ms=ms, s_t=None)

    def values(it):
        return dict(it, nd=_values_t(it["e_t"], it["w_t"]), e_t=None)

    def finish(it):
        o = _normalise_t(it["nd"], it["ms"], it["sinks"])
        write_out(it, (o * read_gate(it)).astype(BF16))
        return None

    def hold(it):
        return it

    stages = [scores]
    for gap, stage in zip(SKEW_GAPS, (softmax, values, finish)):
        stages += [hold] * gap + [stage]
    _run_skewed(items, stages)


def _ada_kernel(cond_ref, w_ref, b_ref, o_ref):
    a = _silu(cond_ref[...]).astype(BF16)
    o_ref[...] = jnp.dot(a, w_ref[...].astype(BF16), preferred_element_type=F32) + b_ref[...]


def _ada_all(cond, ada_w, ada_b):
    return pl.pallas_call(
        _ada_kernel,
        grid=(DEPTH, 3),
        in_specs=[
            pl.BlockSpec((COND_ROWS, D_MODEL), lambda l, p: (0, 0)),
            pl.BlockSpec((None, D_MODEL, D_MODEL), lambda l, p: (l, 0, p)),
            pl.BlockSpec((None, None, 1, D_MODEL), lambda l, p: (l, p, 0, 0)),
        ],
        out_specs=pl.BlockSpec((None, None, COND_ROWS, D_MODEL), lambda l, p: (l, p, 0, 0)),
        out_shape=jax.ShapeDtypeStruct((DEPTH, 3, COND_ROWS, D_MODEL), F32),
        compiler_params=pltpu.CompilerParams(
            dimension_semantics=("parallel", "parallel"), vmem_limit_bytes=VMEM_LIMIT),
        name="ada_mod",
    )(cond, ada_w, ada_b.reshape(DEPTH, 3, 1, D_MODEL))


def _ctx_even_mix(i, sink_ref, gq_ref, gk_ref, kv_refs, p_ref, o_ref):
    n_b, seq, _ = p_ref.shape
    gq2 = gq_ref[i:i + 1, :] * Q_SCALE
    gk2 = gk_ref[i:i + 1, :]
    items = []
    for mixer, (q0, k0, v0, g0) in enumerate(((QA, KA, VA, GA), (QB, KB, VB, GB))):
        kv = []
        for bb in range(n_b):
            k2 = p_ref[bb, :, k0:k0 + KV_W]
            v2 = p_ref[bb, :, v0:v0 + KV_W]
            if mixer == 0:
                k2 = _rms_pair(k2, gk2)
            v2t = v2.T
            kv_refs[2 * mixer][bb, i] = k2.T.reshape(N_KV, HEAD_DIM, seq)
            kv_refs[2 * mixer + 1][bb, i] = v2t.reshape(N_KV, HEAD_DIM, seq)
            kv.append((k2, v2t))
        for j in range(N_KV):
            ops = [_kv_operands(k2, v2t, j) for k2, v2t in kv]
            for cc in range(GROUP // 2):
                c0 = CHUNK * (j * (GROUP // 2) + cc)
                sinks = None if mixer == 0 else _sink_pair(sink_ref, i, c0 // HEAD_DIM)
                for bb, (k_cat, w_t) in enumerate(ops):
                    items.append(dict(rows=bb, q=q0 + c0, g=g0 + c0, o=mixer * MIX_W + c0,
                                      norm=mixer == 0, sinks=sinks, mask=None, k_cat=k_cat, w_t=w_t))

    def read_q(it):
        qc = p_ref[it["rows"], :, it["q"]:it["q"] + CHUNK]
        return _rms_pair(qc, gq2) if it["norm"] else qc * Q_SCALE

    def read_gate(it):
        return _silu(p_ref[it["rows"], :, it["g"]:it["g"] + CHUNK])

    def write_out(it, value):
        o_ref[it["rows"], :, it["o"]:it["o"] + CHUNK] = value

    _attention_chunks(items, read_q, read_gate, write_out)


def _spectrum_operands(t):
    upper = [t[:, C_GROUP_DIM * g + HALF_GROUP:C_GROUP_DIM * (g + 1)] for g in range(C_GROUPS)]
    return t.astype(BF16), jnp.concatenate(upper, axis=1).astype(BF16)


def _spectrum_combine(m_cos, m_sin, read_gate, write_out):
    first = lax.broadcasted_iota(jnp.int32, (m_cos.shape[0], HALF_GROUP), 1) == 0
    for g in range(C_GROUPS):
        c0 = C_GROUP_DIM * g
        p = m_cos[:, c0:c0 + HALF_GROUP]
        p_mid = m_cos[:, c0 + HALF_GROUP:c0 + C_GROUP_DIM]
        q = jnp.where(first, 0.0, m_sin[:, HALF_GROUP * g:HALF_GROUP * (g + 1)])
        write_out(c0, ((p - q) * read_gate(c0)).astype(BF16))
        upper = jnp.where(first, p_mid, p + q)
        write_out(c0 + HALF_GROUP, (upper * read_gate(c0 + HALF_GROUP)).astype(BF16))


def _ctx_odd_mix(cs_ref, ss_ref, p_ref, o_ref):
    n_b = p_ref.shape[0]
    for bb in range(n_b):
        for_cos, for_sin = _spectrum_operands(p_ref[bb, :, 0:D_MODEL])
        m_cos = jnp.dot(cs_ref[...], for_cos, preferred_element_type=F32)
        m_sin = jnp.dot(ss_ref[...], for_sin, preferred_element_type=F32)

        def read_gate(c0, bb=bb):
            return _silu(p_ref[bb, :, D_MODEL + c0:D_MODEL + c0 + HALF_GROUP])

        def write_out(c0, value, bb=bb):
            o_ref[bb, :, c0:c0 + HALF_GROUP] = value

        _spectrum_combine(m_cos, m_sin, read_gate, write_out)


def _fold_odd_weights(owin_ref, owout_ref, epack_ref, perm_ref):
    for layer in range(owin_ref.shape[0]):
        for g in range(C_GROUPS):
            c0 = C_GROUP_DIM * g
            owin_ref[layer, :, c0:c0 + C_GROUP_DIM] = jnp.dot(
                owin_ref[layer, :, c0:c0 + C_GROUP_DIM], epack_ref[...],
                preferred_element_type=F32).astype(BF16)
            g0 = D_MODEL + c0 + HALF_GROUP
            owin_ref[layer, :, g0:g0 + HALF_GROUP] = jnp.dot(
                owin_ref[layer, :, g0:g0 + HALF_GROUP], perm_ref[...], preferred_element_type=F32).astype(BF16)
            r0 = c0 + HALF_GROUP
            owout_ref[layer, r0:r0 + HALF_GROUP, :] = jnp.dot(
                perm_ref[...], owout_ref[layer, r0:r0 + HALF_GROUP, :], preferred_element_type=F32).astype(BF16)


def _stage_copy(w_hbm, stage_ref, sem, chunk, slot):
    _, n_rows, cols = w_hbm.shape
    rows = stage_ref.shape[1]
    per_layer = n_rows // rows
    src = w_hbm.at[chunk // per_layer, pl.ds((chunk % per_layer) * rows, rows), :]
    return pltpu.make_async_copy(src, stage_ref.at[slot, :, pl.ds(0, cols)], sem.at[slot])


def _convert_weights(w_hbm, w_ref, stage_ref, in_sem):
    n_layers, n_rows, cols = w_hbm.shape
    n_slots, rows, _ = stage_ref.shape
    per_layer = n_rows // rows
    n_chunks = n_layers * per_layer
    for c in range(n_slots - 1):
        _stage_copy(w_hbm, stage_ref, in_sem, c, c).start()

    def body(c, carry):
        slot = c % n_slots
        ahead = c + n_slots - 1

        @pl.when(ahead < n_chunks)
        def _():
            _stage_copy(w_hbm, stage_ref, in_sem, ahead, ahead % n_slots).start()

        _stage_copy(w_hbm, stage_ref, in_sem, c, slot).wait()
        r0 = pl.multiple_of((c % per_layer) * rows, rows)
        w_ref[c // per_layer, pl.ds(r0, rows), :] = stage_ref[slot, :, 0:cols].astype(BF16)
        return carry

    lax.fori_loop(0, n_chunks, body, 0)


def _ctx_kernel(sink_ref, x_ref, mod_ref, g_ref, ewin_hbm, ewout_hbm, owin_hbm, owout_hbm, gq_ref, gk_ref,
                epack_ref, perm_ref, cs_ref, ss_ref, fg_ref,
                xo_ref, ka_ref, va_ref, kb_ref, vb_ref, ewin_o, ewout_o, owin_o, owout_o,
                xs_ref, p_ref, o_ref, ewin_ref, ewout_ref, owin_ref, owout_ref,
                stage_ref, in_sem, out_sem):
    weights = ((ewin_hbm, ewin_o, ewin_ref), (ewout_hbm, ewout_o, ewout_ref),
               (owin_hbm, owin_o, owin_ref), (owout_hbm, owout_o, owout_ref))
    exports = [pltpu.make_async_copy(res, dst, out_sem.at[k]) for k, (_, dst, res) in enumerate(weights)]

    @pl.when(pl.program_id(0) == 0)
    def _():
        for src, _, res in weights:
            _convert_weights(src, res, stage_ref, in_sem)
        _fold_odd_weights(owin_ref, owout_ref, epack_ref, perm_ref)
        for export in exports:
            export.start()

    @pl.when(pl.program_id(0) == pl.num_programs(0) - 1)
    def _():
        for export in exports:
            export.wait()

    n_b = x_ref.shape[0]
    for l in range(DEPTH):
        i = l // 2
        even = l % 2 == 0
        win_ref, wout_ref, width = (ewin_ref, ewout_ref, EVEN_IN) if even else (owin_ref, owout_ref, ODD_IN)
        shift, scale, gate = (mod_ref[l, part, 0:1, :] for part in range(3))
        for bb in range(n_b):
            x = x_ref[bb] if l == 0 else xs_ref[bb]
            h = _norm_mod(x, g_ref[l:l + 1, :], shift, scale)
            p_ref[bb, :, 0:width] = jnp.dot(h.astype(BF16), win_ref[i], preferred_element_type=F32)
        if even:
            _ctx_even_mix(i, sink_ref, gq_ref, gk_ref, (ka_ref, va_ref, kb_ref, vb_ref), p_ref, o_ref)
        else:
            _ctx_odd_mix(cs_ref, ss_ref, p_ref, o_ref)
        for bb in range(n_b):
            y = jnp.dot(o_ref[bb], wout_ref[i], preferred_element_type=F32)
            xn = (x_ref[bb] if l == 0 else xs_ref[bb]) + gate * y
            if l == DEPTH - 1:
                xo_ref[bb] = _rms_norm(xn, fg_ref[...])
            else:
                xs_ref[bb] = xn


def _resident(shape):
    return pl.BlockSpec(shape, lambda i: (0,) * len(shape), pipeline_mode=pl.Buffered(1))


def _ctx_path(x, mod, norm_g, ewin, ewout, owin, owout, gq2, gk2, sink, epack, perm, cs, ss, fg):
    b, s, _ = x.shape
    n_even = ewin.shape[0]
    bb = CTX_BATCH_PER_STEP
    kv_shape = jax.ShapeDtypeStruct((b, n_even, N_KV, HEAD_DIM, s), F32)
    kv_spec = pl.BlockSpec((bb, n_even, N_KV, HEAD_DIM, s), lambda i: (i, 0, 0, 0, 0))
    x_spec = pl.BlockSpec((bb, s, D_MODEL), lambda i: (i, 0, 0))
    hbm = pl.BlockSpec(memory_space=pl.ANY)
    weights = (ewin, ewout, owin, owout)
    return pl.pallas_call(
        _ctx_kernel,
        grid=(b // bb,),
        in_specs=[
            pl.BlockSpec(memory_space=pltpu.SMEM),
            x_spec,
            _resident(mod.shape), _resident(norm_g.shape),
            hbm, hbm, hbm, hbm,
            _resident(gq2.shape), _resident(gk2.shape),
            _resident(epack.shape), _resident(perm.shape), _resident(cs.shape), _resident(ss.shape),
            _resident(fg.shape),
        ],
        out_specs=[x_spec, kv_spec, kv_spec, kv_spec, kv_spec, hbm, hbm, hbm, hbm],
        out_shape=[jax.ShapeDtypeStruct(x.shape, F32), kv_shape, kv_shape, kv_shape, kv_shape]
                  + [jax.ShapeDtypeStruct(w.shape, BF16) for w in weights],
        scratch_shapes=[pltpu.VMEM((bb, s, D_MODEL), F32), pltpu.VMEM((bb, s, EVEN_IN), F32),
                        pltpu.VMEM((bb, s, D_MODEL), BF16)]
                       + [pltpu.VMEM(w.shape, BF16) for w in weights]
                       + [pltpu.VMEM((STAGE_SLOTS, STAGE_ROWS, max(w.shape[2] for w in weights)), F32),
                          pltpu.SemaphoreType.DMA((STAGE_SLOTS,)), pltpu.SemaphoreType.DMA((len(weights),))],
        compiler_params=pltpu.CompilerParams(
            dimension_semantics=("arbitrary",), vmem_limit_bytes=CTX_VMEM_LIMIT),
        name="ctx_path",
    )(sink, x, mod, norm_g, *weights, gq2, gk2, epack, perm, cs, ss, fg)


def _lat_weight_copies(l, w_hbms, win_buf, wout_buf, w_sem):
    ewin_hbm, ewout_hbm, owin_hbm, owout_hbm = w_hbms
    i, slot = l // 2, l % 2
    w_in, w_out = (ewin_hbm, ewout_hbm) if l % 2 == 0 else (owin_hbm, owout_hbm)
    return (pltpu.make_async_copy(w_in.at[i], win_buf.at[slot, :, pl.ds(0, w_in.shape[2])], w_sem.at[2 * slot]),
            pltpu.make_async_copy(w_out.at[i], wout_buf.at[slot], w_sem.at[2 * slot + 1]))


def _lat_attention_items(n, layer_i, sink_ref, kv_ref, kcat_ref, wt_ref, cache_refs):
    seq = kv_ref.shape[1]
    ckb_t, cvb_t = cache_refs[2][layer_i], cache_refs[3][layer_i]
    prev0 = pl.multiple_of(jnp.maximum(n * ROWS - WINDOW, 0), WINDOW)
    own0 = pl.multiple_of(n * ROWS, ROWS)
    next0 = pl.multiple_of(jnp.minimum(n * ROWS + ROWS, seq - WINDOW), WINDOW)
    win_len = ROWS + 2 * WINDOW
    ctx_len = ckb_t.shape[1]
    cj = lax.broadcasted_iota(jnp.int32, (win_len + ctx_len, ROWS), 0)
    qi = lax.broadcasted_iota(jnp.int32, (win_len + ctx_len, ROWS), 1)
    kpos = n * ROWS - WINDOW + cj
    in_win = (jnp.abs(cj - WINDOW - qi) <= WINDOW) & (kpos >= 0) & (kpos < seq)
    mask_b = in_win | (cj >= win_len)

    def window(idx):
        return jnp.concatenate([kv_ref[idx, pl.ds(prev0, WINDOW), :], kv_ref[idx, pl.ds(own0, ROWS), :],
                                kv_ref[idx, pl.ds(next0, WINDOW), :]], axis=0)

    k2b = jnp.concatenate([window(2), ckb_t.T], axis=0)
    v2bt = jnp.concatenate([window(3).T, cvb_t], axis=1)
    items = []
    for mixer, (q0, g0) in enumerate(((LQA, LGA), (LQB, LGB))):
        for j in range(N_KV):
            k_cat, w_t = (kcat_ref[j], wt_ref[j]) if mixer == 0 else _kv_operands(k2b, v2bt, j)
            for cc in range(GROUP // 2):
                c0 = CHUNK * (j * (GROUP // 2) + cc)
                if mixer == 0:
                    sinks, mask = None, None
                else:
                    sinks, mask = _sink_pair(sink_ref, layer_i, c0 // HEAD_DIM), mask_b
                items.append(dict(q=q0 + c0, g=g0 + c0, o=mixer * MIX_W + c0, sinks=sinks, mask=mask,
                                  k_cat=k_cat, w_t=w_t))
    return items


def _lat_kernel(sink_ref, x_hbm, mod_ref, g_ref, ewin_hbm, ewout_hbm, owin_hbm, owout_hbm, gq_ref, gk_ref,
                cos_ref, sn_ref, sp_ref, cka_ref, cva_ref, ckb_ref, cvb_ref, cs_ref, ss_ref, fg_ref,
                y_hbm,
                xw_ref, win_buf, wout_buf, kv_ref, kcat_ref, wt_ref, p_ref, o_ref, fcos_ref, fsin_ref, gate_ref,
                x_sem, w_sem):
    b = pl.program_id(0)
    seq = xw_ref.shape[0]
    n_blocks = seq // ROWS
    w_hbms = (ewin_hbm, ewout_hbm, owin_hbm, owout_hbm)

    x_in = pltpu.make_async_copy(x_hbm.at[b], xw_ref, x_sem.at[0])
    x_in.start()

    def start_weights(l):
        for copy in _lat_weight_copies(l, w_hbms, win_buf, wout_buf, w_sem):
            copy.start()

    pl.when(b == 0)(lambda: start_weights(0))
    x_in.wait()

    for l in range(DEPTH):
        i, slot = l // 2, l % 2
        for copy in _lat_weight_copies(l, w_hbms, win_buf, wout_buf, w_sem):
            copy.wait()
        if l + 1 < DEPTH:
            start_weights(l + 1)
        else:
            pl.when(b + 1 < pl.num_programs(0))(lambda: start_weights(0))
        shift, scale, gate = (mod_ref[l, part, pl.ds(1 + b, 1), :] for part in range(3))
        g = g_ref[l:l + 1, :]

        def normed(rows):
            return _norm_mod(xw_ref[rows, :], g, shift, scale).astype(BF16)

        def w_in(c0, width):
            return win_buf[slot, :, c0:c0 + width]

        if l % 2 == 0:
            gq2 = gq_ref[i:i + 1, :] * Q_SCALE
            gk2 = gk_ref[i:i + 1, :]

            def project_kv(n, carry):
                rows = pl.ds(pl.multiple_of(n * ROWS, ROWS), ROWS)
                h = normed(rows)
                cos, sn, sp = cos_ref[rows, :], sn_ref[rows, :], sp_ref[rows, :]
                for mixer, c0 in enumerate((KA, KB)):
                    kv = jnp.dot(h, w_in(c0, 2 * KV_W), preferred_element_type=F32)
                    k2 = kv[:, :KV_W]
                    if mixer == 0:
                        k2 = _rms_pair(k2, gk2)
                    kv_ref[2 * mixer, rows, :] = _rope(k2, cos, sn, sp)
                    kv_ref[2 * mixer + 1, rows, :] = kv[:, KV_W:]
                return carry

            lax.fori_loop(0, n_blocks, project_kv, 0)
            k2a = jnp.concatenate([kv_ref[0], cka_ref[i].T], axis=0)
            v2at = jnp.concatenate([kv_ref[1].T, cva_ref[i]], axis=1)
            for j in range(N_KV):
                kcat_ref[j], wt_ref[j] = _kv_operands(k2a, v2at, j)

            def attend(n, carry):
                rows = pl.ds(pl.multiple_of(n * ROWS, ROWS), ROWS)
                h = normed(rows)
                cos, sn, sp = cos_ref[rows, :], sn_ref[rows, :], sp_ref[rows, :]
                qa = jnp.dot(h, w_in(QA, MIX_W), preferred_element_type=F32)
                for c0 in range(0, MIX_W, CHUNK):
                    p_ref[:, LQA + c0:LQA + c0 + CHUNK] = _rope(_rms_pair(qa[:, c0:c0 + CHUNK], gq2), cos, sn, sp)
                mid = jnp.dot(h, w_in(GA, 2 * MIX_W), preferred_element_type=F32)
                p_ref[:, LGA:LGA + MIX_W] = _silu(mid[:, :MIX_W])
                for c0 in range(0, MIX_W, CHUNK):
                    qb = mid[:, MIX_W + c0:MIX_W + c0 + CHUNK] * Q_SCALE
                    p_ref[:, LQB + c0:LQB + c0 + CHUNK] = _rope(qb, cos, sn, sp)
                p_ref[:, LGB:LGB + MIX_W] = _silu(jnp.dot(h, w_in(GB, MIX_W), preferred_element_type=F32))
                items = _lat_attention_items(n, i, sink_ref, kv_ref, kcat_ref, wt_ref,
                                             (cka_ref, cva_ref, ckb_ref, cvb_ref))

                def write_out(it, value):
                    o_ref[:, it["o"]:it["o"] + CHUNK] = value

                _attention_chunks(items, lambda it: p_ref[:, it["q"]:it["q"] + CHUNK],
                                  lambda it: p_ref[:, it["g"]:it["g"] + CHUNK], write_out)
                y = jnp.dot(o_ref[...], wout_buf[slot], preferred_element_type=F32)
                xw_ref[rows, :] = xw_ref[rows, :] + gate * y
                return carry

            lax.fori_loop(0, n_blocks, attend, 0)
        else:
            def project(n, carry):
                rows = pl.ds(pl.multiple_of(n * ROWS, ROWS), ROWS)
                p = jnp.dot(normed(rows), w_in(0, ODD_IN), preferred_element_type=F32)
                fcos_ref[rows, :], fsin_ref[rows, :] = _spectrum_operands(p[:, 0:D_MODEL])
                gate_ref[rows, :] = _silu(p[:, D_MODEL:ODD_IN])
                return carry

            lax.fori_loop(0, n_blocks, project, 0)

            def mix(n, carry):
                rows = pl.ds(pl.multiple_of(n * ROWS, ROWS), ROWS)
                m_cos = jnp.dot(cs_ref[rows, :], fcos_ref[...], preferred_element_type=F32)
                m_sin = jnp.dot(ss_ref[rows, :], fsin_ref[...], preferred_element_type=F32)

                def write_out(c0, value):
                    o_ref[:, c0:c0 + HALF_GROUP] = value

                _spectrum_combine(m_cos, m_sin, lambda c0: gate_ref[rows, c0:c0 + HALF_GROUP], write_out)
                y = jnp.dot(o_ref[...], wout_buf[slot], preferred_element_type=F32)
                xn = xw_ref[rows, :] + gate * y
                if l == DEPTH - 1:
                    xn = _rms_norm(xn, fg_ref[...])
                xw_ref[rows, :] = xn
                return carry

            lax.fori_loop(0, n_blocks, mix, 0)

    y_out = pltpu.make_async_copy(xw_ref, y_hbm.at[b], x_sem.at[1])
    y_out.start()
    y_out.wait()


def _lat_path(x, mod, norm_g, ewin, ewout, owin, owout, gq2, gk2, sink, cos, sn, sp, caches, cs, ss, fg):
    b, s, _ = x.shape
    n_even, past = caches[0].shape[1], caches[0].shape[3]
    hbm = pl.BlockSpec(memory_space=pl.ANY)
    cache_spec = pl.BlockSpec((None, n_even, KV_W, past), lambda i: (i, 0, 0, 0))
    return pl.pallas_call(
        _lat_kernel,
        grid=(b,),
        in_specs=[
            pl.BlockSpec(memory_space=pltpu.SMEM),
            hbm,
            _resident(mod.shape), _resident(norm_g.shape),
            hbm, hbm, hbm, hbm,
            _resident(gq2.shape), _resident(gk2.shape),
            _resident(cos.shape), _resident(sn.shape), _resident(sp.shape),
            cache_spec, cache_spec, cache_spec, cache_spec,
            _resident(cs.shape), _resident(ss.shape), _resident(fg.shape),
        ],
        out_specs=hbm,
        out_shape=jax.ShapeDtypeStruct(x.shape, F32),
        scratch_shapes=[
            pltpu.VMEM((s, D_MODEL), F32),
            pltpu.VMEM((2, D_MODEL, EVEN_IN), BF16),
            pltpu.VMEM((2, D_MODEL, D_MODEL), BF16),
            pltpu.VMEM((4, s, KV_W), F32),
            pltpu.VMEM((N_KV, 2 * (s + past), KV_W), BF16),
            pltpu.VMEM((N_KV, 2 * HEAD_DIM + DEN_ROWS, 2 * (s + past)), BF16),
            pltpu.VMEM((ROWS, 4 * MIX_W), F32),
            pltpu.VMEM((ROWS, D_MODEL), BF16),
            pltpu.VMEM((s, D_MODEL), BF16),
            pltpu.VMEM((s, D_MODEL // 2), BF16),
            pltpu.VMEM((s, D_MODEL), F32),
            pltpu.SemaphoreType.DMA((2,)), pltpu.SemaphoreType.DMA((4,)),
        ],
        compiler_params=pltpu.CompilerParams(
            dimension_semantics=("arbitrary",), vmem_limit_bytes=LAT_VMEM_LIMIT),
        name="lat_path",
    )(sink, x, mod, norm_g, ewin, ewout, owin, owout, gq2, gk2, cos, sn, sp, *caches, cs, ss, fg)


def kernel(x_prompt, x_sample, cache_k_a, cache_v_a, cache_k_b, cache_v_b, c, c_ctx, norm_g, ada_w, ada_b,
           even_w_in, even_w_out, qk_g_q, qk_g_k, sink_logit, odd_w_in, odd_w_out, final_g):
    batch, seq, _ = x_prompt.shape
    dec_batch, dec_seq, _ = x_sample.shape
    n_even = even_w_in.shape[0]
    past = cache_k_a.shape[2]

    cond = jnp.concatenate(
        [c_ctx[None, :], c, jnp.zeros((COND_ROWS - 1 - dec_batch, D_MODEL), F32)], axis=0)
    mod = _ada_all(cond, ada_w, ada_b)

    epack = jnp.asarray(_packed_channel_dft()).astype(BF16)
    perm = jnp.asarray(_mirror_perm()).astype(BF16)
    cs_ctx, ss_ctx = (jnp.asarray(t).astype(BF16) for t in _position_dft(seq))
    cs_lat, ss_lat = (jnp.asarray(t).astype(BF16) for t in _position_dft(dec_seq))
    cos, sn, sp = (jnp.asarray(t) for t in _rope_tables(dec_seq))

    caches = [jnp.transpose(a, (0, 1, 3, 4, 2)).reshape(dec_batch, n_even, KV_W, past)
              for a in (cache_k_a, cache_v_a, cache_k_b, cache_v_b)]
    fg = final_g.reshape(1, D_MODEL)
    gq2 = jnp.tile(qk_g_q, (1, 2))
    gk2 = jnp.tile(qk_g_k, (1, 2))

    xc, *outs = _ctx_path(x_prompt, mod, norm_g, even_w_in, even_w_out, odd_w_in, odd_w_out, gq2, gk2,
                          sink_logit, epack, perm, cs_ctx, ss_ctx, fg)
    new_kv = [jnp.transpose(a, (0, 1, 4, 2, 3)) for a in outs[:4]]
    ewin, ewout, owin, owout = outs[4:]

    xl = _lat_path(x_sample, mod, norm_g, ewin, ewout, owin, owout, gq2, gk2, sink_logit, cos, sn, sp,
                   caches, cs_lat, ss_lat, fg)
    return (xc, xl, *new_kv)
```

```python
import numpy as np
import jax
import jax.numpy as jnp
from jax import lax
from jax.experimental import pallas as pl
from jax.experimental.pallas import tpu as pltpu

D_MODEL = 1024
DEPTH = 4
HEAD_DIM = 64
N_HEADS = 8
N_KV = 2
GROUP = N_HEADS // N_KV
MIX_W = N_HEADS * HEAD_DIM
KV_W = N_KV * HEAD_DIM
CHUNK = 2 * HEAD_DIM
EVEN_IN = 2 * (2 * MIX_W + 2 * KV_W)
ODD_IN = 2 * D_MODEL
GRID_W = 64
WINDOW = 128
ROPE_BASE = 10000.0
C_GROUPS = 4
C_GROUP_DIM = D_MODEL // C_GROUPS
EPS = 1e-6
NEG_BIG = -1e30
ROWS = 256
COND_ROWS = 8
VMEM_LIMIT = 48 * 1024 * 1024
CTX_VMEM_LIMIT = 56 * 1024 * 1024
LAT_VMEM_LIMIT = 56 * 1024 * 1024
STAGE_ROWS = 64
STAGE_SLOTS = 8
CTX_BATCH_PER_STEP = 2
SKEW_GAPS = (1, 0, 2)

QA, KA, VA, GA = 0, 512, 640, 768
QB, KB, VB, GB = 1280, 1792, 1920, 2048
LQA, LGA, LQB, LGB = 0, 512, 1024, 1536

F32 = jnp.float32
BF16 = jnp.bfloat16


def _dft_tables(n):
    k = np.arange(n, dtype=np.int64)
    ang = ((k[:, None] * k[None, :]) % n).astype(np.float64) * (2.0 * np.pi / n)
    return np.cos(ang).astype(np.float32), np.sin(ang).astype(np.float32)


HALF_GROUP = C_GROUP_DIM // 2


def _packed_channel_dft():
    c, s = _dft_tables(C_GROUP_DIM)
    return np.concatenate([c[:, :HALF_GROUP], c[:, HALF_GROUP:HALF_GROUP + 1], s[:, 1:HALF_GROUP]], axis=1)


def _position_dft(n):
    scale = np.float32(1.0 / np.sqrt(float(n * C_GROUP_DIM)))
    c, s = _dft_tables(n)
    return c * scale, s * scale


def _mirror_perm():
    t = np.array([0] + [HALF_GROUP - m for m in range(1, HALF_GROUP)])
    p = np.zeros((HALF_GROUP, HALF_GROUP), np.float32)
    p[t, np.arange(HALF_GROUP)] = 1.0
    return p


def _rope_tables(n_tok):
    rows = n_tok // GRID_W
    row = np.repeat(np.arange(rows), GRID_W).astype(np.float64)
    col = np.tile(np.arange(GRID_W), rows).astype(np.float64)
    half = HEAD_DIM // 2
    inv = 1.0 / (ROPE_BASE ** (np.arange(0, half, 2, dtype=np.float64) / half))
    ang_r = row[:, None] * inv
    ang_c = col[:, None] * inv
    zeros = np.zeros_like(ang_r)
    cos_h = np.concatenate([np.cos(ang_r), np.cos(ang_r), np.cos(ang_c), np.cos(ang_c)], axis=1)
    nxt_h = np.concatenate([-np.sin(ang_r), zeros, -np.sin(ang_c), zeros], axis=1)
    prv_h = np.concatenate([zeros, np.sin(ang_r), zeros, np.sin(ang_c)], axis=1)
    two = lambda t: np.concatenate([t, t], axis=1).astype(np.float32)
    return two(cos_h), two(nxt_h), two(prv_h)


def _silu(x):
    return x * jax.nn.sigmoid(x)


def _norm_mod(x, g, shift, scale):
    ms = jnp.mean(x * x, axis=-1, keepdims=True)
    return x * lax.rsqrt(ms + EPS) * (g * (1.0 + scale)) + shift


def _rms_norm(x, g):
    ms = jnp.mean(x * x, axis=-1, keepdims=True)
    return x * lax.rsqrt(ms + EPS) * g


def _rms_pair(xc, g2):
    lo = lax.broadcasted_iota(jnp.int32, xc.shape, 1) < HEAD_DIM
    ss = xc * xc
    s_lo = jnp.sum(jnp.where(lo, ss, 0.0), axis=-1, keepdims=True)
    s_hi = jnp.sum(jnp.where(lo, 0.0, ss), axis=-1, keepdims=True)
    ms = jnp.where(lo, s_lo, s_hi) * (1.0 / HEAD_DIM)
    return xc * lax.rsqrt(ms + EPS) * g2


def _rope(xc, cos, sin_next, sin_prev):
    quarter = HEAD_DIM // 4
    nxt = pltpu.roll(xc, CHUNK - quarter, 1)
    prv = pltpu.roll(xc, quarter, 1)
    return xc * cos + nxt * sin_next + prv * sin_prev


DEN_ROWS = 16
LOG2E = float(np.log2(np.e))
Q_SCALE = (HEAD_DIM ** -0.5) * LOG2E


def _sink_pair(sink_ref, i, head):
    return sink_ref[i, head] * LOG2E, sink_ref[i, head + 1] * LOG2E


def _kv_operands(k2, v2t, j):
    tk = k2.shape[0]
    low = lax.broadcasted_iota(jnp.int32, k2.shape, 1) < HEAD_DIM
    km = jnp.where(low if j == 0 else jnp.logical_not(low), k2, 0.0)
    kr = pltpu.roll(km, HEAD_DIM, 1)
    k_lo, k_hi = (km, kr) if j == 0 else (kr, km)
    k_cat = jnp.concatenate([k_lo, k_hi], axis=0).astype(BF16)
    vjt = v2t[HEAD_DIM * j:HEAD_DIM * (j + 1), :]
    zero = jnp.zeros_like(vjt)
    row = lax.broadcasted_iota(jnp.int32, (DEN_ROWS, 2 * tk), 0)
    col = lax.broadcasted_iota(jnp.int32, (DEN_ROWS, 2 * tk), 1)
    ones = jnp.where(((row == 0) & (col < tk)) | ((row == 1) & (col >= tk)), 1.0, 0.0)
    w_t = jnp.concatenate([jnp.concatenate([vjt, zero], axis=1),
                           jnp.concatenate([zero, vjt], axis=1), ones], axis=0).astype(BF16)
    return k_cat, w_t


def _scores_t(qc, k_cat):
    return lax.dot_general(k_cat, qc.astype(BF16), (((1,), (1,)), ((), ())), preferred_element_type=F32)


def _softmax_t(s_t, sinks=None, mask_t=None):
    tk = s_t.shape[0] // 2
    es, ms = [], []
    for hh in range(2):
        sh = s_t[hh * tk:(hh + 1) * tk]
        if mask_t is not None:
            sh = jnp.where(mask_t, sh, NEG_BIG)
        m = jnp.max(sh, axis=0, keepdims=True)
        if sinks is not None:
            m = jnp.maximum(m, sinks[hh])
        es.append(jnp.exp2(sh - m))
        ms.append(m)
    return jnp.concatenate(es, axis=0).astype(BF16), ms


def _values_t(e_t, w_t):
    return jnp.dot(w_t, e_t, preferred_element_type=F32)


def _normalise_t(nd, ms, sinks=None):
    dens = [nd[2 * HEAD_DIM + hh:2 * HEAD_DIM + hh + 1] for hh in range(2)]
    if sinks is not None:
        dens = [dens[hh] + jnp.exp2(sinks[hh] - ms[hh]) for hh in range(2)]
    inv = [1.0 / d for d in dens]
    o_t = jnp.concatenate([nd[0:HEAD_DIM] * inv[0], nd[HEAD_DIM:2 * HEAD_DIM] * inv[1]], axis=0)
    return o_t.T


def _run_skewed(items, stages):
    state = list(items)
    for t in range(len(items) + len(stages) - 1):
        for k, stage in enumerate(stages):
            if 0 <= t - k < len(items):
                state[t - k] = stage(state[t - k])


def _attention_chunks(items, read_q, read_gate, write_out):
    def scores(it):
        return dict(it, s_t=_scores_t(read_q(it), it["k_cat"]))

    def softmax(it):
        e_t, ms = _softmax_t(it["s_t"], it["sinks"], it["mask"])
        return dict(it, e_t=e_t, ms=ms, s_t=None)

    def values(it):
        return dict(it, nd=_values_t(it["e_t"], it["w_t"]), e_t=None)

    def finish(it):
        o = _normalise_t(it["nd"], it["ms"], it["sinks"])
        write_out(it, (o * read_gate(it)).astype(BF16))
        return None

    def hold(it):
        return it

    stages = [scores]
    for gap, stage in zip(SKEW_GAPS, (softmax, values, finish)):
        stages += [hold] * gap + [stage]
    _run_skewed(items, stages)


def _ada_kernel(cond_ref, w_ref, b_ref, o_ref):
    a = _silu(cond_ref[...]).astype(BF16)
    o_ref[...] = jnp.dot(a, w_ref[...].astype(BF16), preferred_element_type=F32) + b_ref[...]


def _ada_all(cond, ada_w, ada_b):
    return pl.pallas_call(
        _ada_kernel,
        grid=(DEPTH, 3),
        in_specs=[
            pl.BlockSpec((COND_ROWS, D_MODEL), lambda l, p: (0, 0)),
            pl.BlockSpec((None, D_MODEL, D_MODEL), lambda l, p: (l, 0, p)),
            pl.BlockSpec((None, None, 1, D_MODEL), lambda l, p: (l, p, 0, 0)),
        ],
        out_specs=pl.BlockSpec((None, None, COND_ROWS, D_MODEL), lambda l, p: (l, p, 0, 0)),
        out_shape=jax.ShapeDtypeStruct((DEPTH, 3, COND_ROWS, D_MODEL), F32),
        compiler_params=pltpu.CompilerParams(
            dimension_semantics=("parallel", "parallel"), vmem_limit_bytes=VMEM_LIMIT),
        name="ada_mod",
    )(cond, ada_w, ada_b.reshape(DEPTH, 3, 1, D_MODEL))


def _ctx_even_mix(i, sink_ref, gq_ref, gk_ref, kv_refs, p_ref, o_ref):
    n_b, seq, _ = p_ref.shape
    gq2 = gq_ref[i:i + 1, :] * Q_SCALE
    gk2 = gk_ref[i:i + 1, :]
    items = []
    for mixer, (q0, k0, v0, g0) in enumerate(((QA, KA, VA, GA), (QB, KB, VB, GB))):
        kv = []
        for bb in range(n_b):
            k2 = p_ref[bb, :, k0:k0 + KV_W]
            v2 = p_ref[bb, :, v0:v0 + KV_W]
            if mixer == 0:
                k2 = _rms_pair(k2, gk2)
            v2t = v2.T
            kv_refs[2 * mixer][bb, i] = k2.T.reshape(N_KV, HEAD_DIM, seq)
            kv_refs[2 * mixer + 1][bb, i] = v2t.reshape(N_KV, HEAD_DIM, seq)
            kv.append((k2, v2t))
        for j in range(N_KV):
            ops = [_kv_operands(k2, v2t, j) for k2, v2t in kv]
            for cc in range(GROUP // 2):
                c0 = CHUNK * (j * (GROUP // 2) + cc)
                sinks = None if mixer == 0 else _sink_pair(sink_ref, i, c0 // HEAD_DIM)
                for bb, (k_cat, w_t) in enumerate(ops):
                    items.append(dict(rows=bb, q=q0 + c0, g=g0 + c0, o=mixer * MIX_W + c0,
                                      norm=mixer == 0, sinks=sinks, mask=None, k_cat=k_cat, w_t=w_t))

    def read_q(it):
        qc = p_ref[it["rows"], :, it["q"]:it["q"] + CHUNK]
        return _rms_pair(qc, gq2) if it["norm"] else qc * Q_SCALE

    def read_gate(it):
        return _silu(p_ref[it["rows"], :, it["g"]:it["g"] + CHUNK])

    def write_out(it, value):
        o_ref[it["rows"], :, it["o"]:it["o"] + CHUNK] = value

    _attention_chunks(items, read_q, read_gate, write_out)


def _spectrum_operands(t):
    upper = [t[:, C_GROUP_DIM * g + HALF_GROUP:C_GROUP_DIM * (g + 1)] for g in range(C_GROUPS)]
    return t.astype(BF16), jnp.concatenate(upper, axis=1).astype(BF16)


def _spectrum_combine(m_cos, m_sin, read_gate, write_out):
    first = lax.broadcasted_iota(jnp.int32, (m_cos.shape[0], HALF_GROUP), 1) == 0
    for g in range(C_GROUPS):
        c0 = C_GROUP_DIM * g
        p = m_cos[:, c0:c0 + HALF_GROUP]
        p_mid = m_cos[:, c0 + HALF_GROUP:c0 + C_GROUP_DIM]
        q = jnp.where(first, 0.0, m_sin[:, HALF_GROUP * g:HALF_GROUP * (g + 1)])
        write_out(c0, ((p - q) * read_gate(c0)).astype(BF16))
        upper = jnp.where(first, p_mid, p + q)
        write_out(c0 + HALF_GROUP, (upper * read_gate(c0 + HALF_GROUP)).astype(BF16))


def _ctx_odd_mix(cs_ref, ss_ref, p_ref, o_ref):
    n_b = p_ref.shape[0]
    for bb in range(n_b):
        for_cos, for_sin = _spectrum_operands(p_ref[bb, :, 0:D_MODEL])
        m_cos = jnp.dot(cs_ref[...], for_cos, preferred_element_type=F32)
        m_sin = jnp.dot(ss_ref[...], for_sin, preferred_element_type=F32)

        def read_gate(c0, bb=bb):
            return _silu(p_ref[bb, :, D_MODEL + c0:D_MODEL + c0 + HALF_GROUP])

        def write_out(c0, value, bb=bb):
            o_ref[bb, :, c0:c0 + HALF_GROUP] = value

        _spectrum_combine(m_cos, m_sin, read_gate, write_out)


def _fold_odd_weights(owin_ref, owout_ref, epack_ref, perm_ref):
    for layer in range(owin_ref.shape[0]):
        for g in range(C_GROUPS):
            c0 = C_GROUP_DIM * g
            owin_ref[layer, :, c0:c0 + C_GROUP_DIM] = jnp.dot(
                owin_ref[layer, :, c0:c0 + C_GROUP_DIM], epack_ref[...],
                preferred_element_type=F32).astype(BF16)
            g0 = D_MODEL + c0 + HALF_GROUP
            owin_ref[layer, :, g0:g0 + HALF_GROUP] = jnp.dot(
                owin_ref[layer, :, g0:g0 + HALF_GROUP], perm_ref[...], preferred_element_type=F32).astype(BF16)
            r0 = c0 + HALF_GROUP
            owout_ref[layer, r0:r0 + HALF_GROUP, :] = jnp.dot(
                perm_ref[...], owout_ref[layer, r0:r0 + HALF_GROUP, :], preferred_element_type=F32).astype(BF16)


def _stage_copy(w_hbm, stage_ref, sem, chunk, slot):
    _, n_rows, cols = w_hbm.shape
    rows = stage_ref.shape[1]
    per_layer = n_rows // rows
    src = w_hbm.at[chunk // per_layer, pl.ds((chunk % per_layer) * rows, rows), :]
    return pltpu.make_async_copy(src, stage_ref.at[slot, :, pl.ds(0, cols)], sem.at[slot])


def _convert_weights(w_hbm, w_ref, stage_ref, in_sem):
    n_layers, n_rows, cols = w_hbm.shape
    n_slots, rows, _ = stage_ref.shape
    per_layer = n_rows // rows
    n_chunks = n_layers * per_layer
    for c in range(n_slots - 1):
        _stage_copy(w_hbm, stage_ref, in_sem, c, c).start()

    def body(c, carry):
        slot = c % n_slots
        ahead = c + n_slots - 1

        @pl.when(ahead < n_chunks)
        def _():
            _stage_copy(w_hbm, stage_ref, in_sem, ahead, ahead % n_slots).start()

        _stage_copy(w_hbm, stage_ref, in_sem, c, slot).wait()
        r0 = pl.multiple_of((c % per_layer) * rows, rows)
        w_ref[c // per_layer, pl.ds(r0, rows), :] = stage_ref[slot, :, 0:cols].astype(BF16)
        return carry

    lax.fori_loop(0, n_chunks, body, 0)


def _ctx_kernel(sink_ref, x_ref, mod_ref, g_ref, ewin_hbm, ewout_hbm, owin_hbm, owout_hbm, gq_ref, gk_ref,
                epack_ref, perm_ref, cs_ref, ss_ref, fg_ref,
                xo_ref, ka_ref, va_ref, kb_ref, vb_ref, ewin_o, ewout_o, owin_o, owout_o,
                xs_ref, p_ref, o_ref, ewin_ref, ewout_ref, owin_ref, owout_ref,
                stage_ref, in_sem, out_sem):
    weights = ((ewin_hbm, ewin_o, ewin_ref), (ewout_hbm, ewout_o, ewout_ref),
               (owin_hbm, owin_o, owin_ref), (owout_hbm, owout_o, owout_ref))
    exports = [pltpu.make_async_copy(res, dst, out_sem.at[k]) for k, (_, dst, res) in enumerate(weights)]

    @pl.when(pl.program_id(0) == 0)
    def _():
        for src, _, res in weights:
            _convert_weights(src, res, stage_ref, in_sem)
        _fold_odd_weights(owin_ref, owout_ref, epack_ref, perm_ref)
        for export in exports:
            export.start()

    @pl.when(pl.program_id(0) == pl.num_programs(0) - 1)
    def _():
        for export in exports:
            export.wait()

    n_b = x_ref.shape[0]
    for l in range(DEPTH):
        i = l // 2
        even = l % 2 == 0
        win_ref, wout_ref, width = (ewin_ref, ewout_ref, EVEN_IN) if even else (owin_ref, owout_ref, ODD_IN)
        shift, scale, gate = (mod_ref[l, part, 0:1, :] for part in range(3))
        for bb in range(n_b):
            x = x_ref[bb] if l == 0 else xs_ref[bb]
            h = _norm_mod(x, g_ref[l:l + 1, :], shift, scale)
            p_ref[bb, :, 0:width] = jnp.dot(h.astype(BF16), win_ref[i], preferred_element_type=F32)
        if even:
            _ctx_even_mix(i, sink_ref, gq_ref, gk_ref, (ka_ref, va_ref, kb_ref, vb_ref), p_ref, o_ref)
        else:
            _ctx_odd_mix(cs_ref, ss_ref, p_ref, o_ref)
        for bb in range(n_b):
            y = jnp.dot(o_ref[bb], wout_ref[i], preferred_element_type=F32)
            xn = (x_ref[bb] if l == 0 else xs_ref[bb]) + gate * y
            if l == DEPTH - 1:
                xo_ref[bb] = _rms_norm(xn, fg_ref[...])
            else:
                xs_ref[bb] = xn


def _resident(shape):
    return pl.BlockSpec(shape, lambda i: (0,) * len(shape), pipeline_mode=pl.Buffered(1))


def _ctx_path(x, mod, norm_g, ewin, ewout, owin, owout, gq2, gk2, sink, epack, perm, cs, ss, fg):
    b, s, _ = x.shape
    n_even = ewin.shape[0]
    bb = CTX_BATCH_PER_STEP
    kv_shape = jax.ShapeDtypeStruct((b, n_even, N_KV, HEAD_DIM, s), F32)
    kv_spec = pl.BlockSpec((bb, n_even, N_KV, HEAD_DIM, s), lambda i: (i, 0, 0, 0, 0))
    x_spec = pl.BlockSpec((bb, s, D_MODEL), lambda i: (i, 0, 0))
    hbm = pl.BlockSpec(memory_space=pl.ANY)
    weights = (ewin, ewout, owin, owout)
    return pl.pallas_call(
        _ctx_kernel,
        grid=(b // bb,),
        in_specs=[
            pl.BlockSpec(memory_space=pltpu.SMEM),
            x_spec,
            _resident(mod.shape), _resident(norm_g.shape),
            hbm, hbm, hbm, hbm,
            _resident(gq2.shape), _resident(gk2.shape),
            _resident(epack.shape), _resident(perm.shape), _resident(cs.shape), _resident(ss.shape),
            _resident(fg.shape),
        ],
        out_specs=[x_spec, kv_spec, kv_spec, kv_spec, kv_spec, hbm, hbm, hbm, hbm],
        out_shape=[jax.ShapeDtypeStruct(x.shape, F32), kv_shape, kv_shape, kv_shape, kv_shape]
                  + [jax.ShapeDtypeStruct(w.shape, BF16) for w in weights],
        scratch_shapes=[pltpu.VMEM((bb, s, D_MODEL), F32), pltpu.VMEM((bb, s, EVEN_IN), F32),
                        pltpu.VMEM((bb, s, D_MODEL), BF16)]
                       + [pltpu.VMEM(w.shape, BF16) for w in weights]
                       + [pltpu.VMEM((STAGE_SLOTS, STAGE_ROWS, max(w.shape[2] for w in weights)), F32),
                          pltpu.SemaphoreType.DMA((STAGE_SLOTS,)), pltpu.SemaphoreType.DMA((len(weights),))],
        compiler_params=pltpu.CompilerParams(
            dimension_semantics=("arbitrary",), vmem_limit_bytes=CTX_VMEM_LIMIT),
        name="ctx_path",
    )(sink, x, mod, norm_g, *weights, gq2, gk2, epack, perm, cs, ss, fg)


def _lat_weight_copies(l, w_hbms, win_buf, wout_buf, w_sem):
    ewin_hbm, ewout_hbm, owin_hbm, owout_hbm = w_hbms
    i, slot = l // 2, l % 2
    w_in, w_out = (ewin_hbm, ewout_hbm) if l % 2 == 0 else (owin_hbm, owout_hbm)
    return (pltpu.make_async_copy(w_in.at[i], win_buf.at[slot, :, pl.ds(0, w_in.shape[2])], w_sem.at[2 * slot]),
            pltpu.make_async_copy(w_out.at[i], wout_buf.at[slot], w_sem.at[2 * slot + 1]))


def _lat_attention_items(n, layer_i, sink_ref, kv_ref, kcat_ref, wt_ref, cache_refs):
    seq = kv_ref.shape[1]
    ckb_t, cvb_t = cache_refs[2][layer_i], cache_refs[3][layer_i]
    prev0 = pl.multiple_of(jnp.maximum(n * ROWS - WINDOW, 0), WINDOW)
    own0 = pl.multiple_of(n * ROWS, ROWS)
    next0 = pl.multiple_of(jnp.minimum(n * ROWS + ROWS, seq - WINDOW), WINDOW)
    win_len = ROWS + 2 * WINDOW
    ctx_len = ckb_t.shape[1]
    cj = lax.broadcasted_iota(jnp.int32, (win_len + ctx_len, ROWS), 0)
    qi = lax.broadcasted_iota(jnp.int32, (win_len + ctx_len, ROWS), 1)
    kpos = n * ROWS - WINDOW + cj
    in_win = (jnp.abs(cj - WINDOW - qi) <= WINDOW) & (kpos >= 0) & (kpos < seq)
    mask_b = in_win | (cj >= win_len)

    def window(idx):
        return jnp.concatenate([kv_ref[idx, pl.ds(prev0, WINDOW), :], kv_ref[idx, pl.ds(own0, ROWS), :],
                                kv_ref[idx, pl.ds(next0, WINDOW), :]], axis=0)

    k2b = jnp.concatenate([window(2), ckb_t.T], axis=0)
    v2bt = jnp.concatenate([window(3).T, cvb_t], axis=1)
    items = []
    for mixer, (q0, g0) in enumerate(((LQA, LGA), (LQB, LGB))):
        for j in range(N_KV):
            k_cat, w_t = (kcat_ref[j], wt_ref[j]) if mixer == 0 else _kv_operands(k2b, v2bt, j)
            for cc in range(GROUP // 2):
                c0 = CHUNK * (j * (GROUP // 2) + cc)
                if mixer == 0:
                    sinks, mask = None, None
                else:
                    sinks, mask = _sink_pair(sink_ref, layer_i, c0 // HEAD_DIM), mask_b
                items.append(dict(q=q0 + c0, g=g0 + c0, o=mixer * MIX_W + c0, sinks=sinks, mask=mask,
                                  k_cat=k_cat, w_t=w_t))
    return items


def _lat_kernel(sink_ref, x_hbm, mod_ref, g_ref, ewin_hbm, ewout_hbm, owin_hbm, owout_hbm, gq_ref, gk_ref,
                cos_ref, sn_ref, sp_ref, cka_ref, cva_ref, ckb_ref, cvb_ref, cs_ref, ss_ref, fg_ref,
                y_hbm,
                xw_ref, win_buf, wout_buf, kv_ref, kcat_ref, wt_ref, p_ref, o_ref, fcos_ref, fsin_ref, gate_ref,
                x_sem, w_sem):
    b = pl.program_id(0)
    seq = xw_ref.shape[0]
    n_blocks = seq // ROWS
    w_hbms = (ewin_hbm, ewout_hbm, owin_hbm, owout_hbm)

    x_in = pltpu.make_async_copy(x_hbm.at[b], xw_ref, x_sem.at[0])
    x_in.start()

    def start_weights(l):
        for copy in _lat_weight_copies(l, w_hbms, win_buf, wout_buf, w_sem):
            copy.start()

    pl.when(b == 0)(lambda: start_weights(0))
    x_in.wait()

    for l in range(DEPTH):
        i, slot = l // 2, l % 2
        for copy in _lat_weight_copies(l, w_hbms, win_buf, wout_buf, w_sem):
            copy.wait()
        if l + 1 < DEPTH:
            start_weights(l + 1)
        else:
            pl.when(b + 1 < pl.num_programs(0))(lambda: start_weights(0))
        shift, scale, gate = (mod_ref[l, part, pl.ds(1 + b, 1), :] for part in range(3))
        g = g_ref[l:l + 1, :]

        def normed(rows):
            return _norm_mod(xw_ref[rows, :], g, shift, scale).astype(BF16)

        def w_in(c0, width):
            return win_buf[slot, :, c0:c0 + width]

        if l % 2 == 0:
            gq2 = gq_ref[i:i + 1, :] * Q_SCALE
            gk2 = gk_ref[i:i + 1, :]

            def project_kv(n, carry):
                rows = pl.ds(pl.multiple_of(n * ROWS, ROWS), ROWS)
                h = normed(rows)
                cos, sn, sp = cos_ref[rows, :], sn_ref[rows, :], sp_ref[rows, :]
                for mixer, c0 in enumerate((KA, KB)):
                    kv = jnp.dot(h, w_in(c0, 2 * KV_W), preferred_element_type=F32)
                    k2 = kv[:, :KV_W]
                    if mixer == 0:
                        k2 = _rms_pair(k2, gk2)
                    kv_ref[2 * mixer, rows, :] = _rope(k2, cos, sn, sp)
                    kv_ref[2 * mixer + 1, rows, :] = kv[:, KV_W:]
                return carry

            lax.fori_loop(0, n_blocks, project_kv, 0)
            k2a = jnp.concatenate([kv_ref[0], cka_ref[i].T], axis=0)
            v2at = jnp.concatenate([kv_ref[1].T, cva_ref[i]], axis=1)
            for j in range(N_KV):
                kcat_ref[j], wt_ref[j] = _kv_operands(k2a, v2at, j)

            def attend(n, carry):
                rows = pl.ds(pl.multiple_of(n * ROWS, ROWS), ROWS)
                h = normed(rows)
                cos, sn, sp = cos_ref[rows, :], sn_ref[rows, :], sp_ref[rows, :]
                qa = jnp.dot(h, w_in(QA, MIX_W), preferred_element_type=F32)
                for c0 in range(0, MIX_W, CHUNK):
                    p_ref[:, LQA + c0:LQA + c0 + CHUNK] = _rope(_rms_pair(qa[:, c0:c0 + CHUNK], gq2), cos, sn, sp)
                mid = jnp.dot(h, w_in(GA, 2 * MIX_W), preferred_element_type=F32)
                p_ref[:, LGA:LGA + MIX_W] = _silu(mid[:, :MIX_W])
                for c0 in range(0, MIX_W, CHUNK):
                    qb = mid[:, MIX_W + c0:MIX_W + c0 + CHUNK] * Q_SCALE
                    p_ref[:, LQB + c0:LQB + c0 + CHUNK] = _rope(qb, cos, sn, sp)
                p_ref[:, LGB:LGB + MIX_W] = _silu(jnp.dot(h, w_in(GB, MIX_W), preferred_element_type=F32))
                items = _lat_attention_items(n, i, sink_ref, kv_ref, kcat_ref, wt_ref,
                                             (cka_ref, cva_ref, ckb_ref, cvb_ref))

                def write_out(it, value):
                    o_ref[:, it["o"]:it["o"] + CHUNK] = value

                _attention_chunks(items, lambda it: p_ref[:, it["q"]:it["q"] + CHUNK],
                                  lambda it: p_ref[:, it["g"]:it["g"] + CHUNK], write_out)
                y = jnp.dot(o_ref[...], wout_buf[slot], preferred_element_type=F32)
                xw_ref[rows, :] = xw_ref[rows, :] + gate * y
                return carry

            lax.fori_loop(0, n_blocks, attend, 0)
        else:
            def project(n, carry):
                rows = pl.ds(pl.multiple_of(n * ROWS, ROWS), ROWS)
                p = jnp.dot(normed(rows), w_in(0, ODD_IN), preferred_element_type=F32)
                fcos_ref[rows, :], fsin_ref[rows, :] = _spectrum_operands(p[:, 0:D_MODEL])
                gate_ref[rows, :] = _silu(p[:, D_MODEL:ODD_IN])
                return carry

            lax.fori_loop(0, n_blocks, project, 0)

            def mix(n, carry):
                rows = pl.ds(pl.multiple_of(n * ROWS, ROWS), ROWS)
                m_cos = jnp.dot(cs_ref[rows, :], fcos_ref[...], preferred_element_type=F32)
                m_sin = jnp.dot(ss_ref[rows, :], fsin_ref[...], preferred_element_type=F32)

                def write_out(c0, value):
                    o_ref[:, c0:c0 + HALF_GROUP] = value

                _spectrum_combine(m_cos, m_sin, lambda c0: gate_ref[rows, c0:c0 + HALF_GROUP], write_out)
                y = jnp.dot(o_ref[...], wout_buf[slot], preferred_element_type=F32)
                xn = xw_ref[rows, :] + gate * y
                if l == DEPTH - 1:
                    xn = _rms_norm(xn, fg_ref[...])
                xw_ref[rows, :] = xn
                return carry

            lax.fori_loop(0, n_blocks, mix, 0)

    y_out = pltpu.make_async_copy(xw_ref, y_hbm.at[b], x_sem.at[1])
    y_out.start()
    y_out.wait()


def _lat_path(x, mod, norm_g, ewin, ewout, owin, owout, gq2, gk2, sink, cos, sn, sp, caches, cs, ss, fg):
    b, s, _ = x.shape
    n_even, past = caches[0].shape[1], caches[0].shape[3]
    hbm = pl.BlockSpec(memory_space=pl.ANY)
    cache_spec = pl.BlockSpec((None, n_even, KV_W, past), lambda i: (i, 0, 0, 0))
    return pl.pallas_call(
        _lat_kernel,
        grid=(b,),
        in_specs=[
            pl.BlockSpec(memory_space=pltpu.SMEM),
            hbm,
            _resident(mod.shape), _resident(norm_g.shape),
            hbm, hbm, hbm, hbm,
            _resident(gq2.shape), _resident(gk2.shape),
            _resident(cos.shape), _resident(sn.shape), _resident(sp.shape),
            cache_spec, cache_spec, cache_spec, cache_spec,
            _resident(cs.shape), _resident(ss.shape), _resident(fg.shape),
        ],
        out_specs=hbm,
        out_shape=jax.ShapeDtypeStruct(x.shape, F32),
        scratch_shapes=[
            pltpu.VMEM((s, D_MODEL), F32),
            pltpu.VMEM((2, D_MODEL, EVEN_IN), BF16),
            pltpu.VMEM((2, D_MODEL, D_MODEL), BF16),
            pltpu.VMEM((4, s, KV_W), F32),
            pltpu.VMEM((N_KV, 2 * (s + past), KV_W), BF16),
            pltpu.VMEM((N_KV, 2 * HEAD_DIM + DEN_ROWS, 2 * (s + past)), BF16),
            pltpu.VMEM((ROWS, 4 * MIX_W), F32),
            pltpu.VMEM((ROWS, D_MODEL), BF16),
            pltpu.VMEM((s, D_MODEL), BF16),
            pltpu.VMEM((s, D_MODEL // 2), BF16),
            pltpu.VMEM((s, D_MODEL), F32),
            pltpu.SemaphoreType.DMA((2,)), pltpu.SemaphoreType.DMA((4,)),
        ],
        compiler_params=pltpu.CompilerParams(
            dimension_semantics=("arbitrary",), vmem_limit_bytes=LAT_VMEM_LIMIT),
        name="lat_path",
    )(sink, x, mod, norm_g, ewin, ewout, owin, owout, gq2, gk2, cos, sn, sp, *caches, cs, ss, fg)


def kernel(x_prompt, x_sample, cache_k_a, cache_v_a, cache_k_b, cache_v_b, c, c_ctx, norm_g, ada_w, ada_b,
           even_w_in, even_w_out, qk_g_q, qk_g_k, sink_logit, odd_w_in, odd_w_out, final_g):
    batch, seq, _ = x_prompt.shape
    dec_batch, dec_seq, _ = x_sample.shape
    n_even = even_w_in.shape[0]
    past = cache_k_a.shape[2]

    cond = jnp.concatenate(
        [c_ctx[None, :], c, jnp.zeros((COND_ROWS - 1 - dec_batch, D_MODEL), F32)], axis=0)
    mod = _ada_all(cond, ada_w, ada_b)

    epack = jnp.asarray(_packed_channel_dft()).astype(BF16)
    perm = jnp.asarray(_mirror_perm()).astype(BF16)
    cs_ctx, ss_ctx = (jnp.asarray(t).astype(BF16) for t in _position_dft(seq))
    cs_lat, ss_lat = (jnp.asarray(t).astype(BF16) for t in _position_dft(dec_seq))
    cos, sn, sp = (jnp.asarray(t) for t in _rope_tables(dec_seq))

    caches = [jnp.transpose(a, (0, 1, 3, 4, 2)).reshape(dec_batch, n_even, KV_W, past)
              for a in (cache_k_a, cache_v_a, cache_k_b, cache_v_b)]
    fg = final_g.reshape(1, D_MODEL)
    gq2 = jnp.tile(qk_g_q, (1, 2))
    gk2 = jnp.tile(qk_g_k, (1, 2))

    xc, *outs = _ctx_path(x_prompt, mod, norm_g, even_w_in, even_w_out, odd_w_in, odd_w_out, gq2, gk2,
                          sink_logit, epack, perm, cs_ctx, ss_ctx, fg)
    new_kv = [jnp.transpose(a, (0, 1, 4, 2, 3)) for a in outs[:4]]
    ewin, ewout, owin, owout = outs[4:]

    xl = _lat_path(x_sample, mod, norm_g, ewin, ewout, owin, owout, gq2, gk2, sink_logit, cos, sn, sp,
                   caches, cs_lat, ss_lat, fg)
    return (xc, xl, *new_kv)
```

```python
import numpy as np
import jax
import jax.numpy as jnp
from jax import lax
from jax.experimental import pallas as pl
from jax.experimental.pallas import tpu as pltpu

D_MODEL = 1024
DEPTH = 4
HEAD_DIM = 64
N_HEADS = 8
N_KV = 2
GROUP = N_HEADS // N_KV
MIX_W = N_HEADS * HEAD_DIM
KV_W = N_KV * HEAD_DIM
CHUNK = 2 * HEAD_DIM
EVEN_IN = 2 * (2 * MIX_W + 2 * KV_W)
ODD_IN = 2 * D_MODEL
GRID_W = 64
WINDOW = 128
ROPE_BASE = 10000.0
C_GROUPS = 4
C_GROUP_DIM = D_MODEL // C_GROUPS
EPS = 1e-6
NEG_BIG = -1e30
ROWS = 256
COND_ROWS = 8
VMEM_LIMIT = 48 * 1024 * 1024
CTX_VMEM_LIMIT = 56 * 1024 * 1024
LAT_VMEM_LIMIT = 56 * 1024 * 1024
STAGE_ROWS = 64
STAGE_SLOTS = 8
CTX_BATCH_PER_STEP = 2
SKEW_GAPS = (1, 0, 3)

QA, KA, VA, GA = 0, 512, 640, 768
QB, KB, VB, GB = 1280, 1792, 1920, 2048
LQA, LGA, LQB, LGB = 0, 512, 1024, 1536

F32 = jnp.float32
BF16 = jnp.bfloat16


def _dft_tables(n):
    k = np.arange(n, dtype=np.int64)
    ang = ((k[:, None] * k[None, :]) % n).astype(np.float64) * (2.0 * np.pi / n)
    return np.cos(ang).astype(np.float32), np.sin(ang).astype(np.float32)


HALF_GROUP = C_GROUP_DIM // 2


def _packed_channel_dft():
    c, s = _dft_tables(C_GROUP_DIM)
    return np.concatenate([c[:, :HALF_GROUP], c[:, HALF_GROUP:HALF_GROUP + 1], s[:, 1:HALF_GROUP]], axis=1)


def _position_dft(n):
    scale = np.float32(1.0 / np.sqrt(float(n * C_GROUP_DIM)))
    c, s = _dft_tables(n)
    return c * scale, s * scale


def _mirror_perm():
    t = np.array([0] + [HALF_GROUP - m for m in range(1, HALF_GROUP)])
    p = np.zeros((HALF_GROUP, HALF_GROUP), np.float32)
    p[t, np.arange(HALF_GROUP)] = 1.0
    return p


def _rope_tables(n_tok):
    rows = n_tok // GRID_W
    row = np.repeat(np.arange(rows), GRID_W).astype(np.float64)
    col = np.tile(np.arange(GRID_W), rows).astype(np.float64)
    half = HEAD_DIM // 2
    inv = 1.0 / (ROPE_BASE ** (np.arange(0, half, 2, dtype=np.float64) / half))
    ang_r = row[:, None] * inv
    ang_c = col[:, None] * inv
    zeros = np.zeros_like(ang_r)
    cos_h = np.concatenate([np.cos(ang_r), np.cos(ang_r), np.cos(ang_c), np.cos(ang_c)], axis=1)
    nxt_h = np.concatenate([-np.sin(ang_r), zeros, -np.sin(ang_c), zeros], axis=1)
    prv_h = np.concatenate([zeros, np.sin(ang_r), zeros, np.sin(ang_c)], axis=1)
    two = lambda t: np.concatenate([t, t], axis=1).astype(np.float32)
    return two(cos_h), two(nxt_h), two(prv_h)


def _silu(x):
    return x * jax.nn.sigmoid(x)


def _norm_mod(x, g, shift, scale):
    ms = jnp.mean(x * x, axis=-1, keepdims=True)
    return x * lax.rsqrt(ms + EPS) * (g * (1.0 + scale)) + shift


def _rms_norm(x, g):
    ms = jnp.mean(x * x, axis=-1, keepdims=True)
    return x * lax.rsqrt(ms + EPS) * g


def _rms_pair(xc, g2):
    lo = lax.broadcasted_iota(jnp.int32, xc.shape, 1) < HEAD_DIM
    ss = xc * xc
    s_lo = jnp.sum(jnp.where(lo, ss, 0.0), axis=-1, keepdims=True)
    s_hi = jnp.sum(jnp.where(lo, 0.0, ss), axis=-1, keepdims=True)
    ms = jnp.where(lo, s_lo, s_hi) * (1.0 / HEAD_DIM)
    return xc * lax.rsqrt(ms + EPS) * g2


def _rope(xc, cos, sin_next, sin_prev):
    quarter = HEAD_DIM // 4
    nxt = pltpu.roll(xc, CHUNK - quarter, 1)
    prv = pltpu.roll(xc, quarter, 1)
    return xc * cos + nxt * sin_next + prv * sin_prev


DEN_ROWS = 16
LOG2E = float(np.log2(np.e))
Q_SCALE = (HEAD_DIM ** -0.5) * LOG2E


def _sink_pair(sink_ref, i, head):
    return sink_ref[i, head] * LOG2E, sink_ref[i, head + 1] * LOG2E


def _kv_operands(k2, v2t, j):
    tk = k2.shape[0]
    low = lax.broadcasted_iota(jnp.int32, k2.shape, 1) < HEAD_DIM
    km = jnp.where(low if j == 0 else jnp.logical_not(low), k2, 0.0)
    kr = pltpu.roll(km, HEAD_DIM, 1)
    k_lo, k_hi = (km, kr) if j == 0 else (kr, km)
    k_cat = jnp.concatenate([k_lo, k_hi], axis=0).astype(BF16)
    vjt = v2t[HEAD_DIM * j:HEAD_DIM * (j + 1), :]
    zero = jnp.zeros_like(vjt)
    row = lax.broadcasted_iota(jnp.int32, (DEN_ROWS, 2 * tk), 0)
    col = lax.broadcasted_iota(jnp.int32, (DEN_ROWS, 2 * tk), 1)
    ones = jnp.where(((row == 0) & (col < tk)) | ((row == 1) & (col >= tk)), 1.0, 0.0)
    w_t = jnp.concatenate([jnp.concatenate([vjt, zero], axis=1),
                           jnp.concatenate([zero, vjt], axis=1), ones], axis=0).astype(BF16)
    return k_cat, w_t


def _scores_t(qc, k_cat):
    return lax.dot_general(k_cat, qc.astype(BF16), (((1,), (1,)), ((), ())), preferred_element_type=F32)


def _softmax_t(s_t, sinks=None, mask_t=None):
    tk = s_t.shape[0] // 2
    es, ms = [], []
    for hh in range(2):
        sh = s_t[hh * tk:(hh + 1) * tk]
        if mask_t is not None:
            sh = jnp.where(mask_t, sh, NEG_BIG)
        m = jnp.max(sh, axis=0, keepdims=True)
        if sinks is not None:
            m = jnp.maximum(m, sinks[hh])
        es.append(jnp.exp2(sh - m))
        ms.append(m)
    return jnp.concatenate(es, axis=0).astype(BF16), ms


def _values_t(e_t, w_t):
    return jnp.dot(w_t, e_t, preferred_element_type=F32)


def _normalise_t(nd, ms, sinks=None):
    dens = [nd[2 * HEAD_DIM + hh:2 * HEAD_DIM + hh + 1] for hh in range(2)]
    if sinks is not None:
        dens = [dens[hh] + jnp.exp2(sinks[hh] - ms[hh]) for hh in range(2)]
    inv = [1.0 / d for d in dens]
    o_t = jnp.concatenate([nd[0:HEAD_DIM] * inv[0], nd[HEAD_DIM:2 * HEAD_DIM] * inv[1]], axis=0)
    return o_t.T


def _run_skewed(items, stages):
    state = list(items)
    for t in range(len(items) + len(stages) - 1):
        for k, stage in enumerate(stages):
            if 0 <= t - k < len(items):
                state[t - k] = stage(state[t - k])


def _attention_chunks(items, read_q, read_gate, write_out):
    def scores(it):
        return dict(it, s_t=_scores_t(read_q(it), it["k_cat"]))

    def softmax(it):
        e_t, ms = _softmax_t(it["s_t"], it["sinks"], it["mask"])
        return dict(it, e_t=e_t, ms=ms, s_t=None)

    def values(it):
        return dict(it, nd=_values_t(it["e_t"], it["w_t"]), e_t=None)

    def finish(it):
        o = _normalise_t(it["nd"], it["ms"], it["sinks"])
        write_out(it, (o * read_gate(it)).astype(BF16))
        return None

    def hold(it):
        return it

    stages = [scores]
    for gap, stage in zip(SKEW_GAPS, (softmax, values, finish)):
        stages += [hold] * gap + [stage]
    _run_skewed(items, stages)


def _ada_kernel(cond_ref, w_ref, b_ref, o_ref):
    a = _silu(cond_ref[...]).astype(BF16)
    o_ref[...] = jnp.dot(a, w_ref[...].astype(BF16), preferred_element_type=F32) + b_ref[...]


def _ada_all(cond, ada_w, ada_b):
    return pl.pallas_call(
        _ada_kernel,
        grid=(DEPTH, 3),
        in_specs=[
            pl.BlockSpec((COND_ROWS, D_MODEL), lambda l, p: (0, 0)),
            pl.BlockSpec((None, D_MODEL, D_MODEL), lambda l, p: (l, 0, p)),
            pl.BlockSpec((None, None, 1, D_MODEL), lambda l, p: (l, p, 0, 0)),
        ],
        out_specs=pl.BlockSpec((None, None, COND_ROWS, D_MODEL), lambda l, p: (l, p, 0, 0)),
        out_shape=jax.ShapeDtypeStruct((DEPTH, 3, COND_ROWS, D_MODEL), F32),
        compiler_params=pltpu.CompilerParams(
            dimension_semantics=("parallel", "parallel"), vmem_limit_bytes=VMEM_LIMIT),
        name="ada_mod",
    )(cond, ada_w, ada_b.reshape(DEPTH, 3, 1, D_MODEL))


def _ctx_even_mix(i, sink_ref, gq_ref, gk_ref, kv_refs, p_ref, o_ref):
    n_b, seq, _ = p_ref.shape
    gq2 = gq_ref[i:i + 1, :] * Q_SCALE
    gk2 = gk_ref[i:i + 1, :]
    items = []
    for mixer, (q0, k0, v0, g0) in enumerate(((QA, KA, VA, GA), (QB, KB, VB, GB))):
        kv = []
        for bb in range(n_b):
            k2 = p_ref[bb, :, k0:k0 + KV_W]
            v2 = p_ref[bb, :, v0:v0 + KV_W]
            if mixer == 0:
                k2 = _rms_pair(k2, gk2)
            v2t = v2.T
            kv_refs[2 * mixer][bb, i] = k2.T.reshape(N_KV, HEAD_DIM, seq)
            kv_refs[2 * mixer + 1][bb, i] = v2t.reshape(N_KV, HEAD_DIM, seq)
            kv.append((k2, v2t))
        for j in range(N_KV):
            ops = [_kv_operands(k2, v2t, j) for k2, v2t in kv]
            for cc in range(GROUP // 2):
                c0 = CHUNK * (j * (GROUP // 2) + cc)
                sinks = None if mixer == 0 else _sink_pair(sink_ref, i, c0 // HEAD_DIM)
                for bb, (k_cat, w_t) in enumerate(ops):
                    items.append(dict(rows=bb, q=q0 + c0, g=g0 + c0, o=mixer * MIX_W + c0,
                                      norm=mixer == 0, sinks=sinks, mask=None, k_cat=k_cat, w_t=w_t))

    def read_q(it):
        qc = p_ref[it["rows"], :, it["q"]:it["q"] + CHUNK]
        return _rms_pair(qc, gq2) if it["norm"] else qc * Q_SCALE

    def read_gate(it):
        return _silu(p_ref[it["rows"], :, it["g"]:it["g"] + CHUNK])

    def write_out(it, value):
        o_ref[it["rows"], :, it["o"]:it["o"] + CHUNK] = value

    _attention_chunks(items, read_q, read_gate, write_out)


def _spectrum_operands(t):
    upper = [t[:, C_GROUP_DIM * g + HALF_GROUP:C_GROUP_DIM * (g + 1)] for g in range(C_GROUPS)]
    return t.astype(BF16), jnp.concatenate(upper, axis=1).astype(BF16)


def _spectrum_combine(m_cos, m_sin, read_gate, write_out):
    first = lax.broadcasted_iota(jnp.int32, (m_cos.shape[0], HALF_GROUP), 1) == 0
    for g in range(C_GROUPS):
        c0 = C_GROUP_DIM * g
        p = m_cos[:, c0:c0 + HALF_GROUP]
        p_mid = m_cos[:, c0 + HALF_GROUP:c0 + C_GROUP_DIM]
        q = jnp.where(first, 0.0, m_sin[:, HALF_GROUP * g:HALF_GROUP * (g + 1)])
        write_out(c0, ((p - q) * read_gate(c0)).astype(BF16))
        upper = jnp.where(first, p_mid, p + q)
        write_out(c0 + HALF_GROUP, (upper * read_gate(c0 + HALF_GROUP)).astype(BF16))


def _ctx_odd_mix(cs_ref, ss_ref, p_ref, o_ref):
    n_b = p_ref.shape[0]
    for bb in range(n_b):
        for_cos, for_sin = _spectrum_operands(p_ref[bb, :, 0:D_MODEL])
        m_cos = jnp.dot(cs_ref[...], for_cos, preferred_element_type=F32)
        m_sin = jnp.dot(ss_ref[...], for_sin, preferred_element_type=F32)

        def read_gate(c0, bb=bb):
            return _silu(p_ref[bb, :, D_MODEL + c0:D_MODEL + c0 + HALF_GROUP])

        def write_out(c0, value, bb=bb):
            o_ref[bb, :, c0:c0 + HALF_GROUP] = value

        _spectrum_combine(m_cos, m_sin, read_gate, write_out)


def _fold_odd_weights(owin_ref, owout_ref, epack_ref, perm_ref):
    for layer in range(owin_ref.shape[0]):
        for g in range(C_GROUPS):
            c0 = C_GROUP_DIM * g
            owin_ref[layer, :, c0:c0 + C_GROUP_DIM] = jnp.dot(
                owin_ref[layer, :, c0:c0 + C_GROUP_DIM], epack_ref[...],
                preferred_element_type=F32).astype(BF16)
            g0 = D_MODEL + c0 + HALF_GROUP
            owin_ref[layer, :, g0:g0 + HALF_GROUP] = jnp.dot(
                owin_ref[layer, :, g0:g0 + HALF_GROUP], perm_ref[...], preferred_element_type=F32).astype(BF16)
            r0 = c0 + HALF_GROUP
            owout_ref[layer, r0:r0 + HALF_GROUP, :] = jnp.dot(
                perm_ref[...], owout_ref[layer, r0:r0 + HALF_GROUP, :], preferred_element_type=F32).astype(BF16)


def _stage_copy(w_hbm, stage_ref, sem, chunk, slot):
    _, n_rows, cols = w_hbm.shape
    rows = stage_ref.shape[1]
    per_layer = n_rows // rows
    src = w_hbm.at[chunk // per_layer, pl.ds((chunk % per_layer) * rows, rows), :]
    return pltpu.make_async_copy(src, stage_ref.at[slot, :, pl.ds(0, cols)], sem.at[slot])


def _convert_weights(w_hbm, w_ref, stage_ref, in_sem):
    n_layers, n_rows, cols = w_hbm.shape
    n_slots, rows, _ = stage_ref.shape
    per_layer = n_rows // rows
    n_chunks = n_layers * per_layer
    for c in range(n_slots - 1):
        _stage_copy(w_hbm, stage_ref, in_sem, c, c).start()

    def body(c, carry):
        slot = c % n_slots
        ahead = c + n_slots - 1

        @pl.when(ahead < n_chunks)
        def _():
            _stage_copy(w_hbm, stage_ref, in_sem, ahead, ahead % n_slots).start()

        _stage_copy(w_hbm, stage_ref, in_sem, c, slot).wait()
        r0 = pl.multiple_of((c % per_layer) * rows, rows)
        w_ref[c // per_layer, pl.ds(r0, rows), :] = stage_ref[slot, :, 0:cols].astype(BF16)
        return carry

    lax.fori_loop(0, n_chunks, body, 0)


def _ctx_kernel(sink_ref, x_ref, mod_ref, g_ref, ewin_hbm, ewout_hbm, owin_hbm, owout_hbm, gq_ref, gk_ref,
                epack_ref, perm_ref, cs_ref, ss_ref, fg_ref,
                xo_ref, ka_ref, va_ref, kb_ref, vb_ref, ewin_o, ewout_o, owin_o, owout_o,
                xs_ref, p_ref, o_ref, ewin_ref, ewout_ref, owin_ref, owout_ref,
                stage_ref, in_sem, out_sem):
    weights = ((ewin_hbm, ewin_o, ewin_ref), (ewout_hbm, ewout_o, ewout_ref),
               (owin_hbm, owin_o, owin_ref), (owout_hbm, owout_o, owout_ref))
    exports = [pltpu.make_async_copy(res, dst, out_sem.at[k]) for k, (_, dst, res) in enumerate(weights)]

    @pl.when(pl.program_id(0) == 0)
    def _():
        for src, _, res in weights:
            _convert_weights(src, res, stage_ref, in_sem)
        _fold_odd_weights(owin_ref, owout_ref, epack_ref, perm_ref)
        for export in exports:
            export.start()

    @pl.when(pl.program_id(0) == pl.num_programs(0) - 1)
    def _():
        for export in exports:
            export.wait()

    n_b = x_ref.shape[0]
    for l in range(DEPTH):
        i = l // 2
        even = l % 2 == 0
        win_ref, wout_ref, width = (ewin_ref, ewout_ref, EVEN_IN) if even else (owin_ref, owout_ref, ODD_IN)
        shift, scale, gate = (mod_ref[l, part, 0:1, :] for part in range(3))
        for bb in range(n_b):
            x = x_ref[bb] if l == 0 else xs_ref[bb]
            h = _norm_mod(x, g_ref[l:l + 1, :], shift, scale)
            p_ref[bb, :, 0:width] = jnp.dot(h.astype(BF16), win_ref[i], preferred_element_type=F32)
        if even:
            _ctx_even_mix(i, sink_ref, gq_ref, gk_ref, (ka_ref, va_ref, kb_ref, vb_ref), p_ref, o_ref)
        else:
            _ctx_odd_mix(cs_ref, ss_ref, p_ref, o_ref)
        for bb in range(n_b):
            y = jnp.dot(o_ref[bb], wout_ref[i], preferred_element_type=F32)
            xn = (x_ref[bb] if l == 0 else xs_ref[bb]) + gate * y
            if l == DEPTH - 1:
                xo_ref[bb] = _rms_norm(xn, fg_ref[...])
            else:
                xs_ref[bb] = xn


def _resident(shape):
    return pl.BlockSpec(shape, lambda i: (0,) * len(shape), pipeline_mode=pl.Buffered(1))


def _ctx_path(x, mod, norm_g, ewin, ewout, owin, owout, gq2, gk2, sink, epack, perm, cs, ss, fg):
    b, s, _ = x.shape
    n_even = ewin.shape[0]
    bb = CTX_BATCH_PER_STEP
    kv_shape = jax.ShapeDtypeStruct((b, n_even, N_KV, HEAD_DIM, s), F32)
    kv_spec = pl.BlockSpec((bb, n_even, N_KV, HEAD_DIM, s), lambda i: (i, 0, 0, 0, 0))
    x_spec = pl.BlockSpec((bb, s, D_MODEL), lambda i: (i, 0, 0))
    hbm = pl.BlockSpec(memory_space=pl.ANY)
    weights = (ewin, ewout, owin, owout)
    return pl.pallas_call(
        _ctx_kernel,
        grid=(b // bb,),
        in_specs=[
            pl.BlockSpec(memory_space=pltpu.SMEM),
            x_spec,
            _resident(mod.shape), _resident(norm_g.shape),
            hbm, hbm, hbm, hbm,
            _resident(gq2.shape), _resident(gk2.shape),
            _resident(epack.shape), _resident(perm.shape), _resident(cs.shape), _resident(ss.shape),
            _resident(fg.shape),
        ],
        out_specs=[x_spec, kv_spec, kv_spec, kv_spec, kv_spec, hbm, hbm, hbm, hbm],
        out_shape=[jax.ShapeDtypeStruct(x.shape, F32), kv_shape, kv_shape, kv_shape, kv_shape]
                  + [jax.ShapeDtypeStruct(w.shape, BF16) for w in weights],
        scratch_shapes=[pltpu.VMEM((bb, s, D_MODEL), F32), pltpu.VMEM((bb, s, EVEN_IN), F32),
                        pltpu.VMEM((bb, s, D_MODEL), BF16)]
                       + [pltpu.VMEM(w.shape, BF16) for w in weights]
                       + [pltpu.VMEM((STAGE_SLOTS, STAGE_ROWS, max(w.shape[2] for w in weights)), F32),
                          pltpu.SemaphoreType.DMA((STAGE_SLOTS,)), pltpu.SemaphoreType.DMA((len(weights),))],
        compiler_params=pltpu.CompilerParams(
            dimension_semantics=("arbitrary",), vmem_limit_bytes=CTX_VMEM_LIMIT),
        name="ctx_path",
    )(sink, x, mod, norm_g, *weights, gq2, gk2, epack, perm, cs, ss, fg)


def _lat_weight_copies(l, w_hbms, win_buf, wout_buf, w_sem):
    ewin_hbm, ewout_hbm, owin_hbm, owout_hbm = w_hbms
    i, slot = l // 2, l % 2
    w_in, w_out = (ewin_hbm, ewout_hbm) if l % 2 == 0 else (owin_hbm, owout_hbm)
    return (pltpu.make_async_copy(w_in.at[i], win_buf.at[slot, :, pl.ds(0, w_in.shape[2])], w_sem.at[2 * slot]),
            pltpu.make_async_copy(w_out.at[i], wout_buf.at[slot], w_sem.at[2 * slot + 1]))


def _lat_attention_items(n, layer_i, sink_ref, kv_ref, kcat_ref, wt_ref, cache_refs):
    seq = kv_ref.shape[1]
    ckb_t, cvb_t = cache_refs[2][layer_i], cache_refs[3][layer_i]
    prev0 = pl.multiple_of(jnp.maximum(n * ROWS - WINDOW, 0), WINDOW)
    own0 = pl.multiple_of(n * ROWS, ROWS)
    next0 = pl.multiple_of(jnp.minimum(n * ROWS + ROWS, seq - WINDOW), WINDOW)
    win_len = ROWS + 2 * WINDOW
    ctx_len = ckb_t.shape[1]
    cj = lax.broadcasted_iota(jnp.int32, (win_len + ctx_len, ROWS), 0)
    qi = lax.broadcasted_iota(jnp.int32, (win_len + ctx_len, ROWS), 1)
    kpos = n * ROWS - WINDOW + cj
    in_win = (jnp.abs(cj - WINDOW - qi) <= WINDOW) & (kpos >= 0) & (kpos < seq)
    mask_b = in_win | (cj >= win_len)

    def window(idx):
        return jnp.concatenate([kv_ref[idx, pl.ds(prev0, WINDOW), :], kv_ref[idx, pl.ds(own0, ROWS), :],
                                kv_ref[idx, pl.ds(next0, WINDOW), :]], axis=0)

    k2b = jnp.concatenate([window(2), ckb_t.T], axis=0)
    v2bt = jnp.concatenate([window(3).T, cvb_t], axis=1)
    items = []
    for mixer, (q0, g0) in enumerate(((LQA, LGA), (LQB, LGB))):
        for j in range(N_KV):
            k_cat, w_t = (kcat_ref[j], wt_ref[j]) if mixer == 0 else _kv_operands(k2b, v2bt, j)
            for cc in range(GROUP // 2):
                c0 = CHUNK * (j * (GROUP // 2) + cc)
                if mixer == 0:
                    sinks, mask = None, None
                else:
                    sinks, mask = _sink_pair(sink_ref, layer_i, c0 // HEAD_DIM), mask_b
                items.append(dict(q=q0 + c0, g=g0 + c0, o=mixer * MIX_W + c0, sinks=sinks, mask=mask,
                                  k_cat=k_cat, w_t=w_t))
    return items


def _lat_kernel(sink_ref, x_hbm, mod_ref, g_ref, ewin_hbm, ewout_hbm, owin_hbm, owout_hbm, gq_ref, gk_ref,
                cos_ref, sn_ref, sp_ref, cka_ref, cva_ref, ckb_ref, cvb_ref, cs_ref, ss_ref, fg_ref,
                y_hbm,
                xw_ref, win_buf, wout_buf, kv_ref, kcat_ref, wt_ref, p_ref, o_ref, fcos_ref, fsin_ref, gate_ref,
                x_sem, w_sem):
    b = pl.program_id(0)
    seq = xw_ref.shape[0]
    n_blocks = seq // ROWS
    w_hbms = (ewin_hbm, ewout_hbm, owin_hbm, owout_hbm)

    x_in = pltpu.make_async_copy(x_hbm.at[b], xw_ref, x_sem.at[0])
    x_in.start()

    def start_weights(l):
        for copy in _lat_weight_copies(l, w_hbms, win_buf, wout_buf, w_sem):
            copy.start()

    pl.when(b == 0)(lambda: start_weights(0))
    x_in.wait()

    for l in range(DEPTH):
        i, slot = l // 2, l % 2
        for copy in _lat_weight_copies(l, w_hbms, win_buf, wout_buf, w_sem):
            copy.wait()
        if l + 1 < DEPTH:
            start_weights(l + 1)
        else:
            pl.when(b + 1 < pl.num_programs(0))(lambda: start_weights(0))
        shift, scale, gate = (mod_ref[l, part, pl.ds(1 + b, 1), :] for part in range(3))
        g = g_ref[l:l + 1, :]

        def normed(rows):
            return _norm_mod(xw_ref[rows, :], g, shift, scale).astype(BF16)

        def w_in(c0, width):
            return win_buf[slot, :, c0:c0 + width]

        if l % 2 == 0:
            gq2 = gq_ref[i:i + 1, :] * Q_SCALE
            gk2 = gk_ref[i:i + 1, :]

            def project_kv(n, carry):
                rows = pl.ds(pl.multiple_of(n * ROWS, ROWS), ROWS)
                h = normed(rows)
                cos, sn, sp = cos_ref[rows, :], sn_ref[rows, :], sp_ref[rows, :]
                for mixer, c0 in enumerate((KA, KB)):
                    kv = jnp.dot(h, w_in(c0, 2 * KV_W), preferred_element_type=F32)
                    k2 = kv[:, :KV_W]
                    if mixer == 0:
                        k2 = _rms_pair(k2, gk2)
                    kv_ref[2 * mixer, rows, :] = _rope(k2, cos, sn, sp)
                    kv_ref[2 * mixer + 1, rows, :] = kv[:, KV_W:]
                return carry

            lax.fori_loop(0, n_blocks, project_kv, 0)
            k2a = jnp.concatenate([kv_ref[0], cka_ref[i].T], axis=0)
            v2at = jnp.concatenate([kv_ref[1].T, cva_ref[i]], axis=1)
            for j in range(N_KV):
                kcat_ref[j], wt_ref[j] = _kv_operands(k2a, v2at, j)

            def attend(n, carry):
                rows = pl.ds(pl.multiple_of(n * ROWS, ROWS), ROWS)
                h = normed(rows)
                cos, sn, sp = cos_ref[rows, :], sn_ref[rows, :], sp_ref[rows, :]
                qa = jnp.dot(h, w_in(QA, MIX_W), preferred_element_type=F32)
                for c0 in range(0, MIX_W, CHUNK):
                    p_ref[:, LQA + c0:LQA + c0 + CHUNK] = _rope(_rms_pair(qa[:, c0:c0 + CHUNK], gq2), cos, sn, sp)
                mid = jnp.dot(h, w_in(GA, 2 * MIX_W), preferred_element_type=F32)
                p_ref[:, LGA:LGA + MIX_W] = _silu(mid[:, :MIX_W])
                for c0 in range(0, MIX_W, CHUNK):
                    qb = mid[:, MIX_W + c0:MIX_W + c0 + CHUNK] * Q_SCALE
                    p_ref[:, LQB + c0:LQB + c0 + CHUNK] = _rope(qb, cos, sn, sp)
                p_ref[:, LGB:LGB + MIX_W] = _silu(jnp.dot(h, w_in(GB, MIX_W), preferred_element_type=F32))
                items = _lat_attention_items(n, i, sink_ref, kv_ref, kcat_ref, wt_ref,
                                             (cka_ref, cva_ref, ckb_ref, cvb_ref))

                def write_out(it, value):
                    o_ref[:, it["o"]:it["o"] + CHUNK] = value

                _attention_chunks(items, lambda it: p_ref[:, it["q"]:it["q"] + CHUNK],
                                  lambda it: p_ref[:, it["g"]:it["g"] + CHUNK], write_out)
                y = jnp.dot(o_ref[...], wout_buf[slot], preferred_element_type=F32)
                xw_ref[rows, :] = xw_ref[rows, :] + gate * y
                return carry

            lax.fori_loop(0, n_blocks, attend, 0)
        else:
            def project(n, carry):
                rows = pl.ds(pl.multiple_of(n * ROWS, ROWS), ROWS)
                p = jnp.dot(normed(rows), w_in(0, ODD_IN), preferred_element_type=F32)
                fcos_ref[rows, :], fsin_ref[rows, :] = _spectrum_operands(p[:, 0:D_MODEL])
                gate_ref[rows, :] = _silu(p[:, D_MODEL:ODD_IN])
                return carry

            lax.fori_loop(0, n_blocks, project, 0)

            def mix(n, carry):
                rows = pl.ds(pl.multiple_of(n * ROWS, ROWS), ROWS)
                m_cos = jnp.dot(cs_ref[rows, :], fcos_ref[...], preferred_element_type=F32)
                m_sin = jnp.dot(ss_ref[rows, :], fsin_ref[...], preferred_element_type=F32)

                def write_out(c0, value):
                    o_ref[:, c0:c0 + HALF_GROUP] = value

                _spectrum_combine(m_cos, m_sin, lambda c0: gate_ref[rows, c0:c0 + HALF_GROUP], write_out)
                y = jnp.dot(o_ref[...], wout_buf[slot], preferred_element_type=F32)
                xn = xw_ref[rows, :] + gate * y
                if l == DEPTH - 1:
                    xn = _rms_norm(xn, fg_ref[...])
                xw_ref[rows, :] = xn
                return carry

            lax.fori_loop(0, n_blocks, mix, 0)

    y_out = pltpu.make_async_copy(xw_ref, y_hbm.at[b], x_sem.at[1])
    y_out.start()
    y_out.wait()


def _lat_path(x, mod, norm_g, ewin, ewout, owin, owout, gq2, gk2, sink, cos, sn, sp, caches, cs, ss, fg):
    b, s, _ = x.shape
    n_even, past = caches[0].shape[1], caches[0].shape[3]
    hbm = pl.BlockSpec(memory_space=pl.ANY)
    cache_spec = pl.BlockSpec((None, n_even, KV_W, past), lambda i: (i, 0, 0, 0))
    return pl.pallas_call(
        _lat_kernel,
        grid=(b,),
        in_specs=[
            pl.BlockSpec(memory_space=pltpu.SMEM),
            hbm,
            _resident(mod.shape), _resident(norm_g.shape),
            hbm, hbm, hbm, hbm,
            _resident(gq2.shape), _resident(gk2.shape),
            _resident(cos.shape), _resident(sn.shape), _resident(sp.shape),
            cache_spec, cache_spec, cache_spec, cache_spec,
            _resident(cs.shape), _resident(ss.shape), _resident(fg.shape),
        ],
        out_specs=hbm,
        out_shape=jax.ShapeDtypeStruct(x.shape, F32),
        scratch_shapes=[
            pltpu.VMEM((s, D_MODEL), F32),
            pltpu.VMEM((2, D_MODEL, EVEN_IN), BF16),
            pltpu.VMEM((2, D_MODEL, D_MODEL), BF16),
            pltpu.VMEM((4, s, KV_W), F32),
            pltpu.VMEM((N_KV, 2 * (s + past), KV_W), BF16),
            pltpu.VMEM((N_KV, 2 * HEAD_DIM + DEN_ROWS, 2 * (s + past)), BF16),
            pltpu.VMEM((ROWS, 4 * MIX_W), F32),
            pltpu.VMEM((ROWS, D_MODEL), BF16),
            pltpu.VMEM((s, D_MODEL), BF16),
            pltpu.VMEM((s, D_MODEL // 2), BF16),
            pltpu.VMEM((s, D_MODEL), F32),
            pltpu.SemaphoreType.DMA((2,)), pltpu.SemaphoreType.DMA((4,)),
        ],
        compiler_params=pltpu.CompilerParams(
            dimension_semantics=("arbitrary",), vmem_limit_bytes=LAT_VMEM_LIMIT),
        name="lat_path",
    )(sink, x, mod, norm_g, ewin, ewout, owin, owout, gq2, gk2, cos, sn, sp, *caches, cs, ss, fg)


def kernel(x_prompt, x_sample, cache_k_a, cache_v_a, cache_k_b, cache_v_b, c, c_ctx, norm_g, ada_w, ada_b,
           even_w_in, even_w_out, qk_g_q, qk_g_k, sink_logit, odd_w_in, odd_w_out, final_g):
    batch, seq, _ = x_prompt.shape
    dec_batch, dec_seq, _ = x_sample.shape
    n_even = even_w_in.shape[0]
    past = cache_k_a.shape[2]

    cond = jnp.concatenate(
        [c_ctx[None, :], c, jnp.zeros((COND_ROWS - 1 - dec_batch, D_MODEL), F32)], axis=0)
    mod = _ada_all(cond, ada_w, ada_b)

    epack = jnp.asarray(_packed_channel_dft()).astype(BF16)
    perm = jnp.asarray(_mirror_perm()).astype(BF16)
    cs_ctx, ss_ctx = (jnp.asarray(t).astype(BF16) for t in _position_dft(seq))
    cs_lat, ss_lat = (jnp.asarray(t).astype(BF16) for t in _position_dft(dec_seq))
    cos, sn, sp = (jnp.asarray(t) for t in _rope_tables(dec_seq))

    caches = [jnp.transpose(a, (0, 1, 3, 4, 2)).reshape(dec_batch, n_even, KV_W, past)
              for a in (cache_k_a, cache_v_a, cache_k_b, cache_v_b)]
    fg = final_g.reshape(1, D_MODEL)
    gq2 = jnp.tile(qk_g_q, (1, 2))
    gk2 = jnp.tile(qk_g_k, (1, 2))

    xc, *outs = _ctx_path(x_prompt, mod, norm_g, even_w_in, even_w_out, odd_w_in, odd_w_out, gq2, gk2,
                          sink_logit, epack, perm, cs_ctx, ss_ctx, fg)
    new_kv = [jnp.transpose(a, (0, 1, 4, 2, 3)) for a in outs[:4]]
    ewin, ewout, owin, owout = outs[4:]

    xl = _lat_path(x_sample, mod, norm_g, ewin, ewout, owin, owout, gq2, gk2, sink_logit, cos, sn, sp,
                   caches, cs_lat, ss_lat, fg)
    return (xc, xl, *new_kv)
```

```python
import numpy as np
import jax
import jax.numpy as jnp
from jax import lax
from jax.experimental import pallas as pl
from jax.experimental.pallas import tpu as pltpu

D_MODEL = 1024
DEPTH = 4
HEAD_DIM = 64
N_HEADS = 8
N_KV = 2
GROUP = N_HEADS // N_KV
MIX_W = N_HEADS * HEAD_DIM
KV_W = N_KV * HEAD_DIM
CHUNK = 2 * HEAD_DIM
EVEN_IN = 2 * (2 * MIX_W + 2 * KV_W)
ODD_IN = 2 * D_MODEL
GRID_W = 64
WINDOW = 128
ROPE_BASE = 10000.0
C_GROUPS = 4
C_GROUP_DIM = D_MODEL // C_GROUPS
EPS = 1e-6
NEG_BIG = -1e30
ROWS = 256
COND_ROWS = 8
VMEM_LIMIT = 48 * 1024 * 1024
CTX_VMEM_LIMIT = 56 * 1024 * 1024
LAT_VMEM_LIMIT = 56 * 1024 * 1024
STAGE_ROWS = 64
STAGE_SLOTS = 8
CTX_BATCH_PER_STEP = 2
SKEW_GAPS = (0, 0, 2)

QA, KA, VA, GA = 0, 512, 640, 768
QB, KB, VB, GB = 1280, 1792, 1920, 2048
LQA, LGA, LQB, LGB = 0, 512, 1024, 1536

F32 = jnp.float32
BF16 = jnp.bfloat16


def _dft_tables(n):
    k = np.arange(n, dtype=np.int64)
    ang = ((k[:, None] * k[None, :]) % n).astype(np.float64) * (2.0 * np.pi / n)
    return np.cos(ang).astype(np.float32), np.sin(ang).astype(np.float32)


HALF_GROUP = C_GROUP_DIM // 2


def _packed_channel_dft():
    c, s = _dft_tables(C_GROUP_DIM)
    return np.concatenate([c[:, :HALF_GROUP], c[:, HALF_GROUP:HALF_GROUP + 1], s[:, 1:HALF_GROUP]], axis=1)


def _position_dft(n):
    scale = np.float32(1.0 / np.sqrt(float(n * C_GROUP_DIM)))
    c, s = _dft_tables(n)
    return c * scale, s * scale


def _mirror_perm():
    t = np.array([0] + [HALF_GROUP - m for m in range(1, HALF_GROUP)])
    p = np.zeros((HALF_GROUP, HALF_GROUP), np.float32)
    p[t, np.arange(HALF_GROUP)] = 1.0
    return p


def _rope_tables(n_tok):
    rows = n_tok // GRID_W
    row = np.repeat(np.arange(rows), GRID_W).astype(np.float64)
    col = np.tile(np.arange(GRID_W), rows).astype(np.float64)
    half = HEAD_DIM // 2
    inv = 1.0 / (ROPE_BASE ** (np.arange(0, half, 2, dtype=np.float64) / half))
    ang_r = row[:, None] * inv
    ang_c = col[:, None] * inv
    zeros = np.zeros_like(ang_r)
    cos_h = np.concatenate([np.cos(ang_r), np.cos(ang_r), np.cos(ang_c), np.cos(ang_c)], axis=1)
    nxt_h = np.concatenate([-np.sin(ang_r), zeros, -np.sin(ang_c), zeros], axis=1)
    prv_h = np.concatenate([zeros, np.sin(ang_r), zeros, np.sin(ang_c)], axis=1)
    two = lambda t: np.concatenate([t, t], axis=1).astype(np.float32)
    return two(cos_h), two(nxt_h), two(prv_h)


def _silu(x):
    return x * jax.nn.sigmoid(x)


def _norm_mod(x, g, shift, scale):
    ms = jnp.mean(x * x, axis=-1, keepdims=True)
    return x * lax.rsqrt(ms + EPS) * (g * (1.0 + scale)) + shift


def _rms_norm(x, g):
    ms = jnp.mean(x * x, axis=-1, keepdims=True)
    return x * lax.rsqrt(ms + EPS) * g


def _rms_pair(xc, g2):
    lo = lax.broadcasted_iota(jnp.int32, xc.shape, 1) < HEAD_DIM
    ss = xc * xc
    s_lo = jnp.sum(jnp.where(lo, ss, 0.0), axis=-1, keepdims=True)
    s_hi = jnp.sum(jnp.where(lo, 0.0, ss), axis=-1, keepdims=True)
    ms = jnp.where(lo, s_lo, s_hi) * (1.0 / HEAD_DIM)
    return xc * lax.rsqrt(ms + EPS) * g2


def _rope(xc, cos, sin_next, sin_prev):
    quarter = HEAD_DIM // 4
    nxt = pltpu.roll(xc, CHUNK - quarter, 1)
    prv = pltpu.roll(xc, quarter, 1)
    return xc * cos + nxt * sin_next + prv * sin_prev


DEN_ROWS = 16
LOG2E = float(np.log2(np.e))
Q_SCALE = (HEAD_DIM ** -0.5) * LOG2E


def _sink_pair(sink_ref, i, head):
    return sink_ref[i, head] * LOG2E, sink_ref[i, head + 1] * LOG2E


def _kv_operands(k2, v2t, j):
    tk = k2.shape[0]
    low = lax.broadcasted_iota(jnp.int32, k2.shape, 1) < HEAD_DIM
    km = jnp.where(low if j == 0 else jnp.logical_not(low), k2, 0.0)
    kr = pltpu.roll(km, HEAD_DIM, 1)
    k_lo, k_hi = (km, kr) if j == 0 else (kr, km)
    k_cat = jnp.concatenate([k_lo, k_hi], axis=0).astype(BF16)
    vjt = v2t[HEAD_DIM * j:HEAD_DIM * (j + 1), :]
    zero = jnp.zeros_like(vjt)
    row = lax.broadcasted_iota(jnp.int32, (DEN_ROWS, 2 * tk), 0)
    col = lax.broadcasted_iota(jnp.int32, (DEN_ROWS, 2 * tk), 1)
    ones = jnp.where(((row == 0) & (col < tk)) | ((row == 1) & (col >= tk)), 1.0, 0.0)
    w_t = jnp.concatenate([jnp.concatenate([vjt, zero], axis=1),
                           jnp.concatenate([zero, vjt], axis=1), ones], axis=0).astype(BF16)
    return k_cat, w_t


def _scores_t(qc, k_cat):
    return lax.dot_general(k_cat, qc.astype(BF16), (((1,), (1,)), ((), ())), preferred_element_type=F32)


def _softmax_t(s_t, sinks=None, mask_t=None):
    tk = s_t.shape[0] // 2
    es, ms = [], []
    for hh in range(2):
        sh = s_t[hh * tk:(hh + 1) * tk]
        if mask_t is not None:
            sh = jnp.where(mask_t, sh, NEG_BIG)
        m = jnp.max(sh, axis=0, keepdims=True)
        if sinks is not None:
            m = jnp.maximum(m, sinks[hh])
        es.append(jnp.exp2(sh - m))
        ms.append(m)
    return jnp.concatenate(es, axis=0).astype(BF16), ms


def _values_t(e_t, w_t):
    return jnp.dot(w_t, e_t, preferred_element_type=F32)


def _normalise_t(nd, ms, sinks=None):
    dens = [nd[2 * HEAD_DIM + hh:2 * HEAD_DIM + hh + 1] for hh in range(2)]
    if sinks is not None:
        dens = [dens[hh] + jnp.exp2(sinks[hh] - ms[hh]) for hh in range(2)]
    inv = [1.0 / d for d in dens]
    o_t = jnp.concatenate([nd[0:HEAD_DIM] * inv[0], nd[HEAD_DIM:2 * HEAD_DIM] * inv[1]], axis=0)
    return o_t.T


def _run_skewed(items, stages):
    state = list(items)
    for t in range(len(items) + len(stages) - 1):
        for k, stage in enumerate(stages):
            if 0 <= t - k < len(items):
                state[t - k] = stage(state[t - k])


def _attention_chunks(items, read_q, read_gate, write_out):
    def scores(it):
        return dict(it, s_t=_scores_t(read_q(it), it["k_cat"]))

    def softmax(it):
        e_t, ms = _softmax_t(it["s_t"], it["sinks"], it["mask"])
        return dict(it, e_t=e_t, ms=ms, s_t=None)

    def values(it):
        return dict(it, nd=_values_t(it["e_t"], it["w_t"]), e_t=None)

    def finish(it):
        o = _normalise_t(it["nd"], it["ms"], it["sinks"])
        write_out(it, (o * read_gate(it)).astype(BF16))
        return None

    def hold(it):
        return it

    stages = [scores]
    for gap, stage in zip(SKEW_GAPS, (softmax, values, finish)):
        stages += [hold] * gap + [stage]
    _run_skewed(items, stages)


def _ada_kernel(cond_ref, w_ref, b_ref, o_ref):
    a = _silu(cond_ref[...]).astype(BF16)
    o_ref[...] = jnp.dot(a, w_ref[...].astype(BF16), preferred_element_type=F32) + b_ref[...]


def _ada_all(cond, ada_w, ada_b):
    return pl.pallas_call(
        _ada_kernel,
        grid=(DEPTH, 3),
        in_specs=[
            pl.BlockSpec((COND_ROWS, D_MODEL), lambda l, p: (0, 0)),
            pl.BlockSpec((None, D_MODEL, D_MODEL), lambda l, p: (l, 0, p)),
            pl.BlockSpec((None, None, 1, D_MODEL), lambda l, p: (l, p, 0, 0)),
        ],
        out_specs=pl.BlockSpec((None, None, COND_ROWS, D_MODEL), lambda l, p: (l, p, 0, 0)),
        out_shape=jax.ShapeDtypeStruct((DEPTH, 3, COND_ROWS, D_MODEL), F32),
        compiler_params=pltpu.CompilerParams(
            dimension_semantics=("parallel", "parallel"), vmem_limit_bytes=VMEM_LIMIT),
        name="ada_mod",
    )(cond, ada_w, ada_b.reshape(DEPTH, 3, 1, D_MODEL))


def _ctx_even_mix(i, sink_ref, gq_ref, gk_ref, kv_refs, p_ref, o_ref):
    n_b, seq, _ = p_ref.shape
    gq2 = gq_ref[i:i + 1, :] * Q_SCALE
    gk2 = gk_ref[i:i + 1, :]
    items = []
    for mixer, (q0, k0, v0, g0) in enumerate(((QA, KA, VA, GA), (QB, KB, VB, GB))):
        kv = []
        for bb in range(n_b):
            k2 = p_ref[bb, :, k0:k0 + KV_W]
            v2 = p_ref[bb, :, v0:v0 + KV_W]
            if mixer == 0:
                k2 = _rms_pair(k2, gk2)
            v2t = v2.T
            kv_refs[2 * mixer][bb, i] = k2.T.reshape(N_KV, HEAD_DIM, seq)
            kv_refs[2 * mixer + 1][bb, i] = v2t.reshape(N_KV, HEAD_DIM, seq)
            kv.append((k2, v2t))
        for j in range(N_KV):
            ops = [_kv_operands(k2, v2t, j) for k2, v2t in kv]
            for cc in range(GROUP // 2):
                c0 = CHUNK * (j * (GROUP // 2) + cc)
                sinks = None if mixer == 0 else _sink_pair(sink_ref, i, c0 // HEAD_DIM)
                for bb, (k_cat, w_t) in enumerate(ops):
                    items.append(dict(rows=bb, q=q0 + c0, g=g0 + c0, o=mixer * MIX_W + c0,
                                      norm=mixer == 0, sinks=sinks, mask=None, k_cat=k_cat, w_t=w_t))

    def read_q(it):
        qc = p_ref[it["rows"], :, it["q"]:it["q"] + CHUNK]
        return _rms_pair(qc, gq2) if it["norm"] else qc * Q_SCALE

    def read_gate(it):
        return _silu(p_ref[it["rows"], :, it["g"]:it["g"] + CHUNK])

    def write_out(it, value):
        o_ref[it["rows"], :, it["o"]:it["o"] + CHUNK] = value

    _attention_chunks(items, read_q, read_gate, write_out)


def _spectrum_operands(t):
    upper = [t[:, C_GROUP_DIM * g + HALF_GROUP:C_GROUP_DIM * (g + 1)] for g in range(C_GROUPS)]
    return t.astype(BF16), jnp.concatenate(upper, axis=1).astype(BF16)


def _spectrum_combine(m_cos, m_sin, read_gate, write_out):
    first = lax.broadcasted_iota(jnp.int32, (m_cos.shape[0], HALF_GROUP), 1) == 0
    for g in range(C_GROUPS):
        c0 = C_GROUP_DIM * g
        p = m_cos[:, c0:c0 + HALF_GROUP]
        p_mid = m_cos[:, c0 + HALF_GROUP:c0 + C_GROUP_DIM]
        q = jnp.where(first, 0.0, m_sin[:, HALF_GROUP * g:HALF_GROUP * (g + 1)])
        write_out(c0, ((p - q) * read_gate(c0)).astype(BF16))
        upper = jnp.where(first, p_mid, p + q)
        write_out(c0 + HALF_GROUP, (upper * read_gate(c0 + HALF_GROUP)).astype(BF16))


def _ctx_odd_mix(cs_ref, ss_ref, p_ref, o_ref):
    n_b = p_ref.shape[0]
    for bb in range(n_b):
        for_cos, for_sin = _spectrum_operands(p_ref[bb, :, 0:D_MODEL])
        m_cos = jnp.dot(cs_ref[...], for_cos, preferred_element_type=F32)
        m_sin = jnp.dot(ss_ref[...], for_sin, preferred_element_type=F32)

        def read_gate(c0, bb=bb):
            return _silu(p_ref[bb, :, D_MODEL + c0:D_MODEL + c0 + HALF_GROUP])

        def write_out(c0, value, bb=bb):
            o_ref[bb, :, c0:c0 + HALF_GROUP] = value

        _spectrum_combine(m_cos, m_sin, read_gate, write_out)


def _fold_odd_weights(owin_ref, owout_ref, epack_ref, perm_ref):
    for layer in range(owin_ref.shape[0]):
        for g in range(C_GROUPS):
            c0 = C_GROUP_DIM * g
            owin_ref[layer, :, c0:c0 + C_GROUP_DIM] = jnp.dot(
                owin_ref[layer, :, c0:c0 + C_GROUP_DIM], epack_ref[...],
                preferred_element_type=F32).astype(BF16)
            g0 = D_MODEL + c0 + HALF_GROUP
            owin_ref[layer, :, g0:g0 + HALF_GROUP] = jnp.dot(
                owin_ref[layer, :, g0:g0 + HALF_GROUP], perm_ref[...], preferred_element_type=F32).astype(BF16)
            r0 = c0 + HALF_GROUP
            owout_ref[layer, r0:r0 + HALF_GROUP, :] = jnp.dot(
                perm_ref[...], owout_ref[layer, r0:r0 + HALF_GROUP, :], preferred_element_type=F32).astype(BF16)


def _stage_copy(w_hbm, stage_ref, sem, chunk, slot):
    _, n_rows, cols = w_hbm.shape
    rows = stage_ref.shape[1]
    per_layer = n_rows // rows
    src = w_hbm.at[chunk // per_layer, pl.ds((chunk % per_layer) * rows, rows), :]
    return pltpu.make_async_copy(src, stage_ref.at[slot, :, pl.ds(0, cols)], sem.at[slot])


def _convert_weights(w_hbm, w_ref, stage_ref, in_sem):
    n_layers, n_rows, cols = w_hbm.shape
    n_slots, rows, _ = stage_ref.shape
    per_layer = n_rows // rows
    n_chunks = n_layers * per_layer
    for c in range(n_slots - 1):
        _stage_copy(w_hbm, stage_ref, in_sem, c, c).start()

    def body(c, carry):
        slot = c % n_slots
        ahead = c + n_slots - 1

        @pl.when(ahead < n_chunks)
        def _():
            _stage_copy(w_hbm, stage_ref, in_sem, ahead, ahead % n_slots).start()

        _stage_copy(w_hbm, stage_ref, in_sem, c, slot).wait()
        r0 = pl.multiple_of((c % per_layer) * rows, rows)
        w_ref[c // per_layer, pl.ds(r0, rows), :] = stage_ref[slot, :, 0:cols].astype(BF16)
        return carry

    lax.fori_loop(0, n_chunks, body, 0)


def _ctx_kernel(sink_ref, x_ref, mod_ref, g_ref, ewin_hbm, ewout_hbm, owin_hbm, owout_hbm, gq_ref, gk_ref,
                epack_ref, perm_ref, cs_ref, ss_ref, fg_ref,
                xo_ref, ka_ref, va_ref, kb_ref, vb_ref, ewin_o, ewout_o, owin_o, owout_o,
                xs_ref, p_ref, o_ref, ewin_ref, ewout_ref, owin_ref, owout_ref,
                stage_ref, in_sem, out_sem):
    weights = ((ewin_hbm, ewin_o, ewin_ref), (ewout_hbm, ewout_o, ewout_ref),
               (owin_hbm, owin_o, owin_ref), (owout_hbm, owout_o, owout_ref))
    exports = [pltpu.make_async_copy(res, dst, out_sem.at[k]) for k, (_, dst, res) in enumerate(weights)]

    @pl.when(pl.program_id(0) == 0)
    def _():
        for src, _, res in weights:
            _convert_weights(src, res, stage_ref, in_sem)
        _fold_odd_weights(owin_ref, owout_ref, epack_ref, perm_ref)
        for export in exports:
            export.start()

    @pl.when(pl.program_id(0) == pl.num_programs(0) - 1)
    def _():
        for export in exports:
            export.wait()

    n_b = x_ref.shape[0]
    for l in range(DEPTH):
        i = l // 2
        even = l % 2 == 0
        win_ref, wout_ref, width = (ewin_ref, ewout_ref, EVEN_IN) if even else (owin_ref, owout_ref, ODD_IN)
        shift, scale, gate = (mod_ref[l, part, 0:1, :] for part in range(3))
        for bb in range(n_b):
            x = x_ref[bb] if l == 0 else xs_ref[bb]
            h = _norm_mod(x, g_ref[l:l + 1, :], shift, scale)
            p_ref[bb, :, 0:width] = jnp.dot(h.astype(BF16), win_ref[i], preferred_element_type=F32)
        if even:
            _ctx_even_mix(i, sink_ref, gq_ref, gk_ref, (ka_ref, va_ref, kb_ref, vb_ref), p_ref, o_ref)
        else:
            _ctx_odd_mix(cs_ref, ss_ref, p_ref, o_ref)
        for bb in range(n_b):
            y = jnp.dot(o_ref[bb], wout_ref[i], preferred_element_type=F32)
            xn = (x_ref[bb] if l == 0 else xs_ref[bb]) + gate * y
            if l == DEPTH - 1:
                xo_ref[bb] = _rms_norm(xn, fg_ref[...])
            else:
                xs_ref[bb] = xn


def _resident(shape):
    return pl.BlockSpec(shape, lambda i: (0,) * len(shape), pipeline_mode=pl.Buffered(1))


def _ctx_path(x, mod, norm_g, ewin, ewout, owin, owout, gq2, gk2, sink, epack, perm, cs, ss, fg):
    b, s, _ = x.shape
    n_even = ewin.shape[0]
    bb = CTX_BATCH_PER_STEP
    kv_shape = jax.ShapeDtypeStruct((b, n_even, N_KV, HEAD_DIM, s), F32)
    kv_spec = pl.BlockSpec((bb, n_even, N_KV, HEAD_DIM, s), lambda i: (i, 0, 0, 0, 0))
    x_spec = pl.BlockSpec((bb, s, D_MODEL), lambda i: (i, 0, 0))
    hbm = pl.BlockSpec(memory_space=pl.ANY)
    weights = (ewin, ewout, owin, owout)
    return pl.pallas_call(
        _ctx_kernel,
        grid=(b // bb,),
        in_specs=[
            pl.BlockSpec(memory_space=pltpu.SMEM),
            x_spec,
            _resident(mod.shape), _resident(norm_g.shape),
            hbm, hbm, hbm, hbm,
            _resident(gq2.shape), _resident(gk2.shape),
            _resident(epack.shape), _resident(perm.shape), _resident(cs.shape), _resident(ss.shape),
            _resident(fg.shape),
        ],
        out_specs=[x_spec, kv_spec, kv_spec, kv_spec, kv_spec, hbm, hbm, hbm, hbm],
        out_shape=[jax.ShapeDtypeStruct(x.shape, F32), kv_shape, kv_shape, kv_shape, kv_shape]
                  + [jax.ShapeDtypeStruct(w.shape, BF16) for w in weights],
        scratch_shapes=[pltpu.VMEM((bb, s, D_MODEL), F32), pltpu.VMEM((bb, s, EVEN_IN), F32),
                        pltpu.VMEM((bb, s, D_MODEL), BF16)]
                       + [pltpu.VMEM(w.shape, BF16) for w in weights]
                       + [pltpu.VMEM((STAGE_SLOTS, STAGE_ROWS, max(w.shape[2] for w in weights)), F32),
                          pltpu.SemaphoreType.DMA((STAGE_SLOTS,)), pltpu.SemaphoreType.DMA((len(weights),))],
        compiler_params=pltpu.CompilerParams(
            dimension_semantics=("arbitrary",), vmem_limit_bytes=CTX_VMEM_LIMIT),
        name="ctx_path",
    )(sink, x, mod, norm_g, *weights, gq2, gk2, epack, perm, cs, ss, fg)


def _lat_weight_copies(l, w_hbms, win_buf, wout_buf, w_sem):
    ewin_hbm, ewout_hbm, owin_hbm, owout_hbm = w_hbms
    i, slot = l // 2, l % 2
    w_in, w_out = (ewin_hbm, ewout_hbm) if l % 2 == 0 else (owin_hbm, owout_hbm)
    return (pltpu.make_async_copy(w_in.at[i], win_buf.at[slot, :, pl.ds(0, w_in.shape[2])], w_sem.at[2 * slot]),
            pltpu.make_async_copy(w_out.at[i], wout_buf.at[slot], w_sem.at[2 * slot + 1]))


def _lat_attention_items(n, layer_i, sink_ref, kv_ref, kcat_ref, wt_ref, cache_refs):
    seq = kv_ref.shape[1]
    ckb_t, cvb_t = cache_refs[2][layer_i], cache_refs[3][layer_i]
    prev0 = pl.multiple_of(jnp.maximum(n * ROWS - WINDOW, 0), WINDOW)
    own0 = pl.multiple_of(n * ROWS, ROWS)
    next0 = pl.multiple_of(jnp.minimum(n * ROWS + ROWS, seq - WINDOW), WINDOW)
    win_len = ROWS + 2 * WINDOW
    ctx_len = ckb_t.shape[1]
    cj = lax.broadcasted_iota(jnp.int32, (win_len + ctx_len, ROWS), 0)
    qi = lax.broadcasted_iota(jnp.int32, (win_len + ctx_len, ROWS), 1)
    kpos = n * ROWS - WINDOW + cj
    in_win = (jnp.abs(cj - WINDOW - qi) <= WINDOW) & (kpos >= 0) & (kpos < seq)
    mask_b = in_win | (cj >= win_len)

    def window(idx):
        return jnp.concatenate([kv_ref[idx, pl.ds(prev0, WINDOW), :], kv_ref[idx, pl.ds(own0, ROWS), :],
                                kv_ref[idx, pl.ds(next0, WINDOW), :]], axis=0)

    k2b = jnp.concatenate([window(2), ckb_t.T], axis=0)
    v2bt = jnp.concatenate([window(3).T, cvb_t], axis=1)
    items = []
    for mixer, (q0, g0) in enumerate(((LQA, LGA), (LQB, LGB))):
        for j in range(N_KV):
            k_cat, w_t = (kcat_ref[j], wt_ref[j]) if mixer == 0 else _kv_operands(k2b, v2bt, j)
            for cc in range(GROUP // 2):
                c0 = CHUNK * (j * (GROUP // 2) + cc)
                if mixer == 0:
                    sinks, mask = None, None
                else:
                    sinks, mask = _sink_pair(sink_ref, layer_i, c0 // HEAD_DIM), mask_b
                items.append(dict(q=q0 + c0, g=g0 + c0, o=mixer * MIX_W + c0, sinks=sinks, mask=mask,
                                  k_cat=k_cat, w_t=w_t))
    return items


def _lat_kernel(sink_ref, x_hbm, mod_ref, g_ref, ewin_hbm, ewout_hbm, owin_hbm, owout_hbm, gq_ref, gk_ref,
                cos_ref, sn_ref, sp_ref, cka_ref, cva_ref, ckb_ref, cvb_ref, cs_ref, ss_ref, fg_ref,
                y_hbm,
                xw_ref, win_buf, wout_buf, kv_ref, kcat_ref, wt_ref, p_ref, o_ref, fcos_ref, fsin_ref, gate_ref,
                x_sem, w_sem):
    b = pl.program_id(0)
    seq = xw_ref.shape[0]
    n_blocks = seq // ROWS
    w_hbms = (ewin_hbm, ewout_hbm, owin_hbm, owout_hbm)

    x_in = pltpu.make_async_copy(x_hbm.at[b], xw_ref, x_sem.at[0])
    x_in.start()

    def start_weights(l):
        for copy in _lat_weight_copies(l, w_hbms, win_buf, wout_buf, w_sem):
            copy.start()

    pl.when(b == 0)(lambda: start_weights(0))
    x_in.wait()

    for l in range(DEPTH):
        i, slot = l // 2, l % 2
        for copy in _lat_weight_copies(l, w_hbms, win_buf, wout_buf, w_sem):
            copy.wait()
        if l + 1 < DEPTH:
            start_weights(l + 1)
        else:
            pl.when(b + 1 < pl.num_programs(0))(lambda: start_weights(0))
        shift, scale, gate = (mod_ref[l, part, pl.ds(1 + b, 1), :] for part in range(3))
        g = g_ref[l:l + 1, :]

        def normed(rows):
            return _norm_mod(xw_ref[rows, :], g, shift, scale).astype(BF16)

        def w_in(c0, width):
            return win_buf[slot, :, c0:c0 + width]

        if l % 2 == 0:
            gq2 = gq_ref[i:i + 1, :] * Q_SCALE
            gk2 = gk_ref[i:i + 1, :]

            def project_kv(n, carry):
                rows = pl.ds(pl.multiple_of(n * ROWS, ROWS), ROWS)
                h = normed(rows)
                cos, sn, sp = cos_ref[rows, :], sn_ref[rows, :], sp_ref[rows, :]
                for mixer, c0 in enumerate((KA, KB)):
                    kv = jnp.dot(h, w_in(c0, 2 * KV_W), preferred_element_type=F32)
                    k2 = kv[:, :KV_W]
                    if mixer == 0:
                        k2 = _rms_pair(k2, gk2)
                    kv_ref[2 * mixer, rows, :] = _rope(k2, cos, sn, sp)
                    kv_ref[2 * mixer + 1, rows, :] = kv[:, KV_W:]
                return carry

            lax.fori_loop(0, n_blocks, project_kv, 0)
            k2a = jnp.concatenate([kv_ref[0], cka_ref[i].T], axis=0)
            v2at = jnp.concatenate([kv_ref[1].T, cva_ref[i]], axis=1)
            for j in range(N_KV):
                kcat_ref[j], wt_ref[j] = _kv_operands(k2a, v2at, j)

            def attend(n, carry):
                rows = pl.ds(pl.multiple_of(n * ROWS, ROWS), ROWS)
                h = normed(rows)
                cos, sn, sp = cos_ref[rows, :], sn_ref[rows, :], sp_ref[rows, :]
                qa = jnp.dot(h, w_in(QA, MIX_W), preferred_element_type=F32)
                for c0 in range(0, MIX_W, CHUNK):
                    p_ref[:, LQA + c0:LQA + c0 + CHUNK] = _rope(_rms_pair(qa[:, c0:c0 + CHUNK], gq2), cos, sn, sp)
                mid = jnp.dot(h, w_in(GA, 2 * MIX_W), preferred_element_type=F32)
                p_ref[:, LGA:LGA + MIX_W] = _silu(mid[:, :MIX_W])
                for c0 in range(0, MIX_W, CHUNK):
                    qb = mid[:, MIX_W + c0:MIX_W + c0 + CHUNK] * Q_SCALE
                    p_ref[:, LQB + c0:LQB + c0 + CHUNK] = _rope(qb, cos, sn, sp)
                p_ref[:, LGB:LGB + MIX_W] = _silu(jnp.dot(h, w_in(GB, MIX_W), preferred_element_type=F32))
                items = _lat_attention_items(n, i, sink_ref, kv_ref, kcat_ref, wt_ref,
                                             (cka_ref, cva_ref, ckb_ref, cvb_ref))

                def write_out(it, value):
                    o_ref[:, it["o"]:it["o"] + CHUNK] = value

                _attention_chunks(items, lambda it: p_ref[:, it["q"]:it["q"] + CHUNK],
                                  lambda it: p_ref[:, it["g"]:it["g"] + CHUNK], write_out)
                y = jnp.dot(o_ref[...], wout_buf[slot], preferred_element_type=F32)
                xw_ref[rows, :] = xw_ref[rows, :] + gate * y
                return carry

            lax.fori_loop(0, n_blocks, attend, 0)
        else:
            def project(n, carry):
                rows = pl.ds(pl.multiple_of(n * ROWS, ROWS), ROWS)
                p = jnp.dot(normed(rows), w_in(0, ODD_IN), preferred_element_type=F32)
                fcos_ref[rows, :], fsin_ref[rows, :] = _spectrum_operands(p[:, 0:D_MODEL])
                gate_ref[rows, :] = _silu(p[:, D_MODEL:ODD_IN])
                return carry

            lax.fori_loop(0, n_blocks, project, 0)

            def mix(n, carry):
                rows = pl.ds(pl.multiple_of(n * ROWS, ROWS), ROWS)
                m_cos = jnp.dot(cs_ref[rows, :], fcos_ref[...], preferred_element_type=F32)
                m_sin = jnp.dot(ss_ref[rows, :], fsin_ref[...], preferred_element_type=F32)

                def write_out(c0, value):
                    o_ref[:, c0:c0 + HALF_GROUP] = value

                _spectrum_combine(m_cos, m_sin, lambda c0: gate_ref[rows, c0:c0 + HALF_GROUP], write_out)
                y = jnp.dot(o_ref[...], wout_buf[slot], preferred_element_type=F32)
                xn = xw_ref[rows, :] + gate * y
                if l == DEPTH - 1:
                    xn = _rms_norm(xn, fg_ref[...])
                xw_ref[rows, :] = xn
                return carry

            lax.fori_loop(0, n_blocks, mix, 0)

    y_out = pltpu.make_async_copy(xw_ref, y_hbm.at[b], x_sem.at[1])
    y_out.start()
    y_out.wait()


def _lat_path(x, mod, norm_g, ewin, ewout, owin, owout, gq2, gk2, sink, cos, sn, sp, caches, cs, ss, fg):
    b, s, _ = x.shape
    n_even, past = caches[0].shape[1], caches[0].shape[3]
    hbm = pl.BlockSpec(memory_space=pl.ANY)
    cache_spec = pl.BlockSpec((None, n_even, KV_W, past), lambda i: (i, 0, 0, 0))
    return pl.pallas_call(
        _lat_kernel,
        grid=(b,),
        in_specs=[
            pl.BlockSpec(memory_space=pltpu.SMEM),
            hbm,
            _resident(mod.shape), _resident(norm_g.shape),
            hbm, hbm, hbm, hbm,
            _resident(gq2.shape), _resident(gk2.shape),
            _resident(cos.shape), _resident(sn.shape), _resident(sp.shape),
            cache_spec, cache_spec, cache_spec, cache_spec,
            _resident(cs.shape), _resident(ss.shape), _resident(fg.shape),
        ],
        out_specs=hbm,
        out_shape=jax.ShapeDtypeStruct(x.shape, F32),
        scratch_shapes=[
            pltpu.VMEM((s, D_MODEL), F32),
            pltpu.VMEM((2, D_MODEL, EVEN_IN), BF16),
            pltpu.VMEM((2, D_MODEL, D_MODEL), BF16),
            pltpu.VMEM((4, s, KV_W), F32),
            pltpu.VMEM((N_KV, 2 * (s + past), KV_W), BF16),
            pltpu.VMEM((N_KV, 2 * HEAD_DIM + DEN_ROWS, 2 * (s + past)), BF16),
            pltpu.VMEM((ROWS, 4 * MIX_W), F32),
            pltpu.VMEM((ROWS, D_MODEL), BF16),
            pltpu.VMEM((s, D_MODEL), BF16),
            pltpu.VMEM((s, D_MODEL // 2), BF16),
            pltpu.VMEM((s, D_MODEL), F32),
            pltpu.SemaphoreType.DMA((2,)), pltpu.SemaphoreType.DMA((4,)),
        ],
        compiler_params=pltpu.CompilerParams(
            dimension_semantics=("arbitrary",), vmem_limit_bytes=LAT_VMEM_LIMIT),
        name="lat_path",
    )(sink, x, mod, norm_g, ewin, ewout, owin, owout, gq2, gk2, cos, sn, sp, *caches, cs, ss, fg)


def kernel(x_prompt, x_sample, cache_k_a, cache_v_a, cache_k_b, cache_v_b, c, c_ctx, norm_g, ada_w, ada_b,
           even_w_in, even_w_out, qk_g_q, qk_g_k, sink_logit, odd_w_in, odd_w_out, final_g):
    batch, seq, _ = x_prompt.shape
    dec_batch, dec_seq, _ = x_sample.shape
    n_even = even_w_in.shape[0]
    past = cache_k_a.shape[2]

    cond = jnp.concatenate(
        [c_ctx[None, :], c, jnp.zeros((COND_ROWS - 1 - dec_batch, D_MODEL), F32)], axis=0)
    mod = _ada_all(cond, ada_w, ada_b)

    epack = jnp.asarray(_packed_channel_dft()).astype(BF16)
    perm = jnp.asarray(_mirror_perm()).astype(BF16)
    cs_ctx, ss_ctx = (jnp.asarray(t).astype(BF16) for t in _position_dft(seq))
    cs_lat, ss_lat = (jnp.asarray(t).astype(BF16) for t in _position_dft(dec_seq))
    cos, sn, sp = (jnp.asarray(t) for t in _rope_tables(dec_seq))

    caches = [jnp.transpose(a, (0, 1, 3, 4, 2)).reshape(dec_batch, n_even, KV_W, past)
              for a in (cache_k_a, cache_v_a, cache_k_b, cache_v_b)]
    fg = final_g.reshape(1, D_MODEL)
    gq2 = jnp.tile(qk_g_q, (1, 2))
    gk2 = jnp.tile(qk_g_k, (1, 2))

    xc, *outs = _ctx_path(x_prompt, mod, norm_g, even_w_in, even_w_out, odd_w_in, odd_w_out, gq2, gk2,
                          sink_logit, epack, perm, cs_ctx, ss_ctx, fg)
    new_kv = [jnp.transpose(a, (0, 1, 4, 2, 3)) for a in outs[:4]]
    ewin, ewout, owin, owout = outs[4:]

    xl = _lat_path(x_sample, mod, norm_g, ewin, ewout, owin, owout, gq2, gk2, sink_logit, cos, sn, sp,
                   caches, cs_lat, ss_lat, fg)
    return (xc, xl, *new_kv)
```
